```python
import math
import jax, jax.numpy as jnp
from jax import lax
import numpy as np

D_MODEL = 1024
BATCH = 8
SEQ = 2048
DEPTH = 2
DEC_BATCH = 32
DEC_SEQ = 1
PAST_LEN = 16384
PAGE_SIZE = 128

D_PLE = 256
N_CONV_LAYERS = (DEPTH + 1) // 2
N_ATTN_LAYERS = DEPTH // 2
CONV_W = 31
N_HEADS = 16
HEAD_DIM = D_MODEL // N_HEADS
N_KV = 4
HPG = N_HEADS // N_KV
KV_W = N_KV * HEAD_DIM
Q_W = N_HEADS * HEAD_DIM
IN_COLS = Q_W + 6 * KV_W + 3 * N_HEADS
CMP_STRIDE = 16
CMP_LEN = 2 * CMP_STRIDE
CMP_HID = 2 * HEAD_DIM
SLC_BLOCK = 64
N_SELECT = 16
WINDOW = 512
NUM_BUCKETS = 32
MAX_DISTANCE = 128
D_FF = 4 * D_MODEL
EPS = 1e-6
SCALE = HEAD_DIM ** -0.5
WIN_QBLOCK = 128
SEL_QBLOCK = 32

kernel_name = 'hybrid_conformer_nsa_decode_step'


def rmsnorm(x, g):
    xf = x.astype(jnp.float32)
    y = xf * lax.rsqrt(jnp.mean(xf * xf, axis=-1, keepdims=True) + EPS)
    return (y * g.astype(jnp.float32)).astype(x.dtype)


def layernorm(x, g, b):
    xf = x.astype(jnp.float32)
    mu = jnp.mean(xf, axis=-1, keepdims=True)
    var = jnp.mean(jnp.square(xf - mu), axis=-1, keepdims=True)
    y = (xf - mu) * lax.rsqrt(var + EPS)
    return (y * g.astype(jnp.float32) + b.astype(jnp.float32)).astype(x.dtype)


def masked_softmax(s, mask):
    s = jnp.where(mask, s.astype(jnp.float32), -jnp.inf)
    m = jnp.max(s, axis=-1, keepdims=True)
    m = jnp.where(jnp.isfinite(m), m, 0.0)
    e = jnp.where(mask, jnp.exp(s - m), 0.0)
    return e / jnp.maximum(jnp.sum(e, axis=-1, keepdims=True), 1e-30)


def rel_bucket(dist):
    n = jnp.maximum(dist, 0)
    max_exact = NUM_BUCKETS // 2
    nf = jnp.maximum(n, 1).astype(jnp.float32)
    large = max_exact + (jnp.log(nf / max_exact) / math.log(MAX_DISTANCE / max_exact)
                         * (NUM_BUCKETS - max_exact)).astype(jnp.int32)
    large = jnp.minimum(large, NUM_BUCKETS - 1)
    return jnp.where(n < max_exact, n, large)


def head_bias(table, dist):
    b = table[rel_bucket(dist)].astype(jnp.float32)
    return b.reshape(dist.shape + (N_KV, HPG)).transpose(0, 2, 3, 1)


def compress(k, w1, pe, w2):
    b, t, g, hd = k.shape
    c = -(-t // CMP_STRIDE)
    k = jnp.pad(k, ((0, 0), (0, c * CMP_STRIDE - t), (0, 0), (0, 0)))
    kc = k.reshape(b, c, CMP_STRIDE, g, hd).transpose(0, 1, 3, 2, 4).reshape(b, c, g, CMP_STRIDE * hd)
    half = CMP_STRIDE * hd
    h = kc[:, :-1] @ w1[:half] + kc[:, 1:] @ w1[half:] + pe.reshape(-1) @ w1
    return jax.nn.silu(h) @ w2


def cmp_branch(qg, kcmp, vcmp, qpos, table):
    nc = kcmp.shape[1]
    ends = CMP_STRIDE * jnp.arange(nc, dtype=jnp.int32) + (CMP_LEN - 1)
    dist = qpos[:, None] - ends[None, :]
    s = jnp.einsum('btghd,bngd->btghn', qg, kcmp).astype(jnp.float32) * SCALE + head_bias(table, dist)
    mask = (ends[None, :] <= qpos[:, None])[:, None, None, :]
    p = masked_softmax(s, mask)
    o = jnp.einsum('btghn,bngd->btghd', p.astype(vcmp.dtype), vcmp)
    return o, p


def select_blocks(p_cmp, qpos, n_blocks):
    pg = jnp.sum(p_cmp, axis=3)
    nc = pg.shape[-1]
    r = SLC_BLOCK // CMP_STRIDE
    m = CMP_LEN // CMP_STRIDE
    pad = jnp.pad(pg, ((0, 0), (0, 0), (0, 0), (m - 1, r * n_blocks - nc)))
    score = sum(pad[..., a - bb + m - 1: a - bb + m - 1 + r * n_blocks: r]
                for a in range(r) for bb in range(m))
    j = jnp.arange(n_blocks, dtype=jnp.int32)[None, :]
    cur = (qpos // SLC_BLOCK)[:, None]
    valid = j * SLC_BLOCK <= qpos[:, None]
    forced = (j == 0) | (j == cur) | (j == cur - 1)
    score = jnp.where(forced[:, None, :], jnp.inf, jnp.where(valid[:, None, :], score, -jnp.inf))
    _, idx = lax.top_k(score, min(N_SELECT, n_blocks))
    return idx.astype(jnp.int32)


def slc_attend(qg, idx, qpos, ksb, vsb, table_g):
    b, q, g, nsel = idx.shape
    flat = idx.transpose(0, 2, 1, 3).reshape(b, g, q * nsel)
    take = jax.vmap(jax.vmap(lambda blocks, i: blocks[i]))
    kg = take(ksb, flat).reshape(b, g, q, nsel * SLC_BLOCK, HEAD_DIM)
    vg = take(vsb, flat).reshape(b, g, q, nsel * SLC_BLOCK, HEAD_DIM)
    kpos = (idx[..., None] * SLC_BLOCK + jnp.arange(SLC_BLOCK, dtype=jnp.int32)).reshape(b, q, g, nsel * SLC_BLOCK)
    dist = qpos[None, :, None, None] - kpos
    bias = table_g[jnp.arange(g)[None, None, :, None], rel_bucket(dist)].astype(jnp.float32)
    bias = bias.transpose(0, 1, 2, 4, 3)
    s = jnp.einsum('bqghd,bgqmd->bqghm', qg, kg).astype(jnp.float32) * SCALE + bias
    mask = (kpos <= qpos[None, :, None, None])[:, :, :, None, :]
    p = masked_softmax(s, mask)
    return jnp.einsum('bqghm,bgqmd->bqghd', p.astype(vg.dtype), vg)


def win_attend(qg, kw, vw, qpos, kpos, table):
    dist = qpos[:, None] - kpos[None, :]
    s = jnp.einsum('bqghd,bkgd->bqghk', qg, kw).astype(jnp.float32) * SCALE + head_bias(table, dist)
    mask = ((kpos[None, :] <= qpos[:, None]) & (kpos[None, :] >= qpos[:, None] - WINDOW)
            & (kpos[None, :] >= 0))[:, None, None, :]
    p = masked_softmax(s, mask)
    return jnp.einsum('bqghk,bkgd->bqghd', p.astype(vw.dtype), vw)


def win_banded(qg, kw, vw, table):
    b, t = qg.shape[:2]
    if t <= WIN_QBLOCK or t % WIN_QBLOCK:
        pos = jnp.arange(t, dtype=jnp.int32)
        return win_attend(qg, kw, vw, pos, pos, table)
    nb = t // WIN_QBLOCK
    span = WIN_QBLOCK + WINDOW
    kp = jnp.pad(kw, ((0, 0), (WINDOW, 0), (0, 0), (0, 0)))
    vp = jnp.pad(vw, ((0, 0), (WINDOW, 0), (0, 0), (0, 0)))
    qb = qg.reshape((b, nb, WIN_QBLOCK) + qg.shape[2:]).swapaxes(0, 1)

    def one(args):
        i, qi = args
        start = i * WIN_QBLOCK
        ki = lax.dynamic_slice_in_dim(kp, start, span, axis=1)
        vi = lax.dynamic_slice_in_dim(vp, start, span, axis=1)
        qpos = start + jnp.arange(WIN_QBLOCK, dtype=jnp.int32)
        kpos = start - WINDOW + jnp.arange(span, dtype=jnp.int32)
        return win_attend(qi, ki, vi, qpos, kpos, table)

    out = lax.map(one, (jnp.arange(nb, dtype=jnp.int32), qb))
    return out.swapaxes(0, 1).reshape(qg.shape)


def map_query_blocks(fn, block, qg, idx, qpos):
    t = qg.shape[1]
    if t <= block or t % block:
        return fn(qg, idx, qpos)
    n = t // block

    def split(a):
        return a.reshape((a.shape[0], n, block) + a.shape[2:]).swapaxes(0, 1)

    out = lax.map(lambda a: fn(a[0], a[1], a[2]), (split(qg), split(idx), qpos.reshape(n, block)))
    return out.swapaxes(0, 1).reshape(qg.shape)


def nsa_mixer(h, pos0, past, w_in, w_out, ck_w1, ck_pe, ck_w2, cv_w1, cv_pe, cv_w2, table):
    b, t, _ = h.shape
    proj = h @ w_in
    cuts = [Q_W + KV_W * i for i in range(7)]
    q, kc, vc, ks, vs, kw, vw, gates = jnp.split(proj, cuts, axis=-1)
    qg = q.reshape(b, t, N_KV, HPG, HEAD_DIM)
    kc, vc, ks, vs, kw, vw = [a.reshape(b, t, N_KV, HEAD_DIM) for a in (kc, vc, ks, vs, kw, vw)]
    gates = jax.nn.sigmoid(gates.astype(jnp.float32)).reshape(b, t, 3, N_KV, HPG, 1)
    qpos = pos0 + jnp.arange(t, dtype=jnp.int32)
    if past is None:
        kc_all, vc_all, ks_all, vs_all = kc, vc, ks, vs
    else:
        pkc, pvc, pks, pvs, wbk, wbv = past
        kc_all = jnp.concatenate([pkc.astype(kc.dtype), kc], axis=1)
        vc_all = jnp.concatenate([pvc.astype(vc.dtype), vc], axis=1)
        ks_all = jnp.concatenate([pks.astype(ks.dtype), ks], axis=1)
        vs_all = jnp.concatenate([pvs.astype(vs.dtype), vs], axis=1)
    kcmp = compress(kc_all, ck_w1, ck_pe, ck_w2)
    vcmp = compress(vc_all, cv_w1, cv_pe, cv_w2)
    o_cmp, p_cmp = cmp_branch(qg, kcmp, vcmp, qpos, table)
    tk = ks_all.shape[1]
    ns = -(-tk // SLC_BLOCK)
    idx = select_blocks(p_cmp, qpos, ns)

    def to_blocks(a):
        a = jnp.pad(a, ((0, 0), (0, ns * SLC_BLOCK - tk), (0, 0), (0, 0)))
        return a.reshape(b, ns, SLC_BLOCK, N_KV, HEAD_DIM).transpose(0, 3, 1, 2, 4)

    ksb, vsb = to_blocks(ks_all), to_blocks(vs_all)
    table_g = table.reshape(NUM_BUCKETS, N_KV, HPG).transpose(1, 0, 2)
    o_slc = map_query_blocks(lambda qq, ii, pp: slc_attend(qq, ii, pp, ksb, vsb, table_g),
                             SEL_QBLOCK, qg, idx, qpos)
    if past is None:
        o_win = win_banded(qg, kw, vw, table)
        keep = min(WINDOW, t)
        win_k, win_v = kw[:, t - keep:], vw[:, t - keep:]
    else:
        wb = wbk.shape[1]
        kw_all = jnp.concatenate([wbk.astype(kw.dtype), kw], axis=1)
        vw_all = jnp.concatenate([wbv.astype(vw.dtype), vw], axis=1)
        kpos = pos0 - wb + jnp.arange(wb + t, dtype=jnp.int32)
        o_win = win_attend(qg, kw_all, vw_all, qpos, kpos, table)
        win_k, win_v = kw_all[:, t:], vw_all[:, t:]
    o = gates[:, :, 0] * o_cmp + gates[:, :, 1] * o_slc + gates[:, :, 2] * o_win
    out = o.astype(h.dtype).reshape(b, t, Q_W) @ w_out
    return out, (kc, vc, ks, vs, win_k, win_v)


def conv_mixer(h, past, w1, b1, dw, dwb, ln_g, ln_b, w2, b2):
    u = h @ w1 + b1
    a, g = jnp.split(u, 2, axis=-1)
    u = a * jax.nn.sigmoid(g)
    if past is None:
        ctx = jnp.pad(u, ((0, 0), (CONV_W - 1, 0), (0, 0)))
    else:
        ctx = jnp.concatenate([past.astype(u.dtype), u], axis=1)
    y = lax.conv_general_dilated(ctx, dw[:, None, :].astype(ctx.dtype), (1,), 'VALID',
                                 dimension_numbers=('NWC', 'WIO', 'NWC'),
                                 feature_group_count=D_MODEL) + dwb
    y = jax.nn.silu(layernorm(y, ln_g, ln_b))
    return y @ w2 + b2, ctx[:, ctx.shape[1] - (CONV_W - 1):]


def trunk(x, p, pos0, conv_past, attn_past, w):
    conv_states, attn_states = [], []
    for i in range(DEPTH):
        j = i // 2
        h = rmsnorm(x, w['norm_mix'][i])
        if i % 2 == 0:
            mix, st = conv_mixer(h, None if conv_past is None else conv_past(j),
                                 w['conv_w1'][j], w['conv_b1'][j], w['conv_dw'][j], w['conv_dwb'][j],
                                 w['conv_ln_g'][j], w['conv_ln_b'][j], w['conv_w2'][j], w['conv_b2'][j])
            conv_states.append(st)
        else:
            mix, st = nsa_mixer(h, pos0, None if attn_past is None else attn_past(j),
                                w['attn_w_in'][j], w['attn_w_out'][j],
                                w['cmpk_w1'][j], w['cmpk_pe'][j], w['cmpk_w2'][j],
                                w['cmpv_w1'][j], w['cmpv_pe'][j], w['cmpv_w2'][j], w['rel_table'])
            attn_states.append(st)
        x = x + mix
        h = rmsnorm(x, w['norm_ffn'][i])
        x = x + jnp.square(jax.nn.relu(h @ w['mlp_up'][i])) @ w['mlp_down'][i]
        gate = jax.nn.sigmoid(rmsnorm(x, w['norm_ple'][i]) @ w['ple_gate'][i])
        x = x + gate * (p[i] @ w['ple_proj'][i])
    y = rmsnorm(x, w['norm_final'])
    conv_new = jnp.stack(conv_states)
    attn_new = [jnp.stack([s[k] for s in attn_states]) for k in range(6)]
    return y, conv_new, attn_new


def setup_inputs(seed: int = 0) -> dict:
    key = jax.random.key(seed)
    keys = iter(jax.random.split(key, 64))

    def nrm(shape, scale):
        return jax.random.normal(next(keys), shape, jnp.float32) * scale

    n_pages = PAST_LEN // PAGE_SIZE
    n_pool = (5 * DEC_BATCH * n_pages + 3) // 4
    wb = min(WINDOW, PAST_LEN)
    na, nc = N_ATTN_LAYERS, N_CONV_LAYERS
    kv_shape = (na, n_pool, PAGE_SIZE, N_KV, HEAD_DIM)
    x_prompt = nrm((BATCH, SEQ, D_MODEL), 1.0)
    x_sample = nrm((DEC_BATCH, DEC_SEQ, D_MODEL), 1.0)
    cache_cmp_k = nrm(kv_shape, 1.0)
    cache_cmp_v = nrm(kv_shape, 1.0)
    cache_slc_k = nrm(kv_shape, 1.0)
    cache_slc_v = nrm(kv_shape, 1.0)
    state_win_k = nrm((na, DEC_BATCH, wb, N_KV, HEAD_DIM), 1.0)
    state_win_v = nrm((na, DEC_BATCH, wb, N_KV, HEAD_DIM), 1.0)
    state_conv = nrm((nc, DEC_BATCH, CONV_W - 1, D_MODEL), 0.5)
    page_table = jax.random.permutation(next(keys), n_pool)[:DEC_BATCH * n_pages].reshape(
        DEC_BATCH, n_pages).astype(jnp.int32)
    p_prompt = nrm((DEPTH, BATCH, SEQ, D_PLE), 1.0)
    p_sample = nrm((DEPTH, DEC_BATCH, DEC_SEQ, D_PLE), 1.0)
    return {
        'x_prompt': x_prompt, 'x_sample': x_sample,
        'cache_cmp_k': cache_cmp_k, 'cache_cmp_v': cache_cmp_v,
        'cache_slc_k': cache_slc_k, 'cache_slc_v': cache_slc_v,
        'state_win_k': state_win_k, 'state_win_v': state_win_v, 'state_conv': state_conv,
        'page_table': page_table, 'p_prompt': p_prompt, 'p_sample': p_sample,
        'rel_table': nrm((NUM_BUCKETS, N_HEADS), 0.5),
        'norm_mix': 1.0 + nrm((DEPTH, D_MODEL), 0.05),
        'norm_ffn': 1.0 + nrm((DEPTH, D_MODEL), 0.05),
        'norm_ple': 1.0 + nrm((DEPTH, D_MODEL), 0.05),
        'norm_final': 1.0 + nrm((D_MODEL,), 0.05),
        'conv_w1': nrm((nc, D_MODEL, 2 * D_MODEL), D_MODEL ** -0.5),
        'conv_b1': nrm((nc, 2 * D_MODEL), 0.02),
        'conv_dw': nrm((nc, CONV_W, D_MODEL), CONV_W ** -0.5),
        'conv_dwb': nrm((nc, D_MODEL), 0.02),
        'conv_ln_g': 1.0 + nrm((nc, D_MODEL), 0.05),
        'conv_ln_b': nrm((nc, D_MODEL), 0.02),
        'conv_w2': nrm((nc, D_MODEL, D_MODEL), D_MODEL ** -0.5),
        'conv_b2': nrm((nc, D_MODEL), 0.02),
        'attn_w_in': nrm((na, D_MODEL, IN_COLS), D_MODEL ** -0.5),
        'attn_w_out': nrm((na, Q_W, D_MODEL), Q_W ** -0.5),
        'cmpk_w1': nrm((na, CMP_LEN * HEAD_DIM, CMP_HID), (CMP_LEN * HEAD_DIM) ** -0.5),
        'cmpk_pe': nrm((na, CMP_LEN, HEAD_DIM), 0.1),
        'cmpk_w2': nrm((na, CMP_HID, HEAD_DIM), CMP_HID ** -0.5),
        'cmpv_w1': nrm((na, CMP_LEN * HEAD_DIM, CMP_HID), (CMP_LEN * HEAD_DIM) ** -0.5),
        'cmpv_pe': nrm((na, CMP_LEN, HEAD_DIM), 0.1),
        'cmpv_w2': nrm((na, CMP_HID, HEAD_DIM), CMP_HID ** -0.5),
        'mlp_up': nrm((DEPTH, D_MODEL, D_FF), D_MODEL ** -0.5),
        'mlp_down': nrm((DEPTH, D_FF, D_MODEL), D_FF ** -0.5),
        'ple_proj': nrm((DEPTH, D_PLE, D_MODEL), D_PLE ** -0.5),
        'ple_gate': nrm((DEPTH, D_MODEL, D_MODEL), D_MODEL ** -0.5),
    }


def reference(x_prompt, x_sample, cache_cmp_k, cache_cmp_v, cache_slc_k, cache_slc_v,
              state_win_k, state_win_v, state_conv, page_table, p_prompt, p_sample,
              rel_table, norm_mix, norm_ffn, norm_ple, norm_final,
              conv_w1, conv_b1, conv_dw, conv_dwb, conv_ln_g, conv_ln_b, conv_w2, conv_b2,
              attn_w_in, attn_w_out, cmpk_w1, cmpk_pe, cmpk_w2, cmpv_w1, cmpv_pe, cmpv_w2,
              mlp_up, mlp_down, ple_proj, ple_gate):
    w = dict(rel_table=rel_table, norm_mix=norm_mix, norm_ffn=norm_ffn, norm_ple=norm_ple,
             norm_final=norm_final, conv_w1=conv_w1, conv_b1=conv_b1, conv_dw=conv_dw,
             conv_dwb=conv_dwb, conv_ln_g=conv_ln_g, conv_ln_b=conv_ln_b, conv_w2=conv_w2,
             conv_b2=conv_b2, attn_w_in=attn_w_in, attn_w_out=attn_w_out, cmpk_w1=cmpk_w1,
             cmpk_pe=cmpk_pe, cmpk_w2=cmpk_w2, cmpv_w1=cmpv_w1, cmpv_pe=cmpv_pe, cmpv_w2=cmpv_w2,
             mlp_up=mlp_up, mlp_down=mlp_down, ple_proj=ple_proj, ple_gate=ple_gate)
    y_prompt, conv_p, attn_p = trunk(x_prompt, p_prompt, 0, None, None, w)
    db = page_table.shape[0]
    past_len = page_table.shape[1] * cache_cmp_k.shape[2]

    def gather(c, j):
        return c[j, page_table].reshape(db, past_len, N_KV, HEAD_DIM)

    def conv_past(j):
        return state_conv[j]

    def attn_past(j):
        return (gather(cache_cmp_k, j), gather(cache_cmp_v, j), gather(cache_slc_k, j),
                gather(cache_slc_v, j), state_win_k[j], state_win_v[j])

    y_sample, conv_s, attn_s = trunk(x_sample, p_sample, past_len, conv_past, attn_past, w)
    cmp_k_p, cmp_v_p, slc_k_p, slc_v_p, win_k_p, win_v_p = attn_p
    cmp_k_s, cmp_v_s, slc_k_s, slc_v_s, win_k_s, win_v_s = attn_s
    return (y_prompt, y_sample, cmp_k_p, cmp_v_p, slc_k_p, slc_v_p, win_k_p, win_v_p, conv_p,
            cmp_k_s, cmp_v_s, slc_k_s, slc_v_s, win_k_s, win_v_s, conv_s)
```

```python
import functools
import math

import numpy as np
import jax
import jax.numpy as jnp
from jax import lax
from jax.experimental import pallas as pl
from jax.experimental.pallas import tpu as pltpu

F32 = jnp.float32
BF16 = jnp.bfloat16

D_MODEL = 1024
D_PLE = 256
CONV_W = 31
N_HEADS = 16
HEAD_DIM = 64
N_KV = 4
HPG = 4
KV_W = N_KV * HEAD_DIM
CMP_STRIDE = 16
CMP_HID = 128
SLC_BLOCK = 64
N_SELECT = 16
WINDOW = 512
NUM_BUCKETS = 32
MAX_DISTANCE = 128
D_FF = 4096
EPS = 1e-6
SCALE = HEAD_DIM ** -0.5
NEG_INF = float("-inf")

VMEM_LIMIT = 56 * 1024 * 1024
HALO = 32


def _sigmoid(x):
    return 1.0 / (1.0 + jnp.exp(-x))


def _rms(x, g):
    return x * lax.rsqrt(jnp.mean(x * x, axis=-1, keepdims=True) + EPS) * g


def _dot(a, b):
    return jnp.dot(a, b, preferred_element_type=F32)


def _split3(x):
    hi = x.astype(BF16)
    r1 = x - hi.astype(F32)
    mid = r1.astype(BF16)
    lo = (r1 - mid.astype(F32)).astype(BF16)
    return hi, mid, lo


def _rel_bucket(dist):
    n = jnp.maximum(dist, 0)
    max_exact = NUM_BUCKETS // 2
    nf = jnp.maximum(n, 1).astype(F32)
    large = max_exact + (jnp.log(nf / max_exact) / math.log(MAX_DISTANCE / max_exact)
                         * (NUM_BUCKETS - max_exact)).astype(jnp.int32)
    large = jnp.minimum(large, NUM_BUCKETS - 1)
    return jnp.where(n < max_exact, n, large)


def _params(sem):
    return pltpu.CompilerParams(dimension_semantics=sem, vmem_limit_bytes=VMEM_LIMIT)


def _const_spec(shape, single=False):
    n = len(shape)
    if single:
        return pl.BlockSpec(shape, lambda *_: (0,) * n, pipeline_mode=pl.Buffered(1))
    return pl.BlockSpec(shape, lambda *_: (0,) * n)


def _conv_in_body(x_ref, g_ref, w_ref, b_ref, o_ref):
    h = _rms(x_ref[...], g_ref[...]).astype(BF16)
    u = _dot(h, w_ref[...]) + b_ref[...]
    o_ref[...] = u[:, :D_MODEL] * _sigmoid(u[:, D_MODEL:])


def _conv_in(x, g, w1, b1, tm):
    n = x.shape[0]
    return pl.pallas_call(
        _conv_in_body,
        grid=(n // tm,),
        in_specs=[pl.BlockSpec((tm, D_MODEL), lambda i: (i, 0)),
                  _const_spec((1, D_MODEL)),
                  _const_spec((D_MODEL, 2 * D_MODEL)),
                  _const_spec((1, 2 * D_MODEL))],
        out_specs=pl.BlockSpec((tm, D_MODEL), lambda i: (i, 0)),
        out_shape=jax.ShapeDtypeStruct((n, D_MODEL), F32),
        compiler_params=_params(("parallel",)),
        name="conv_in",
    )(x, g, w1, b1)


def _ln_silu_proj(y, lng, lnb, w2, b2, x):
    mu = jnp.mean(y, axis=-1, keepdims=True)
    yc = y - mu
    var = jnp.mean(yc * yc, axis=-1, keepdims=True)
    z = yc * lax.rsqrt(var + EPS) * lng + lnb
    z = z * _sigmoid(z)
    return _dot(z.astype(BF16), w2) + b2 + x


def _conv_out_body(tt, cur_ref, halo_ref, x_ref, dw_ref, dwb_ref, lng_ref, lnb_ref, w2_ref, b2_ref,
                   o_ref, ctx_ref, y_ref):
    i = pl.program_id(1)
    n_strip = D_MODEL // 128
    keep = (i > 0).astype(F32)
    for c in range(n_strip):
        ctx_ref[c, 0:HALO, :] = halo_ref[0, :, 128 * c:128 * (c + 1)] * keep
        ctx_ref[c, HALO:, :] = cur_ref[0, :, 128 * c:128 * (c + 1)]

    def strip(c, carry):
        acc = jnp.broadcast_to(dwb_ref[c], (tt, 128))
        for k in range(CONV_W):
            off = k + HALO - (CONV_W - 1)
            acc = acc + ctx_ref[c, off:off + tt, :] * dw_ref[c, k:k + 1, :]
        y_ref[c] = acc
        return carry

    lax.fori_loop(0, n_strip, strip, 0)
    y = jnp.concatenate([y_ref[c] for c in range(n_strip)], axis=1)
    o_ref[0] = _ln_silu_proj(y, lng_ref[...], lnb_ref[...], w2_ref[...], b2_ref[...], x_ref[0])


def _conv_out(glu, x, dw3, dwb3, lng, lnb, w2, b2, tt):
    b, t, _ = glu.shape
    hb = tt // HALO
    return pl.pallas_call(
        functools.partial(_conv_out_body, tt),
        grid=(b, t // tt),
        in_specs=[pl.BlockSpec((1, tt, D_MODEL), lambda bi, i: (bi, i, 0)),
                  pl.BlockSpec((1, HALO, D_MODEL), lambda bi, i: (bi, jnp.maximum(i * hb - 1, 0), 0)),
                  pl.BlockSpec((1, tt, D_MODEL), lambda bi, i: (bi, i, 0)),
                  _const_spec((D_MODEL // 128, CONV_W, 128)),
                  _const_spec((D_MODEL // 128, 1, 128)),
                  _const_spec((1, D_MODEL)),
                  _const_spec((1, D_MODEL)),
                  _const_spec((D_MODEL, D_MODEL)),
                  _const_spec((1, D_MODEL))],
        out_specs=pl.BlockSpec((1, tt, D_MODEL), lambda bi, i: (bi, i, 0)),
        out_shape=jax.ShapeDtypeStruct((b, t, D_MODEL), F32),
        scratch_shapes=[pltpu.VMEM((D_MODEL // 128, tt + HALO, 128), F32),
                        pltpu.VMEM((D_MODEL // 128, tt, 128), F32)],
        compiler_params=_params(("parallel", "arbitrary")),
        name="conv_out",
    )(glu, glu, x, dw3, dwb3, lng, lnb, w2, b2)


def _conv_step_body(st_ref, u_ref, x_ref, dw_ref, dwb_ref, lng_ref, lnb_ref, w2_ref, b2_ref,
                    o_ref, ns_ref):
    nb = st_ref.shape[0]
    hist = CONV_W - 1
    dwh = dw_ref[0:hist, :]
    rows = []
    for bi in range(nb):
        rows.append(jnp.sum(st_ref[bi] * dwh, axis=0, keepdims=True))
        ns_ref[bi, 0:hist - 1, :] = st_ref[bi, 1:hist, :]
        ns_ref[bi, hist - 1:hist, :] = u_ref[bi:bi + 1, :]
    y = jnp.concatenate(rows, axis=0) + u_ref[...] * dw_ref[hist:hist + 1, :] + dwb_ref[...]
    o_ref[...] = _ln_silu_proj(y, lng_ref[...], lnb_ref[...], w2_ref[...], b2_ref[...], x_ref[...])


def _conv_step(state, glu, x, dw, dwb, lng, lnb, w2, b2):
    nb, hist, _ = state.shape
    return pl.pallas_call(
        _conv_step_body,
        grid=(1,),
        in_specs=[_const_spec((nb, hist, D_MODEL)), _const_spec((nb, D_MODEL)), _const_spec((nb, D_MODEL)),
                  _const_spec((CONV_W, D_MODEL)), _const_spec((1, D_MODEL)), _const_spec((1, D_MODEL)),
                  _const_spec((1, D_MODEL)), _const_spec((D_MODEL, D_MODEL)), _const_spec((1, D_MODEL))],
        out_specs=[_const_spec((nb, D_MODEL)), _const_spec((nb, hist, D_MODEL))],
        out_shape=[jax.ShapeDtypeStruct((nb, D_MODEL), F32),
                   jax.ShapeDtypeStruct((nb, hist, D_MODEL), F32)],
        compiler_params=_params(("arbitrary",)),
        name="conv_step",
    )(state, glu, x, dw, dwb, lng, lnb, w2, b2)


def _mlp_body(final, pre, *refs):
    if pre:
        o_in_ref, wo_ref, refs = refs[0], refs[1], refs[2:]
    (x_ref, gf_ref, up_ref, dn_ref, gp_ref, wg_ref, p_ref, wp_ref, gfin_ref,
     o_ref, h_ref, acc_ref, x1_ref) = refs
    j = pl.program_id(1)

    @pl.when(j == 0)
    def _():
        x1 = x_ref[...]
        if pre:
            x1 = x1 + _dot(o_in_ref[...].astype(BF16), wo_ref[...])
        x1_ref[...] = x1
        h_ref[...] = _rms(x1, gf_ref[...]).astype(BF16)
        acc_ref[...] = jnp.zeros_like(acc_ref)

    a = jnp.maximum(_dot(h_ref[...], up_ref[...]), 0.0)
    acc_ref[...] += _dot((a * a).astype(BF16), dn_ref[...])

    @pl.when(j == pl.num_programs(1) - 1)
    def _():
        x2 = x1_ref[...] + acc_ref[...]
        gate = _sigmoid(_dot(_rms(x2, gp_ref[...]).astype(BF16), wg_ref[...]))
        x3 = x2 + gate * _dot(p_ref[...].astype(BF16), wp_ref[...])
        if final:
            x3 = _rms(x3, gfin_ref[...])
        o_ref[...] = x3


def _mlp_ple(x, gf, up, dn, gp, wg, p, wp, gfin, tm, tf, final, pre=None):
    n = x.shape[0]
    tok = lambda i, j: (i, 0)
    in_specs, args = [], []
    if pre is not None:
        in_specs += [pl.BlockSpec((tm, D_MODEL), tok), _const_spec((D_MODEL, D_MODEL))]
        args += list(pre)
    in_specs += [pl.BlockSpec((tm, D_MODEL), tok),
                 _const_spec((1, D_MODEL)),
                 pl.BlockSpec((D_MODEL, tf), lambda i, j: (0, j)),
                 pl.BlockSpec((tf, D_MODEL), lambda i, j: (j, 0)),
                 _const_spec((1, D_MODEL)),
                 _const_spec((D_MODEL, D_MODEL)),
                 pl.BlockSpec((tm, D_PLE), tok),
                 _const_spec((D_PLE, D_MODEL)),
                 _const_spec((1, D_MODEL))]
    args += [x, gf, up, dn, gp, wg, p, wp, gfin]
    return pl.pallas_call(
        functools.partial(_mlp_body, final, pre is not None),
        grid=(n // tm, D_FF // tf),
        in_specs=in_specs,
        out_specs=pl.BlockSpec((tm, D_MODEL), tok),
        out_shape=jax.ShapeDtypeStruct((n, D_MODEL), F32),
        scratch_shapes=[pltpu.VMEM((tm, D_MODEL), BF16), pltpu.VMEM((tm, D_MODEL), F32),
                        pltpu.VMEM((tm, D_MODEL), F32)],
        compiler_params=_params(("parallel", "arbitrary")),
        name="mlp_ple",
    )(*args)


def _attn_in_body(x_ref, g_ref, wq_ref, wkv_ref, wg_ref, q_ref, kc_ref, vc_ref, ks_ref, vs_ref,
                  kw_ref, vw_ref, gt_ref):
    h = _rms(x_ref[...], g_ref[...]).astype(BF16)
    q_ref[...] = _dot(h, wq_ref[...])
    kv = _dot(h, wkv_ref[...])
    for i, r in enumerate((kc_ref, vc_ref, ks_ref, vs_ref, kw_ref, vw_ref)):
        r[...] = kv[:, KV_W * i:KV_W * (i + 1)]
    gt_ref[...] = _sigmoid(_dot(h, wg_ref[...]))


def _attn_in(x, g, wq, wkv, wg, tm):
    n = x.shape[0]
    tok = lambda i: (i, 0)
    kv_spec = pl.BlockSpec((tm, KV_W), tok)
    kv_shape = jax.ShapeDtypeStruct((n, KV_W), F32)
    return pl.pallas_call(
        _attn_in_body,
        grid=(n // tm,),
        in_specs=[pl.BlockSpec((tm, D_MODEL), tok), _const_spec((1, D_MODEL)),
                  _const_spec((D_MODEL, D_MODEL)), _const_spec((D_MODEL, 6 * KV_W)),
                  _const_spec((D_MODEL, 128))],
        out_specs=[pl.BlockSpec((tm, D_MODEL), tok)] + [kv_spec] * 6 + [pl.BlockSpec((tm, 128), tok)],
        out_shape=[jax.ShapeDtypeStruct((n, D_MODEL), F32)] + [kv_shape] * 6
                  + [jax.ShapeDtypeStruct((n, 128), F32)],
        compiler_params=_params(("parallel",)),
        name="attn_in",
    )(x, g, wq, wkv, wg)


def _compress_hidden(load_rows, n_chunks, wexp_ref):
    acc = jnp.zeros((2 * n_chunks, 4 * CMP_HID), F32)
    for r in range(CMP_STRIDE):
        x = jnp.concatenate([load_rows(2 * r), load_rows(2 * r + 1)], axis=0)
        acc = acc + _dot(x.astype(BF16), wexp_ref[r])
    lo, hi = acc[:n_chunks], acc[n_chunks:]
    w = 2 * CMP_HID
    return jnp.concatenate([lo[:, :w], hi[:, :w], lo[:, w:], hi[:, w:]], axis=1)


def _pe_term(pe_ref, w1_ref):
    t = _dot(pe_ref[...], w1_ref[...])[0:1, :]
    return jnp.concatenate([t] * N_KV, axis=1)


def _compress_p_body(nc, kc_ref, vc_ref, wek_ref, pek_ref, w1k_ref, w2k_ref,
                     wev_ref, pev_ref, w1v_ref, w2v_ref, ok_ref, ov_ref):
    half = N_KV * CMP_HID
    row = lax.broadcasted_iota(jnp.int32, (nc, KV_W), 0)
    for src, we, pe, w1, w2, out in ((kc_ref, wek_ref, pek_ref, w1k_ref, w2k_ref, ok_ref),
                                     (vc_ref, wev_ref, pev_ref, w1v_ref, w2v_ref, ov_ref)):
        hh = _compress_hidden(lambda k: src[0, pl.ds(k, nc, stride=2 * CMP_STRIDE), :], nc, we)
        nxt = pltpu.roll(hh[:, half:], nc - 1, 0)
        a = hh[:, :half] + nxt + _pe_term(pe, w1)
        a = a * _sigmoid(a)
        res = _dot(a.astype(BF16), w2[...])
        out[0] = jnp.where(row < nc - 1, res, 0.0)


def _compress_prompt(kc, vc, wk, wv):
    b, t2, _ = kc.shape
    nc = t2 // (2 * CMP_STRIDE)
    seq = pl.BlockSpec((1, t2, 128), lambda i: (i, 0, 0))
    wspecs = [_const_spec((CMP_STRIDE, 128, 4 * CMP_HID)),
              _const_spec((8, 2 * CMP_STRIDE * HEAD_DIM)),
              _const_spec((2 * CMP_STRIDE * HEAD_DIM, CMP_HID)), _const_spec((N_KV * CMP_HID, KV_W))]
    out = pl.BlockSpec((1, nc, KV_W), lambda i: (i, 0, 0))
    return pl.pallas_call(
        functools.partial(_compress_p_body, nc),
        grid=(b,),
        in_specs=[seq, seq] + wspecs + wspecs,
        out_specs=[out, out],
        out_shape=[jax.ShapeDtypeStruct((b, nc, KV_W), F32)] * 2,
        compiler_params=_params(("parallel",)),
        name="compress_prompt",
    )(kc, vc, *wk, *wv)


def _softmax_cols(s, mask):
    s = jnp.where(mask, s, NEG_INF)
    m = jnp.max(s, axis=0, keepdims=True)
    m = jnp.where(m == NEG_INF, 0.0, m)
    e = jnp.where(mask, jnp.exp(s - m), 0.0)
    return e / jnp.maximum(jnp.sum(e, axis=0, keepdims=True), 1e-30)


def _attn_p_body(tq, t, q_ref, gt_ref, ks_ref, vs_ref, kw_ref, vw_ref, kcm_ref, vcm_ref,
                 tbl_ref, bt_ref, bc_ref, msel_ref, o_ref,
                 ksb, kwb, vst, vwt, kcb, vct, btoep, bcmp, qt, selt, ot):
    b = pl.program_id(0)
    qi = pl.program_id(1)
    nkt = t // tq
    ncmp = kcm_ref.shape[1]
    q0 = qi * tq

    @pl.when((b == 0) & (qi == 0))
    def _():
        def per_head(h, carry):
            for d in range(2):
                bk = bt_ref[d]
                acc = jnp.zeros((tq, tq), F32)
                for bb in range(NUM_BUCKETS):
                    acc = jnp.where(bk == bb, tbl_ref[bb, h], acc)
                btoep[h, d] = acc
            btoep[h, 2] = jnp.full((tq, tq), tbl_ref[NUM_BUCKETS - 1, h], F32)
            bk = bc_ref[...]
            acc = jnp.zeros(bk.shape, F32)
            for bb in range(NUM_BUCKETS):
                acc = jnp.where(bk == bb, tbl_ref[bb, h], acc)
            bcmp[h] = acc
            return carry
        lax.fori_loop(0, N_HEADS, per_head, 0)

    @pl.when(qi == 0)
    def _():
        for kt in range(nkt):
            rows = slice(kt * tq, (kt + 1) * tq)
            for src, dstk in ((ks_ref, ksb), (kw_ref, kwb)):
                blk = src[0, rows, :]
                for g in range(N_KV):
                    dstk[g, rows, :] = blk[:, HEAD_DIM * g:HEAD_DIM * (g + 1)].astype(BF16)
            for src, dstv in ((vs_ref, vst), (vw_ref, vwt)):
                blk_t = src[0, rows, :].T
                for g in range(N_KV):
                    dstv[g, kt] = blk_t[HEAD_DIM * g:HEAD_DIM * (g + 1), :].astype(BF16)
        kc = kcm_ref[0]
        vc_t = vcm_ref[0].T
        for g in range(N_KV):
            kcb[g] = kc[:, HEAD_DIM * g:HEAD_DIM * (g + 1)].astype(BF16)
            vct[g] = vc_t[HEAD_DIM * g:HEAD_DIM * (g + 1), :].astype(BF16)

    qt[...] = q_ref[0].T.astype(BF16)
    gt = gt_ref[0].T

    qpos_row = q0 + lax.broadcasted_iota(jnp.int32, (1, tq), 1)
    i_iota = lax.broadcasted_iota(jnp.int32, (tq, tq), 1)
    j_iota = lax.broadcasted_iota(jnp.int32, (tq, tq), 0)
    rel = i_iota - j_iota

    c_iota = lax.broadcasted_iota(jnp.int32, (ncmp, tq), 0)
    cmask = (CMP_STRIDE * c_iota + (2 * CMP_STRIDE - 1)) <= qpos_row
    coff = pl.multiple_of((nkt - 1 - qi) * (tq // CMP_STRIDE), 8)

    nblk = t // SLC_BLOCK
    blk_iota = lax.broadcasted_iota(jnp.int32, (nblk, tq), 0)
    cur = qpos_row // SLC_BLOCK
    valid = blk_iota * SLC_BLOCK <= qpos_row
    forced = (blk_iota == 0) | (blk_iota == cur) | (blk_iota == cur - 1)

    for g in range(N_KV):
        pg = jnp.zeros((ncmp, tq), F32)
        for hh in range(HPG):
            h = g * HPG + hh
            qh = qt[HEAD_DIM * h:HEAD_DIM * (h + 1), :]
            s = _dot(kcb[g], qh) * SCALE + bcmp[h, pl.ds(coff, ncmp), :]
            p = _softmax_cols(s, cmask)
            pg = pg + p
            oc = _dot(vct[g], p.astype(BF16))
            ot[HEAD_DIM * h:HEAD_DIM * (h + 1), :] = gt[h:h + 1, :] * oc
        hi, mid, lo = _split3(pg)
        msel = msel_ref[...]
        score = _dot(msel, hi) + _dot(msel, mid) + _dot(msel, lo)
        score = jnp.where(forced, jnp.inf, jnp.where(valid, score, NEG_INF))
        rank = jnp.zeros((nblk, tq), jnp.int32)
        for i in range(nblk):
            si = score[i:i + 1, :]
            beats = (si > score) | ((blk_iota > i) & (si == score))
            rank = rank + beats.astype(jnp.int32)
        sel = (rank < N_SELECT).astype(F32)
        for i in range(nblk):
            selt[i] = sel[i:i + 1, :]

        for hh in range(HPG):
            h = g * HPG + hh
            qh = qt[HEAD_DIM * h:HEAD_DIM * (h + 1), :]

            def flash(kb, vt, lo_kt, masker):
                def body(kt, carry):
                    m, l, acc = carry
                    k0 = pl.multiple_of(kt * tq, tq)
                    s = _dot(kb[g, pl.ds(k0, tq), :], qh) * SCALE
                    s = s + btoep[h, jnp.minimum(qi - kt, 2)]
                    s = jnp.where(masker(kt, q0 - k0 + rel), s, NEG_INF)
                    m_new = jnp.maximum(m, jnp.max(s, axis=0, keepdims=True))
                    m_safe = jnp.where(m_new == NEG_INF, 0.0, m_new)
                    alpha = jnp.exp(m - m_safe)
                    p = jnp.exp(s - m_safe)
                    l = alpha * l + jnp.sum(p, axis=0, keepdims=True)
                    acc = alpha * acc + _dot(vt[g, kt], p.astype(BF16))
                    return m_new, l, acc
                init = (jnp.full((1, tq), NEG_INF, F32), jnp.zeros((1, tq), F32),
                        jnp.zeros((HEAD_DIM, tq), F32))
                _, l, acc = lax.fori_loop(lo_kt, qi + 1, body, init)
                return acc / jnp.maximum(l, 1e-30)

            def slc_mask(kt, dist):
                half = tq // SLC_BLOCK
                rows = [jnp.broadcast_to(selt[kt * half + r], (SLC_BLOCK, tq)) for r in range(half)]
                return (jnp.concatenate(rows, axis=0) > 0.5) & (dist >= 0)

            def win_mask(kt, dist):
                return (dist >= 0) & (dist <= WINDOW)

            o_slc = flash(ksb, vst, 0, slc_mask)
            o_win = flash(kwb, vwt, jnp.maximum(qi - WINDOW // tq, 0), win_mask)
            rows = slice(HEAD_DIM * h, HEAD_DIM * (h + 1))
            ot[rows, :] = ot[rows, :] + gt[16 + h:17 + h, :] * o_slc + gt[32 + h:33 + h, :] * o_win

    o_ref[0] = ot[...].T


def _attn_prompt(q, gates, ks, vs, kw, vw, kcmp, vcmp, table, bkt_toep, bkt_cmp, msel, tq):
    b, t, _ = q.shape
    nkt = t // tq
    ncmp = kcmp.shape[1]
    qspec = pl.BlockSpec((1, tq, D_MODEL), lambda bi, i: (bi, i, 0))
    seq = pl.BlockSpec((1, t, KV_W), lambda bi, i: (bi, 0, 0))
    cmp_spec = pl.BlockSpec((1, ncmp, KV_W), lambda bi, i: (bi, 0, 0))
    return pl.pallas_call(
        functools.partial(_attn_p_body, tq, t),
        grid=(b, nkt),
        in_specs=[qspec, pl.BlockSpec((1, tq, 128), lambda bi, i: (bi, i, 0)),
                  seq, seq, seq, seq, cmp_spec, cmp_spec,
                  pl.BlockSpec(memory_space=pltpu.SMEM),
                  _const_spec(bkt_toep.shape), _const_spec(bkt_cmp.shape), _const_spec(msel.shape)],
        out_specs=qspec,
        out_shape=jax.ShapeDtypeStruct((b, t, D_MODEL), F32),
        scratch_shapes=[pltpu.VMEM((N_KV, t, HEAD_DIM), BF16), pltpu.VMEM((N_KV, t, HEAD_DIM), BF16),
                        pltpu.VMEM((N_KV, nkt, HEAD_DIM, tq), BF16), pltpu.VMEM((N_KV, nkt, HEAD_DIM, tq), BF16),
                        pltpu.VMEM((N_KV, ncmp, HEAD_DIM), BF16), pltpu.VMEM((N_KV, HEAD_DIM, ncmp), BF16),
                        pltpu.VMEM((N_HEADS, 3, tq, tq), F32), pltpu.VMEM((N_HEADS,) + bkt_cmp.shape, F32),
                        pltpu.VMEM((D_MODEL, tq), BF16), pltpu.VMEM((t // SLC_BLOCK, 1, tq), F32),
                        pltpu.VMEM((D_MODEL, tq), F32)],
        compiler_params=_params(("arbitrary", "arbitrary")),
        name="attn_prompt",
    )(q, gates, ks, vs, kw, vw, kcmp, vcmp, table, bkt_toep, bkt_cmp, msel)


PAGE = 128
PAGES_PER_GROUP = 16
Q_ROWS = 8


def _rank_desc(score_row, n):
    a = jnp.broadcast_to(score_row, (n, n))
    at = a.T
    i = lax.broadcasted_iota(jnp.int32, (n, n), 0)
    j = lax.broadcasted_iota(jnp.int32, (n, n), 1)
    beats = (at > a) | ((i < j) & (at == a))
    return jnp.sum(beats.astype(F32), axis=0, keepdims=True)


def _bias_rows(bucket, tcols):
    r = tcols.shape[0]
    out = jnp.zeros((r, bucket.shape[1]), F32)
    for bb in range(NUM_BUCKETS):
        out = jnp.where(bucket == bb, tcols[:, bb:bb + 1], out)
    return out


def _softmax_rows(s, mask):
    s = jnp.where(mask, s, NEG_INF)
    m = jnp.max(s, axis=1, keepdims=True)
    m = jnp.where(m == NEG_INF, 0.0, m)
    e = jnp.where(mask, jnp.exp(s - m), 0.0)
    return e / jnp.maximum(jnp.sum(e, axis=1, keepdims=True), 1e-30)


def _dot_nt(a, b):
    return lax.dot_general(a, b, (((1,), (1,)), ((), ())), preferred_element_type=F32)


def _cmp_s_body(past, pt_ref, ck_hbm, cv_hbm, kcn_ref, vcn_ref, q_ref,
                wek_ref, pek_ref, w1k_ref, w2k_ref, wev_ref, pev_ref, w1v_ref, w2v_ref,
                tbl_ref, bkt_ref, msel_ref, ocmp_ref, idx_ref,
                kbuf, vbuf, tbuf, hk, hv, sem):
    b = pl.program_id(0)
    rows_pg = PAGES_PER_GROUP * PAGE
    cpg = rows_pg // CMP_STRIDE
    ngrp = past // rows_pg
    nc = past // CMP_STRIDE
    half = N_KV * CMP_HID

    def copies(gi, slot):
        out = []
        for p in range(PAGES_PER_GROUP):
            page = pt_ref[b, gi * PAGES_PER_GROUP + p]
            out.append(pltpu.make_async_copy(ck_hbm.at[page], kbuf.at[slot, p], sem.at[0, slot]))
            out.append(pltpu.make_async_copy(cv_hbm.at[page], vbuf.at[slot, p], sem.at[1, slot]))
        return out

    for c in copies(0, 0):
        c.start()

    def group(gi, carry):
        slot = gi % 2

        @pl.when(gi + 1 < ngrp)
        def _():
            for c in copies(gi + 1, 1 - slot):
                c.start()

        for c in copies(gi, slot):
            c.wait()
        r0 = pl.multiple_of(gi * cpg, cpg)
        for buf, we, hdst in ((kbuf, wek_ref, hk), (vbuf, wev_ref, hv)):
            for p in range(PAGES_PER_GROUP):
                for hf in range(2):
                    tbuf[hf, p * PAGE:(p + 1) * PAGE, :] = buf[slot, p, hf * 128:(hf + 1) * 128, :].T
            hdst[pl.ds(r0, cpg), :] = _compress_hidden(
                lambda k: tbuf[k % 2, pl.ds(k // 2, cpg, stride=CMP_STRIDE), :], cpg, we)
        return carry

    lax.fori_loop(0, ngrp, group, 0)

    col = lax.broadcasted_iota(jnp.int32, (1, nc), 1)
    cmask = (CMP_STRIDE * col + (2 * CMP_STRIDE - 1)) <= past
    row8 = lax.broadcasted_iota(jnp.int32, (Q_ROWS, nc), 0)
    cmpd = []
    for new_ref, we, pe, w1, w2, hdst in ((kcn_ref, wek_ref, pek_ref, w1k_ref, w2k_ref, hk),
                                          (vcn_ref, wev_ref, pev_ref, w1v_ref, w2v_ref, hv)):
        new = new_ref[0]
        new_lo = _dot(jnp.broadcast_to(new[:, :128], (8, 128)).astype(BF16), we[0])
        new_hi = _dot(jnp.broadcast_to(new[:, 128:], (8, 128)).astype(BF16), we[0])
        w = 2 * CMP_HID
        hdst[nc:nc + 8, :] = jnp.concatenate([new_lo[:, :w], new_hi[:, :w], new_lo[:, w:], new_hi[:, w:]], axis=1)
        a = hdst[0:nc, 0:half] + hdst[1:nc + 1, half:2 * half] + _pe_term(pe, w1)
        a = (a * _sigmoid(a)).astype(BF16)
        cmpd.append([_dot(a[:, CMP_HID * g:CMP_HID * (g + 1)], w2[...]).astype(BF16) for g in range(N_KV)])

    nsel = msel_ref.shape[1]
    lane = lax.broadcasted_iota(jnp.int32, (1, nsel), 1)
    n_blocks = past // SLC_BLOCK + 1
    cur = past // SLC_BLOCK
    forced = (lane == 0) | (lane == cur) | (lane == cur - 1)
    valid = (lane * SLC_BLOCK <= past) & (lane < n_blocks)
    k_iota = lax.broadcasted_iota(jnp.int32, (N_SELECT, nsel), 0).astype(F32)
    lane_f = lax.broadcasted_iota(jnp.int32, (N_SELECT, nsel), 1).astype(F32)
    for g in range(N_KV):
        qg = q_ref[0, g].astype(BF16)
        bias = _bias_rows(bkt_ref[...], tbl_ref[g])
        s = _dot_nt(qg, cmpd[0][g]) * SCALE + bias
        p = _softmax_rows(s, cmask)
        ocmp_ref[0, g] = _dot(p.astype(BF16), cmpd[1][g])
        pg = jnp.sum(jnp.where(row8 < HPG, p, 0.0), axis=0, keepdims=True)
        hi, mid, lo = _split3(jnp.broadcast_to(pg, (8, nc)))
        msel = msel_ref[...]
        score = (_dot(hi, msel) + _dot(mid, msel) + _dot(lo, msel))[0:1, :]
        score = jnp.where(forced, jnp.inf, jnp.where(valid, score, NEG_INF))
        rank = _rank_desc(score, nsel)
        hit = jnp.where(jnp.broadcast_to(rank, (N_SELECT, nsel)) == k_iota, lane_f, 0.0)
        idx_ref[0, g] = jnp.sum(hit, axis=1, keepdims=True).astype(jnp.int32)


def _cmp_sample(page_table, cache_k, cache_v, kc_new, vc_new, q8, wk, wv, tbl8, bkt, msel, past):
    nb = page_table.shape[0]
    rows_pg = PAGES_PER_GROUP * PAGE
    nc = past // CMP_STRIDE
    any_spec = pl.BlockSpec(memory_space=pl.ANY)
    new_spec = pl.BlockSpec((1, 1, KV_W), lambda i, pt: (i, 0, 0))
    cs = lambda shape: pl.BlockSpec(shape, lambda i, pt: (0,) * len(shape))
    wspecs = [cs((CMP_STRIDE, 128, 4 * CMP_HID)),
              cs((8, 2 * CMP_STRIDE * HEAD_DIM)),
              cs((2 * CMP_STRIDE * HEAD_DIM, CMP_HID)), cs((CMP_HID, HEAD_DIM))]
    grid_spec = pltpu.PrefetchScalarGridSpec(
        num_scalar_prefetch=1,
        grid=(nb,),
        in_specs=[any_spec, any_spec, new_spec, new_spec,
                  pl.BlockSpec((1, N_KV, Q_ROWS, HEAD_DIM), lambda i, pt: (i, 0, 0, 0))]
                 + wspecs + wspecs + [cs(tbl8.shape), cs(bkt.shape), cs(msel.shape)],
        out_specs=[pl.BlockSpec((1, N_KV, Q_ROWS, HEAD_DIM), lambda i, pt: (i, 0, 0, 0)),
                   pl.BlockSpec((1, N_KV, N_SELECT, 1), lambda i, pt: (i, 0, 0, 0))],
        scratch_shapes=[pltpu.VMEM((2, PAGES_PER_GROUP, 2 * PAGE, 128), F32),
                        pltpu.VMEM((2, PAGES_PER_GROUP, 2 * PAGE, 128), F32),
                        pltpu.VMEM((2, rows_pg, 128), F32),
                        pltpu.VMEM((nc + 8, 2 * N_KV * CMP_HID), F32),
                        pltpu.VMEM((nc + 8, 2 * N_KV * CMP_HID), F32),
                        pltpu.SemaphoreType.DMA((2, 2))])
    return pl.pallas_call(
        functools.partial(_cmp_s_body, past),
        grid_spec=grid_spec,
        out_shape=[jax.ShapeDtypeStruct((nb, N_KV, Q_ROWS, HEAD_DIM), F32),
                   jax.ShapeDtypeStruct((nb, N_KV, N_SELECT, 1), jnp.int32)],
        compiler_params=_params(("arbitrary",)),
        name="cmp_sample",
    )(page_table, cache_k, cache_v, kc_new, vc_new, q8, *wk, *wv, tbl8, bkt, msel)


NEW_LANES = 128


def _attend_kt(qg, kt, vt, bias, mask):
    s = _dot(qg, kt.astype(BF16)) * SCALE + bias
    p = _softmax_rows(s, mask)
    return _dot_nt(p.astype(BF16), vt.astype(BF16))


def _dec_s_body(past, idx_ref, pt_ref, csk_hbm, csv_hbm,
                q_ref, ksn_ref, vsn_ref, kwn_ref, vwn_ref, wk_ref, wv_ref, gt_ref, ocmp_ref,
                tbl_ref, bktw_ref, o_ref, wko_ref, wvo_ref,
                kbuf, vbuf, kwbuf, vwbuf, sem):
    b = pl.program_id(0)
    n_pages = past // PAGE
    cur = past // SLC_BLOCK
    bpp = PAGE // SLC_BLOCK
    lsel = N_SELECT * PAGE
    wb = wk_ref.shape[3]

    @pl.when(b == 0)
    def _():
        kbuf[...] = jnp.zeros_like(kbuf)
        vbuf[...] = jnp.zeros_like(vbuf)
        kwbuf[...] = jnp.zeros_like(kwbuf)
        vwbuf[...] = jnp.zeros_like(vwbuf)

    copies = []
    for g in range(N_KV):
        for k in range(N_SELECT):
            blk = idx_ref[b, g * N_SELECT + k]
            page = pt_ref[b, jnp.minimum(blk // bpp, n_pages - 1)]
            dst = pl.ds(k * PAGE, PAGE)
            copies.append(pltpu.make_async_copy(csk_hbm.at[page, g], kbuf.at[g, :, dst], sem.at[0]))
            copies.append(pltpu.make_async_copy(csv_hbm.at[page, g], vbuf.at[g, :, dst], sem.at[1]))
    for c in copies:
        c.start()

    lane_wb = lax.broadcasted_iota(jnp.int32, (HEAD_DIM, wb), 1)
    for g in range(N_KV):
        for src, new_ref, dst, buf in ((wk_ref, kwn_ref, wko_ref, kwbuf), (wv_ref, vwn_ref, wvo_ref, vwbuf)):
            st = src[0, g]
            newc = new_ref[0, g]
            dst[0, g] = jnp.where(lane_wb == wb - 1, newc, pltpu.roll(st, wb - 1, 1))
            buf[g, :, 0:wb] = st
            buf[g, :, wb:wb + 1] = newc

    for c in copies:
        c.wait()

    ls = lsel + NEW_LANES
    lw = wb + NEW_LANES
    lane_s = lax.broadcasted_iota(jnp.int32, (1, ls), 1)
    lane_w = lax.broadcasted_iota(jnp.int32, (1, lw), 1)
    wmask = lane_w <= wb
    tok = lane_s % PAGE
    for g in range(N_KV):
        kbuf[g, :, lsel:lsel + 1] = ksn_ref[0, g]
        vbuf[g, :, lsel:lsel + 1] = vsn_ref[0, g]
        qg = q_ref[0, g].astype(BF16)
        tcols = tbl_ref[g]

        kpos = past + (lane_s - lsel)
        gathered = lane_s < 0
        has_new = jnp.int32(0)
        for k in range(N_SELECT):
            blk = idx_ref[b, g * N_SELECT + k]
            in_slot = (lane_s // PAGE) == k
            kpos = jnp.where(in_slot, (blk // bpp) * PAGE + tok, kpos)
            gathered = gathered | (in_slot & (blk < cur) & ((tok // SLC_BLOCK) == (blk % bpp)))
            has_new = has_new | (blk == cur).astype(jnp.int32)
        smask = (gathered | ((lane_s == lsel) & (has_new > 0))) & (kpos <= past)
        sbias = _bias_rows(_rel_bucket(past - kpos), tcols)
        o_slc = _attend_kt(qg, kbuf[g], vbuf[g], sbias, smask)

        wbias = _bias_rows(bktw_ref[...], tcols)
        o_win = _attend_kt(qg, kwbuf[g], vwbuf[g], wbias, wmask)

        gt = gt_ref[0, g]
        o_ref[0, g] = gt[:, 0:1] * ocmp_ref[0, g] + gt[:, 1:2] * o_slc + gt[:, 2:3] * o_win


def _dec_sample(idx, page_table, cache_sk, cache_sv, q8, ks_new, vs_new, kw_new, vw_new, state_wk, state_wv,
                gates8, ocmp, tbl8, bktw, past):
    nb, wb = state_wk.shape[0], state_wk.shape[3]
    any_spec = pl.BlockSpec(memory_space=pl.ANY)
    new_spec = pl.BlockSpec((1, N_KV, HEAD_DIM, 1), lambda i, *_: (i, 0, 0, 0))
    st_spec = pl.BlockSpec((1, N_KV, HEAD_DIM, wb), lambda i, *_: (i, 0, 0, 0))
    head_spec = pl.BlockSpec((1, N_KV, Q_ROWS, HEAD_DIM), lambda i, *_: (i, 0, 0, 0))
    cs = lambda shape: pl.BlockSpec(shape, lambda i, *_: (0,) * len(shape))
    lsel = N_SELECT * PAGE
    grid_spec = pltpu.PrefetchScalarGridSpec(
        num_scalar_prefetch=2,
        grid=(nb,),
        in_specs=[any_spec, any_spec, head_spec,
                  new_spec, new_spec, new_spec, new_spec, st_spec, st_spec,
                  pl.BlockSpec((1, N_KV, Q_ROWS, 3), lambda i, *_: (i, 0, 0, 0)), head_spec,
                  cs(tbl8.shape), cs(bktw.shape)],
        out_specs=[head_spec, st_spec, st_spec],
        scratch_shapes=[pltpu.VMEM((N_KV, HEAD_DIM, lsel + NEW_LANES), F32),
                        pltpu.VMEM((N_KV, HEAD_DIM, lsel + NEW_LANES), F32),
                        pltpu.VMEM((N_KV, HEAD_DIM, wb + NEW_LANES), F32),
                        pltpu.VMEM((N_KV, HEAD_DIM, wb + NEW_LANES), F32),
                        pltpu.SemaphoreType.DMA((2,))])
    return pl.pallas_call(
        functools.partial(_dec_s_body, past),
        grid_spec=grid_spec,
        out_shape=[jax.ShapeDtypeStruct((nb, N_KV, Q_ROWS, HEAD_DIM), F32),
                   jax.ShapeDtypeStruct(state_wk.shape, F32),
                   jax.ShapeDtypeStruct(state_wv.shape, F32)],
        compiler_params=_params(("arbitrary",)),
        name="dec_sample",
    )(idx, page_table, cache_sk, cache_sv, q8, ks_new, vs_new, kw_new, vw_new,
      state_wk, state_wv, gates8, ocmp, tbl8, bktw)


def _expand_w1(w1):
    eye = jnp.eye(2, dtype=w1.dtype)
    parts = []
    for part in (w1[:CMP_STRIDE * HEAD_DIM], w1[CMP_STRIDE * HEAD_DIM:]):
        w = part.reshape(CMP_STRIDE, HEAD_DIM, CMP_HID)
        parts.append(jnp.einsum("gh,rdn->rgdhn", eye, w).reshape(CMP_STRIDE, 2 * HEAD_DIM, 2 * CMP_HID))
    return jnp.concatenate(parts, axis=-1).astype(BF16)


def _expand_w2(w2):
    eye = jnp.eye(N_KV, dtype=w2.dtype)
    return jnp.einsum("gh,nd->gnhd", eye, w2).reshape(N_KV * CMP_HID, KV_W).astype(BF16)


def _pe_rows(pe):
    return jnp.broadcast_to(pe.reshape(1, -1), (8, pe.size)).astype(BF16)


def _select_matrix(n_cmp, n_blocks):
    r = SLC_BLOCK // CMP_STRIDE
    m = np.zeros((n_cmp, n_blocks), np.float32)
    for s in range(n_blocks):
        for a in range(r):
            for bb in range(2):
                c = r * s + a - bb
                if 0 <= c < n_cmp:
                    m[c, s] += 1.0
    return m


def _pad_to(a, size, axis):
    pad = [(0, 0)] * a.ndim
    pad[axis] = (0, size - a.shape[axis])
    return jnp.pad(a, pad)


def kernel(x_prompt, x_sample, cache_cmp_k, cache_cmp_v, cache_slc_k, cache_slc_v, state_win_k, state_win_v, state_conv, page_table, p_prompt, p_sample, rel_table, norm_mix, norm_ffn, norm_ple, norm_final, conv_w1, conv_b1, conv_dw, conv_dwb, conv_ln_g, conv_ln_b, conv_w2, conv_b2, attn_w_in, attn_w_out, cmpk_w1, cmpk_pe, cmpk_w2, cmpv_w1, cmpv_pe, cmpv_w2, mlp_up, mlp_down, ple_proj, ple_gate):
    nbp, t, _ = x_prompt.shape
    nbs = x_sample.shape[0]
    n_p = nbp * t
    n_pool = cache_cmp_k.shape[1]
    past = page_table.shape[1] * cache_cmp_k.shape[2]
    wb = state_win_k.shape[2]
    hist = CONV_W - 1
    tm_p, tf, tt, tq = 512, 1024, 256, 128

    row = lambda a: a.reshape(1, -1)
    bf = lambda a: a.astype(BF16)

    w1c, w2c = bf(conv_w1[0]), bf(conv_w2[0])
    dw3 = conv_dw[0].reshape(CONV_W, D_MODEL // 128, 128).transpose(1, 0, 2)
    dwb3 = conv_dwb[0].reshape(D_MODEL // 128, 1, 128)
    up, dn = bf(mlp_up), bf(mlp_down)
    wg, wp = bf(ple_gate), bf(ple_proj)
    w_in = attn_w_in[0]
    wq = bf(w_in[:, :D_MODEL])
    wkv = bf(w_in[:, D_MODEL:D_MODEL + 6 * KV_W])
    wgt = bf(_pad_to(w_in[:, D_MODEL + 6 * KV_W:], 128, 1))
    w_out = bf(attn_w_out[0])
    cw = []
    for w1, pe, w2 in ((cmpk_w1[0], cmpk_pe[0], cmpk_w2[0]), (cmpv_w1[0], cmpv_pe[0], cmpv_w2[0])):
        cw.append((_expand_w1(w1), _pe_rows(pe), bf(w1), _expand_w2(w2), bf(w2)))
    wk_p, wv_p = [c[:4] for c in cw]
    wk_s, wv_s = [c[:3] + c[4:] for c in cw]

    def tail(x, i, p, tm, final, pre=None):
        return _mlp_ple(x, row(norm_ffn[i]), up[i], dn[i], row(norm_ple[i]), wg[i], p, wp[i],
                        row(norm_final), tm, tf, final, pre)

    conv_args = (row(conv_ln_g[0]), row(conv_ln_b[0]), w2c, row(conv_b2[0]))

    xp = x_prompt.reshape(n_p, D_MODEL)
    glu = _conv_in(xp, row(norm_mix[0]), w1c, row(conv_b1[0]), tm_p).reshape(nbp, t, D_MODEL)
    x1 = _conv_out(glu, x_prompt, dw3, dwb3, *conv_args, tt).reshape(n_p, D_MODEL)
    conv_p = glu[:, t - hist:][None]
    x2 = tail(x1, 0, p_prompt[0].reshape(n_p, D_PLE), tm_p, False)
    q, kc, vc, ks, vs, kw, vw, gates = _attn_in(x2, row(norm_mix[1]), wq, wkv, wgt, tm_p)
    seq = lambda a: a.reshape(nbp, t, -1)
    half_rows = lambda a: a.reshape(nbp, 2 * t, 128)
    kcmp, vcmp = _compress_prompt(half_rows(kc), half_rows(vc), wk_p, wv_p)

    nkt = t // tq
    ncmp = t // CMP_STRIDE
    ii = jnp.arange(tq, dtype=jnp.int32)[None, :]
    jj = jnp.arange(tq, dtype=jnp.int32)[:, None]
    bkt_toep = jnp.stack([_rel_bucket(d * tq + ii - jj) for d in range(2)])
    nshift = (nkt - 1) * (tq // CMP_STRIDE)
    rr = jnp.arange(ncmp + nshift, dtype=jnp.int32)[:, None]
    bkt_cmp = _rel_bucket(ii - CMP_STRIDE * (rr - nshift) - (2 * CMP_STRIDE - 1))
    msel_p = jnp.asarray(_select_matrix(ncmp, t // SLC_BLOCK).T, BF16)
    o_p = _attn_prompt(seq(q), seq(gates), seq(ks), seq(vs), seq(kw), seq(vw), kcmp, vcmp,
                       rel_table, bkt_toep, bkt_cmp, msel_p, tq)
    y_p = tail(x2, 1, p_prompt[1].reshape(n_p, D_PLE), tm_p, True, pre=(o_p.reshape(n_p, D_MODEL), w_out))

    kv5 = lambda a: a.reshape(1, nbp, t, N_KV, HEAD_DIM)
    keep = min(WINDOW, t)
    win5 = lambda a: seq(a)[:, t - keep:].reshape(1, nbp, keep, N_KV, HEAD_DIM)

    xs = x_sample.reshape(nbs, D_MODEL)
    glu_s = _conv_in(xs, row(norm_mix[0]), w1c, row(conv_b1[0]), nbs)
    x1s, conv_s = _conv_step(state_conv[0], glu_s, xs, conv_dw[0], row(conv_dwb[0]), *conv_args)
    x2s = tail(x1s, 0, p_sample[0].reshape(nbs, D_PLE), nbs, False)
    qs, kcs, vcs, kss, vss, kws, vws, gts = _attn_in(x2s, row(norm_mix[1]), wq, wkv, wgt, nbs)

    q8 = _pad_to(qs.reshape(nbs, N_KV, HPG, HEAD_DIM), Q_ROWS, 2)
    tbl8 = _pad_to(rel_table.T.reshape(N_KV, HPG, NUM_BUCKETS), Q_ROWS, 1)
    nc_s = past // CMP_STRIDE
    cc = jnp.arange(nc_s, dtype=jnp.int32)[None, :]
    bkt_s = _rel_bucket(past - (CMP_STRIDE * cc + 2 * CMP_STRIDE - 1))
    n_blocks_s = past // SLC_BLOCK + 1
    msel_s = jnp.asarray(_select_matrix(nc_s, -(-n_blocks_s // 128) * 128), BF16)
    new3 = lambda a: a.reshape(nbs, 1, KV_W)
    fmajor = lambda c: jnp.transpose(c[0], (0, 2, 3, 1))
    cache2 = lambda c: fmajor(c).reshape(n_pool, 2 * PAGE, PAGE)
    ocmp, idx = _cmp_sample(page_table, cache2(cache_cmp_k), cache2(cache_cmp_v), new3(kcs), new3(vcs),
                            q8, wk_s, wv_s, tbl8, bkt_s, msel_s, past)

    ww = jnp.arange(wb + NEW_LANES, dtype=jnp.int32)[None, :]
    bkt_w = _rel_bucket(wb - ww)
    gates8 = _pad_to(gts[:, :3 * N_HEADS].reshape(nbs, 3, N_KV, HPG).transpose(0, 2, 3, 1), Q_ROWS, 2)
    col4 = lambda a: a.reshape(nbs, N_KV, HEAD_DIM, 1)
    o8, wk_new, wv_new = _dec_sample(idx.reshape(nbs, N_KV * N_SELECT), page_table,
                                     fmajor(cache_slc_k), fmajor(cache_slc_v), q8,
                                     col4(kss), col4(vss), col4(kws), col4(vws),
                                     fmajor(state_win_k), fmajor(state_win_v), gates8, ocmp, tbl8, bkt_w, past)
    o_s = o8[:, :, :HPG].reshape(nbs, D_MODEL)
    y_s = tail(x2s, 1, p_sample[1].reshape(nbs, D_PLE), nbs, True, pre=(o_s, w_out))

    kv5s = lambda a: a.reshape(1, nbs, 1, N_KV, HEAD_DIM)
    win5s = lambda a: jnp.transpose(a, (0, 3, 1, 2))[None]
    return (y_p.reshape(nbp, t, D_MODEL), y_s.reshape(nbs, 1, D_MODEL),
            kv5(kc), kv5(vc), kv5(ks), kv5(vs), win5(kw), win5(vw), conv_p,
            kv5s(kcs), kv5s(vcs), kv5s(kss), kv5s(vss), win5s(wk_new), win5s(wv_new), conv_s[None])
```

```python
import functools
import math

import numpy as np
import jax
import jax.numpy as jnp
from jax import lax
from jax.experimental import pallas as pl
from jax.experimental.pallas import tpu as pltpu

F32 = jnp.float32
BF16 = jnp.bfloat16

D_MODEL = 1024
D_PLE = 256
CONV_W = 31
N_HEADS = 16
HEAD_DIM = 64
N_KV = 4
HPG = 4
KV_W = N_KV * HEAD_DIM
CMP_STRIDE = 16
CMP_HID = 128
SLC_BLOCK = 64
N_SELECT = 16
WINDOW = 512
NUM_BUCKETS = 32
MAX_DISTANCE = 128
D_FF = 4096
EPS = 1e-6
SCALE = HEAD_DIM ** -0.5
NEG_INF = float("-inf")

VMEM_LIMIT = 56 * 1024 * 1024
HALO = 32


def _sigmoid(x):
    return 1.0 / (1.0 + jnp.exp(-x))


def _rms(x, g):
    return x * lax.rsqrt(jnp.mean(x * x, axis=-1, keepdims=True) + EPS) * g


def _dot(a, b):
    return jnp.dot(a, b, preferred_element_type=F32)


def _split3(x):
    hi = x.astype(BF16)
    r1 = x - hi.astype(F32)
    mid = r1.astype(BF16)
    lo = (r1 - mid.astype(F32)).astype(BF16)
    return hi, mid, lo


def _rel_bucket(dist):
    n = jnp.maximum(dist, 0)
    max_exact = NUM_BUCKETS // 2
    nf = jnp.maximum(n, 1).astype(F32)
    large = max_exact + (jnp.log(nf / max_exact) / math.log(MAX_DISTANCE / max_exact)
                         * (NUM_BUCKETS - max_exact)).astype(jnp.int32)
    large = jnp.minimum(large, NUM_BUCKETS - 1)
    return jnp.where(n < max_exact, n, large)


def _params(sem):
    return pltpu.CompilerParams(dimension_semantics=sem, vmem_limit_bytes=VMEM_LIMIT)


def _const_spec(shape, single=False):
    n = len(shape)
    if single:
        return pl.BlockSpec(shape, lambda *_: (0,) * n, pipeline_mode=pl.Buffered(1))
    return pl.BlockSpec(shape, lambda *_: (0,) * n)


def _conv_in_body(x_ref, g_ref, w_ref, b_ref, o_ref):
    h = _rms(x_ref[...], g_ref[...]).astype(BF16)
    u = _dot(h, w_ref[...]) + b_ref[...]
    o_ref[...] = u[:, :D_MODEL] * _sigmoid(u[:, D_MODEL:])


def _conv_in(x, g, w1, b1, tm):
    n = x.shape[0]
    return pl.pallas_call(
        _conv_in_body,
        grid=(n // tm,),
        in_specs=[pl.BlockSpec((tm, D_MODEL), lambda i: (i, 0)),
                  _const_spec((1, D_MODEL)),
                  _const_spec((D_MODEL, 2 * D_MODEL)),
                  _const_spec((1, 2 * D_MODEL))],
        out_specs=pl.BlockSpec((tm, D_MODEL), lambda i: (i, 0)),
        out_shape=jax.ShapeDtypeStruct((n, D_MODEL), F32),
        compiler_params=_params(("parallel",)),
        name="conv_in",
    )(x, g, w1, b1)


def _ln_silu_proj(y, lng, lnb, w2, b2, x):
    mu = jnp.mean(y, axis=-1, keepdims=True)
    yc = y - mu
    var = jnp.mean(yc * yc, axis=-1, keepdims=True)
    z = yc * lax.rsqrt(var + EPS) * lng + lnb
    z = z * _sigmoid(z)
    return _dot(z.astype(BF16), w2) + b2 + x


def _conv_out_body(tt, cur_ref, halo_ref, x_ref, dw_ref, dwb_ref, lng_ref, lnb_ref, w2_ref, b2_ref,
                   o_ref, ctx_ref, y_ref):
    i = pl.program_id(1)
    n_strip = D_MODEL // 128
    keep = (i > 0).astype(F32)
    for c in range(n_strip):
        ctx_ref[c, 0:HALO, :] = halo_ref[0, :, 128 * c:128 * (c + 1)] * keep
        ctx_ref[c, HALO:, :] = cur_ref[0, :, 128 * c:128 * (c + 1)]

    def strip(c, carry):
        acc = jnp.broadcast_to(dwb_ref[c], (tt, 128))
        for k in range(CONV_W):
            off = k + HALO - (CONV_W - 1)
            acc = acc + ctx_ref[c, off:off + tt, :] * dw_ref[c, k:k + 1, :]
        y_ref[c] = acc
        return carry

    lax.fori_loop(0, n_strip, strip, 0)
    y = jnp.concatenate([y_ref[c] for c in range(n_strip)], axis=1)
    o_ref[0] = _ln_silu_proj(y, lng_ref[...], lnb_ref[...], w2_ref[...], b2_ref[...], x_ref[0])


def _conv_out(glu, x, dw3, dwb3, lng, lnb, w2, b2, tt):
    b, t, _ = glu.shape
    hb = tt // HALO
    return pl.pallas_call(
        functools.partial(_conv_out_body, tt),
        grid=(b, t // tt),
        in_specs=[pl.BlockSpec((1, tt, D_MODEL), lambda bi, i: (bi, i, 0)),
                  pl.BlockSpec((1, HALO, D_MODEL), lambda bi, i: (bi, jnp.maximum(i * hb - 1, 0), 0)),
                  pl.BlockSpec((1, tt, D_MODEL), lambda bi, i: (bi, i, 0)),
                  _const_spec((D_MODEL // 128, CONV_W, 128)),
                  _const_spec((D_MODEL // 128, 1, 128)),
                  _const_spec((1, D_MODEL)),
                  _const_spec((1, D_MODEL)),
                  _const_spec((D_MODEL, D_MODEL)),
                  _const_spec((1, D_MODEL))],
        out_specs=pl.BlockSpec((1, tt, D_MODEL), lambda bi, i: (bi, i, 0)),
        out_shape=jax.ShapeDtypeStruct((b, t, D_MODEL), F32),
        scratch_shapes=[pltpu.VMEM((D_MODEL // 128, tt + HALO, 128), F32),
                        pltpu.VMEM((D_MODEL // 128, tt, 128), F32)],
        compiler_params=_params(("parallel", "arbitrary")),
        name="conv_out",
    )(glu, glu, x, dw3, dwb3, lng, lnb, w2, b2)


def _conv_step_body(st_ref, u_ref, x_ref, dw_ref, dwb_ref, lng_ref, lnb_ref, w2_ref, b2_ref,
                    o_ref, ns_ref):
    nb = st_ref.shape[0]
    hist = CONV_W - 1
    dwh = dw_ref[0:hist, :]
    rows = []
    for bi in range(nb):
        rows.append(jnp.sum(st_ref[bi] * dwh, axis=0, keepdims=True))
        ns_ref[bi, 0:hist - 1, :] = st_ref[bi, 1:hist, :]
        ns_ref[bi, hist - 1:hist, :] = u_ref[bi:bi + 1, :]
    y = jnp.concatenate(rows, axis=0) + u_ref[...] * dw_ref[hist:hist + 1, :] + dwb_ref[...]
    o_ref[...] = _ln_silu_proj(y, lng_ref[...], lnb_ref[...], w2_ref[...], b2_ref[...], x_ref[...])


def _conv_step(state, glu, x, dw, dwb, lng, lnb, w2, b2):
    nb, hist, _ = state.shape
    return pl.pallas_call(
        _conv_step_body,
        grid=(1,),
        in_specs=[_const_spec((nb, hist, D_MODEL)), _const_spec((nb, D_MODEL)), _const_spec((nb, D_MODEL)),
                  _const_spec((CONV_W, D_MODEL)), _const_spec((1, D_MODEL)), _const_spec((1, D_MODEL)),
                  _const_spec((1, D_MODEL)), _const_spec((D_MODEL, D_MODEL)), _const_spec((1, D_MODEL))],
        out_specs=[_const_spec((nb, D_MODEL)), _const_spec((nb, hist, D_MODEL))],
        out_shape=[jax.ShapeDtypeStruct((nb, D_MODEL), F32),
                   jax.ShapeDtypeStruct((nb, hist, D_MODEL), F32)],
        compiler_params=_params(("arbitrary",)),
        name="conv_step",
    )(state, glu, x, dw, dwb, lng, lnb, w2, b2)


def _mlp_body(final, pre, *refs):
    if pre:
        o_in_ref, wo_ref, refs = refs[0], refs[1], refs[2:]
    (x_ref, gf_ref, up_ref, dn_ref, gp_ref, wg_ref, p_ref, wp_ref, gfin_ref,
     o_ref, h_ref, acc_ref, x1_ref) = refs
    j = pl.program_id(1)

    @pl.when(j == 0)
    def _():
        x1 = x_ref[...]
        if pre:
            x1 = x1 + _dot(o_in_ref[...].astype(BF16), wo_ref[...])
        x1_ref[...] = x1
        h_ref[...] = _rms(x1, gf_ref[...]).astype(BF16)
        acc_ref[...] = jnp.zeros_like(acc_ref)

    a = jnp.maximum(_dot(h_ref[...], up_ref[...]), 0.0)
    acc_ref[...] += _dot((a * a).astype(BF16), dn_ref[...])

    @pl.when(j == pl.num_programs(1) - 1)
    def _():
        x2 = x1_ref[...] + acc_ref[...]
        gate = _sigmoid(_dot(_rms(x2, gp_ref[...]).astype(BF16), wg_ref[...]))
        x3 = x2 + gate * _dot(p_ref[...].astype(BF16), wp_ref[...])
        if final:
            x3 = _rms(x3, gfin_ref[...])
        o_ref[...] = x3


def _mlp_ple(x, gf, up, dn, gp, wg, p, wp, gfin, tm, tf, final, pre=None):
    n = x.shape[0]
    tok = lambda i, j: (i, 0)
    in_specs, args = [], []
    if pre is not None:
        in_specs += [pl.BlockSpec((tm, D_MODEL), tok), _const_spec((D_MODEL, D_MODEL))]
        args += list(pre)
    in_specs += [pl.BlockSpec((tm, D_MODEL), tok),
                 _const_spec((1, D_MODEL)),
                 pl.BlockSpec((D_MODEL, tf), lambda i, j: (0, j)),
                 pl.BlockSpec((tf, D_MODEL), lambda i, j: (j, 0)),
                 _const_spec((1, D_MODEL)),
                 _const_spec((D_MODEL, D_MODEL)),
                 pl.BlockSpec((tm, D_PLE), tok),
                 _const_spec((D_PLE, D_MODEL)),
                 _const_spec((1, D_MODEL))]
    args += [x, gf, up, dn, gp, wg, p, wp, gfin]
    return pl.pallas_call(
        functools.partial(_mlp_body, final, pre is not None),
        grid=(n // tm, D_FF // tf),
        in_specs=in_specs,
        out_specs=pl.BlockSpec((tm, D_MODEL), tok),
        out_shape=jax.ShapeDtypeStruct((n, D_MODEL), F32),
        scratch_shapes=[pltpu.VMEM((tm, D_MODEL), BF16), pltpu.VMEM((tm, D_MODEL), F32),
                        pltpu.VMEM((tm, D_MODEL), F32)],
        compiler_params=_params(("parallel", "arbitrary")),
        name="mlp_ple",
    )(*args)


def _attn_in_body(x_ref, g_ref, wq_ref, wkv_ref, wg_ref, q_ref, kc_ref, vc_ref, ks_ref, vs_ref,
                  kw_ref, vw_ref, gt_ref):
    h = _rms(x_ref[...], g_ref[...]).astype(BF16)
    q_ref[...] = _dot(h, wq_ref[...])
    kv = _dot(h, wkv_ref[...])
    for i, r in enumerate((kc_ref, vc_ref, ks_ref, vs_ref, kw_ref, vw_ref)):
        r[...] = kv[:, KV_W * i:KV_W * (i + 1)]
    gt_ref[...] = _sigmoid(_dot(h, wg_ref[...]))


def _attn_in(x, g, wq, wkv, wg, tm):
    n = x.shape[0]
    tok = lambda i: (i, 0)
    kv_spec = pl.BlockSpec((tm, KV_W), tok)
    kv_shape = jax.ShapeDtypeStruct((n, KV_W), F32)
    return pl.pallas_call(
        _attn_in_body,
        grid=(n // tm,),
        in_specs=[pl.BlockSpec((tm, D_MODEL), tok), _const_spec((1, D_MODEL)),
                  _const_spec((D_MODEL, D_MODEL)), _const_spec((D_MODEL, 6 * KV_W)),
                  _const_spec((D_MODEL, 128))],
        out_specs=[pl.BlockSpec((tm, D_MODEL), tok)] + [kv_spec] * 6 + [pl.BlockSpec((tm, 128), tok)],
        out_shape=[jax.ShapeDtypeStruct((n, D_MODEL), F32)] + [kv_shape] * 6
                  + [jax.ShapeDtypeStruct((n, 128), F32)],
        compiler_params=_params(("parallel",)),
        name="attn_in",
    )(x, g, wq, wkv, wg)


def _compress_hidden(load_rows, n_chunks, wexp_ref):
    acc = jnp.zeros((2 * n_chunks, 4 * CMP_HID), F32)
    for r in range(CMP_STRIDE):
        x = jnp.concatenate([load_rows(2 * r), load_rows(2 * r + 1)], axis=0)
        acc = acc + _dot(x.astype(BF16), wexp_ref[r])
    lo, hi = acc[:n_chunks], acc[n_chunks:]
    w = 2 * CMP_HID
    return jnp.concatenate([lo[:, :w], hi[:, :w], lo[:, w:], hi[:, w:]], axis=1)


def _pe_term(pe_ref, w1_ref):
    t = _dot(pe_ref[...], w1_ref[...])[0:1, :]
    return jnp.concatenate([t] * N_KV, axis=1)


def _compress_p_body(nc, kc_ref, vc_ref, wek_ref, pek_ref, w1k_ref, w2k_ref,
                     wev_ref, pev_ref, w1v_ref, w2v_ref, ok_ref, ov_ref):
    half = N_KV * CMP_HID
    row = lax.broadcasted_iota(jnp.int32, (nc, KV_W), 0)
    for src, we, pe, w1, w2, out in ((kc_ref, wek_ref, pek_ref, w1k_ref, w2k_ref, ok_ref),
                                     (vc_ref, wev_ref, pev_ref, w1v_ref, w2v_ref, ov_ref)):
        hh = _compress_hidden(lambda k: src[0, pl.ds(k, nc, stride=2 * CMP_STRIDE), :], nc, we)
        nxt = pltpu.roll(hh[:, half:], nc - 1, 0)
        a = hh[:, :half] + nxt + _pe_term(pe, w1)
        a = a * _sigmoid(a)
        res = _dot(a.astype(BF16), w2[...])
        out[0] = jnp.where(row < nc - 1, res, 0.0)


def _compress_prompt(kc, vc, wk, wv):
    b, t2, _ = kc.shape
    nc = t2 // (2 * CMP_STRIDE)
    seq = pl.BlockSpec((1, t2, 128), lambda i: (i, 0, 0))
    wspecs = [_const_spec((CMP_STRIDE, 128, 4 * CMP_HID)),
              _const_spec((8, 2 * CMP_STRIDE * HEAD_DIM)),
              _const_spec((2 * CMP_STRIDE * HEAD_DIM, CMP_HID)), _const_spec((N_KV * CMP_HID, KV_W))]
    out = pl.BlockSpec((1, nc, KV_W), lambda i: (i, 0, 0))
    return pl.pallas_call(
        functools.partial(_compress_p_body, nc),
        grid=(b,),
        in_specs=[seq, seq] + wspecs + wspecs,
        out_specs=[out, out],
        out_shape=[jax.ShapeDtypeStruct((b, nc, KV_W), F32)] * 2,
        compiler_params=_params(("parallel",)),
        name="compress_prompt",
    )(kc, vc, *wk, *wv)


def _softmax_cols(s, mask):
    s = jnp.where(mask, s, NEG_INF)
    m = jnp.max(s, axis=0, keepdims=True)
    m = jnp.where(m == NEG_INF, 0.0, m)
    e = jnp.where(mask, jnp.exp(s - m), 0.0)
    return e / jnp.maximum(jnp.sum(e, axis=0, keepdims=True), 1e-30)


def _attn_p_body(tq, t, q_ref, gt_ref, ks_ref, vs_ref, kw_ref, vw_ref, kcm_ref, vcm_ref,
                 tbl_ref, bt_ref, bc_ref, msel_ref, o_ref,
                 ksb, kwb, vst, vwt, kcb, vct, btoep, bcmp, qt, selt, ot, m_ref, l_ref, acc_ref):
    b = pl.program_id(0)
    qi = pl.program_id(1)
    nkt = t // tq
    ncmp = kcm_ref.shape[1]
    q0 = qi * tq

    @pl.when((b == 0) & (qi == 0))
    def _():
        def per_head(h, carry):
            for d in range(2):
                bk = bt_ref[d]
                acc = jnp.zeros((tq, tq), F32)
                for bb in range(NUM_BUCKETS):
                    acc = jnp.where(bk == bb, tbl_ref[bb, h], acc)
                btoep[h, d] = acc
            btoep[h, 2] = jnp.full((tq, tq), tbl_ref[NUM_BUCKETS - 1, h], F32)
            bk = bc_ref[...]
            acc = jnp.zeros(bk.shape, F32)
            for bb in range(NUM_BUCKETS):
                acc = jnp.where(bk == bb, tbl_ref[bb, h], acc)
            bcmp[h] = acc
            return carry
        lax.fori_loop(0, N_HEADS, per_head, 0)

    @pl.when(qi == 0)
    def _():
        for kt in range(nkt):
            rows = slice(kt * tq, (kt + 1) * tq)
            for src, dstk in ((ks_ref, ksb), (kw_ref, kwb)):
                blk = src[0, rows, :]
                for g in range(N_KV):
                    dstk[g, rows, :] = blk[:, HEAD_DIM * g:HEAD_DIM * (g + 1)].astype(BF16)
            for src, dstv in ((vs_ref, vst), (vw_ref, vwt)):
                blk_t = src[0, rows, :].T
                for g in range(N_KV):
                    dstv[g, kt] = blk_t[HEAD_DIM * g:HEAD_DIM * (g + 1), :].astype(BF16)
        kc = kcm_ref[0]
        vc_t = vcm_ref[0].T
        for g in range(N_KV):
            kcb[g] = kc[:, HEAD_DIM * g:HEAD_DIM * (g + 1)].astype(BF16)
            vct[g] = vc_t[HEAD_DIM * g:HEAD_DIM * (g + 1), :].astype(BF16)

    qt[...] = q_ref[0].T.astype(BF16)
    gt = gt_ref[0].T

    qpos_row = q0 + lax.broadcasted_iota(jnp.int32, (1, tq), 1)
    i_iota = lax.broadcasted_iota(jnp.int32, (tq, tq), 1)
    j_iota = lax.broadcasted_iota(jnp.int32, (tq, tq), 0)
    rel = i_iota - j_iota

    c_iota = lax.broadcasted_iota(jnp.int32, (ncmp, tq), 0)
    cmask = (CMP_STRIDE * c_iota + (2 * CMP_STRIDE - 1)) <= qpos_row
    coff = pl.multiple_of((nkt - 1 - qi) * (tq // CMP_STRIDE), 8)

    nblk = t // SLC_BLOCK
    blk_iota = lax.broadcasted_iota(jnp.int32, (nblk, tq), 0)
    cur = qpos_row // SLC_BLOCK
    valid = blk_iota * SLC_BLOCK <= qpos_row
    forced = (blk_iota == 0) | (blk_iota == cur) | (blk_iota == cur - 1)

    for g in range(N_KV):
        pg = jnp.zeros((ncmp, tq), F32)
        for hh in range(HPG):
            h = g * HPG + hh
            qh = qt[HEAD_DIM * h:HEAD_DIM * (h + 1), :]
            s = _dot(kcb[g], qh) * SCALE + bcmp[h, pl.ds(coff, ncmp), :]
            p = _softmax_cols(s, cmask)
            pg = pg + p
            oc = _dot(vct[g], p.astype(BF16))
            ot[HEAD_DIM * h:HEAD_DIM * (h + 1), :] = gt[h:h + 1, :] * oc
        hi, mid, lo = _split3(pg)
        msel = msel_ref[...]
        score = _dot(msel, hi) + _dot(msel, mid) + _dot(msel, lo)
        score = jnp.where(forced, jnp.inf, jnp.where(valid, score, NEG_INF))
        rank = jnp.zeros((nblk, tq), jnp.int32)
        for i in range(nblk):
            si = score[i:i + 1, :]
            beats = (si > score) | ((blk_iota > i) & (si == score))
            rank = rank + beats.astype(jnp.int32)
        sel = (rank < N_SELECT).astype(F32)
        for i in range(nblk):
            selt[g, i] = sel[i:i + 1, :]

    def flash(kb, vt, lo_kt, gate_row0, masker):
        m_ref[...] = jnp.full(m_ref.shape, NEG_INF, F32)
        l_ref[...] = jnp.zeros(l_ref.shape, F32)
        acc_ref[...] = jnp.zeros(acc_ref.shape, F32)

        def body(kt, carry):
            k0 = pl.multiple_of(kt * tq, tq)
            dist = q0 - k0 + rel
            dd = jnp.minimum(qi - kt, 2)
            for g in range(N_KV):
                kblk = kb[g, pl.ds(k0, tq), :]
                vblk = vt[g, kt]
                msk = masker(g, kt, dist)
                for hh in range(HPG):
                    h = g * HPG + hh
                    rows = slice(HEAD_DIM * h, HEAD_DIM * (h + 1))
                    s = _dot(kblk, qt[rows, :]) * SCALE + btoep[h, dd]
                    s = jnp.where(msk, s, NEG_INF)
                    m_old = m_ref[h]
                    m_new = jnp.maximum(m_old, jnp.max(s, axis=0, keepdims=True))
                    m_safe = jnp.where(m_new == NEG_INF, 0.0, m_new)
                    alpha = jnp.exp(m_old - m_safe)
                    p = jnp.exp(s - m_safe)
                    l_ref[h] = alpha * l_ref[h] + jnp.sum(p, axis=0, keepdims=True)
                    acc_ref[rows, :] = alpha * acc_ref[rows, :] + _dot(vblk, p.astype(BF16))
                    m_ref[h] = m_new
            return carry

        lax.fori_loop(lo_kt, qi + 1, body, 0)
        for h in range(N_HEADS):
            rows = slice(HEAD_DIM * h, HEAD_DIM * (h + 1))
            o = acc_ref[rows, :] / jnp.maximum(l_ref[h], 1e-30)
            ot[rows, :] = ot[rows, :] + gt[gate_row0 + h:gate_row0 + h + 1, :] * o

    def slc_mask(g, kt, dist):
        half = tq // SLC_BLOCK
        rows = [jnp.broadcast_to(selt[g, kt * half + r], (SLC_BLOCK, tq)) for r in range(half)]
        return (jnp.concatenate(rows, axis=0) > 0.5) & (dist >= 0)

    def win_mask(g, kt, dist):
        return (dist >= 0) & (dist <= WINDOW)

    flash(ksb, vst, 0, N_HEADS, slc_mask)
    flash(kwb, vwt, jnp.maximum(qi - WINDOW // tq, 0), 2 * N_HEADS, win_mask)

    o_ref[0] = ot[...].T


def _attn_prompt(q, gates, ks, vs, kw, vw, kcmp, vcmp, table, bkt_toep, bkt_cmp, msel, tq):
    b, t, _ = q.shape
    nkt = t // tq
    ncmp = kcmp.shape[1]
    qspec = pl.BlockSpec((1, tq, D_MODEL), lambda bi, i: (bi, i, 0))
    seq = pl.BlockSpec((1, t, KV_W), lambda bi, i: (bi, 0, 0))
    cmp_spec = pl.BlockSpec((1, ncmp, KV_W), lambda bi, i: (bi, 0, 0))
    return pl.pallas_call(
        functools.partial(_attn_p_body, tq, t),
        grid=(b, nkt),
        in_specs=[qspec, pl.BlockSpec((1, tq, 128), lambda bi, i: (bi, i, 0)),
                  seq, seq, seq, seq, cmp_spec, cmp_spec,
                  pl.BlockSpec(memory_space=pltpu.SMEM),
                  _const_spec(bkt_toep.shape), _const_spec(bkt_cmp.shape), _const_spec(msel.shape)],
        out_specs=qspec,
        out_shape=jax.ShapeDtypeStruct((b, t, D_MODEL), F32),
        scratch_shapes=[pltpu.VMEM((N_KV, t, HEAD_DIM), BF16), pltpu.VMEM((N_KV, t, HEAD_DIM), BF16),
                        pltpu.VMEM((N_KV, nkt, HEAD_DIM, tq), BF16), pltpu.VMEM((N_KV, nkt, HEAD_DIM, tq), BF16),
                        pltpu.VMEM((N_KV, ncmp, HEAD_DIM), BF16), pltpu.VMEM((N_KV, HEAD_DIM, ncmp), BF16),
                        pltpu.VMEM((N_HEADS, 3, tq, tq), F32), pltpu.VMEM((N_HEADS,) + bkt_cmp.shape, F32),
                        pltpu.VMEM((D_MODEL, tq), BF16), pltpu.VMEM((N_KV, t // SLC_BLOCK, 1, tq), F32),
                        pltpu.VMEM((D_MODEL, tq), F32),
                        pltpu.VMEM((N_HEADS, 1, tq), F32), pltpu.VMEM((N_HEADS, 1, tq), F32),
                        pltpu.VMEM((D_MODEL, tq), F32)],
        compiler_params=_params(("arbitrary", "arbitrary")),
        name="attn_prompt",
    )(q, gates, ks, vs, kw, vw, kcmp, vcmp, table, bkt_toep, bkt_cmp, msel)


PAGE = 128
PAGES_PER_GROUP = 16
Q_ROWS = 8


def _rank_desc(score_row, n):
    a = jnp.broadcast_to(score_row, (n, n))
    at = a.T
    i = lax.broadcasted_iota(jnp.int32, (n, n), 0)
    j = lax.broadcasted_iota(jnp.int32, (n, n), 1)
    beats = (at > a) | ((i < j) & (at == a))
    return jnp.sum(beats.astype(F32), axis=0, keepdims=True)


def _bias_rows(bucket, tcols):
    r = tcols.shape[0]
    out = jnp.zeros((r, bucket.shape[1]), F32)
    for bb in range(NUM_BUCKETS):
        out = jnp.where(bucket == bb, tcols[:, bb:bb + 1], out)
    return out


def _softmax_rows(s, mask):
    s = jnp.where(mask, s, NEG_INF)
    m = jnp.max(s, axis=1, keepdims=True)
    m = jnp.where(m == NEG_INF, 0.0, m)
    e = jnp.where(mask, jnp.exp(s - m), 0.0)
    return e / jnp.maximum(jnp.sum(e, axis=1, keepdims=True), 1e-30)


def _dot_nt(a, b):
    return lax.dot_general(a, b, (((1,), (1,)), ((), ())), preferred_element_type=F32)


def _cmp_s_body(past, pt_ref, ck_hbm, cv_hbm, kcn_ref, vcn_ref, q_ref,
                wek_ref, pek_ref, w1k_ref, w2k_ref, wev_ref, pev_ref, w1v_ref, w2v_ref,
                tbl_ref, bkt_ref, msel_ref, ocmp_ref, idx_ref,
                kbuf, vbuf, tbuf, hk, hv, sem):
    b = pl.program_id(0)
    rows_pg = PAGES_PER_GROUP * PAGE
    cpg = rows_pg // CMP_STRIDE
    ngrp = past // rows_pg
    nc = past // CMP_STRIDE
    half = N_KV * CMP_HID

    def copies(gi, slot):
        out = []
        for p in range(PAGES_PER_GROUP):
            page = pt_ref[b, gi * PAGES_PER_GROUP + p]
            out.append(pltpu.make_async_copy(ck_hbm.at[page], kbuf.at[slot, p], sem.at[0, slot]))
            out.append(pltpu.make_async_copy(cv_hbm.at[page], vbuf.at[slot, p], sem.at[1, slot]))
        return out

    for c in copies(0, 0):
        c.start()

    def group(gi, carry):
        slot = gi % 2

        @pl.when(gi + 1 < ngrp)
        def _():
            for c in copies(gi + 1, 1 - slot):
                c.start()

        for c in copies(gi, slot):
            c.wait()
        r0 = pl.multiple_of(gi * cpg, cpg)
        for buf, we, hdst in ((kbuf, wek_ref, hk), (vbuf, wev_ref, hv)):
            for p in range(PAGES_PER_GROUP):
                for hf in range(2):
                    tbuf[hf, p * PAGE:(p + 1) * PAGE, :] = buf[slot, p, hf * 128:(hf + 1) * 128, :].T
            hdst[pl.ds(r0, cpg), :] = _compress_hidden(
                lambda k: tbuf[k % 2, pl.ds(k // 2, cpg, stride=CMP_STRIDE), :], cpg, we)
        return carry

    lax.fori_loop(0, ngrp, group, 0)

    col = lax.broadcasted_iota(jnp.int32, (1, nc), 1)
    cmask = (CMP_STRIDE * col + (2 * CMP_STRIDE - 1)) <= past
    row8 = lax.broadcasted_iota(jnp.int32, (Q_ROWS, nc), 0)
    cmpd = []
    for new_ref, we, pe, w1, w2, hdst in ((kcn_ref, wek_ref, pek_ref, w1k_ref, w2k_ref, hk),
                                          (vcn_ref, wev_ref, pev_ref, w1v_ref, w2v_ref, hv)):
        new = new_ref[0]
        new_lo = _dot(jnp.broadcast_to(new[:, :128], (8, 128)).astype(BF16), we[0])
        new_hi = _dot(jnp.broadcast_to(new[:, 128:], (8, 128)).astype(BF16), we[0])
        w = 2 * CMP_HID
        hdst[nc:nc + 8, :] = jnp.concatenate([new_lo[:, :w], new_hi[:, :w], new_lo[:, w:], new_hi[:, w:]], axis=1)
        a = hdst[0:nc, 0:half] + hdst[1:nc + 1, half:2 * half] + _pe_term(pe, w1)
        a = (a * _sigmoid(a)).astype(BF16)
        cmpd.append([_dot(a[:, CMP_HID * g:CMP_HID * (g + 1)], w2[...]).astype(BF16) for g in range(N_KV)])

    nsel = msel_ref.shape[1]
    lane = lax.broadcasted_iota(jnp.int32, (1, nsel), 1)
    n_blocks = past // SLC_BLOCK + 1
    cur = past // SLC_BLOCK
    forced = (lane == 0) | (lane == cur) | (lane == cur - 1)
    valid = (lane * SLC_BLOCK <= past) & (lane < n_blocks)
    k_iota = lax.broadcasted_iota(jnp.int32, (N_SELECT, nsel), 0).astype(F32)
    lane_f = lax.broadcasted_iota(jnp.int32, (N_SELECT, nsel), 1).astype(F32)
    for g in range(N_KV):
        qg = q_ref[0, g].astype(BF16)
        bias = _bias_rows(bkt_ref[...], tbl_ref[g])
        s = _dot_nt(qg, cmpd[0][g]) * SCALE + bias
        p = _softmax_rows(s, cmask)
        ocmp_ref[0, g] = _dot(p.astype(BF16), cmpd[1][g])
        pg = jnp.sum(jnp.where(row8 < HPG, p, 0.0), axis=0, keepdims=True)
        hi, mid, lo = _split3(jnp.broadcast_to(pg, (8, nc)))
        msel = msel_ref[...]
        score = (_dot(hi, msel) + _dot(mid, msel) + _dot(lo, msel))[0:1, :]
        score = jnp.where(forced, jnp.inf, jnp.where(valid, score, NEG_INF))
        rank = _rank_desc(score, nsel)
        hit = jnp.where(jnp.broadcast_to(rank, (N_SELECT, nsel)) == k_iota, lane_f, 0.0)
        idx_ref[0, g] = jnp.sum(hit, axis=1, keepdims=True).astype(jnp.int32)


def _cmp_sample(page_table, cache_k, cache_v, kc_new, vc_new, q8, wk, wv, tbl8, bkt, msel, past):
    nb = page_table.shape[0]
    rows_pg = PAGES_PER_GROUP * PAGE
    nc = past // CMP_STRIDE
    any_spec = pl.BlockSpec(memory_space=pl.ANY)
    new_spec = pl.BlockSpec((1, 1, KV_W), lambda i, pt: (i, 0, 0))
    cs = lambda shape: pl.BlockSpec(shape, lambda i, pt: (0,) * len(shape))
    wspecs = [cs((CMP_STRIDE, 128, 4 * CMP_HID)),
              cs((8, 2 * CMP_STRIDE * HEAD_DIM)),
              cs((2 * CMP_STRIDE * HEAD_DIM, CMP_HID)), cs((CMP_HID, HEAD_DIM))]
    grid_spec = pltpu.PrefetchScalarGridSpec(
        num_scalar_prefetch=1,
        grid=(nb,),
        in_specs=[any_spec, any_spec, new_spec, new_spec,
                  pl.BlockSpec((1, N_KV, Q_ROWS, HEAD_DIM), lambda i, pt: (i, 0, 0, 0))]
                 + wspecs + wspecs + [cs(tbl8.shape), cs(bkt.shape), cs(msel.shape)],
        out_specs=[pl.BlockSpec((1, N_KV, Q_ROWS, HEAD_DIM), lambda i, pt: (i, 0, 0, 0)),
                   pl.BlockSpec((1, N_KV, N_SELECT, 1), lambda i, pt: (i, 0, 0, 0))],
        scratch_shapes=[pltpu.VMEM((2, PAGES_PER_GROUP, 2 * PAGE, 128), F32),
                        pltpu.VMEM((2, PAGES_PER_GROUP, 2 * PAGE, 128), F32),
                        pltpu.VMEM((2, rows_pg, 128), F32),
                        pltpu.VMEM((nc + 8, 2 * N_KV * CMP_HID), F32),
                        pltpu.VMEM((nc + 8, 2 * N_KV * CMP_HID), F32),
                        pltpu.SemaphoreType.DMA((2, 2))])
    return pl.pallas_call(
        functools.partial(_cmp_s_body, past),
        grid_spec=grid_spec,
        out_shape=[jax.ShapeDtypeStruct((nb, N_KV, Q_ROWS, HEAD_DIM), F32),
                   jax.ShapeDtypeStruct((nb, N_KV, N_SELECT, 1), jnp.int32)],
        compiler_params=_params(("arbitrary",)),
        name="cmp_sample",
    )(page_table, cache_k, cache_v, kc_new, vc_new, q8, *wk, *wv, tbl8, bkt, msel)


NEW_LANES = 128


def _attend_kt(qg, kt, vt, bias, mask):
    s = _dot(qg, kt.astype(BF16)) * SCALE + bias
    p = _softmax_rows(s, mask)
    return _dot_nt(p.astype(BF16), vt.astype(BF16))


def _dec_s_body(past, idx_ref, pt_ref, csk_hbm, csv_hbm,
                q_ref, ksn_ref, vsn_ref, kwn_ref, vwn_ref, wk_ref, wv_ref, gt_ref, ocmp_ref,
                tbl_ref, bktw_ref, o_ref, wko_ref, wvo_ref,
                kbuf, vbuf, kwbuf, vwbuf, sem):
    b = pl.program_id(0)
    n_pages = past // PAGE
    cur = past // SLC_BLOCK
    bpp = PAGE // SLC_BLOCK
    lsel = N_SELECT * PAGE
    wb = wk_ref.shape[3]

    @pl.when(b == 0)
    def _():
        kbuf[...] = jnp.zeros_like(kbuf)
        vbuf[...] = jnp.zeros_like(vbuf)
        kwbuf[...] = jnp.zeros_like(kwbuf)
        vwbuf[...] = jnp.zeros_like(vwbuf)

    copies = []
    for g in range(N_KV):
        for k in range(N_SELECT):
            blk = idx_ref[b, g * N_SELECT + k]
            page = pt_ref[b, jnp.minimum(blk // bpp, n_pages - 1)]
            dst = pl.ds(k * PAGE, PAGE)
            copies.append(pltpu.make_async_copy(csk_hbm.at[page, g], kbuf.at[g, :, dst], sem.at[0]))
            copies.append(pltpu.make_async_copy(csv_hbm.at[page, g], vbuf.at[g, :, dst], sem.at[1]))
    for c in copies:
        c.start()

    lane_wb = lax.broadcasted_iota(jnp.int32, (HEAD_DIM, wb), 1)
    for g in range(N_KV):
        for src, new_ref, dst, buf in ((wk_ref, kwn_ref, wko_ref, kwbuf), (wv_ref, vwn_ref, wvo_ref, vwbuf)):
            st = src[0, g]
            newc = new_ref[0, g]
            dst[0, g] = jnp.where(lane_wb == wb - 1, newc, pltpu.roll(st, wb - 1, 1))
            buf[g, :, 0:wb] = st
            buf[g, :, wb:wb + 1] = newc

    for c in copies:
        c.wait()

    ls = lsel + NEW_LANES
    lw = wb + NEW_LANES
    lane_s = lax.broadcasted_iota(jnp.int32, (1, ls), 1)
    lane_w = lax.broadcasted_iota(jnp.int32, (1, lw), 1)
    wmask = lane_w <= wb
    tok = lane_s % PAGE
    for g in range(N_KV):
        kbuf[g, :, lsel:lsel + 1] = ksn_ref[0, g]
        vbuf[g, :, lsel:lsel + 1] = vsn_ref[0, g]
        qg = q_ref[0, g].astype(BF16)
        tcols = tbl_ref[g]

        kpos = past + (lane_s - lsel)
        gathered = lane_s < 0
        has_new = jnp.int32(0)
        for k in range(N_SELECT):
            blk = idx_ref[b, g * N_SELECT + k]
            in_slot = (lane_s // PAGE) == k
            kpos = jnp.where(in_slot, (blk // bpp) * PAGE + tok, kpos)
            gathered = gathered | (in_slot & (blk < cur) & ((tok // SLC_BLOCK) == (blk % bpp)))
            has_new = has_new | (blk == cur).astype(jnp.int32)
        smask = (gathered | ((lane_s == lsel) & (has_new > 0))) & (kpos <= past)
        sbias = _bias_rows(_rel_bucket(past - kpos), tcols)
        o_slc = _attend_kt(qg, kbuf[g], vbuf[g], sbias, smask)

        wbias = _bias_rows(bktw_ref[...], tcols)
        o_win = _attend_kt(qg, kwbuf[g], vwbuf[g], wbias, wmask)

        gt = gt_ref[0, g]
        o_ref[0, g] = gt[:, 0:1] * ocmp_ref[0, g] + gt[:, 1:2] * o_slc + gt[:, 2:3] * o_win


def _dec_sample(idx, page_table, cache_sk, cache_sv, q8, ks_new, vs_new, kw_new, vw_new, state_wk, state_wv,
                gates8, ocmp, tbl8, bktw, past):
    nb, wb = state_wk.shape[0], state_wk.shape[3]
    any_spec = pl.BlockSpec(memory_space=pl.ANY)
    new_spec = pl.BlockSpec((1, N_KV, HEAD_DIM, 1), lambda i, *_: (i, 0, 0, 0))
    st_spec = pl.BlockSpec((1, N_KV, HEAD_DIM, wb), lambda i, *_: (i, 0, 0, 0))
    head_spec = pl.BlockSpec((1, N_KV, Q_ROWS, HEAD_DIM), lambda i, *_: (i, 0, 0, 0))
    cs = lambda shape: pl.BlockSpec(shape, lambda i, *_: (0,) * len(shape))
    lsel = N_SELECT * PAGE
    grid_spec = pltpu.PrefetchScalarGridSpec(
        num_scalar_prefetch=2,
        grid=(nb,),
        in_specs=[any_spec, any_spec, head_spec,
                  new_spec, new_spec, new_spec, new_spec, st_spec, st_spec,
                  pl.BlockSpec((1, N_KV, Q_ROWS, 3), lambda i, *_: (i, 0, 0, 0)), head_spec,
                  cs(tbl8.shape), cs(bktw.shape)],
        out_specs=[head_spec, st_spec, st_spec],
        scratch_shapes=[pltpu.VMEM((N_KV, HEAD_DIM, lsel + NEW_LANES), F32),
                        pltpu.VMEM((N_KV, HEAD_DIM, lsel + NEW_LANES), F32),
                        pltpu.VMEM((N_KV, HEAD_DIM, wb + NEW_LANES), F32),
                        pltpu.VMEM((N_KV, HEAD_DIM, wb + NEW_LANES), F32),
                        pltpu.SemaphoreType.DMA((2,))])
    return pl.pallas_call(
        functools.partial(_dec_s_body, past),
        grid_spec=grid_spec,
        out_shape=[jax.ShapeDtypeStruct((nb, N_KV, Q_ROWS, HEAD_DIM), F32),
                   jax.ShapeDtypeStruct(state_wk.shape, F32),
                   jax.ShapeDtypeStruct(state_wv.shape, F32)],
        compiler_params=_params(("arbitrary",)),
        name="dec_sample",
    )(idx, page_table, cache_sk, cache_sv, q8, ks_new, vs_new, kw_new, vw_new,
      state_wk, state_wv, gates8, ocmp, tbl8, bktw)


def _expand_w1(w1):
    eye = jnp.eye(2, dtype=w1.dtype)
    parts = []
    for part in (w1[:CMP_STRIDE * HEAD_DIM], w1[CMP_STRIDE * HEAD_DIM:]):
        w = part.reshape(CMP_STRIDE, HEAD_DIM, CMP_HID)
        parts.append(jnp.einsum("gh,rdn->rgdhn", eye, w).reshape(CMP_STRIDE, 2 * HEAD_DIM, 2 * CMP_HID))
    return jnp.concatenate(parts, axis=-1).astype(BF16)


def _expand_w2(w2):
    eye = jnp.eye(N_KV, dtype=w2.dtype)
    return jnp.einsum("gh,nd->gnhd", eye, w2).reshape(N_KV * CMP_HID, KV_W).astype(BF16)


def _pe_rows(pe):
    return jnp.broadcast_to(pe.reshape(1, -1), (8, pe.size)).astype(BF16)


def _select_matrix(n_cmp, n_blocks):
    r = SLC_BLOCK // CMP_STRIDE
    m = np.zeros((n_cmp, n_blocks), np.float32)
    for s in range(n_blocks):
        for a in range(r):
            for bb in range(2):
                c = r * s + a - bb
                if 0 <= c < n_cmp:
                    m[c, s] += 1.0
    return m


def _pad_to(a, size, axis):
    pad = [(0, 0)] * a.ndim
    pad[axis] = (0, size - a.shape[axis])
    return jnp.pad(a, pad)


def kernel(x_prompt, x_sample, cache_cmp_k, cache_cmp_v, cache_slc_k, cache_slc_v, state_win_k, state_win_v, state_conv, page_table, p_prompt, p_sample, rel_table, norm_mix, norm_ffn, norm_ple, norm_final, conv_w1, conv_b1, conv_dw, conv_dwb, conv_ln_g, conv_ln_b, conv_w2, conv_b2, attn_w_in, attn_w_out, cmpk_w1, cmpk_pe, cmpk_w2, cmpv_w1, cmpv_pe, cmpv_w2, mlp_up, mlp_down, ple_proj, ple_gate):
    nbp, t, _ = x_prompt.shape
    nbs = x_sample.shape[0]
    n_p = nbp * t
    n_pool = cache_cmp_k.shape[1]
    past = page_table.shape[1] * cache_cmp_k.shape[2]
    wb = state_win_k.shape[2]
    hist = CONV_W - 1
    tm_p, tf, tt, tq = 512, 1024, 256, 128

    row = lambda a: a.reshape(1, -1)
    bf = lambda a: a.astype(BF16)

    w1c, w2c = bf(conv_w1[0]), bf(conv_w2[0])
    dw3 = conv_dw[0].reshape(CONV_W, D_MODEL // 128, 128).transpose(1, 0, 2)
    dwb3 = conv_dwb[0].reshape(D_MODEL // 128, 1, 128)
    up, dn = bf(mlp_up), bf(mlp_down)
    wg, wp = bf(ple_gate), bf(ple_proj)
    w_in = attn_w_in[0]
    wq = bf(w_in[:, :D_MODEL])
    wkv = bf(w_in[:, D_MODEL:D_MODEL + 6 * KV_W])
    wgt = bf(_pad_to(w_in[:, D_MODEL + 6 * KV_W:], 128, 1))
    w_out = bf(attn_w_out[0])
    cw = []
    for w1, pe, w2 in ((cmpk_w1[0], cmpk_pe[0], cmpk_w2[0]), (cmpv_w1[0], cmpv_pe[0], cmpv_w2[0])):
        cw.append((_expand_w1(w1), _pe_rows(pe), bf(w1), _expand_w2(w2), bf(w2)))
    wk_p, wv_p = [c[:4] for c in cw]
    wk_s, wv_s = [c[:3] + c[4:] for c in cw]

    def tail(x, i, p, tm, final, pre=None):
        return _mlp_ple(x, row(norm_ffn[i]), up[i], dn[i], row(norm_ple[i]), wg[i], p, wp[i],
                        row(norm_final), tm, tf, final, pre)

    conv_args = (row(conv_ln_g[0]), row(conv_ln_b[0]), w2c, row(conv_b2[0]))

    xp = x_prompt.reshape(n_p, D_MODEL)
    glu = _conv_in(xp, row(norm_mix[0]), w1c, row(conv_b1[0]), tm_p).reshape(nbp, t, D_MODEL)
    x1 = _conv_out(glu, x_prompt, dw3, dwb3, *conv_args, tt).reshape(n_p, D_MODEL)
    conv_p = glu[:, t - hist:][None]
    x2 = tail(x1, 0, p_prompt[0].reshape(n_p, D_PLE), tm_p, False)
    q, kc, vc, ks, vs, kw, vw, gates = _attn_in(x2, row(norm_mix[1]), wq, wkv, wgt, tm_p)
    seq = lambda a: a.reshape(nbp, t, -1)
    half_rows = lambda a: a.reshape(nbp, 2 * t, 128)
    kcmp, vcmp = _compress_prompt(half_rows(kc), half_rows(vc), wk_p, wv_p)

    nkt = t // tq
    ncmp = t // CMP_STRIDE
    ii = jnp.arange(tq, dtype=jnp.int32)[None, :]
    jj = jnp.arange(tq, dtype=jnp.int32)[:, None]
    bkt_toep = jnp.stack([_rel_bucket(d * tq + ii - jj) for d in range(2)])
    nshift = (nkt - 1) * (tq // CMP_STRIDE)
    rr = jnp.arange(ncmp + nshift, dtype=jnp.int32)[:, None]
    bkt_cmp = _rel_bucket(ii - CMP_STRIDE * (rr - nshift) - (2 * CMP_STRIDE - 1))
    msel_p = jnp.asarray(_select_matrix(ncmp, t // SLC_BLOCK).T, BF16)
    o_p = _attn_prompt(seq(q), seq(gates), seq(ks), seq(vs), seq(kw), seq(vw), kcmp, vcmp,
                       rel_table, bkt_toep, bkt_cmp, msel_p, tq)
    y_p = tail(x2, 1, p_prompt[1].reshape(n_p, D_PLE), tm_p, True, pre=(o_p.reshape(n_p, D_MODEL), w_out))

    kv5 = lambda a: a.reshape(1, nbp, t, N_KV, HEAD_DIM)
    keep = min(WINDOW, t)
    win5 = lambda a: seq(a)[:, t - keep:].reshape(1, nbp, keep, N_KV, HEAD_DIM)

    xs = x_sample.reshape(nbs, D_MODEL)
    glu_s = _conv_in(xs, row(norm_mix[0]), w1c, row(conv_b1[0]), nbs)
    x1s, conv_s = _conv_step(state_conv[0], glu_s, xs, conv_dw[0], row(conv_dwb[0]), *conv_args)
    x2s = tail(x1s, 0, p_sample[0].reshape(nbs, D_PLE), nbs, False)
    qs, kcs, vcs, kss, vss, kws, vws, gts = _attn_in(x2s, row(norm_mix[1]), wq, wkv, wgt, nbs)

    q8 = _pad_to(qs.reshape(nbs, N_KV, HPG, HEAD_DIM), Q_ROWS, 2)
    tbl8 = _pad_to(rel_table.T.reshape(N_KV, HPG, NUM_BUCKETS), Q_ROWS, 1)
    nc_s = past // CMP_STRIDE
    cc = jnp.arange(nc_s, dtype=jnp.int32)[None, :]
    bkt_s = _rel_bucket(past - (CMP_STRIDE * cc + 2 * CMP_STRIDE - 1))
    n_blocks_s = past // SLC_BLOCK + 1
    msel_s = jnp.asarray(_select_matrix(nc_s, -(-n_blocks_s // 128) * 128), BF16)
    new3 = lambda a: a.reshape(nbs, 1, KV_W)
    fmajor = lambda c: jnp.transpose(c[0], (0, 2, 3, 1))
    cache2 = lambda c: fmajor(c).reshape(n_pool, 2 * PAGE, PAGE)
    ocmp, idx = _cmp_sample(page_table, cache2(cache_cmp_k), cache2(cache_cmp_v), new3(kcs), new3(vcs),
                            q8, wk_s, wv_s, tbl8, bkt_s, msel_s, past)

    ww = jnp.arange(wb + NEW_LANES, dtype=jnp.int32)[None, :]
    bkt_w = _rel_bucket(wb - ww)
    gates8 = _pad_to(gts[:, :3 * N_HEADS].reshape(nbs, 3, N_KV, HPG).transpose(0, 2, 3, 1), Q_ROWS, 2)
    col4 = lambda a: a.reshape(nbs, N_KV, HEAD_DIM, 1)
    o8, wk_new, wv_new = _dec_sample(idx.reshape(nbs, N_KV * N_SELECT), page_table,
                                     fmajor(cache_slc_k), fmajor(cache_slc_v), q8,
                                     col4(kss), col4(vss), col4(kws), col4(vws),
                                     fmajor(state_win_k), fmajor(state_win_v), gates8, ocmp, tbl8, bkt_w, past)
    o_s = o8[:, :, :HPG].reshape(nbs, D_MODEL)
    y_s = tail(x2s, 1, p_sample[1].reshape(nbs, D_PLE), nbs, True, pre=(o_s, w_out))

    kv5s = lambda a: a.reshape(1, nbs, 1, N_KV, HEAD_DIM)
    win5s = lambda a: jnp.transpose(a, (0, 3, 1, 2))[None]
    return (y_p.reshape(nbp, t, D_MODEL), y_s.reshape(nbs, 1, D_MODEL),
            kv5(kc), kv5(vc), kv5(ks), kv5(vs), win5(kw), win5(vw), conv_p,
            kv5s(kcs), kv5s(vcs), kv5s(kss), kv5s(vss), win5s(wk_new), win5s(wv_new), conv_s[None])
```

```python
import functools
import math

import numpy as np
import jax
import jax.numpy as jnp
from jax import lax
from jax.experimental import pallas as pl
from jax.experimental.pallas import tpu as pltpu

F32 = jnp.float32
BF16 = jnp.bfloat16

D_MODEL = 1024
D_PLE = 256
CONV_W = 31
N_HEADS = 16
HEAD_DIM = 64
N_KV = 4
HPG = 4
KV_W = N_KV * HEAD_DIM
CMP_STRIDE = 16
CMP_HID = 128
SLC_BLOCK = 64
N_SELECT = 16
WINDOW = 512
NUM_BUCKETS = 32
MAX_DISTANCE = 128
D_FF = 4096
EPS = 1e-6
SCALE = HEAD_DIM ** -0.5
assert math.frexp(SCALE)[0] == 0.5
NEG_INF = float("-inf")

ATTN_TK = 128
assert ATTN_TK >= MAX_DISTANCE

VMEM_LIMIT = 56 * 1024 * 1024
HALO = 32


def _sigmoid(x):
    return 1.0 / (1.0 + jnp.exp(-x))


def _rms(x, g):
    return x * lax.rsqrt(jnp.mean(x * x, axis=-1, keepdims=True) + EPS) * g


def _dot(a, b):
    return jnp.dot(a, b, preferred_element_type=F32)


def _dot_lanes(a, b):
    n = b.shape[1]
    return jnp.concatenate([_dot(a, b[:, c:c + 128]) for c in range(0, n, 128)], axis=1)


def _split3(x):
    hi = x.astype(BF16)
    r1 = x - hi.astype(F32)
    mid = r1.astype(BF16)
    lo = (r1 - mid.astype(F32)).astype(BF16)
    return hi, mid, lo


def _rel_bucket(dist):
    n = jnp.maximum(dist, 0)
    max_exact = NUM_BUCKETS // 2
    nf = jnp.maximum(n, 1).astype(F32)
    large = max_exact + (jnp.log(nf / max_exact) / math.log(MAX_DISTANCE / max_exact)
                         * (NUM_BUCKETS - max_exact)).astype(jnp.int32)
    large = jnp.minimum(large, NUM_BUCKETS - 1)
    return jnp.where(n < max_exact, n, large)


def _params(sem):
    return pltpu.CompilerParams(dimension_semantics=sem, vmem_limit_bytes=VMEM_LIMIT)


def _const_spec(shape, single=False):
    n = len(shape)
    if single:
        return pl.BlockSpec(shape, lambda *_: (0,) * n, pipeline_mode=pl.Buffered(1))
    return pl.BlockSpec(shape, lambda *_: (0,) * n)


def _conv_in_body(x_ref, g_ref, w_ref, b_ref, o_ref):
    h = _rms(x_ref[...], g_ref[...]).astype(BF16)
    u = _dot(h, w_ref[...]) + b_ref[...]
    o_ref[...] = u[:, :D_MODEL] * _sigmoid(u[:, D_MODEL:])


def _conv_in(x, g, w1, b1, tm):
    n = x.shape[0]
    return pl.pallas_call(
        _conv_in_body,
        grid=(n // tm,),
        in_specs=[pl.BlockSpec((tm, D_MODEL), lambda i: (i, 0)),
                  _const_spec((1, D_MODEL)),
                  _const_spec((D_MODEL, 2 * D_MODEL)),
                  _const_spec((1, 2 * D_MODEL))],
        out_specs=pl.BlockSpec((tm, D_MODEL), lambda i: (i, 0)),
        out_shape=jax.ShapeDtypeStruct((n, D_MODEL), F32),
        compiler_params=_params(("parallel",)),
        name="conv_in",
    )(x, g, w1, b1)


def _ln_silu_proj(y, lng, lnb, w2, b2, x):
    mu = jnp.mean(y, axis=-1, keepdims=True)
    yc = y - mu
    var = jnp.mean(yc * yc, axis=-1, keepdims=True)
    z = yc * lax.rsqrt(var + EPS) * lng + lnb
    z = z * _sigmoid(z)
    return _dot(z.astype(BF16), w2) + b2 + x


def _conv_out_body(tt, cur_ref, halo_ref, x_ref, dw_ref, dwb_ref, lng_ref, lnb_ref, w2_ref, b2_ref,
                   o_ref, ctx_ref, y_ref):
    i = pl.program_id(1)
    n_strip = D_MODEL // 128
    keep = (i > 0).astype(F32)
    for c in range(n_strip):
        ctx_ref[c, 0:HALO, :] = halo_ref[0, :, 128 * c:128 * (c + 1)] * keep
        ctx_ref[c, HALO:, :] = cur_ref[0, :, 128 * c:128 * (c + 1)]

    def strip(c, carry):
        acc = jnp.broadcast_to(dwb_ref[c], (tt, 128))
        for k in range(CONV_W):
            off = k + HALO - (CONV_W - 1)
            acc = acc + ctx_ref[c, off:off + tt, :] * dw_ref[c, k:k + 1, :]
        y_ref[c] = acc
        return carry

    lax.fori_loop(0, n_strip, strip, 0)
    y = jnp.concatenate([y_ref[c] for c in range(n_strip)], axis=1)
    o_ref[0] = _ln_silu_proj(y, lng_ref[...], lnb_ref[...], w2_ref[...], b2_ref[...], x_ref[0])


def _conv_out(glu, x, dw3, dwb3, lng, lnb, w2, b2, tt):
    b, t, _ = glu.shape
    hb = tt // HALO
    return pl.pallas_call(
        functools.partial(_conv_out_body, tt),
        grid=(b, t // tt),
        in_specs=[pl.BlockSpec((1, tt, D_MODEL), lambda bi, i: (bi, i, 0)),
                  pl.BlockSpec((1, HALO, D_MODEL), lambda bi, i: (bi, jnp.maximum(i * hb - 1, 0), 0)),
                  pl.BlockSpec((1, tt, D_MODEL), lambda bi, i: (bi, i, 0)),
                  _const_spec((D_MODEL // 128, CONV_W, 128)),
                  _const_spec((D_MODEL // 128, 1, 128)),
                  _const_spec((1, D_MODEL)),
                  _const_spec((1, D_MODEL)),
                  _const_spec((D_MODEL, D_MODEL)),
                  _const_spec((1, D_MODEL))],
        out_specs=pl.BlockSpec((1, tt, D_MODEL), lambda bi, i: (bi, i, 0)),
        out_shape=jax.ShapeDtypeStruct((b, t, D_MODEL), F32),
        scratch_shapes=[pltpu.VMEM((D_MODEL // 128, tt + HALO, 128), F32),
                        pltpu.VMEM((D_MODEL // 128, tt, 128), F32)],
        compiler_params=_params(("parallel", "arbitrary")),
        name="conv_out",
    )(glu, glu, x, dw3, dwb3, lng, lnb, w2, b2)


def _conv_step_body(st_ref, u_ref, x_ref, dw_ref, dwb_ref, lng_ref, lnb_ref, w2_ref, b2_ref,
                    o_ref, ns_ref):
    nb = st_ref.shape[0]
    hist = CONV_W - 1
    dwh = dw_ref[0:hist, :]
    rows = []
    for bi in range(nb):
        rows.append(jnp.sum(st_ref[bi] * dwh, axis=0, keepdims=True))
        ns_ref[bi, 0:hist - 1, :] = st_ref[bi, 1:hist, :]
        ns_ref[bi, hist - 1:hist, :] = u_ref[bi:bi + 1, :]
    y = jnp.concatenate(rows, axis=0) + u_ref[...] * dw_ref[hist:hist + 1, :] + dwb_ref[...]
    o_ref[...] = _ln_silu_proj(y, lng_ref[...], lnb_ref[...], w2_ref[...], b2_ref[...], x_ref[...])


def _conv_step(state, glu, x, dw, dwb, lng, lnb, w2, b2):
    nb, hist, _ = state.shape
    return pl.pallas_call(
        _conv_step_body,
        grid=(1,),
        in_specs=[_const_spec((nb, hist, D_MODEL)), _const_spec((nb, D_MODEL)), _const_spec((nb, D_MODEL)),
                  _const_spec((CONV_W, D_MODEL)), _const_spec((1, D_MODEL)), _const_spec((1, D_MODEL)),
                  _const_spec((1, D_MODEL)), _const_spec((D_MODEL, D_MODEL)), _const_spec((1, D_MODEL))],
        out_specs=[_const_spec((nb, D_MODEL)), _const_spec((nb, hist, D_MODEL))],
        out_shape=[jax.ShapeDtypeStruct((nb, D_MODEL), F32),
                   jax.ShapeDtypeStruct((nb, hist, D_MODEL), F32)],
        compiler_params=_params(("arbitrary",)),
        name="conv_step",
    )(state, glu, x, dw, dwb, lng, lnb, w2, b2)


def _mlp_body(final, pre, *refs):
    if pre:
        o_in_ref, wo_ref, refs = refs[0], refs[1], refs[2:]
    (x_ref, gf_ref, up_ref, dn_ref, gp_ref, wg_ref, p_ref, wp_ref, gfin_ref,
     o_ref, h_ref, acc_ref, x1_ref) = refs
    j = pl.program_id(1)

    @pl.when(j == 0)
    def _():
        x1 = x_ref[...]
        if pre:
            x1 = x1 + _dot(o_in_ref[...].astype(BF16), wo_ref[...])
        x1_ref[...] = x1
        h_ref[...] = _rms(x1, gf_ref[...]).astype(BF16)
        acc_ref[...] = jnp.zeros_like(acc_ref)

    a = jnp.maximum(_dot(h_ref[...], up_ref[...]), 0.0)
    acc_ref[...] += _dot((a * a).astype(BF16), dn_ref[...])

    @pl.when(j == pl.num_programs(1) - 1)
    def _():
        x2 = x1_ref[...] + acc_ref[...]
        gate = _sigmoid(_dot(_rms(x2, gp_ref[...]).astype(BF16), wg_ref[...]))
        x3 = x2 + gate * _dot(p_ref[...].astype(BF16), wp_ref[...])
        if final:
            x3 = _rms(x3, gfin_ref[...])
        o_ref[...] = x3


def _mlp_ple(x, gf, up, dn, gp, wg, p, wp, gfin, tm, tf, final, pre=None):
    n = x.shape[0]
    tok = lambda i, j: (i, 0)
    in_specs, args = [], []
    if pre is not None:
        in_specs += [pl.BlockSpec((tm, D_MODEL), tok), _const_spec((D_MODEL, D_MODEL))]
        args += list(pre)
    in_specs += [pl.BlockSpec((tm, D_MODEL), tok),
                 _const_spec((1, D_MODEL)),
                 pl.BlockSpec((D_MODEL, tf), lambda i, j: (0, j)),
                 pl.BlockSpec((tf, D_MODEL), lambda i, j: (j, 0)),
                 _const_spec((1, D_MODEL)),
                 _const_spec((D_MODEL, D_MODEL)),
                 pl.BlockSpec((tm, D_PLE), tok),
                 _const_spec((D_PLE, D_MODEL)),
                 _const_spec((1, D_MODEL))]
    args += [x, gf, up, dn, gp, wg, p, wp, gfin]
    return pl.pallas_call(
        functools.partial(_mlp_body, final, pre is not None),
        grid=(n // tm, D_FF // tf),
        in_specs=in_specs,
        out_specs=pl.BlockSpec((tm, D_MODEL), tok),
        out_shape=jax.ShapeDtypeStruct((n, D_MODEL), F32),
        scratch_shapes=[pltpu.VMEM((tm, D_MODEL), BF16), pltpu.VMEM((tm, D_MODEL), F32),
                        pltpu.VMEM((tm, D_MODEL), F32)],
        compiler_params=_params(("parallel", "arbitrary")),
        name="mlp_ple",
    )(*args)


def _attn_in_body(x_ref, g_ref, wq_ref, wkv_ref, wg_ref, q_ref, kc_ref, vc_ref, ks_ref, vs_ref,
                  kw_ref, vw_ref, gt_ref):
    h = _rms(x_ref[...], g_ref[...]).astype(BF16)
    q_ref[...] = _dot(h, wq_ref[...])
    kv = _dot(h, wkv_ref[...])
    for i, r in enumerate((kc_ref, vc_ref, ks_ref, vs_ref, kw_ref, vw_ref)):
        r[...] = kv[:, KV_W * i:KV_W * (i + 1)]
    gt_ref[...] = _sigmoid(_dot(h, wg_ref[...]))


def _attn_in(x, g, wq, wkv, wg, tm):
    n = x.shape[0]
    tok = lambda i: (i, 0)
    kv_spec = pl.BlockSpec((tm, KV_W), tok)
    kv_shape = jax.ShapeDtypeStruct((n, KV_W), F32)
    return pl.pallas_call(
        _attn_in_body,
        grid=(n // tm,),
        in_specs=[pl.BlockSpec((tm, D_MODEL), tok), _const_spec((1, D_MODEL)),
                  _const_spec((D_MODEL, D_MODEL)), _const_spec((D_MODEL, 6 * KV_W)),
                  _const_spec((D_MODEL, 128))],
        out_specs=[pl.BlockSpec((tm, D_MODEL), tok)] + [kv_spec] * 6 + [pl.BlockSpec((tm, 128), tok)],
        out_shape=[jax.ShapeDtypeStruct((n, D_MODEL), F32)] + [kv_shape] * 6
                  + [jax.ShapeDtypeStruct((n, 128), F32)],
        compiler_params=_params(("parallel",)),
        name="attn_in",
    )(x, g, wq, wkv, wg)


def _compress_hidden(load_rows, n_chunks, wexp_ref):
    acc = jnp.zeros((2 * n_chunks, 4 * CMP_HID), F32)
    for r in range(0, CMP_STRIDE, 2):
        lo = jnp.concatenate([load_rows(2 * r), load_rows(2 * r + 2)], axis=1)
        hi = jnp.concatenate([load_rows(2 * r + 1), load_rows(2 * r + 3)], axis=1)
        x = jnp.concatenate([lo, hi], axis=0)
        acc = acc + _dot(x.astype(BF16), wexp_ref[r // 2])
    lo, hi = acc[:n_chunks], acc[n_chunks:]
    w = 2 * CMP_HID
    return jnp.concatenate([lo[:, :w], hi[:, :w], lo[:, w:], hi[:, w:]], axis=1)


def _pe_term(pe_ref, w1_ref):
    t = _dot(pe_ref[...], w1_ref[...])[0:1, :]
    return jnp.concatenate([t] * N_KV, axis=1)


def _compress_p_body(nc, kc_ref, vc_ref, wek_ref, pek_ref, w1k_ref, w2k_ref,
                     wev_ref, pev_ref, w1v_ref, w2v_ref, ok_ref, ov_ref):
    half = N_KV * CMP_HID
    row = lax.broadcasted_iota(jnp.int32, (nc, KV_W), 0)
    for src, we, pe, w1, w2, out in ((kc_ref, wek_ref, pek_ref, w1k_ref, w2k_ref, ok_ref),
                                     (vc_ref, wev_ref, pev_ref, w1v_ref, w2v_ref, ov_ref)):
        hh = _compress_hidden(lambda k: src[0, pl.ds(k, nc, stride=2 * CMP_STRIDE), :], nc, we)
        nxt = pltpu.roll(hh[:, half:], nc - 1, 0)
        a = hh[:, :half] + nxt + _pe_term(pe, w1)
        a = a * _sigmoid(a)
        res = _dot(a.astype(BF16), w2[...])
        out[0] = jnp.where(row < nc - 1, res, 0.0)


def _compress_prompt(kc, vc, wk, wv):
    b, t2, _ = kc.shape
    nc = t2 // (2 * CMP_STRIDE)
    seq = pl.BlockSpec((1, t2, 128), lambda i: (i, 0, 0))
    wspecs = [_const_spec((CMP_STRIDE // 2, 256, 4 * CMP_HID)),
              _const_spec((8, 2 * CMP_STRIDE * HEAD_DIM)),
              _const_spec((2 * CMP_STRIDE * HEAD_DIM, CMP_HID)), _const_spec((N_KV * CMP_HID, KV_W))]
    out = pl.BlockSpec((1, nc, KV_W), lambda i: (i, 0, 0))
    return pl.pallas_call(
        functools.partial(_compress_p_body, nc),
        grid=(b,),
        in_specs=[seq, seq] + wspecs + wspecs,
        out_specs=[out, out],
        out_shape=[jax.ShapeDtypeStruct((b, nc, KV_W), F32)] * 2,
        compiler_params=_params(("parallel",)),
        name="compress_prompt",
    )(kc, vc, *wk, *wv)


def _softmax_cols(s, mask):
    s = jnp.where(mask, s, NEG_INF)
    m = jnp.max(s, axis=0, keepdims=True)
    m = jnp.where(m == NEG_INF, 0.0, m)
    e = jnp.where(mask, jnp.exp(s - m), 0.0)
    return e / jnp.maximum(jnp.sum(e, axis=0, keepdims=True), 1e-30)


def _attn_p_body(tq, t, q_ref, gt_ref, ks_ref, vs_ref, kw_ref, vw_ref, kcm_ref, vcm_ref,
                 tbl_ref, bt_ref, bc_ref, msel_ref, o_ref,
                 ksb, kwb, vst, vwt, kcb, vct, btoep, bcmp, qt, selt, ot, m_ref, l_ref, acc_ref, gt_s):
    b = pl.program_id(0)
    qi = pl.program_id(1)
    tk = ATTN_TK
    nkt = t // tk
    nqt = t // tq
    r_qk = tq // tk
    ncmp = kcm_ref.shape[1]
    q0 = qi * tq
    n_near = bt_ref.shape[0]

    @pl.when((b == 0) & (qi == 0))
    def _():
        def per_head(h, carry):
            far = tbl_ref[NUM_BUCKETS - 1, h]
            for d in range(n_near):
                bk = bt_ref[d]
                acc = jnp.zeros((tk, tq), F32)
                for bb in range(NUM_BUCKETS):
                    acc = jnp.where(bk == bb, tbl_ref[bb, h] - far, acc)
                btoep[h, d] = acc
            btoep[h, n_near] = jnp.zeros((tk, tq), F32)
            bk = bc_ref[...]
            acc = jnp.zeros(bk.shape, F32)
            for bb in range(NUM_BUCKETS):
                acc = jnp.where(bk == bb, tbl_ref[bb, h], acc)
            bcmp[h] = acc
            return carry
        lax.fori_loop(0, N_HEADS, per_head, 0)

    @pl.when(qi == 0)
    def _():
        for kt in range(nkt):
            rows = slice(kt * tk, (kt + 1) * tk)
            for src, dstk in ((ks_ref, ksb), (kw_ref, kwb)):
                blk = src[0, rows, :]
                for g in range(N_KV):
                    dstk[g, rows, :] = blk[:, HEAD_DIM * g:HEAD_DIM * (g + 1)].astype(BF16)
            for src, dstv in ((vs_ref, vst), (vw_ref, vwt)):
                blk_t = src[0, rows, :].T
                for g in range(N_KV):
                    dstv[g, kt] = blk_t[HEAD_DIM * g:HEAD_DIM * (g + 1), :].astype(BF16)
        kc = kcm_ref[0]
        vc_t = vcm_ref[0].T
        for g in range(N_KV):
            kcb[g] = kc[:, HEAD_DIM * g:HEAD_DIM * (g + 1)].astype(BF16)
            vct[g] = vc_t[HEAD_DIM * g:HEAD_DIM * (g + 1), :].astype(BF16)

    qt[...] = (q_ref[0].T * SCALE).astype(BF16)
    gt_s[...] = gt_ref[0].T

    qpos_row = q0 + lax.broadcasted_iota(jnp.int32, (1, tq), 1)
    i_iota = lax.broadcasted_iota(jnp.int32, (tk, tq), 1)
    j_iota = lax.broadcasted_iota(jnp.int32, (tk, tq), 0)
    rel = i_iota - j_iota

    c_iota = lax.broadcasted_iota(jnp.int32, (ncmp, tq), 0)
    cmask = (CMP_STRIDE * c_iota + (2 * CMP_STRIDE - 1)) <= qpos_row
    coff = pl.multiple_of((nqt - 1 - qi) * (tq // CMP_STRIDE), 8)

    nblk = t // SLC_BLOCK
    blk_iota = lax.broadcasted_iota(jnp.int32, (nblk, tq), 0)
    cur = qpos_row // SLC_BLOCK
    valid = blk_iota * SLC_BLOCK <= qpos_row
    forced = (blk_iota == 0) | (blk_iota == cur) | (blk_iota == cur - 1)

    for g in range(N_KV):
        pg = jnp.zeros((ncmp, tq), F32)
        for hh in range(HPG):
            h = g * HPG + hh
            qh = qt[HEAD_DIM * h:HEAD_DIM * (h + 1), :]
            s = _dot(kcb[g], qh) + bcmp[h, pl.ds(coff, ncmp), :]
            p = _softmax_cols(s, cmask)
            pg = pg + p
            oc = _dot(vct[g], p.astype(BF16))
            ot[HEAD_DIM * h:HEAD_DIM * (h + 1), :] = gt_s[h:h + 1, :] * oc
        hi, mid, lo = _split3(pg)
        msel = msel_ref[...]
        score = _dot(msel, hi) + _dot(msel, mid) + _dot(msel, lo)
        score = jnp.where(forced, jnp.inf, jnp.where(valid, score, NEG_INF))
        rank = jnp.zeros((nblk, tq), jnp.int32)
        for i in range(nblk):
            si = score[i:i + 1, :]
            beats = (si > score) | ((blk_iota > i) & (si == score))
            rank = rank + beats.astype(jnp.int32)
        sel = (rank < N_SELECT).astype(F32)
        for i in range(nblk):
            selt[g, i] = sel[i:i + 1, :]

    def flash_init():
        m_ref[...] = jnp.full(m_ref.shape, NEG_INF, F32)
        l_ref[...] = jnp.zeros(l_ref.shape, F32)
        acc_ref[...] = jnp.zeros(acc_ref.shape, F32)

    def tile_step(kb, vt, kt, bias_d, add_fn):
        k0 = pl.multiple_of(kt * tk, tk)
        hk = tk // 2
        hd = HEAD_DIM // 2
        for g in range(N_KV):
            kblk = kb[g, pl.ds(k0, tk), :]
            vblk = vt[g, kt]
            add = add_fn(g)
            for hh in range(HPG):
                h = g * HPG + hh
                rows = slice(HEAD_DIM * h, HEAD_DIM * (h + 1))
                qh = qt[rows, :]
                s = _dot(kblk, qh)
                if bias_d is not None:
                    s = s + btoep[h, bias_d]
                if add is not None:
                    s = s + add
                m_old = m_ref[h]
                m_new = jnp.maximum(m_old, jnp.max(s, axis=0, keepdims=True))
                m_safe = jnp.where(m_new == NEG_INF, 0.0, m_new)
                alpha = jnp.exp(m_old - m_safe)
                p = jnp.exp(s - m_safe)
                pb = p.astype(BF16)
                l_ref[h] = alpha * l_ref[h] + jnp.sum(p, axis=0, keepdims=True)
                pv = _dot(vblk, pb)
                acc_ref[rows, :] = alpha * acc_ref[rows, :] + pv
                m_ref[h] = m_new

    def flash_finish(gate_row0):
        for h in range(N_HEADS):
            rows = slice(HEAD_DIM * h, HEAD_DIM * (h + 1))
            o = acc_ref[rows, :] / jnp.maximum(l_ref[h], 1e-30)
            ot[rows, :] = ot[rows, :] + gt_s[gate_row0 + h:gate_row0 + h + 1, :] * o

    def sel_rows(g, kt, ok=None):
        nb = tk // SLC_BLOCK
        rows = [jnp.broadcast_to(selt[g, kt * nb + r], (SLC_BLOCK, tq)) for r in range(nb)]
        thr = 0.5 if ok is None else jnp.where(ok, 0.5, 2.0)
        return jnp.concatenate(rows, axis=0) > thr

    def neg_unless(cond):
        return jnp.where(cond, 0.0, NEG_INF)

    causal = rel >= 0

    flash_init()
    n_far = jnp.maximum(qi - 1, 0)

    def slc_pair(pi, carry):
        for sub in range(2):
            kt = 2 * pi + sub
            ok = kt < n_far
            ktc = jnp.minimum(kt, jnp.maximum(n_far - 1, 0))
            tile_step(ksb, vst, ktc, None, lambda g: neg_unless(sel_rows(g, ktc, ok)))
        return carry

    lax.fori_loop(0, (n_far + 1) // 2, slc_pair, 0)
    kt1 = jnp.maximum(qi - 1, 0)
    tile_step(ksb, vst, kt1, 1, lambda g: neg_unless(sel_rows(g, kt1, qi >= 1)))
    tile_step(ksb, vst, qi, 0, lambda g: neg_unless(sel_rows(g, qi) & causal))
    flash_finish(N_HEADS)

    flash_init()
    wt = WINDOW // tk
    for u in range(wt + 1):
        off = wt - u
        kt = qi - off
        ok_add = jnp.where(kt >= 0, 0.0, NEG_INF)
        ktc = jnp.maximum(kt, 0)
        if off == wt:
            add_fn = lambda g, a=ok_add: neg_unless(rel <= 0) + a
        elif off == 0:
            add_fn = lambda g: neg_unless(causal)
        else:
            add_fn = lambda g, a=ok_add: a
        tile_step(kwb, vwt, ktc, off if off <= 1 else None, add_fn)
    flash_finish(2 * N_HEADS)

    o_ref[0] = ot[...].T


def _attn_prompt(q, gates, ks, vs, kw, vw, kcmp, vcmp, table, bkt_toep, bkt_cmp, msel, tq):
    b, t, _ = q.shape
    tk = ATTN_TK
    nkt = t // tk
    ncmp = kcmp.shape[1]
    n_near = bkt_toep.shape[0]
    qspec = pl.BlockSpec((1, tq, D_MODEL), lambda bi, i: (bi, i, 0))
    seq = pl.BlockSpec((1, t, KV_W), lambda bi, i: (bi, 0, 0))
    cmp_spec = pl.BlockSpec((1, ncmp, KV_W), lambda bi, i: (bi, 0, 0))
    return pl.pallas_call(
        functools.partial(_attn_p_body, tq, t),
        grid=(b, t // tq),
        in_specs=[qspec, pl.BlockSpec((1, tq, 128), lambda bi, i: (bi, i, 0)),
                  seq, seq, seq, seq, cmp_spec, cmp_spec,
                  pl.BlockSpec(memory_space=pltpu.SMEM),
                  _const_spec(bkt_toep.shape), _const_spec(bkt_cmp.shape), _const_spec(msel.shape)],
        out_specs=qspec,
        out_shape=jax.ShapeDtypeStruct((b, t, D_MODEL), F32),
        scratch_shapes=[pltpu.VMEM((N_KV, t, HEAD_DIM), BF16), pltpu.VMEM((N_KV, t, HEAD_DIM), BF16),
                        pltpu.VMEM((N_KV, nkt, HEAD_DIM, tk), BF16), pltpu.VMEM((N_KV, nkt, HEAD_DIM, tk), BF16),
                        pltpu.VMEM((N_KV, ncmp, HEAD_DIM), BF16), pltpu.VMEM((N_KV, HEAD_DIM, ncmp), BF16),
                        pltpu.VMEM((N_HEADS, n_near + 1, tk, tq), F32),
                        pltpu.VMEM((N_HEADS,) + bkt_cmp.shape, F32),
                        pltpu.VMEM((D_MODEL, tq), BF16), pltpu.VMEM((N_KV, t // SLC_BLOCK, 1, tq), F32),
                        pltpu.VMEM((D_MODEL, tq), F32),
                        pltpu.VMEM((N_HEADS, 1, tq), F32), pltpu.VMEM((N_HEADS, 1, tq), F32),
                        pltpu.VMEM((D_MODEL, tq), F32), pltpu.VMEM((128, tq), F32)],
        compiler_params=_params(("arbitrary", "arbitrary")),
        name="attn_prompt",
    )(q, gates, ks, vs, kw, vw, kcmp, vcmp, table, bkt_toep, bkt_cmp, msel)


PAGE = 128
PAGES_PER_GROUP = 16
Q_ROWS = 8


def _rank_desc(score_row, n):
    a = jnp.broadcast_to(score_row, (n, n))
    at = a.T
    i = lax.broadcasted_iota(jnp.int32, (n, n), 0)
    j = lax.broadcasted_iota(jnp.int32, (n, n), 1)
    beats = (at > a) | ((i < j) & (at == a))
    return jnp.sum(beats.astype(F32), axis=0, keepdims=True)


def _bias_rows(bucket, tcols):
    r = tcols.shape[0]
    out = jnp.zeros((r, bucket.shape[1]), F32)
    for bb in range(NUM_BUCKETS):
        out = jnp.where(bucket == bb, tcols[:, bb:bb + 1], out)
    return out


def _softmax_rows(s, mask):
    s = jnp.where(mask, s, NEG_INF)
    m = jnp.max(s, axis=1, keepdims=True)
    m = jnp.where(m == NEG_INF, 0.0, m)
    e = jnp.where(mask, jnp.exp(s - m), 0.0)
    return e / jnp.maximum(jnp.sum(e, axis=1, keepdims=True), 1e-30)


def _dot_nt(a, b):
    return lax.dot_general(a, b, (((1,), (1,)), ((), ())), preferred_element_type=F32)


def _cmp_s_body(past, pt_ref, ck_hbm, cv_hbm, kcn_ref, vcn_ref, q_ref,
                wek_ref, pek_ref, w1k_ref, w2k_ref, wev_ref, pev_ref, w1v_ref, w2v_ref,
                tbl_ref, bkt_ref, msel_ref, ocmp_ref, idx_ref,
                kbuf, vbuf, tbuf, hk, hv, sem):
    b = pl.program_id(0)
    rows_pg = PAGES_PER_GROUP * PAGE
    cpg = rows_pg // CMP_STRIDE
    ngrp = past // rows_pg
    nc = past // CMP_STRIDE
    half = N_KV * CMP_HID

    def copies(gi, slot):
        out = []
        for p in range(PAGES_PER_GROUP):
            page = pt_ref[b, gi * PAGES_PER_GROUP + p]
            out.append(pltpu.make_async_copy(ck_hbm.at[page], kbuf.at[slot, p], sem.at[0, slot]))
            out.append(pltpu.make_async_copy(cv_hbm.at[page], vbuf.at[slot, p], sem.at[1, slot]))
        return out

    for c in copies(0, 0):
        c.start()

    def group(gi, carry):
        slot = gi % 2

        @pl.when(gi + 1 < ngrp)
        def _():
            for c in copies(gi + 1, 1 - slot):
                c.start()

        for c in copies(gi, slot):
            c.wait()
        r0 = pl.multiple_of(gi * cpg, cpg)
        for buf, we, hdst in ((kbuf, wek_ref, hk), (vbuf, wev_ref, hv)):
            for p in range(PAGES_PER_GROUP):
                for hf in range(2):
                    tbuf[hf, p * PAGE:(p + 1) * PAGE, :] = buf[slot, p, hf * 128:(hf + 1) * 128, :].T
            hdst[pl.ds(r0, cpg), :] = _compress_hidden(
                lambda k: tbuf[k % 2, pl.ds(k // 2, cpg, stride=CMP_STRIDE), :], cpg, we)
        return carry

    lax.fori_loop(0, ngrp, group, 0)

    col = lax.broadcasted_iota(jnp.int32, (1, nc), 1)
    cmask = (CMP_STRIDE * col + (2 * CMP_STRIDE - 1)) <= past
    row8 = lax.broadcasted_iota(jnp.int32, (Q_ROWS, nc), 0)
    cmpd = []
    for new_ref, we, pe, w1, w2, hdst in ((kcn_ref, wek_ref, pek_ref, w1k_ref, w2k_ref, hk),
                                          (vcn_ref, wev_ref, pev_ref, w1v_ref, w2v_ref, hv)):
        new = jnp.broadcast_to(new_ref[0], (8, KV_W))
        pad = jnp.zeros((8, 128), F32)
        new_lo = _dot(jnp.concatenate([new[:, :128], pad], axis=1).astype(BF16), we[0])
        new_hi = _dot(jnp.concatenate([new[:, 128:], pad], axis=1).astype(BF16), we[0])
        w = 2 * CMP_HID
        hdst[nc:nc + 8, :] = jnp.concatenate([new_lo[:, :w], new_hi[:, :w], new_lo[:, w:], new_hi[:, w:]], axis=1)
        a = hdst[0:nc, 0:half] + hdst[1:nc + 1, half:2 * half] + _pe_term(pe, w1)
        a = (a * _sigmoid(a)).astype(BF16)
        cmpd.append([_dot(a[:, CMP_HID * g:CMP_HID * (g + 1)], w2[...]).astype(BF16) for g in range(N_KV)])

    nsel = msel_ref.shape[1]
    lane = lax.broadcasted_iota(jnp.int32, (1, nsel), 1)
    n_blocks = past // SLC_BLOCK + 1
    cur = past // SLC_BLOCK
    forced = (lane == 0) | (lane == cur) | (lane == cur - 1)
    valid = (lane * SLC_BLOCK <= past) & (lane < n_blocks)
    k_iota = lax.broadcasted_iota(jnp.int32, (N_SELECT, nsel), 0).astype(F32)
    lane_f = lax.broadcasted_iota(jnp.int32, (N_SELECT, nsel), 1).astype(F32)
    for g in range(N_KV):
        qg = q_ref[0, g].astype(BF16)
        bias = _bias_rows(bkt_ref[...], tbl_ref[g])
        s = _dot_nt(qg, cmpd[0][g]) * SCALE + bias
        p = _softmax_rows(s, cmask)
        ocmp_ref[0, g] = _dot(p.astype(BF16), cmpd[1][g])
        pg = jnp.sum(jnp.where(row8 < HPG, p, 0.0), axis=0, keepdims=True)
        hi, mid, lo = _split3(jnp.broadcast_to(pg, (8, nc)))
        msel = msel_ref[...]
        score = (_dot(hi, msel) + _dot(mid, msel) + _dot(lo, msel))[0:1, :]
        score = jnp.where(forced, jnp.inf, jnp.where(valid, score, NEG_INF))
        rank = _rank_desc(score, nsel)
        hit = jnp.where(jnp.broadcast_to(rank, (N_SELECT, nsel)) == k_iota, lane_f, 0.0)
        idx_ref[0, g] = jnp.sum(hit, axis=1, keepdims=True).astype(jnp.int32)


def _cmp_sample(page_table, cache_k, cache_v, kc_new, vc_new, q8, wk, wv, tbl8, bkt, msel, past):
    nb = page_table.shape[0]
    rows_pg = PAGES_PER_GROUP * PAGE
    nc = past // CMP_STRIDE
    any_spec = pl.BlockSpec(memory_space=pl.ANY)
    new_spec = pl.BlockSpec((1, 1, KV_W), lambda i, pt: (i, 0, 0))
    cs = lambda shape: pl.BlockSpec(shape, lambda i, pt: (0,) * len(shape))
    wspecs = [cs((CMP_STRIDE // 2, 256, 4 * CMP_HID)),
              cs((8, 2 * CMP_STRIDE * HEAD_DIM)),
              cs((2 * CMP_STRIDE * HEAD_DIM, CMP_HID)), cs((CMP_HID, HEAD_DIM))]
    grid_spec = pltpu.PrefetchScalarGridSpec(
        num_scalar_prefetch=1,
        grid=(nb,),
        in_specs=[any_spec, any_spec, new_spec, new_spec,
                  pl.BlockSpec((1, N_KV, Q_ROWS, HEAD_DIM), lambda i, pt: (i, 0, 0, 0))]
                 + wspecs + wspecs + [cs(tbl8.shape), cs(bkt.shape), cs(msel.shape)],
        out_specs=[pl.BlockSpec((1, N_KV, Q_ROWS, HEAD_DIM), lambda i, pt: (i, 0, 0, 0)),
                   pl.BlockSpec((1, N_KV, N_SELECT, 1), lambda i, pt: (i, 0, 0, 0))],
        scratch_shapes=[pltpu.VMEM((2, PAGES_PER_GROUP, 2 * PAGE, 128), F32),
                        pltpu.VMEM((2, PAGES_PER_GROUP, 2 * PAGE, 128), F32),
                        pltpu.VMEM((2, rows_pg, 128), F32),
                        pltpu.VMEM((nc + 8, 2 * N_KV * CMP_HID), F32),
                        pltpu.VMEM((nc + 8, 2 * N_KV * CMP_HID), F32),
                        pltpu.SemaphoreType.DMA((2, 2))])
    return pl.pallas_call(
        functools.partial(_cmp_s_body, past),
        grid_spec=grid_spec,
        out_shape=[jax.ShapeDtypeStruct((nb, N_KV, Q_ROWS, HEAD_DIM), F32),
                   jax.ShapeDtypeStruct((nb, N_KV, N_SELECT, 1), jnp.int32)],
        compiler_params=_params(("arbitrary",)),
        name="cmp_sample",
    )(page_table, cache_k, cache_v, kc_new, vc_new, q8, *wk, *wv, tbl8, bkt, msel)


NEW_LANES = 128


def _attend_kt(qg, kt, vt, bias, mask):
    s = _dot(qg, kt.astype(BF16)) * SCALE + bias
    p = _softmax_rows(s, mask)
    return _dot_nt(p.astype(BF16), vt.astype(BF16))


def _dec_s_body(past, idx_ref, pt_ref, csk_hbm, csv_hbm,
                q_ref, ksn_ref, vsn_ref, kwn_ref, vwn_ref, wk_ref, wv_ref, gt_ref, ocmp_ref,
                tbl_ref, bktw_ref, o_ref, wko_ref, wvo_ref,
                kbuf, vbuf, kwbuf, vwbuf, sem):
    b = pl.program_id(0)
    n_pages = past // PAGE
    cur = past // SLC_BLOCK
    bpp = PAGE // SLC_BLOCK
    lsel = N_SELECT * PAGE
    wb = wk_ref.shape[3]

    @pl.when(b == 0)
    def _():
        kbuf[...] = jnp.zeros_like(kbuf)
        vbuf[...] = jnp.zeros_like(vbuf)
        kwbuf[...] = jnp.zeros_like(kwbuf)
        vwbuf[...] = jnp.zeros_like(vwbuf)

    copies = []
    for g in range(N_KV):
        for k in range(N_SELECT):
            blk = idx_ref[b, g * N_SELECT + k]
            page = pt_ref[b, jnp.minimum(blk // bpp, n_pages - 1)]
            dst = pl.ds(k * PAGE, PAGE)
            copies.append(pltpu.make_async_copy(csk_hbm.at[page, g], kbuf.at[g, :, dst], sem.at[0]))
            copies.append(pltpu.make_async_copy(csv_hbm.at[page, g], vbuf.at[g, :, dst], sem.at[1]))
    for c in copies:
        c.start()

    lane_wb = lax.broadcasted_iota(jnp.int32, (HEAD_DIM, wb), 1)
    for g in range(N_KV):
        for src, new_ref, dst, buf in ((wk_ref, kwn_ref, wko_ref, kwbuf), (wv_ref, vwn_ref, wvo_ref, vwbuf)):
            st = src[0, g]
            newc = new_ref[0, g]
            dst[0, g] = jnp.where(lane_wb == wb - 1, newc, pltpu.roll(st, wb - 1, 1))
            buf[g, :, 0:wb] = st
            buf[g, :, wb:wb + 1] = newc

    for c in copies:
        c.wait()

    ls = lsel + NEW_LANES
    lw = wb + NEW_LANES
    lane_s = lax.broadcasted_iota(jnp.int32, (1, ls), 1)
    lane_w = lax.broadcasted_iota(jnp.int32, (1, lw), 1)
    wmask = lane_w <= wb
    tok = lane_s % PAGE
    for g in range(N_KV):
        kbuf[g, :, lsel:lsel + 1] = ksn_ref[0, g]
        vbuf[g, :, lsel:lsel + 1] = vsn_ref[0, g]
        qg = q_ref[0, g].astype(BF16)
        tcols = tbl_ref[g]

        kpos = past + (lane_s - lsel)
        gathered = lane_s < 0
        has_new = jnp.int32(0)
        for k in range(N_SELECT):
            blk = idx_ref[b, g * N_SELECT + k]
            in_slot = (lane_s // PAGE) == k
            kpos = jnp.where(in_slot, (blk // bpp) * PAGE + tok, kpos)
            gathered = gathered | (in_slot & (blk < cur) & ((tok // SLC_BLOCK) == (blk % bpp)))
            has_new = has_new | (blk == cur).astype(jnp.int32)
        smask = (gathered | ((lane_s == lsel) & (has_new > 0))) & (kpos <= past)
        sbias = _bias_rows(_rel_bucket(past - kpos), tcols)
        o_slc = _attend_kt(qg, kbuf[g], vbuf[g], sbias, smask)

        wbias = _bias_rows(bktw_ref[...], tcols)
        o_win = _attend_kt(qg, kwbuf[g], vwbuf[g], wbias, wmask)

        gt = gt_ref[0, g]
        o_ref[0, g] = gt[:, 0:1] * ocmp_ref[0, g] + gt[:, 1:2] * o_slc + gt[:, 2:3] * o_win


def _dec_sample(idx, page_table, cache_sk, cache_sv, q8, ks_new, vs_new, kw_new, vw_new, state_wk, state_wv,
                gates8, ocmp, tbl8, bktw, past):
    nb, wb = state_wk.shape[0], state_wk.shape[3]
    any_spec = pl.BlockSpec(memory_space=pl.ANY)
    new_spec = pl.BlockSpec((1, N_KV, HEAD_DIM, 1), lambda i, *_: (i, 0, 0, 0))
    st_spec = pl.BlockSpec((1, N_KV, HEAD_DIM, wb), lambda i, *_: (i, 0, 0, 0))
    head_spec = pl.BlockSpec((1, N_KV, Q_ROWS, HEAD_DIM), lambda i, *_: (i, 0, 0, 0))
    cs = lambda shape: pl.BlockSpec(shape, lambda i, *_: (0,) * len(shape))
    lsel = N_SELECT * PAGE
    grid_spec = pltpu.PrefetchScalarGridSpec(
        num_scalar_prefetch=2,
        grid=(nb,),
        in_specs=[any_spec, any_spec, head_spec,
                  new_spec, new_spec, new_spec, new_spec, st_spec, st_spec,
                  pl.BlockSpec((1, N_KV, Q_ROWS, 3), lambda i, *_: (i, 0, 0, 0)), head_spec,
                  cs(tbl8.shape), cs(bktw.shape)],
        out_specs=[head_spec, st_spec, st_spec],
        scratch_shapes=[pltpu.VMEM((N_KV, HEAD_DIM, lsel + NEW_LANES), F32),
                        pltpu.VMEM((N_KV, HEAD_DIM, lsel + NEW_LANES), F32),
                        pltpu.VMEM((N_KV, HEAD_DIM, wb + NEW_LANES), F32),
                        pltpu.VMEM((N_KV, HEAD_DIM, wb + NEW_LANES), F32),
                        pltpu.SemaphoreType.DMA((2,))])
    return pl.pallas_call(
        functools.partial(_dec_s_body, past),
        grid_spec=grid_spec,
        out_shape=[jax.ShapeDtypeStruct((nb, N_KV, Q_ROWS, HEAD_DIM), F32),
                   jax.ShapeDtypeStruct(state_wk.shape, F32),
                   jax.ShapeDtypeStruct(state_wv.shape, F32)],
        compiler_params=_params(("arbitrary",)),
        name="dec_sample",
    )(idx, page_table, cache_sk, cache_sv, q8, ks_new, vs_new, kw_new, vw_new,
      state_wk, state_wv, gates8, ocmp, tbl8, bktw)


def _expand_w1(w1):
    eye = jnp.eye(2, dtype=w1.dtype)
    parts = []
    for part in (w1[:CMP_STRIDE * HEAD_DIM], w1[CMP_STRIDE * HEAD_DIM:]):
        w = part.reshape(CMP_STRIDE, HEAD_DIM, CMP_HID)
        parts.append(jnp.einsum("gh,rdn->rgdhn", eye, w).reshape(CMP_STRIDE, 2 * HEAD_DIM, 2 * CMP_HID))
    w = jnp.concatenate(parts, axis=-1).astype(BF16)
    return w.reshape(CMP_STRIDE // 2, 4 * HEAD_DIM, 4 * CMP_HID)


def _expand_w2(w2):
    eye = jnp.eye(N_KV, dtype=w2.dtype)
    return jnp.einsum("gh,nd->gnhd", eye, w2).reshape(N_KV * CMP_HID, KV_W).astype(BF16)


def _pe_rows(pe):
    return jnp.broadcast_to(pe.reshape(1, -1), (8, pe.size)).astype(BF16)


def _select_matrix(n_cmp, n_blocks):
    r = SLC_BLOCK // CMP_STRIDE
    m = np.zeros((n_cmp, n_blocks), np.float32)
    for s in range(n_blocks):
        for a in range(r):
            for bb in range(2):
                c = r * s + a - bb
                if 0 <= c < n_cmp:
                    m[c, s] += 1.0
    return m


def _pad_to(a, size, axis):
    pad = [(0, 0)] * a.ndim
    pad[axis] = (0, size - a.shape[axis])
    return jnp.pad(a, pad)


def kernel(x_prompt, x_sample, cache_cmp_k, cache_cmp_v, cache_slc_k, cache_slc_v, state_win_k, state_win_v, state_conv, page_table, p_prompt, p_sample, rel_table, norm_mix, norm_ffn, norm_ple, norm_final, conv_w1, conv_b1, conv_dw, conv_dwb, conv_ln_g, conv_ln_b, conv_w2, conv_b2, attn_w_in, attn_w_out, cmpk_w1, cmpk_pe, cmpk_w2, cmpv_w1, cmpv_pe, cmpv_w2, mlp_up, mlp_down, ple_proj, ple_gate):
    nbp, t, _ = x_prompt.shape
    nbs = x_sample.shape[0]
    n_p = nbp * t
    n_pool = cache_cmp_k.shape[1]
    past = page_table.shape[1] * cache_cmp_k.shape[2]
    wb = state_win_k.shape[2]
    hist = CONV_W - 1
    tm_p, tf, tt, tq = 512, 1024, 256, ATTN_TK

    row = lambda a: a.reshape(1, -1)
    bf = lambda a: a.astype(BF16)

    w1c, w2c = bf(conv_w1[0]), bf(conv_w2[0])
    dw3 = conv_dw[0].reshape(CONV_W, D_MODEL // 128, 128).transpose(1, 0, 2)
    dwb3 = conv_dwb[0].reshape(D_MODEL // 128, 1, 128)
    up, dn = bf(mlp_up), bf(mlp_down)
    wg, wp = bf(ple_gate), bf(ple_proj)
    w_in = attn_w_in[0]
    wq = bf(w_in[:, :D_MODEL])
    wkv = bf(w_in[:, D_MODEL:D_MODEL + 6 * KV_W])
    wgt = bf(_pad_to(w_in[:, D_MODEL + 6 * KV_W:], 128, 1))
    w_out = bf(attn_w_out[0])
    cw = []
    for w1, pe, w2 in ((cmpk_w1[0], cmpk_pe[0], cmpk_w2[0]), (cmpv_w1[0], cmpv_pe[0], cmpv_w2[0])):
        cw.append((_expand_w1(w1), _pe_rows(pe), bf(w1), _expand_w2(w2), bf(w2)))
    wk_p, wv_p = [c[:4] for c in cw]
    wk_s, wv_s = [c[:3] + c[4:] for c in cw]

    def tail(x, i, p, tm, final, pre=None):
        return _mlp_ple(x, row(norm_ffn[i]), up[i], dn[i], row(norm_ple[i]), wg[i], p, wp[i],
                        row(norm_final), tm, tf, final, pre)

    conv_args = (row(conv_ln_g[0]), row(conv_ln_b[0]), w2c, row(conv_b2[0]))

    xp = x_prompt.reshape(n_p, D_MODEL)
    glu = _conv_in(xp, row(norm_mix[0]), w1c, row(conv_b1[0]), tm_p).reshape(nbp, t, D_MODEL)
    x1 = _conv_out(glu, x_prompt, dw3, dwb3, *conv_args, tt).reshape(n_p, D_MODEL)
    conv_p = glu[:, t - hist:][None]
    x2 = tail(x1, 0, p_prompt[0].reshape(n_p, D_PLE), tm_p, False)
    q, kc, vc, ks, vs, kw, vw, gates = _attn_in(x2, row(norm_mix[1]), wq, wkv, wgt, tm_p)
    seq = lambda a: a.reshape(nbp, t, -1)
    half_rows = lambda a: a.reshape(nbp, 2 * t, 128)
    kcmp, vcmp = _compress_prompt(half_rows(kc), half_rows(vc), wk_p, wv_p)

    ncmp = t // CMP_STRIDE
    ii = jnp.arange(tq, dtype=jnp.int32)[None, :]
    jj = jnp.arange(ATTN_TK, dtype=jnp.int32)[:, None]
    r_qk = tq // ATTN_TK
    bkt_toep = jnp.stack([_rel_bucket(d * ATTN_TK + ii - jj) for d in range(1 - r_qk, 2)])
    nshift = (t // tq - 1) * (tq // CMP_STRIDE)
    rr = jnp.arange(ncmp + nshift, dtype=jnp.int32)[:, None]
    bkt_cmp = _rel_bucket(ii - CMP_STRIDE * (rr - nshift) - (2 * CMP_STRIDE - 1))
    msel_p = jnp.asarray(_select_matrix(ncmp, t // SLC_BLOCK).T, BF16)
    o_p = _attn_prompt(seq(q), seq(gates), seq(ks), seq(vs), seq(kw), seq(vw), kcmp, vcmp,
                       rel_table, bkt_toep, bkt_cmp, msel_p, tq)
    y_p = tail(x2, 1, p_prompt[1].reshape(n_p, D_PLE), tm_p, True, pre=(o_p.reshape(n_p, D_MODEL), w_out))

    kv5 = lambda a: a.reshape(1, nbp, t, N_KV, HEAD_DIM)
    keep = min(WINDOW, t)
    win5 = lambda a: seq(a)[:, t - keep:].reshape(1, nbp, keep, N_KV, HEAD_DIM)

    xs = x_sample.reshape(nbs, D_MODEL)
    glu_s = _conv_in(xs, row(norm_mix[0]), w1c, row(conv_b1[0]), nbs)
    x1s, conv_s = _conv_step(state_conv[0], glu_s, xs, conv_dw[0], row(conv_dwb[0]), *conv_args)
    x2s = tail(x1s, 0, p_sample[0].reshape(nbs, D_PLE), nbs, False)
    qs, kcs, vcs, kss, vss, kws, vws, gts = _attn_in(x2s, row(norm_mix[1]), wq, wkv, wgt, nbs)

    q8 = _pad_to(qs.reshape(nbs, N_KV, HPG, HEAD_DIM), Q_ROWS, 2)
    tbl8 = _pad_to(rel_table.T.reshape(N_KV, HPG, NUM_BUCKETS), Q_ROWS, 1)
    nc_s = past // CMP_STRIDE
    cc = jnp.arange(nc_s, dtype=jnp.int32)[None, :]
    bkt_s = _rel_bucket(past - (CMP_STRIDE * cc + 2 * CMP_STRIDE - 1))
    n_blocks_s = past // SLC_BLOCK + 1
    msel_s = jnp.asarray(_select_matrix(nc_s, -(-n_blocks_s // 128) * 128), BF16)
    new3 = lambda a: a.reshape(nbs, 1, KV_W)
    fmajor = lambda c: jnp.transpose(c[0], (0, 2, 3, 1))
    cache2 = lambda c: fmajor(c).reshape(n_pool, 2 * PAGE, PAGE)
    ocmp, idx = _cmp_sample(page_table, cache2(cache_cmp_k), cache2(cache_cmp_v), new3(kcs), new3(vcs),
                            q8, wk_s, wv_s, tbl8, bkt_s, msel_s, past)

    ww = jnp.arange(wb + NEW_LANES, dtype=jnp.int32)[None, :]
    bkt_w = _rel_bucket(wb - ww)
    gates8 = _pad_to(gts[:, :3 * N_HEADS].reshape(nbs, 3, N_KV, HPG).transpose(0, 2, 3, 1), Q_ROWS, 2)
    col4 = lambda a: a.reshape(nbs, N_KV, HEAD_DIM, 1)
    o8, wk_new, wv_new = _dec_sample(idx.reshape(nbs, N_KV * N_SELECT), page_table,
                                     fmajor(cache_slc_k), fmajor(cache_slc_v), q8,
                                     col4(kss), col4(vss), col4(kws), col4(vws),
                                     fmajor(state_win_k), fmajor(state_win_v), gates8, ocmp, tbl8, bkt_w, past)
    o_s = o8[:, :, :HPG].reshape(nbs, D_MODEL)
    y_s = tail(x2s, 1, p_sample[1].reshape(nbs, D_PLE), nbs, True, pre=(o_s, w_out))

    kv5s = lambda a: a.reshape(1, nbs, 1, N_KV, HEAD_DIM)
    win5s = lambda a: jnp.transpose(a, (0, 3, 1, 2))[None]
    return (y_p.reshape(nbp, t, D_MODEL), y_s.reshape(nbs, 1, D_MODEL),
            kv5(kc), kv5(vc), kv5(ks), kv5(vs), win5(kw), win5(vw), conv_p,
            kv5s(kcs), kv5s(vcs), kv5s(kss), kv5s(vss), win5s(wk_new), win5s(wv_new), conv_s[None])
```

```python
import functools
import math

import numpy as np
import jax
import jax.numpy as jnp
from jax import lax
from jax.experimental import pallas as pl
from jax.experimental.pallas import tpu as pltpu

F32 = jnp.float32
BF16 = jnp.bfloat16

D_MODEL = 1024
D_PLE = 256
CONV_W = 31
N_HEADS = 16
HEAD_DIM = 64
N_KV = 4
HPG = 4
KV_W = N_KV * HEAD_DIM
CMP_STRIDE = 16
CMP_HID = 128
SLC_BLOCK = 64
N_SELECT = 16
WINDOW = 512
NUM_BUCKETS = 32
MAX_DISTANCE = 128
D_FF = 4096
EPS = 1e-6
SCALE = HEAD_DIM ** -0.5
assert math.frexp(SCALE)[0] == 0.5
NEG_INF = float("-inf")

ATTN_TK = 128
assert ATTN_TK >= MAX_DISTANCE

VMEM_LIMIT = 56 * 1024 * 1024
HALO = 32


def _sigmoid(x):
    return 1.0 / (1.0 + jnp.exp(-x))


def _rms(x, g):
    return x * lax.rsqrt(jnp.mean(x * x, axis=-1, keepdims=True) + EPS) * g


def _dot(a, b):
    return jnp.dot(a, b, preferred_element_type=F32)


def _dot_lanes(a, b):
    n = b.shape[1]
    return jnp.concatenate([_dot(a, b[:, c:c + 128]) for c in range(0, n, 128)], axis=1)


def _split3(x):
    hi = x.astype(BF16)
    r1 = x - hi.astype(F32)
    mid = r1.astype(BF16)
    lo = (r1 - mid.astype(F32)).astype(BF16)
    return hi, mid, lo


def _rel_bucket(dist):
    n = jnp.maximum(dist, 0)
    max_exact = NUM_BUCKETS // 2
    nf = jnp.maximum(n, 1).astype(F32)
    large = max_exact + (jnp.log(nf / max_exact) / math.log(MAX_DISTANCE / max_exact)
                         * (NUM_BUCKETS - max_exact)).astype(jnp.int32)
    large = jnp.minimum(large, NUM_BUCKETS - 1)
    return jnp.where(n < max_exact, n, large)


def _params(sem):
    return pltpu.CompilerParams(dimension_semantics=sem, vmem_limit_bytes=VMEM_LIMIT)


def _const_spec(shape, single=False):
    n = len(shape)
    if single:
        return pl.BlockSpec(shape, lambda *_: (0,) * n, pipeline_mode=pl.Buffered(1))
    return pl.BlockSpec(shape, lambda *_: (0,) * n)


def _conv_in_body(x_ref, g_ref, w_ref, b_ref, o_ref):
    h = _rms(x_ref[...], g_ref[...]).astype(BF16)
    u = _dot(h, w_ref[...]) + b_ref[...]
    o_ref[...] = u[:, :D_MODEL] * _sigmoid(u[:, D_MODEL:])


def _conv_in(x, g, w1, b1, tm):
    n = x.shape[0]
    return pl.pallas_call(
        _conv_in_body,
        grid=(n // tm,),
        in_specs=[pl.BlockSpec((tm, D_MODEL), lambda i: (i, 0)),
                  _const_spec((1, D_MODEL)),
                  _const_spec((D_MODEL, 2 * D_MODEL)),
                  _const_spec((1, 2 * D_MODEL))],
        out_specs=pl.BlockSpec((tm, D_MODEL), lambda i: (i, 0)),
        out_shape=jax.ShapeDtypeStruct((n, D_MODEL), F32),
        compiler_params=_params(("parallel",)),
        name="conv_in",
    )(x, g, w1, b1)


def _ln_silu_proj(y, lng, lnb, w2, b2, x):
    mu = jnp.mean(y, axis=-1, keepdims=True)
    yc = y - mu
    var = jnp.mean(yc * yc, axis=-1, keepdims=True)
    z = yc * lax.rsqrt(var + EPS) * lng + lnb
    z = z * _sigmoid(z)
    return _dot(z.astype(BF16), w2) + b2 + x


def _conv_out_body(tt, cur_ref, halo_ref, x_ref, dw_ref, dwb_ref, lng_ref, lnb_ref, w2_ref, b2_ref,
                   o_ref, ctx_ref, y_ref):
    i = pl.program_id(1)
    n_strip = D_MODEL // 128
    keep = (i > 0).astype(F32)
    for c in range(n_strip):
        ctx_ref[c, 0:HALO, :] = halo_ref[0, :, 128 * c:128 * (c + 1)] * keep
        ctx_ref[c, HALO:, :] = cur_ref[0, :, 128 * c:128 * (c + 1)]

    def strip(c, carry):
        acc = jnp.broadcast_to(dwb_ref[c], (tt, 128))
        for k in range(CONV_W):
            off = k + HALO - (CONV_W - 1)
            acc = acc + ctx_ref[c, off:off + tt, :] * dw_ref[c, k:k + 1, :]
        y_ref[c] = acc
        return carry

    lax.fori_loop(0, n_strip, strip, 0)
    y = jnp.concatenate([y_ref[c] for c in range(n_strip)], axis=1)
    o_ref[0] = _ln_silu_proj(y, lng_ref[...], lnb_ref[...], w2_ref[...], b2_ref[...], x_ref[0])


def _conv_out(glu, x, dw3, dwb3, lng, lnb, w2, b2, tt):
    b, t, _ = glu.shape
    hb = tt // HALO
    return pl.pallas_call(
        functools.partial(_conv_out_body, tt),
        grid=(b, t // tt),
        in_specs=[pl.BlockSpec((1, tt, D_MODEL), lambda bi, i: (bi, i, 0)),
                  pl.BlockSpec((1, HALO, D_MODEL), lambda bi, i: (bi, jnp.maximum(i * hb - 1, 0), 0)),
                  pl.BlockSpec((1, tt, D_MODEL), lambda bi, i: (bi, i, 0)),
                  _const_spec((D_MODEL // 128, CONV_W, 128)),
                  _const_spec((D_MODEL // 128, 1, 128)),
                  _const_spec((1, D_MODEL)),
                  _const_spec((1, D_MODEL)),
                  _const_spec((D_MODEL, D_MODEL)),
                  _const_spec((1, D_MODEL))],
        out_specs=pl.BlockSpec((1, tt, D_MODEL), lambda bi, i: (bi, i, 0)),
        out_shape=jax.ShapeDtypeStruct((b, t, D_MODEL), F32),
        scratch_shapes=[pltpu.VMEM((D_MODEL // 128, tt + HALO, 128), F32),
                        pltpu.VMEM((D_MODEL // 128, tt, 128), F32)],
        compiler_params=_params(("parallel", "arbitrary")),
        name="conv_out",
    )(glu, glu, x, dw3, dwb3, lng, lnb, w2, b2)


def _conv_step_body(st_ref, u_ref, x_ref, dw_ref, dwb_ref, lng_ref, lnb_ref, w2_ref, b2_ref,
                    o_ref, ns_ref):
    nb = st_ref.shape[0]
    hist = CONV_W - 1
    dwh = dw_ref[0:hist, :]
    rows = []
    for bi in range(nb):
        rows.append(jnp.sum(st_ref[bi] * dwh, axis=0, keepdims=True))
        ns_ref[bi, 0:hist - 1, :] = st_ref[bi, 1:hist, :]
        ns_ref[bi, hist - 1:hist, :] = u_ref[bi:bi + 1, :]
    y = jnp.concatenate(rows, axis=0) + u_ref[...] * dw_ref[hist:hist + 1, :] + dwb_ref[...]
    o_ref[...] = _ln_silu_proj(y, lng_ref[...], lnb_ref[...], w2_ref[...], b2_ref[...], x_ref[...])


def _conv_step(state, glu, x, dw, dwb, lng, lnb, w2, b2):
    nb, hist, _ = state.shape
    return pl.pallas_call(
        _conv_step_body,
        grid=(1,),
        in_specs=[_const_spec((nb, hist, D_MODEL)), _const_spec((nb, D_MODEL)), _const_spec((nb, D_MODEL)),
                  _const_spec((CONV_W, D_MODEL)), _const_spec((1, D_MODEL)), _const_spec((1, D_MODEL)),
                  _const_spec((1, D_MODEL)), _const_spec((D_MODEL, D_MODEL)), _const_spec((1, D_MODEL))],
        out_specs=[_const_spec((nb, D_MODEL)), _const_spec((nb, hist, D_MODEL))],
        out_shape=[jax.ShapeDtypeStruct((nb, D_MODEL), F32),
                   jax.ShapeDtypeStruct((nb, hist, D_MODEL), F32)],
        compiler_params=_params(("arbitrary",)),
        name="conv_step",
    )(state, glu, x, dw, dwb, lng, lnb, w2, b2)


def _mlp_body(final, pre, *refs):
    if pre:
        o_in_ref, wo_ref, refs = refs[0], refs[1], refs[2:]
    (x_ref, gf_ref, up_ref, dn_ref, gp_ref, wg_ref, p_ref, wp_ref, gfin_ref,
     o_ref, h_ref, acc_ref, x1_ref) = refs
    j = pl.program_id(1)

    @pl.when(j == 0)
    def _():
        x1 = x_ref[...]
        if pre:
            x1 = x1 + _dot(o_in_ref[...].astype(BF16), wo_ref[...])
        x1_ref[...] = x1
        h_ref[...] = _rms(x1, gf_ref[...]).astype(BF16)
        acc_ref[...] = jnp.zeros_like(acc_ref)

    a = jnp.maximum(_dot(h_ref[...], up_ref[...]), 0.0)
    acc_ref[...] += _dot((a * a).astype(BF16), dn_ref[...])

    @pl.when(j == pl.num_programs(1) - 1)
    def _():
        x2 = x1_ref[...] + acc_ref[...]
        gate = _sigmoid(_dot(_rms(x2, gp_ref[...]).astype(BF16), wg_ref[...]))
        x3 = x2 + gate * _dot(p_ref[...].astype(BF16), wp_ref[...])
        if final:
            x3 = _rms(x3, gfin_ref[...])
        o_ref[...] = x3


def _mlp_ple(x, gf, up, dn, gp, wg, p, layer, wp, gfin, tm, tf, final, pre=None):
    n = x.shape[0]
    tok = lambda i, j: (i, 0)
    in_specs, args = [], []
    if pre is not None:
        in_specs += [pl.BlockSpec((tm, D_MODEL), tok), _const_spec((D_MODEL, D_MODEL))]
        args += list(pre)
    in_specs += [pl.BlockSpec((tm, D_MODEL), tok),
                 _const_spec((1, D_MODEL)),
                 pl.BlockSpec((D_MODEL, tf), lambda i, j: (0, j)),
                 pl.BlockSpec((tf, D_MODEL), lambda i, j: (j, 0)),
                 _const_spec((1, D_MODEL)),
                 _const_spec((D_MODEL, D_MODEL)),
                 pl.BlockSpec((None, tm, D_PLE), lambda i, j: (layer, i, 0)),
                 _const_spec((D_PLE, D_MODEL)),
                 _const_spec((1, D_MODEL))]
    args += [x, gf, up, dn, gp, wg, p, wp, gfin]
    return pl.pallas_call(
        functools.partial(_mlp_body, final, pre is not None),
        grid=(n // tm, D_FF // tf),
        in_specs=in_specs,
        out_specs=pl.BlockSpec((tm, D_MODEL), tok),
        out_shape=jax.ShapeDtypeStruct((n, D_MODEL), F32),
        scratch_shapes=[pltpu.VMEM((tm, D_MODEL), BF16), pltpu.VMEM((tm, D_MODEL), F32),
                        pltpu.VMEM((tm, D_MODEL), F32)],
        compiler_params=_params(("parallel", "arbitrary")),
        name="mlp_ple",
    )(*args)


KV_NAMES = ("kc", "vc", "ks", "vs", "kw", "vw")
KV_TOKEN_MAJOR = ("kc", "vc", "ks", "kw")


def _attn_in_body(seq_len, x_ref, g_ref, wq_ref, wkv_ref, wg_ref, q_ref, gt_ref, *kv_refs):
    h = _rms(x_ref[...], g_ref[...]).astype(BF16)
    q_ref[...] = _dot(h, wq_ref[...])
    gt_ref[...] = _sigmoid(_dot(h, wg_ref[...]))
    kv = _dot(h, wkv_ref[...])
    blocks = {n: kv[:, KV_W * i:KV_W * (i + 1)] for i, n in enumerate(KV_NAMES)}
    if seq_len is None:
        for n, r in zip(KV_NAMES, kv_refs):
            r[...] = blocks[n]
        return
    tm_refs = kv_refs[:len(KV_TOKEN_MAJOR)]
    fm_refs = kv_refs[len(KV_TOKEN_MAJOR):]
    for n, r in zip(KV_TOKEN_MAJOR, tm_refs):
        r[...] = blocks[n]
    for n, r in zip(KV_NAMES, fm_refs):
        bt = blocks[n].T
        for g in range(N_KV):
            r[0, g] = bt[HEAD_DIM * g:HEAD_DIM * (g + 1), :]


def _attn_in(x, g, wq, wkv, wg, tm, seq_len=None):
    n = x.shape[0]
    tok = lambda i: (i, 0)
    kv_spec = pl.BlockSpec((tm, KV_W), tok)
    kv_shape = jax.ShapeDtypeStruct((n, KV_W), F32)
    if seq_len is None:
        kv_specs, kv_shapes = [kv_spec] * 6, [kv_shape] * 6
    else:
        tps = seq_len // tm
        fm_spec = pl.BlockSpec((1, N_KV, HEAD_DIM, tm), lambda i: (i // tps, 0, 0, i % tps))
        fm_shape = jax.ShapeDtypeStruct((n // seq_len, N_KV, HEAD_DIM, seq_len), F32)
        kv_specs = [kv_spec] * len(KV_TOKEN_MAJOR) + [fm_spec] * 6
        kv_shapes = [kv_shape] * len(KV_TOKEN_MAJOR) + [fm_shape] * 6
    return pl.pallas_call(
        functools.partial(_attn_in_body, seq_len),
        grid=(n // tm,),
        in_specs=[pl.BlockSpec((tm, D_MODEL), tok), _const_spec((1, D_MODEL)),
                  _const_spec((D_MODEL, D_MODEL)), _const_spec((D_MODEL, 6 * KV_W)),
                  _const_spec((D_MODEL, 128))],
        out_specs=[pl.BlockSpec((tm, D_MODEL), tok), pl.BlockSpec((tm, 128), tok)] + kv_specs,
        out_shape=[jax.ShapeDtypeStruct((n, D_MODEL), F32), jax.ShapeDtypeStruct((n, 128), F32)] + kv_shapes,
        compiler_params=_params(("parallel",)),
        name="attn_in",
    )(x, g, wq, wkv, wg)


def _compress_hidden(load_rows, n_chunks, wexp_ref):
    lo = jnp.concatenate([load_rows(2 * r) for r in range(CMP_STRIDE)], axis=1)
    hi = jnp.concatenate([load_rows(2 * r + 1) for r in range(CMP_STRIDE)], axis=1)
    acc = _dot(jnp.concatenate([lo, hi], axis=0).astype(BF16), wexp_ref[...])
    lo, hi = acc[:n_chunks], acc[n_chunks:]
    w = 2 * CMP_HID
    return jnp.concatenate([lo[:, :w], hi[:, :w], lo[:, w:], hi[:, w:]], axis=1)


def _pe_term(pe_ref, w1_ref):
    t = _dot(pe_ref[...], w1_ref[...])[0:1, :]
    return jnp.concatenate([t] * N_KV, axis=1)


def _compress_p_body(nc, kc_ref, vc_ref, wek_ref, pek_ref, w1k_ref, w2k_ref,
                     wev_ref, pev_ref, w1v_ref, w2v_ref, ok_ref, ov_ref):
    half = N_KV * CMP_HID
    row = lax.broadcasted_iota(jnp.int32, (nc, KV_W), 0)
    for src, we, pe, w1, w2, out in ((kc_ref, wek_ref, pek_ref, w1k_ref, w2k_ref, ok_ref),
                                     (vc_ref, wev_ref, pev_ref, w1v_ref, w2v_ref, ov_ref)):
        hh = _compress_hidden(lambda k: src[0, pl.ds(k, nc, stride=2 * CMP_STRIDE), :], nc, we)
        nxt = pltpu.roll(hh[:, half:], nc - 1, 0)
        a = hh[:, :half] + nxt + _pe_term(pe, w1)
        a = a * _sigmoid(a)
        res = _dot(a.astype(BF16), w2[...])
        out[0] = jnp.where(row < nc - 1, res, 0.0)


def _compress_prompt(kc, vc, wk, wv):
    b, t2, _ = kc.shape
    nc = t2 // (2 * CMP_STRIDE)
    seq = pl.BlockSpec((1, t2, 128), lambda i: (i, 0, 0))
    wspecs = [_const_spec((CMP_STRIDE * 128, 4 * CMP_HID)),
              _const_spec((8, 2 * CMP_STRIDE * HEAD_DIM)),
              _const_spec((2 * CMP_STRIDE * HEAD_DIM, CMP_HID)), _const_spec((N_KV * CMP_HID, KV_W))]
    out = pl.BlockSpec((1, nc, KV_W), lambda i: (i, 0, 0))
    return pl.pallas_call(
        functools.partial(_compress_p_body, nc),
        grid=(b,),
        in_specs=[seq, seq] + wspecs + wspecs,
        out_specs=[out, out],
        out_shape=[jax.ShapeDtypeStruct((b, nc, KV_W), F32)] * 2,
        compiler_params=_params(("parallel",)),
        name="compress_prompt",
    )(kc, vc, *wk, *wv)


def _softmax_cols(s, mask):
    s = jnp.where(mask, s, NEG_INF)
    m = jnp.max(s, axis=0, keepdims=True)
    m = jnp.where(m == NEG_INF, 0.0, m)
    e = jnp.where(mask, jnp.exp(s - m), 0.0)
    return e * (1.0 / jnp.maximum(jnp.sum(e, axis=0, keepdims=True), 1e-30))


def _attn_p_body(tq, t, q_ref, gt_ref, ks_ref, vs_ref, kw_ref, vw_ref, kcm_ref, vcm_ref,
                 tbl_ref, bt_ref, bc_ref, msel_ref, o_ref,
                 ksb, kwb, vst, vwt, kcb, vct, btoep, bcmp, qt, selt, ot, m_ref, l_ref, acc_ref, gt_s):
    b = pl.program_id(0)
    qi = pl.program_id(1)
    tk = ATTN_TK
    nkt = t // tk
    nqt = t // tq
    r_qk = tq // tk
    ncmp = kcm_ref.shape[1]
    q0 = qi * tq
    n_near = bt_ref.shape[0]

    @pl.when((b == 0) & (qi == 0))
    def _():
        def per_head(h, carry):
            far = tbl_ref[NUM_BUCKETS - 1, h]
            for d in range(n_near):
                bk = bt_ref[d]
                acc = jnp.zeros((tk, tq), F32)
                for bb in range(NUM_BUCKETS):
                    acc = jnp.where(bk == bb, tbl_ref[bb, h] - far, acc)
                btoep[h, d] = acc
            btoep[h, n_near] = jnp.zeros((tk, tq), F32)
            bk = bc_ref[...]
            acc = jnp.zeros(bk.shape, F32)
            for bb in range(NUM_BUCKETS):
                acc = jnp.where(bk == bb, tbl_ref[bb, h], acc)
            bcmp[h] = acc
            return carry
        lax.fori_loop(0, N_HEADS, per_head, 0)

    @pl.when(qi == 0)
    def _():
        for kt in range(nkt):
            rows = slice(kt * tk, (kt + 1) * tk)
            for src, dstk in ((ks_ref, ksb), (kw_ref, kwb)):
                blk = src[0, rows, :]
                for g in range(N_KV):
                    dstk[g, rows, :] = blk[:, HEAD_DIM * g:HEAD_DIM * (g + 1)].astype(BF16)
            for src, dstv in ((vs_ref, vst), (vw_ref, vwt)):
                for g in range(N_KV):
                    dstv[g, kt] = src[0, g, :, rows].astype(BF16)
        kc = kcm_ref[0]
        vc_t = vcm_ref[0].T
        for g in range(N_KV):
            kcb[g] = kc[:, HEAD_DIM * g:HEAD_DIM * (g + 1)].astype(BF16)
            vct[g] = vc_t[HEAD_DIM * g:HEAD_DIM * (g + 1), :].astype(BF16)

    qt[...] = (q_ref[0].T * SCALE).astype(BF16)
    gt_s[...] = gt_ref[0].T

    qpos_row = q0 + lax.broadcasted_iota(jnp.int32, (1, tq), 1)
    i_iota = lax.broadcasted_iota(jnp.int32, (tk, tq), 1)
    j_iota = lax.broadcasted_iota(jnp.int32, (tk, tq), 0)
    rel = i_iota - j_iota

    c_iota = lax.broadcasted_iota(jnp.int32, (ncmp, tq), 0)
    cmask = (CMP_STRIDE * c_iota + (2 * CMP_STRIDE - 1)) <= qpos_row
    coff = pl.multiple_of((nqt - 1 - qi) * (tq // CMP_STRIDE), 8)

    nblk = t // SLC_BLOCK
    blk_iota = lax.broadcasted_iota(jnp.int32, (nblk, tq), 0)
    cur = qpos_row // SLC_BLOCK
    valid = blk_iota * SLC_BLOCK <= qpos_row
    forced = (blk_iota == 0) | (blk_iota == cur) | (blk_iota == cur - 1)

    for g in range(N_KV):
        pg = jnp.zeros((ncmp, tq), F32)
        for hh in range(HPG):
            h = g * HPG + hh
            qh = qt[HEAD_DIM * h:HEAD_DIM * (h + 1), :]
            s = _dot(kcb[g], qh) + bcmp[h, pl.ds(coff, ncmp), :]
            p = _softmax_cols(s, cmask)
            pg = pg + p
            oc = _dot(vct[g], p.astype(BF16))
            ot[HEAD_DIM * h:HEAD_DIM * (h + 1), :] = gt_s[h:h + 1, :] * oc
        hi, mid, lo = _split3(pg)
        msel = msel_ref[...]
        score = _dot(msel, hi) + _dot(msel, mid) + _dot(msel, lo)
        score = jnp.where(forced, jnp.inf, jnp.where(valid, score, NEG_INF))
        slabs = [score[8 * v:8 * (v + 1), :] for v in range(nblk // 8)]
        ranks = [jnp.zeros((8, tq), F32) for _ in slabs]
        row8 = lax.broadcasted_iota(jnp.int32, (8, tq), 0)
        for i in range(nblk):
            si = score[i:i + 1, :]
            for v, sl in enumerate(slabs):
                if 8 * v > i:
                    beats = si >= sl
                elif 8 * v + 7 <= i:
                    beats = si > sl
                else:
                    beats = (si > sl) | ((row8 > i - 8 * v) & (si == sl))
                ranks[v] = ranks[v] + jnp.where(beats, 1.0, 0.0)
        for v, rk in enumerate(ranks):
            sel = (rk < N_SELECT).astype(F32)
            for r in range(8):
                selt[g, 8 * v + r] = sel[r:r + 1, :]

    def flash_init():
        m_ref[...] = jnp.full(m_ref.shape, NEG_INF, F32)
        l_ref[...] = jnp.zeros(l_ref.shape, F32)
        acc_ref[...] = jnp.zeros(acc_ref.shape, F32)

    def tile_step(kb, vt, kt, bias_d, add_fn):
        k0 = pl.multiple_of(kt * tk, tk)
        hk = tk // 2
        hd = HEAD_DIM // 2
        for g in range(N_KV):
            kblk = kb[g, pl.ds(k0, tk), :]
            vblk = vt[g, kt]
            add = add_fn(g)
            for hh in range(HPG):
                h = g * HPG + hh
                rows = slice(HEAD_DIM * h, HEAD_DIM * (h + 1))
                qh = qt[rows, :]
                s = _dot(kblk, qh)
                if bias_d is not None:
                    s = s + btoep[h, bias_d]
                if add is not None:
                    s = s + add
                m_old = m_ref[h]
                m_new = jnp.maximum(m_old, jnp.max(s, axis=0, keepdims=True))
                m_safe = jnp.where(m_new == NEG_INF, 0.0, m_new)
                alpha = jnp.exp(m_old - m_safe)
                p = jnp.exp(s - m_safe)
                pb = p.astype(BF16)
                l_ref[h] = alpha * l_ref[h] + jnp.sum(p, axis=0, keepdims=True)
                pv = _dot(vblk, pb)
                acc_ref[rows, :] = alpha * acc_ref[rows, :] + pv
                m_ref[h] = m_new

    def flash_finish(gate_row0):
        for h in range(N_HEADS):
            rows = slice(HEAD_DIM * h, HEAD_DIM * (h + 1))
            w = gt_s[gate_row0 + h:gate_row0 + h + 1, :] * (1.0 / jnp.maximum(l_ref[h], 1e-30))
            ot[rows, :] = ot[rows, :] + acc_ref[rows, :] * w

    def sel_rows(g, kt, ok=None):
        nb = tk // SLC_BLOCK
        rows = [jnp.broadcast_to(selt[g, kt * nb + r], (SLC_BLOCK, tq)) for r in range(nb)]
        thr = 0.5 if ok is None else jnp.where(ok, 0.5, 2.0)
        return jnp.concatenate(rows, axis=0) > thr

    def neg_unless(cond):
        return jnp.where(cond, 0.0, NEG_INF)

    causal = rel >= 0

    flash_init()
    n_far = jnp.maximum(qi - 1, 0)

    def slc_pair(pi, carry):
        for sub in range(2):
            kt = 2 * pi + sub
            ok = kt < n_far
            ktc = jnp.minimum(kt, jnp.maximum(n_far - 1, 0))
            tile_step(ksb, vst, ktc, None, lambda g: neg_unless(sel_rows(g, ktc, ok)))
        return carry

    lax.fori_loop(0, (n_far + 1) // 2, slc_pair, 0)
    kt1 = jnp.maximum(qi - 1, 0)
    tile_step(ksb, vst, kt1, 1, lambda g: neg_unless(sel_rows(g, kt1, qi >= 1)))
    tile_step(ksb, vst, qi, 0, lambda g: neg_unless(sel_rows(g, qi) & causal))
    flash_finish(N_HEADS)

    flash_init()
    wt = WINDOW // tk
    for u in range(wt + 1):
        off = wt - u
        kt = qi - off
        ok_add = jnp.where(kt >= 0, 0.0, NEG_INF)
        ktc = jnp.maximum(kt, 0)
        if off == wt:
            add_fn = lambda g, a=ok_add: neg_unless(rel <= 0) + a
        elif off == 0:
            add_fn = lambda g: neg_unless(causal)
        else:
            add_fn = lambda g, a=ok_add: a
        tile_step(kwb, vwt, ktc, off if off <= 1 else None, add_fn)
    flash_finish(2 * N_HEADS)

    o_ref[0] = ot[...].T


def _attn_prompt(q, gates, ks, vs, kw, vw, kcmp, vcmp, table, bkt_toep, bkt_cmp, msel, tq):
    b, t, _ = q.shape
    tk = ATTN_TK
    nkt = t // tk
    ncmp = kcmp.shape[1]
    n_near = bkt_toep.shape[0]
    qspec = pl.BlockSpec((1, tq, D_MODEL), lambda bi, i: (bi, i, 0))
    seq = pl.BlockSpec((1, t, KV_W), lambda bi, i: (bi, 0, 0))
    seq_fm = pl.BlockSpec((1, N_KV, HEAD_DIM, t), lambda bi, i: (bi, 0, 0, 0))
    cmp_spec = pl.BlockSpec((1, ncmp, KV_W), lambda bi, i: (bi, 0, 0))
    return pl.pallas_call(
        functools.partial(_attn_p_body, tq, t),
        grid=(b, t // tq),
        in_specs=[qspec, pl.BlockSpec((1, tq, 128), lambda bi, i: (bi, i, 0)),
                  seq, seq_fm, seq, seq_fm, cmp_spec, cmp_spec,
                  pl.BlockSpec(memory_space=pltpu.SMEM),
                  _const_spec(bkt_toep.shape), _const_spec(bkt_cmp.shape), _const_spec(msel.shape)],
        out_specs=qspec,
        out_shape=jax.ShapeDtypeStruct((b, t, D_MODEL), F32),
        scratch_shapes=[pltpu.VMEM((N_KV, t, HEAD_DIM), BF16), pltpu.VMEM((N_KV, t, HEAD_DIM), BF16),
                        pltpu.VMEM((N_KV, nkt, HEAD_DIM, tk), BF16), pltpu.VMEM((N_KV, nkt, HEAD_DIM, tk), BF16),
                        pltpu.VMEM((N_KV, ncmp, HEAD_DIM), BF16), pltpu.VMEM((N_KV, HEAD_DIM, ncmp), BF16),
                        pltpu.VMEM((N_HEADS, n_near + 1, tk, tq), F32),
                        pltpu.VMEM((N_HEADS,) + bkt_cmp.shape, F32),
                        pltpu.VMEM((D_MODEL, tq), BF16), pltpu.VMEM((N_KV, t // SLC_BLOCK, 1, tq), F32),
                        pltpu.VMEM((D_MODEL, tq), F32),
                        pltpu.VMEM((N_HEADS, 1, tq), F32), pltpu.VMEM((N_HEADS, 1, tq), F32),
                        pltpu.VMEM((D_MODEL, tq), F32), pltpu.VMEM((128, tq), F32)],
        compiler_params=_params(("arbitrary", "arbitrary")),
        name="attn_prompt",
    )(q, gates, ks, vs, kw, vw, kcmp, vcmp, table, bkt_toep, bkt_cmp, msel)


PAGE = 128
PAGES_PER_GROUP = 16
Q_ROWS = 8


def _rank_desc(score_row, n):
    a = jnp.broadcast_to(score_row, (n, n))
    at = a.T
    i = lax.broadcasted_iota(jnp.int32, (n, n), 0)
    j = lax.broadcasted_iota(jnp.int32, (n, n), 1)
    beats = (at > a) | ((i < j) & (at == a))
    return jnp.sum(beats.astype(F32), axis=0, keepdims=True)


def _bias_rows(bucket, tcols):
    r = tcols.shape[0]
    out = jnp.zeros((r, bucket.shape[1]), F32)
    for bb in range(NUM_BUCKETS):
        out = jnp.where(bucket == bb, tcols[:, bb:bb + 1], out)
    return out


def _softmax_rows(s, mask):
    s = jnp.where(mask, s, NEG_INF)
    m = jnp.max(s, axis=1, keepdims=True)
    m = jnp.where(m == NEG_INF, 0.0, m)
    e = jnp.where(mask, jnp.exp(s - m), 0.0)
    return e / jnp.maximum(jnp.sum(e, axis=1, keepdims=True), 1e-30)


def _dot_nt(a, b):
    return lax.dot_general(a, b, (((1,), (1,)), ((), ())), preferred_element_type=F32)


def _cmp_s_body(past, pt_ref, ck_hbm, cv_hbm, kcn_ref, vcn_ref, q_ref,
                wek_ref, pek_ref, w1k_ref, w2k_ref, wev_ref, pev_ref, w1v_ref, w2v_ref,
                tbl_ref, bkt_ref, msel_ref, ocmp_ref, idx_ref,
                kbuf, vbuf, tbuf, hk, hv, sem):
    b = pl.program_id(0)
    rows_pg = PAGES_PER_GROUP * PAGE
    cpg = rows_pg // CMP_STRIDE
    ngrp = past // rows_pg
    nc = past // CMP_STRIDE
    half = N_KV * CMP_HID

    def copies(gi, slot):
        out = []
        for p in range(PAGES_PER_GROUP):
            page = pt_ref[b, gi * PAGES_PER_GROUP + p]
            out.append(pltpu.make_async_copy(ck_hbm.at[page], kbuf.at[slot, p], sem.at[0, slot]))
            out.append(pltpu.make_async_copy(cv_hbm.at[page], vbuf.at[slot, p], sem.at[1, slot]))
        return out

    for c in copies(0, 0):
        c.start()

    def group(gi, carry):
        slot = gi % 2

        @pl.when(gi + 1 < ngrp)
        def _():
            for c in copies(gi + 1, 1 - slot):
                c.start()

        for c in copies(gi, slot):
            c.wait()
        r0 = pl.multiple_of(gi * cpg, cpg)
        for ti, (buf, we, hdst) in enumerate(((kbuf, wek_ref, hk), (vbuf, wev_ref, hv))):
            for p in range(PAGES_PER_GROUP):
                for hf in range(2):
                    tbuf[ti, hf, p * PAGE:(p + 1) * PAGE, :] = buf[slot, p, hf * 128:(hf + 1) * 128, :].T
            hdst[pl.ds(r0, cpg), :] = _compress_hidden(
                lambda k, ti=ti: tbuf[ti, k % 2, pl.ds(k // 2, cpg, stride=CMP_STRIDE), :], cpg, we)
        return carry

    lax.fori_loop(0, ngrp, group, 0)

    col = lax.broadcasted_iota(jnp.int32, (1, nc), 1)
    cmask = (CMP_STRIDE * col + (2 * CMP_STRIDE - 1)) <= past
    row8 = lax.broadcasted_iota(jnp.int32, (Q_ROWS, nc), 0)
    cmpd = []
    for new_ref, we, pe, w1, w2, hdst in ((kcn_ref, wek_ref, pek_ref, w1k_ref, w2k_ref, hk),
                                          (vcn_ref, wev_ref, pev_ref, w1v_ref, w2v_ref, hv)):
        new = jnp.broadcast_to(new_ref[0], (8, KV_W)).astype(BF16)
        new_lo = _dot(new[:, :128], we[0:128, :])
        new_hi = _dot(new[:, 128:], we[0:128, :])
        w = 2 * CMP_HID
        hdst[nc:nc + 8, :] = jnp.concatenate([new_lo[:, :w], new_hi[:, :w], new_lo[:, w:], new_hi[:, w:]], axis=1)
        a = hdst[0:nc, 0:half] + hdst[1:nc + 1, half:2 * half] + _pe_term(pe, w1)
        a = (a * _sigmoid(a)).astype(BF16)
        cmpd.append([_dot(a[:, CMP_HID * g:CMP_HID * (g + 1)], w2[...]).astype(BF16) for g in range(N_KV)])

    nsel = msel_ref.shape[1]
    lane = lax.broadcasted_iota(jnp.int32, (1, nsel), 1)
    n_blocks = past // SLC_BLOCK + 1
    cur = past // SLC_BLOCK
    forced = (lane == 0) | (lane == cur) | (lane == cur - 1)
    valid = (lane * SLC_BLOCK <= past) & (lane < n_blocks)
    k_iota = lax.broadcasted_iota(jnp.int32, (N_SELECT, nsel), 0).astype(F32)
    lane_f = lax.broadcasted_iota(jnp.int32, (N_SELECT, nsel), 1).astype(F32)
    for g in range(N_KV):
        qg = q_ref[0, g].astype(BF16)
        bias = _bias_rows(bkt_ref[...], tbl_ref[g])
        s = _dot_nt(qg, cmpd[0][g]) * SCALE + bias
        p = _softmax_rows(s, cmask)
        ocmp_ref[0, g] = _dot(p.astype(BF16), cmpd[1][g])
        pg = jnp.sum(jnp.where(row8 < HPG, p, 0.0), axis=0, keepdims=True)
        hi, mid, lo = _split3(jnp.broadcast_to(pg, (8, nc)))
        msel = msel_ref[...]
        score = (_dot(hi, msel) + _dot(mid, msel) + _dot(lo, msel))[0:1, :]
        score = jnp.where(forced, jnp.inf, jnp.where(valid, score, NEG_INF))
        rank = _rank_desc(score, nsel)
        hit = jnp.where(jnp.broadcast_to(rank, (N_SELECT, nsel)) == k_iota, lane_f, 0.0)
        idx_ref[0, g] = jnp.sum(hit, axis=1, keepdims=True).astype(jnp.int32)


def _cmp_sample(page_table, cache_k, cache_v, kc_new, vc_new, q8, wk, wv, tbl8, bkt, msel, past):
    nb = page_table.shape[0]
    rows_pg = PAGES_PER_GROUP * PAGE
    nc = past // CMP_STRIDE
    any_spec = pl.BlockSpec(memory_space=pl.ANY)
    new_spec = pl.BlockSpec((1, 1, KV_W), lambda i, pt: (i, 0, 0))
    cs = lambda shape: pl.BlockSpec(shape, lambda i, pt: (0,) * len(shape))
    wspecs = [cs((CMP_STRIDE * 128, 4 * CMP_HID)),
              cs((8, 2 * CMP_STRIDE * HEAD_DIM)),
              cs((2 * CMP_STRIDE * HEAD_DIM, CMP_HID)), cs((CMP_HID, HEAD_DIM))]
    grid_spec = pltpu.PrefetchScalarGridSpec(
        num_scalar_prefetch=1,
        grid=(nb,),
        in_specs=[any_spec, any_spec, new_spec, new_spec,
                  pl.BlockSpec((1, N_KV, Q_ROWS, HEAD_DIM), lambda i, pt: (i, 0, 0, 0))]
                 + wspecs + wspecs + [cs(tbl8.shape), cs(bkt.shape), cs(msel.shape)],
        out_specs=[pl.BlockSpec((1, N_KV, Q_ROWS, HEAD_DIM), lambda i, pt: (i, 0, 0, 0)),
                   pl.BlockSpec((1, N_KV, N_SELECT, 1), lambda i, pt: (i, 0, 0, 0))],
        scratch_shapes=[pltpu.VMEM((2, PAGES_PER_GROUP, 2 * PAGE, 128), F32),
                        pltpu.VMEM((2, PAGES_PER_GROUP, 2 * PAGE, 128), F32),
                        pltpu.VMEM((2, 2, rows_pg, 128), F32),
                        pltpu.VMEM((nc + 8, 2 * N_KV * CMP_HID), F32),
                        pltpu.VMEM((nc + 8, 2 * N_KV * CMP_HID), F32),
                        pltpu.SemaphoreType.DMA((2, 2))])
    return pl.pallas_call(
        functools.partial(_cmp_s_body, past),
        grid_spec=grid_spec,
        out_shape=[jax.ShapeDtypeStruct((nb, N_KV, Q_ROWS, HEAD_DIM), F32),
                   jax.ShapeDtypeStruct((nb, N_KV, N_SELECT, 1), jnp.int32)],
        compiler_params=_params(("arbitrary",)),
        name="cmp_sample",
    )(page_table, cache_k, cache_v, kc_new, vc_new, q8, *wk, *wv, tbl8, bkt, msel)


NEW_LANES = 128


def _attend_kt(qg, kt, vt, bias, mask):
    s = _dot(qg, kt.astype(BF16)) * SCALE + bias
    p = _softmax_rows(s, mask)
    return _dot_nt(p.astype(BF16), vt.astype(BF16))


def _dec_s_body(past, idx_ref, pt_ref, csk_hbm, csv_hbm,
                q_ref, ksn_ref, vsn_ref, kwn_ref, vwn_ref, wk_ref, wv_ref, gt_ref, ocmp_ref,
                tbl_ref, bktw_ref, o_ref, wko_ref, wvo_ref,
                kbuf, vbuf, kwbuf, vwbuf, sem):
    b = pl.program_id(0)
    n_pages = past // PAGE
    cur = past // SLC_BLOCK
    bpp = PAGE // SLC_BLOCK
    lsel = N_SELECT * PAGE
    wb = wk_ref.shape[3]

    @pl.when(b == 0)
    def _():
        kbuf[...] = jnp.zeros_like(kbuf)
        vbuf[...] = jnp.zeros_like(vbuf)
        kwbuf[...] = jnp.zeros_like(kwbuf)
        vwbuf[...] = jnp.zeros_like(vwbuf)

    copies = []
    for g in range(N_KV):
        for k in range(N_SELECT):
            blk = idx_ref[b, g * N_SELECT + k]
            page = pt_ref[b, jnp.minimum(blk // bpp, n_pages - 1)]
            dst = pl.ds(k * PAGE, PAGE)
            copies.append(pltpu.make_async_copy(csk_hbm.at[page, g], kbuf.at[g, :, dst], sem.at[0]))
            copies.append(pltpu.make_async_copy(csv_hbm.at[page, g], vbuf.at[g, :, dst], sem.at[1]))
    for c in copies:
        c.start()

    lane_wb = lax.broadcasted_iota(jnp.int32, (HEAD_DIM, wb), 1)
    for g in range(N_KV):
        for src, new_ref, dst, buf in ((wk_ref, kwn_ref, wko_ref, kwbuf), (wv_ref, vwn_ref, wvo_ref, vwbuf)):
            st = src[0, g]
            newc = new_ref[0, g]
            dst[0, g] = jnp.where(lane_wb == wb - 1, newc, pltpu.roll(st, wb - 1, 1))
            buf[g, :, 0:wb] = st
            buf[g, :, wb:wb + 1] = newc

    for c in copies:
        c.wait()

    ls = lsel + NEW_LANES
    lw = wb + NEW_LANES
    lane_s = lax.broadcasted_iota(jnp.int32, (1, ls), 1)
    lane_w = lax.broadcasted_iota(jnp.int32, (1, lw), 1)
    wmask = lane_w <= wb
    tok = lane_s % PAGE
    for g in range(N_KV):
        kbuf[g, :, lsel:lsel + 1] = ksn_ref[0, g]
        vbuf[g, :, lsel:lsel + 1] = vsn_ref[0, g]
        qg = q_ref[0, g].astype(BF16)
        tcols = tbl_ref[g]

        kpos = past + (lane_s - lsel)
        gathered = lane_s < 0
        has_new = jnp.int32(0)
        for k in range(N_SELECT):
            blk = idx_ref[b, g * N_SELECT + k]
            in_slot = (lane_s // PAGE) == k
            kpos = jnp.where(in_slot, (blk // bpp) * PAGE + tok, kpos)
            gathered = gathered | (in_slot & (blk < cur) & ((tok // SLC_BLOCK) == (blk % bpp)))
            has_new = has_new | (blk == cur).astype(jnp.int32)
        smask = (gathered | ((lane_s == lsel) & (has_new > 0))) & (kpos <= past)
        sbias = _bias_rows(_rel_bucket(past - kpos), tcols)
        o_slc = _attend_kt(qg, kbuf[g], vbuf[g], sbias, smask)

        wbias = _bias_rows(bktw_ref[...], tcols)
        o_win = _attend_kt(qg, kwbuf[g], vwbuf[g], wbias, wmask)

        gt = gt_ref[0, g]
        o_ref[0, g] = gt[:, 0:1] * ocmp_ref[0, g] + gt[:, 1:2] * o_slc + gt[:, 2:3] * o_win


def _dec_sample(idx, page_table, cache_sk, cache_sv, q8, ks_new, vs_new, kw_new, vw_new, state_wk, state_wv,
                gates8, ocmp, tbl8, bktw, past):
    nb, wb = state_wk.shape[0], state_wk.shape[3]
    any_spec = pl.BlockSpec(memory_space=pl.ANY)
    new_spec = pl.BlockSpec((1, N_KV, HEAD_DIM, 1), lambda i, *_: (i, 0, 0, 0))
    st_spec = pl.BlockSpec((1, N_KV, HEAD_DIM, wb), lambda i, *_: (i, 0, 0, 0))
    head_spec = pl.BlockSpec((1, N_KV, Q_ROWS, HEAD_DIM), lambda i, *_: (i, 0, 0, 0))
    cs = lambda shape: pl.BlockSpec(shape, lambda i, *_: (0,) * len(shape))
    lsel = N_SELECT * PAGE
    grid_spec = pltpu.PrefetchScalarGridSpec(
        num_scalar_prefetch=2,
        grid=(nb,),
        in_specs=[any_spec, any_spec, head_spec,
                  new_spec, new_spec, new_spec, new_spec, st_spec, st_spec,
                  pl.BlockSpec((1, N_KV, Q_ROWS, 3), lambda i, *_: (i, 0, 0, 0)), head_spec,
                  cs(tbl8.shape), cs(bktw.shape)],
        out_specs=[head_spec, st_spec, st_spec],
        scratch_shapes=[pltpu.VMEM((N_KV, HEAD_DIM, lsel + NEW_LANES), F32),
                        pltpu.VMEM((N_KV, HEAD_DIM, lsel + NEW_LANES), F32),
                        pltpu.VMEM((N_KV, HEAD_DIM, wb + NEW_LANES), F32),
                        pltpu.VMEM((N_KV, HEAD_DIM, wb + NEW_LANES), F32),
                        pltpu.SemaphoreType.DMA((2,))])
    return pl.pallas_call(
        functools.partial(_dec_s_body, past),
        grid_spec=grid_spec,
        out_shape=[jax.ShapeDtypeStruct((nb, N_KV, Q_ROWS, HEAD_DIM), F32),
                   jax.ShapeDtypeStruct(state_wk.shape, F32),
                   jax.ShapeDtypeStruct(state_wv.shape, F32)],
        compiler_params=_params(("arbitrary",)),
        name="dec_sample",
    )(idx, page_table, cache_sk, cache_sv, q8, ks_new, vs_new, kw_new, vw_new,
      state_wk, state_wv, gates8, ocmp, tbl8, bktw)


def _expand_w1(w1):
    eye = jnp.eye(2, dtype=w1.dtype)
    parts = []
    for part in (w1[:CMP_STRIDE * HEAD_DIM], w1[CMP_STRIDE * HEAD_DIM:]):
        w = part.reshape(CMP_STRIDE, HEAD_DIM, CMP_HID)
        parts.append(jnp.einsum("gh,rdn->rgdhn", eye, w).reshape(CMP_STRIDE, 2 * HEAD_DIM, 2 * CMP_HID))
    w = jnp.concatenate(parts, axis=-1).astype(BF16)
    return w.reshape(CMP_STRIDE * 2 * HEAD_DIM, 4 * CMP_HID)


def _expand_w2(w2):
    eye = jnp.eye(N_KV, dtype=w2.dtype)
    return jnp.einsum("gh,nd->gnhd", eye, w2).reshape(N_KV * CMP_HID, KV_W).astype(BF16)


def _pe_rows(pe):
    return jnp.broadcast_to(pe.reshape(1, -1), (8, pe.size)).astype(BF16)


def _select_matrix(n_cmp, n_blocks):
    r = SLC_BLOCK // CMP_STRIDE
    m = np.zeros((n_cmp, n_blocks), np.float32)
    for s in range(n_blocks):
        for a in range(r):
            for bb in range(2):
                c = r * s + a - bb
                if 0 <= c < n_cmp:
                    m[c, s] += 1.0
    return m


def _pad_to(a, size, axis):
    pad = [(0, 0)] * a.ndim
    pad[axis] = (0, size - a.shape[axis])
    return jnp.pad(a, pad)


def kernel(x_prompt, x_sample, cache_cmp_k, cache_cmp_v, cache_slc_k, cache_slc_v, state_win_k, state_win_v, state_conv, page_table, p_prompt, p_sample, rel_table, norm_mix, norm_ffn, norm_ple, norm_final, conv_w1, conv_b1, conv_dw, conv_dwb, conv_ln_g, conv_ln_b, conv_w2, conv_b2, attn_w_in, attn_w_out, cmpk_w1, cmpk_pe, cmpk_w2, cmpv_w1, cmpv_pe, cmpv_w2, mlp_up, mlp_down, ple_proj, ple_gate):
    nbp, t, _ = x_prompt.shape
    nbs = x_sample.shape[0]
    n_p = nbp * t
    n_pool = cache_cmp_k.shape[1]
    past = page_table.shape[1] * cache_cmp_k.shape[2]
    wb = state_win_k.shape[2]
    hist = CONV_W - 1
    tm_p, tf, tt, tq = 512, 1024, 256, ATTN_TK

    row = lambda a: a.reshape(1, -1)
    bf = lambda a: a.astype(BF16)

    w1c, w2c = bf(conv_w1[0]), bf(conv_w2[0])
    dw3 = conv_dw[0].reshape(CONV_W, D_MODEL // 128, 128).transpose(1, 0, 2)
    dwb3 = conv_dwb[0].reshape(D_MODEL // 128, 1, 128)
    up, dn = bf(mlp_up), bf(mlp_down)
    wg, wp = bf(ple_gate), bf(ple_proj)
    w_in = attn_w_in[0]
    wq = bf(w_in[:, :D_MODEL])
    wkv = bf(w_in[:, D_MODEL:D_MODEL + 6 * KV_W])
    wgt = bf(_pad_to(w_in[:, D_MODEL + 6 * KV_W:], 128, 1))
    w_out = bf(attn_w_out[0])
    cw = []
    for w1, pe, w2 in ((cmpk_w1[0], cmpk_pe[0], cmpk_w2[0]), (cmpv_w1[0], cmpv_pe[0], cmpv_w2[0])):
        cw.append((_expand_w1(w1), _pe_rows(pe), bf(w1), _expand_w2(w2), bf(w2)))
    wk_p, wv_p = [c[:4] for c in cw]
    wk_s, wv_s = [c[:3] + c[4:] for c in cw]

    def tail(x, i, p, tm, final, pre=None):
        return _mlp_ple(x, row(norm_ffn[i]), up[i], dn[i], row(norm_ple[i]), wg[i], p, i, wp[i],
                        row(norm_final), tm, tf, final, pre)

    conv_args = (row(conv_ln_g[0]), row(conv_ln_b[0]), w2c, row(conv_b2[0]))

    xp = x_prompt.reshape(n_p, D_MODEL)
    glu = _conv_in(xp, row(norm_mix[0]), w1c, row(conv_b1[0]), tm_p).reshape(nbp, t, D_MODEL)
    x1 = _conv_out(glu, x_prompt, dw3, dwb3, *conv_args, tt).reshape(n_p, D_MODEL)
    conv_p = glu[:, t - hist:][None]
    pp = p_prompt.reshape(-1, n_p, D_PLE)
    ps = p_sample.reshape(-1, nbs, D_PLE)
    x2 = tail(x1, 0, pp, tm_p, False)
    q, gates, kc, vc, ks, kw, *kv_fm = _attn_in(x2, row(norm_mix[1]), wq, wkv, wgt, tm_p, seq_len=t)
    vs_fm, vw_fm = kv_fm[KV_NAMES.index("vs")], kv_fm[KV_NAMES.index("vw")]
    seq = lambda a: a.reshape(nbp, t, -1)
    half_rows = lambda a: a.reshape(nbp, 2 * t, 128)
    kcmp, vcmp = _compress_prompt(half_rows(kc), half_rows(vc), wk_p, wv_p)

    ncmp = t // CMP_STRIDE
    ii = jnp.arange(tq, dtype=jnp.int32)[None, :]
    jj = jnp.arange(ATTN_TK, dtype=jnp.int32)[:, None]
    r_qk = tq // ATTN_TK
    bkt_toep = jnp.stack([_rel_bucket(d * ATTN_TK + ii - jj) for d in range(1 - r_qk, 2)])
    nshift = (t // tq - 1) * (tq // CMP_STRIDE)
    rr = jnp.arange(ncmp + nshift, dtype=jnp.int32)[:, None]
    bkt_cmp = _rel_bucket(ii - CMP_STRIDE * (rr - nshift) - (2 * CMP_STRIDE - 1))
    msel_p = jnp.asarray(_select_matrix(ncmp, t // SLC_BLOCK).T, BF16)
    o_p = _attn_prompt(seq(q), seq(gates), seq(ks), vs_fm, seq(kw), vw_fm, kcmp, vcmp,
                       rel_table, bkt_toep, bkt_cmp, msel_p, tq)
    y_p = tail(x2, 1, pp, tm_p, True, pre=(o_p.reshape(n_p, D_MODEL), w_out))

    from_fm = lambda a: jnp.transpose(a, (0, 3, 1, 2))[None]
    keep = min(WINDOW, t)
    kv_p = [from_fm(a) for a in kv_fm[:4]] + [from_fm(a[..., t - keep:]) for a in kv_fm[4:]]

    xs = x_sample.reshape(nbs, D_MODEL)
    glu_s = _conv_in(xs, row(norm_mix[0]), w1c, row(conv_b1[0]), nbs)
    x1s, conv_s = _conv_step(state_conv[0], glu_s, xs, conv_dw[0], row(conv_dwb[0]), *conv_args)
    x2s = tail(x1s, 0, ps, nbs, False)
    qs, gts, kcs, vcs, kss, vss, kws, vws = _attn_in(x2s, row(norm_mix[1]), wq, wkv, wgt, nbs)

    q8 = _pad_to(qs.reshape(nbs, N_KV, HPG, HEAD_DIM), Q_ROWS, 2)
    tbl8 = _pad_to(rel_table.T.reshape(N_KV, HPG, NUM_BUCKETS), Q_ROWS, 1)
    nc_s = past // CMP_STRIDE
    cc = jnp.arange(nc_s, dtype=jnp.int32)[None, :]
    bkt_s = _rel_bucket(past - (CMP_STRIDE * cc + 2 * CMP_STRIDE - 1))
    n_blocks_s = past // SLC_BLOCK + 1
    msel_s = jnp.asarray(_select_matrix(nc_s, -(-n_blocks_s // 128) * 128), BF16)
    new3 = lambda a: a.reshape(nbs, 1, KV_W)
    fmajor = lambda c: jnp.transpose(c[0], (0, 2, 3, 1))
    cache2 = lambda c: fmajor(c).reshape(n_pool, 2 * PAGE, PAGE)
    ocmp, idx = _cmp_sample(page_table, cache2(cache_cmp_k), cache2(cache_cmp_v), new3(kcs), new3(vcs),
                            q8, wk_s, wv_s, tbl8, bkt_s, msel_s, past)

    ww = jnp.arange(wb + NEW_LANES, dtype=jnp.int32)[None, :]
    bkt_w = _rel_bucket(wb - ww)
    gates8 = _pad_to(gts[:, :3 * N_HEADS].reshape(nbs, 3, N_KV, HPG).transpose(0, 2, 3, 1), Q_ROWS, 2)
    col4 = lambda a: a.reshape(nbs, N_KV, HEAD_DIM, 1)
    o8, wk_new, wv_new = _dec_sample(idx.reshape(nbs, N_KV * N_SELECT), page_table,
                                     fmajor(cache_slc_k), fmajor(cache_slc_v), q8,
                                     col4(kss), col4(vss), col4(kws), col4(vws),
                                     fmajor(state_win_k), fmajor(state_win_v), gates8, ocmp, tbl8, bkt_w, past)
    o_s = o8[:, :, :HPG].reshape(nbs, D_MODEL)
    y_s = tail(x2s, 1, ps, nbs, True, pre=(o_s, w_out))

    kv5s = lambda a: a.reshape(1, nbs, 1, N_KV, HEAD_DIM)
    return (y_p.reshape(nbp, t, D_MODEL), y_s.reshape(nbs, 1, D_MODEL), *kv_p, conv_p,
            kv5s(kcs), kv5s(vcs), kv5s(kss), kv5s(vss), from_fm(wk_new), from_fm(wv_new), conv_s[None])
```

```python
import functools
import math

import numpy as np
import jax
import jax.numpy as jnp
from jax import lax
from jax.experimental import pallas as pl
from jax.experimental.pallas import tpu as pltpu

F32 = jnp.float32
BF16 = jnp.bfloat16

D_MODEL = 1024
D_PLE = 256
CONV_W = 31
N_HEADS = 16
HEAD_DIM = 64
N_KV = 4
HPG = 4
KV_W = N_KV * HEAD_DIM
CMP_STRIDE = 16
CMP_HID = 128
SLC_BLOCK = 64
N_SELECT = 16
WINDOW = 512
NUM_BUCKETS = 32
MAX_DISTANCE = 128
D_FF = 4096
EPS = 1e-6
SCALE = HEAD_DIM ** -0.5
assert math.frexp(SCALE)[0] == 0.5
NEG_INF = float("-inf")

ATTN_TK = 128
assert ATTN_TK >= MAX_DISTANCE

VMEM_LIMIT = 56 * 1024 * 1024
HALO = 32


def _sigmoid(x):
    return 1.0 / (1.0 + jnp.exp(-x))


def _rms(x, g):
    return x * lax.rsqrt(jnp.mean(x * x, axis=-1, keepdims=True) + EPS) * g


def _dot(a, b):
    return jnp.dot(a, b, preferred_element_type=F32)


def _dot_lanes(a, b):
    n = b.shape[1]
    return jnp.concatenate([_dot(a, b[:, c:c + 128]) for c in range(0, n, 128)], axis=1)


def _split3(x):
    hi = x.astype(BF16)
    r1 = x - hi.astype(F32)
    mid = r1.astype(BF16)
    lo = (r1 - mid.astype(F32)).astype(BF16)
    return hi, mid, lo


def _rel_bucket(dist):
    n = jnp.maximum(dist, 0)
    max_exact = NUM_BUCKETS // 2
    nf = jnp.maximum(n, 1).astype(F32)
    large = max_exact + (jnp.log(nf / max_exact) / math.log(MAX_DISTANCE / max_exact)
                         * (NUM_BUCKETS - max_exact)).astype(jnp.int32)
    large = jnp.minimum(large, NUM_BUCKETS - 1)
    return jnp.where(n < max_exact, n, large)


def _params(sem):
    return pltpu.CompilerParams(dimension_semantics=sem, vmem_limit_bytes=VMEM_LIMIT)


def _const_spec(shape, single=False):
    n = len(shape)
    if single:
        return pl.BlockSpec(shape, lambda *_: (0,) * n, pipeline_mode=pl.Buffered(1))
    return pl.BlockSpec(shape, lambda *_: (0,) * n)


def _conv_in_body(x_ref, g_ref, w_ref, b_ref, o_ref):
    h = _rms(x_ref[...], g_ref[...]).astype(BF16)
    u = _dot(h, w_ref[...]) + b_ref[...]
    o_ref[...] = u[:, :D_MODEL] * _sigmoid(u[:, D_MODEL:])


def _conv_in(x, g, w1, b1, tm):
    n = x.shape[0]
    return pl.pallas_call(
        _conv_in_body,
        grid=(n // tm,),
        in_specs=[pl.BlockSpec((tm, D_MODEL), lambda i: (i, 0)),
                  _const_spec((1, D_MODEL)),
                  _const_spec((D_MODEL, 2 * D_MODEL)),
                  _const_spec((1, 2 * D_MODEL))],
        out_specs=pl.BlockSpec((tm, D_MODEL), lambda i: (i, 0)),
        out_shape=jax.ShapeDtypeStruct((n, D_MODEL), F32),
        compiler_params=_params(("parallel",)),
        name="conv_in",
    )(x, g, w1, b1)


def _ln_silu_proj(y, lng, lnb, w2, b2, x):
    mu = jnp.mean(y, axis=-1, keepdims=True)
    yc = y - mu
    var = jnp.mean(yc * yc, axis=-1, keepdims=True)
    z = yc * lax.rsqrt(var + EPS) * lng + lnb
    z = z * _sigmoid(z)
    return _dot(z.astype(BF16), w2) + b2 + x


def _conv_out_body(tt, cur_ref, halo_ref, x_ref, dw_ref, dwb_ref, lng_ref, lnb_ref, w2_ref, b2_ref,
                   o_ref, ctx_ref, y_ref):
    i = pl.program_id(1)
    n_strip = D_MODEL // 128
    keep = (i > 0).astype(F32)
    for c in range(n_strip):
        ctx_ref[c, 0:HALO, :] = halo_ref[0, :, 128 * c:128 * (c + 1)] * keep
        ctx_ref[c, HALO:, :] = cur_ref[0, :, 128 * c:128 * (c + 1)]

    def strip(c, carry):
        acc = jnp.broadcast_to(dwb_ref[c], (tt, 128))
        for k in range(CONV_W):
            off = k + HALO - (CONV_W - 1)
            acc = acc + ctx_ref[c, off:off + tt, :] * dw_ref[c, k:k + 1, :]
        y_ref[c] = acc
        return carry

    lax.fori_loop(0, n_strip, strip, 0)
    y = jnp.concatenate([y_ref[c] for c in range(n_strip)], axis=1)
    o_ref[0] = _ln_silu_proj(y, lng_ref[...], lnb_ref[...], w2_ref[...], b2_ref[...], x_ref[0])


def _conv_out(glu, x, dw3, dwb3, lng, lnb, w2, b2, tt):
    b, t, _ = glu.shape
    hb = tt // HALO
    return pl.pallas_call(
        functools.partial(_conv_out_body, tt),
        grid=(b, t // tt),
        in_specs=[pl.BlockSpec((1, tt, D_MODEL), lambda bi, i: (bi, i, 0)),
                  pl.BlockSpec((1, HALO, D_MODEL), lambda bi, i: (bi, jnp.maximum(i * hb - 1, 0), 0)),
                  pl.BlockSpec((1, tt, D_MODEL), lambda bi, i: (bi, i, 0)),
                  _const_spec((D_MODEL // 128, CONV_W, 128)),
                  _const_spec((D_MODEL // 128, 1, 128)),
                  _const_spec((1, D_MODEL)),
                  _const_spec((1, D_MODEL)),
                  _const_spec((D_MODEL, D_MODEL)),
                  _const_spec((1, D_MODEL))],
        out_specs=pl.BlockSpec((1, tt, D_MODEL), lambda bi, i: (bi, i, 0)),
        out_shape=jax.ShapeDtypeStruct((b, t, D_MODEL), F32),
        scratch_shapes=[pltpu.VMEM((D_MODEL // 128, tt + HALO, 128), F32),
                        pltpu.VMEM((D_MODEL // 128, tt, 128), F32)],
        compiler_params=_params(("parallel", "arbitrary")),
        name="conv_out",
    )(glu, glu, x, dw3, dwb3, lng, lnb, w2, b2)


def _conv_step_body(st_ref, u_ref, x_ref, dw_ref, dwb_ref, lng_ref, lnb_ref, w2_ref, b2_ref,
                    o_ref, ns_ref):
    nb = st_ref.shape[0]
    hist = CONV_W - 1
    dwh = dw_ref[0:hist, :]
    rows = []
    for bi in range(nb):
        rows.append(jnp.sum(st_ref[bi] * dwh, axis=0, keepdims=True))
        ns_ref[bi, 0:hist - 1, :] = st_ref[bi, 1:hist, :]
        ns_ref[bi, hist - 1:hist, :] = u_ref[bi:bi + 1, :]
    y = jnp.concatenate(rows, axis=0) + u_ref[...] * dw_ref[hist:hist + 1, :] + dwb_ref[...]
    o_ref[...] = _ln_silu_proj(y, lng_ref[...], lnb_ref[...], w2_ref[...], b2_ref[...], x_ref[...])


def _conv_step(state, glu, x, dw, dwb, lng, lnb, w2, b2):
    nb, hist, _ = state.shape
    return pl.pallas_call(
        _conv_step_body,
        grid=(1,),
        in_specs=[_const_spec((nb, hist, D_MODEL)), _const_spec((nb, D_MODEL)), _const_spec((nb, D_MODEL)),
                  _const_spec((CONV_W, D_MODEL)), _const_spec((1, D_MODEL)), _const_spec((1, D_MODEL)),
                  _const_spec((1, D_MODEL)), _const_spec((D_MODEL, D_MODEL)), _const_spec((1, D_MODEL))],
        out_specs=[_const_spec((nb, D_MODEL)), _const_spec((nb, hist, D_MODEL))],
        out_shape=[jax.ShapeDtypeStruct((nb, D_MODEL), F32),
                   jax.ShapeDtypeStruct((nb, hist, D_MODEL), F32)],
        compiler_params=_params(("arbitrary",)),
        name="conv_step",
    )(state, glu, x, dw, dwb, lng, lnb, w2, b2)


def _mlp_body(final, pre, *refs):
    if pre:
        o_in_ref, wo_ref, refs = refs[0], refs[1], refs[2:]
    (x_ref, gf_ref, up_ref, dn_ref, gp_ref, wg_ref, p_ref, wp_ref, gfin_ref,
     o_ref, h_ref, acc_ref, x1_ref) = refs
    j = pl.program_id(1)

    @pl.when(j == 0)
    def _():
        x1 = x_ref[...]
        if pre:
            x1 = x1 + _dot(o_in_ref[...].astype(BF16), wo_ref[...])
        x1_ref[...] = x1
        h_ref[...] = _rms(x1, gf_ref[...]).astype(BF16)
        acc_ref[...] = jnp.zeros_like(acc_ref)

    a = jnp.maximum(_dot(h_ref[...], up_ref[...]), 0.0)
    acc_ref[...] += _dot((a * a).astype(BF16), dn_ref[...])

    @pl.when(j == pl.num_programs(1) - 1)
    def _():
        x2 = x1_ref[...] + acc_ref[...]
        gate = _sigmoid(_dot(_rms(x2, gp_ref[...]).astype(BF16), wg_ref[...]))
        x3 = x2 + gate * _dot(p_ref[...].astype(BF16), wp_ref[...])
        if final:
            x3 = _rms(x3, gfin_ref[...])
        o_ref[...] = x3


def _mlp_ple(x, gf, up, dn, gp, wg, p, layer, wp, gfin, tm, tf, final, pre=None):
    n = x.shape[0]
    tok = lambda i, j: (i, 0)
    in_specs, args = [], []
    if pre is not None:
        in_specs += [pl.BlockSpec((tm, D_MODEL), tok), _const_spec((D_MODEL, D_MODEL))]
        args += list(pre)
    in_specs += [pl.BlockSpec((tm, D_MODEL), tok),
                 _const_spec((1, D_MODEL)),
                 pl.BlockSpec((None, D_MODEL, tf), lambda i, j: (layer, 0, j)),
                 pl.BlockSpec((None, tf, D_MODEL), lambda i, j: (layer, j, 0)),
                 _const_spec((1, D_MODEL)),
                 pl.BlockSpec((None, D_MODEL, D_MODEL), lambda i, j: (layer, 0, 0)),
                 pl.BlockSpec((None, tm, D_PLE), lambda i, j: (layer, i, 0)),
                 pl.BlockSpec((None, D_PLE, D_MODEL), lambda i, j: (layer, 0, 0)),
                 _const_spec((1, D_MODEL))]
    args += [x, gf, up, dn, gp, wg, p, wp, gfin]
    return pl.pallas_call(
        functools.partial(_mlp_body, final, pre is not None),
        grid=(n // tm, D_FF // tf),
        in_specs=in_specs,
        out_specs=pl.BlockSpec((tm, D_MODEL), tok),
        out_shape=jax.ShapeDtypeStruct((n, D_MODEL), F32),
        scratch_shapes=[pltpu.VMEM((tm, D_MODEL), BF16), pltpu.VMEM((tm, D_MODEL), F32),
                        pltpu.VMEM((tm, D_MODEL), F32)],
        compiler_params=_params(("parallel", "arbitrary")),
        name="mlp_ple",
    )(*args)


KV_NAMES = ("kc", "vc", "ks", "vs", "kw", "vw")
KV_TOKEN_MAJOR = ("kc", "vc", "ks", "kw")
KV_WINDOWED = ("kw", "vw")


def _attn_in_body(seq_len, x_ref, g_ref, wq_ref, wkv_ref, wg_ref, q_ref, gt_ref, *kv_refs):
    h = _rms(x_ref[...], g_ref[...]).astype(BF16)
    q_ref[...] = _dot(h, wq_ref[...])
    gt_ref[...] = _sigmoid(_dot(h, wg_ref[...]))
    kv = _dot(h, wkv_ref[...])
    blocks = {n: kv[:, KV_W * i:KV_W * (i + 1)] for i, n in enumerate(KV_NAMES)}
    if seq_len is None:
        for n, r in zip(KV_NAMES, kv_refs):
            r[...] = blocks[n]
        return
    n_tm = len(KV_TOKEN_MAJOR)
    tm_refs = kv_refs[:n_tm]
    fm_refs = kv_refs[n_tm:n_tm + len(KV_NAMES)]
    last_refs = dict(zip(KV_WINDOWED, kv_refs[n_tm + len(KV_NAMES):]))
    for n, r in zip(KV_TOKEN_MAJOR, tm_refs):
        r[...] = blocks[n]
    tps = seq_len // x_ref.shape[0]
    is_last = pl.program_id(0) % tps == tps - 1
    for n, r in zip(KV_NAMES, fm_refs):
        bt = blocks[n].T
        for g in range(N_KV):
            r[0, g] = bt[HEAD_DIM * g:HEAD_DIM * (g + 1), :]
        if n in last_refs:

            @pl.when(is_last)
            def _(bt=bt, dst=last_refs[n]):
                for g in range(N_KV):
                    dst[0, g] = bt[HEAD_DIM * g:HEAD_DIM * (g + 1), :]


def _attn_in(x, g, wq, wkv, wg, tm, seq_len=None):
    n = x.shape[0]
    tok = lambda i: (i, 0)
    kv_spec = pl.BlockSpec((tm, KV_W), tok)
    kv_shape = jax.ShapeDtypeStruct((n, KV_W), F32)
    if seq_len is None:
        kv_specs, kv_shapes = [kv_spec] * 6, [kv_shape] * 6
    else:
        tps = seq_len // tm
        fm_spec = pl.BlockSpec((1, N_KV, HEAD_DIM, tm), lambda i: (i // tps, 0, 0, i % tps))
        fm_shape = jax.ShapeDtypeStruct((n // seq_len, N_KV, HEAD_DIM, seq_len), F32)
        assert tm == min(WINDOW, seq_len)
        last_spec = pl.BlockSpec((1, N_KV, HEAD_DIM, tm), lambda i: (i // tps, 0, 0, 0))
        last_shape = jax.ShapeDtypeStruct((n // seq_len, N_KV, HEAD_DIM, tm), F32)
        kv_specs = [kv_spec] * len(KV_TOKEN_MAJOR) + [fm_spec] * 6 + [last_spec] * len(KV_WINDOWED)
        kv_shapes = [kv_shape] * len(KV_TOKEN_MAJOR) + [fm_shape] * 6 + [last_shape] * len(KV_WINDOWED)
    return pl.pallas_call(
        functools.partial(_attn_in_body, seq_len),
        grid=(n // tm,),
        in_specs=[pl.BlockSpec((tm, D_MODEL), tok), _const_spec((1, D_MODEL)),
                  _const_spec((D_MODEL, D_MODEL)), _const_spec((D_MODEL, 6 * KV_W)),
                  _const_spec((D_MODEL, 128))],
        out_specs=[pl.BlockSpec((tm, D_MODEL), tok), pl.BlockSpec((tm, 128), tok)] + kv_specs,
        out_shape=[jax.ShapeDtypeStruct((n, D_MODEL), F32), jax.ShapeDtypeStruct((n, 128), F32)] + kv_shapes,
        compiler_params=_params(("arbitrary",)),
        name="attn_in",
    )(x, g, wq, wkv, wg)


def _compress_hidden(load_rows, n_chunks, wexp_ref):
    lo = jnp.concatenate([load_rows(2 * r) for r in range(CMP_STRIDE)], axis=1)
    hi = jnp.concatenate([load_rows(2 * r + 1) for r in range(CMP_STRIDE)], axis=1)
    acc = _dot(jnp.concatenate([lo, hi], axis=0).astype(BF16), wexp_ref[...])
    lo, hi = acc[:n_chunks], acc[n_chunks:]
    w = 2 * CMP_HID
    return jnp.concatenate([lo[:, :w], hi[:, :w], lo[:, w:], hi[:, w:]], axis=1)


def _pe_term(pe_ref, w1_ref):
    t = _dot(pe_ref[...], w1_ref[...])[0:1, :]
    return jnp.concatenate([t] * N_KV, axis=1)


def _compress_p_body(nc, kc_ref, vc_ref, wek_ref, pek_ref, w1k_ref, w2k_ref,
                     wev_ref, pev_ref, w1v_ref, w2v_ref, ok_ref, ov_ref):
    half = N_KV * CMP_HID
    row = lax.broadcasted_iota(jnp.int32, (nc, KV_W), 0)
    for src, we, pe, w1, w2, out in ((kc_ref, wek_ref, pek_ref, w1k_ref, w2k_ref, ok_ref),
                                     (vc_ref, wev_ref, pev_ref, w1v_ref, w2v_ref, ov_ref)):
        hh = _compress_hidden(lambda k: src[0, pl.ds(k, nc, stride=2 * CMP_STRIDE), :], nc, we)
        nxt = pltpu.roll(hh[:, half:], nc - 1, 0)
        a = hh[:, :half] + nxt + _pe_term(pe, w1)
        a = a * _sigmoid(a)
        res = _dot(a.astype(BF16), w2[...])
        out[0] = jnp.where(row < nc - 1, res, 0.0)


def _compress_prompt(kc, vc, wk, wv):
    b, t2, _ = kc.shape
    nc = t2 // (2 * CMP_STRIDE)
    seq = pl.BlockSpec((1, t2, 128), lambda i: (i, 0, 0))
    wspecs = [_const_spec((CMP_STRIDE * 128, 4 * CMP_HID)),
              _const_spec((8, 2 * CMP_STRIDE * HEAD_DIM)),
              _const_spec((2 * CMP_STRIDE * HEAD_DIM, CMP_HID)), _const_spec((N_KV * CMP_HID, KV_W))]
    out = pl.BlockSpec((1, nc, KV_W), lambda i: (i, 0, 0))
    return pl.pallas_call(
        functools.partial(_compress_p_body, nc),
        grid=(b,),
        in_specs=[seq, seq] + wspecs + wspecs,
        out_specs=[out, out],
        out_shape=[jax.ShapeDtypeStruct((b, nc, KV_W), F32)] * 2,
        compiler_params=_params(("parallel",)),
        name="compress_prompt",
    )(kc, vc, *wk, *wv)


def _softmax_cols(s, mask):
    s = jnp.where(mask, s, NEG_INF)
    m = jnp.max(s, axis=0, keepdims=True)
    m = jnp.where(m == NEG_INF, 0.0, m)
    e = jnp.where(mask, jnp.exp(s - m), 0.0)
    return e * (1.0 / jnp.maximum(jnp.sum(e, axis=0, keepdims=True), 1e-30))


def _attn_p_body(tq, t, q_ref, gt_ref, ks_ref, vs_ref, kw_ref, vw_ref, kcm_ref, vcm_ref,
                 tbl_ref, bt_ref, bc_ref, msel_ref, o_ref,
                 ksb, kwb, vst, vwt, kcb, vct, btoep, bcmp, qt, selt, ot, m_ref, l_ref, acc_ref, gt_s):
    b = pl.program_id(0)
    qi = pl.program_id(1)
    tk = ATTN_TK
    nkt = t // tk
    nqt = t // tq
    r_qk = tq // tk
    ncmp = kcm_ref.shape[1]
    q0 = qi * tq
    n_near = bt_ref.shape[0]

    @pl.when((b == 0) & (qi == 0))
    def _():
        def per_head(h, carry):
            far = tbl_ref[NUM_BUCKETS - 1, h]
            for d in range(n_near):
                bk = bt_ref[d]
                acc = jnp.zeros((tk, tq), F32)
                for bb in range(NUM_BUCKETS):
                    acc = jnp.where(bk == bb, tbl_ref[bb, h] - far, acc)
                btoep[h, d] = acc
            btoep[h, n_near] = jnp.zeros((tk, tq), F32)
            bk = bc_ref[...]
            acc = jnp.zeros(bk.shape, F32)
            for bb in range(NUM_BUCKETS):
                acc = jnp.where(bk == bb, tbl_ref[bb, h], acc)
            bcmp[h] = acc
            return carry
        lax.fori_loop(0, N_HEADS, per_head, 0)

    @pl.when(qi == 0)
    def _():
        for kt in range(nkt):
            rows = slice(kt * tk, (kt + 1) * tk)
            for src, dstk in ((ks_ref, ksb), (kw_ref, kwb)):
                blk = src[0, rows, :]
                for g in range(N_KV):
                    dstk[g, rows, :] = blk[:, HEAD_DIM * g:HEAD_DIM * (g + 1)].astype(BF16)
            for src, dstv in ((vs_ref, vst), (vw_ref, vwt)):
                for g in range(N_KV):
                    dstv[g, kt] = src[0, g, :, rows].astype(BF16)
        kc = kcm_ref[0]
        vc_t = vcm_ref[0].T
        for g in range(N_KV):
            kcb[g] = kc[:, HEAD_DIM * g:HEAD_DIM * (g + 1)].astype(BF16)
            vct[g] = vc_t[HEAD_DIM * g:HEAD_DIM * (g + 1), :].astype(BF16)

    qt[...] = (q_ref[0].T * SCALE).astype(BF16)
    gt_s[...] = gt_ref[0].T

    qpos_row = q0 + lax.broadcasted_iota(jnp.int32, (1, tq), 1)
    i_iota = lax.broadcasted_iota(jnp.int32, (tk, tq), 1)
    j_iota = lax.broadcasted_iota(jnp.int32, (tk, tq), 0)
    rel = i_iota - j_iota

    c_iota = lax.broadcasted_iota(jnp.int32, (ncmp, tq), 0)
    cmask = (CMP_STRIDE * c_iota + (2 * CMP_STRIDE - 1)) <= qpos_row
    coff = pl.multiple_of((nqt - 1 - qi) * (tq // CMP_STRIDE), 8)

    nblk = t // SLC_BLOCK
    blk_iota = lax.broadcasted_iota(jnp.int32, (nblk, tq), 0)
    cur = qpos_row // SLC_BLOCK
    valid = blk_iota * SLC_BLOCK <= qpos_row
    forced = (blk_iota == 0) | (blk_iota == cur) | (blk_iota == cur - 1)

    for g in range(N_KV):
        pg = jnp.zeros((ncmp, tq), F32)
        for hh in range(HPG):
            h = g * HPG + hh
            qh = qt[HEAD_DIM * h:HEAD_DIM * (h + 1), :]
            s = _dot(kcb[g], qh) + bcmp[h, pl.ds(coff, ncmp), :]
            p = _softmax_cols(s, cmask)
            pg = pg + p
            oc = _dot(vct[g], p.astype(BF16))
            ot[HEAD_DIM * h:HEAD_DIM * (h + 1), :] = gt_s[h:h + 1, :] * oc
        hi, mid, lo = _split3(pg)
        msel = msel_ref[...]
        score = _dot(msel, hi) + _dot(msel, mid) + _dot(msel, lo)
        score = jnp.where(forced, jnp.inf, jnp.where(valid, score, NEG_INF))
        slabs = [score[8 * v:8 * (v + 1), :] for v in range(nblk // 8)]
        ranks = [jnp.zeros((8, tq), F32) for _ in slabs]
        row8 = lax.broadcasted_iota(jnp.int32, (8, tq), 0)
        for i in range(nblk):
            si = score[i:i + 1, :]
            for v, sl in enumerate(slabs):
                if 8 * v > i:
                    beats = si >= sl
                elif 8 * v + 7 <= i:
                    beats = si > sl
                else:
                    beats = (si > sl) | ((row8 > i - 8 * v) & (si == sl))
                ranks[v] = ranks[v] + jnp.where(beats, 1.0, 0.0)
        for v, rk in enumerate(ranks):
            sel = (rk < N_SELECT).astype(F32)
            for r in range(8):
                selt[g, 8 * v + r] = sel[r:r + 1, :]

    def flash_init(br):
        m_ref[br] = jnp.full(m_ref.shape[1:], NEG_INF, F32)
        l_ref[br] = jnp.zeros(l_ref.shape[1:], F32)
        acc_ref[br] = jnp.zeros(acc_ref.shape[1:], F32)

    def tile_step(br, kb, vt, kt, bias_d, add_fn):
        k0 = pl.multiple_of(kt * tk, tk)
        for g in range(N_KV):
            kblk = kb[g, pl.ds(k0, tk), :]
            vblk = vt[g, kt]
            add = add_fn(g)
            for hh in range(HPG):
                h = g * HPG + hh
                rows = slice(HEAD_DIM * h, HEAD_DIM * (h + 1))
                s = _dot(kblk, qt[rows, :])
                if bias_d is not None:
                    s = s + btoep[h, bias_d]
                if add is not None:
                    s = s + add
                m_old = m_ref[br, h]
                m_new = jnp.maximum(m_old, jnp.max(s, axis=0, keepdims=True))
                m_safe = jnp.where(m_new == NEG_INF, 0.0, m_new)
                alpha = jnp.exp(m_old - m_safe)
                p = jnp.exp(s - m_safe)
                l_ref[br, h] = alpha * l_ref[br, h] + jnp.sum(p, axis=0, keepdims=True)
                acc_ref[br, rows, :] = alpha * acc_ref[br, rows, :] + _dot(vblk, p.astype(BF16))
                m_ref[br, h] = m_new

    def flash_finish(br, gate_row0):
        for h in range(N_HEADS):
            rows = slice(HEAD_DIM * h, HEAD_DIM * (h + 1))
            w = gt_s[gate_row0 + h:gate_row0 + h + 1, :] * (1.0 / jnp.maximum(l_ref[br, h], 1e-30))
            ot[rows, :] = ot[rows, :] + acc_ref[br, rows, :] * w

    def sel_rows(g, kt, ok=None):
        nb = tk // SLC_BLOCK
        rows = [jnp.broadcast_to(selt[g, kt * nb + r], (SLC_BLOCK, tq)) for r in range(nb)]
        thr = 0.5 if ok is None else jnp.where(ok, 0.5, 2.0)
        return jnp.concatenate(rows, axis=0) > thr

    def neg_unless(cond):
        return jnp.where(cond, 0.0, NEG_INF)

    causal = rel >= 0
    SLC, WIN = 0, 1

    flash_init(SLC)
    flash_init(WIN)
    n_far = jnp.maximum(qi - 1, 0)

    def slc_pair(pi, carry):
        for sub in range(2):
            kt = 2 * pi + sub
            ok = kt < n_far
            ktc = jnp.minimum(kt, jnp.maximum(n_far - 1, 0))
            tile_step(SLC, ksb, vst, ktc, None, lambda g: neg_unless(sel_rows(g, ktc, ok)))
        return carry

    lax.fori_loop(0, (n_far + 1) // 2, slc_pair, 0)
    kt1 = jnp.maximum(qi - 1, 0)
    tile_step(SLC, ksb, vst, kt1, 1, lambda g: neg_unless(sel_rows(g, kt1, qi >= 1)))
    tile_step(SLC, ksb, vst, qi, 0, lambda g: neg_unless(sel_rows(g, qi) & causal))

    wt = WINDOW // tk
    for u in range(wt + 1):
        off = wt - u
        kt = qi - off
        ok_add = jnp.where(kt >= 0, 0.0, NEG_INF)
        ktc = jnp.maximum(kt, 0)
        if off == wt:
            add_fn = lambda g, a=ok_add: neg_unless(rel <= 0) + a
        elif off == 0:
            add_fn = lambda g: neg_unless(causal)
        else:
            add_fn = lambda g, a=ok_add: a
        tile_step(WIN, kwb, vwt, ktc, off if off <= 1 else None, add_fn)
    flash_finish(SLC, N_HEADS)
    flash_finish(WIN, 2 * N_HEADS)

    o_ref[0] = ot[...].T


def _attn_prompt(q, gates, ks, vs, kw, vw, kcmp, vcmp, table, bkt_toep, bkt_cmp, msel, tq):
    b, t, _ = q.shape
    tk = ATTN_TK
    nkt = t // tk
    ncmp = kcmp.shape[1]
    n_near = bkt_toep.shape[0]
    qspec = pl.BlockSpec((1, tq, D_MODEL), lambda bi, i: (bi, i, 0))
    seq = pl.BlockSpec((1, t, KV_W), lambda bi, i: (bi, 0, 0))
    seq_fm = pl.BlockSpec((1, N_KV, HEAD_DIM, t), lambda bi, i: (bi, 0, 0, 0))
    cmp_spec = pl.BlockSpec((1, ncmp, KV_W), lambda bi, i: (bi, 0, 0))
    return pl.pallas_call(
        functools.partial(_attn_p_body, tq, t),
        grid=(b, t // tq),
        in_specs=[qspec, pl.BlockSpec((1, tq, 128), lambda bi, i: (bi, i, 0)),
                  seq, seq_fm, seq, seq_fm, cmp_spec, cmp_spec,
                  pl.BlockSpec(memory_space=pltpu.SMEM),
                  _const_spec(bkt_toep.shape), _const_spec(bkt_cmp.shape), _const_spec(msel.shape)],
        out_specs=qspec,
        out_shape=jax.ShapeDtypeStruct((b, t, D_MODEL), F32),
        scratch_shapes=[pltpu.VMEM((N_KV, t, HEAD_DIM), BF16), pltpu.VMEM((N_KV, t, HEAD_DIM), BF16),
                        pltpu.VMEM((N_KV, nkt, HEAD_DIM, tk), BF16), pltpu.VMEM((N_KV, nkt, HEAD_DIM, tk), BF16),
                        pltpu.VMEM((N_KV, ncmp, HEAD_DIM), BF16), pltpu.VMEM((N_KV, HEAD_DIM, ncmp), BF16),
                        pltpu.VMEM((N_HEADS, n_near + 1, tk, tq), F32),
                        pltpu.VMEM((N_HEADS,) + bkt_cmp.shape, F32),
                        pltpu.VMEM((D_MODEL, tq), BF16), pltpu.VMEM((N_KV, t // SLC_BLOCK, 1, tq), F32),
                        pltpu.VMEM((D_MODEL, tq), F32),
                        pltpu.VMEM((2, N_HEADS, 1, tq), F32), pltpu.VMEM((2, N_HEADS, 1, tq), F32),
                        pltpu.VMEM((2, D_MODEL, tq), F32), pltpu.VMEM((128, tq), F32)],
        compiler_params=_params(("arbitrary", "arbitrary")),
        name="attn_prompt",
    )(q, gates, ks, vs, kw, vw, kcmp, vcmp, table, bkt_toep, bkt_cmp, msel)


PAGE = 128
PAGES_PER_GROUP = 32
Q_ROWS = 8


def _rank_desc(score_row, n):
    a = jnp.broadcast_to(score_row, (n, n))
    at = a.T
    i = lax.broadcasted_iota(jnp.int32, (n, n), 0)
    j = lax.broadcasted_iota(jnp.int32, (n, n), 1)
    beats = (at > a) | ((i < j) & (at == a))
    return jnp.sum(beats.astype(F32), axis=0, keepdims=True)


def _bias_rows(bucket, tcols):
    r = tcols.shape[0]
    out = jnp.zeros((r, bucket.shape[1]), F32)
    for bb in range(NUM_BUCKETS):
        out = jnp.where(bucket == bb, tcols[:, bb:bb + 1], out)
    return out


def _softmax_rows(s, mask):
    s = jnp.where(mask, s, NEG_INF)
    m = jnp.max(s, axis=1, keepdims=True)
    m = jnp.where(m == NEG_INF, 0.0, m)
    e = jnp.where(mask, jnp.exp(s - m), 0.0)
    return e / jnp.maximum(jnp.sum(e, axis=1, keepdims=True), 1e-30)


def _dot_nt(a, b):
    return lax.dot_general(a, b, (((1,), (1,)), ((), ())), preferred_element_type=F32)


def _cmp_s_body(past, pt_ref, ck_hbm, cv_hbm, kcn_ref, vcn_ref, q_ref,
                wek_ref, pek_ref, w1k_ref, w2k_ref, wev_ref, pev_ref, w1v_ref, w2v_ref,
                tbl_ref, bkt_ref, msel_ref, ocmp_ref, idx_ref,
                kbuf, vbuf, tbuf, hk, hv, sem):
    b = pl.program_id(0)
    rows_pg = PAGES_PER_GROUP * PAGE
    cpg = rows_pg // CMP_STRIDE
    ngrp = past // rows_pg
    nc = past // CMP_STRIDE
    half = N_KV * CMP_HID

    def copies(gi, slot):
        out = []
        for p in range(PAGES_PER_GROUP):
            page = pt_ref[b, gi * PAGES_PER_GROUP + p]
            out.append(pltpu.make_async_copy(ck_hbm.at[page], kbuf.at[slot, p], sem.at[0, slot]))
            out.append(pltpu.make_async_copy(cv_hbm.at[page], vbuf.at[slot, p], sem.at[1, slot]))
        return out

    for c in copies(0, 0):
        c.start()

    def group(gi, carry):
        slot = gi % 2

        @pl.when(gi + 1 < ngrp)
        def _():
            for c in copies(gi + 1, 1 - slot):
                c.start()

        for c in copies(gi, slot):
            c.wait()
        r0 = pl.multiple_of(gi * cpg, cpg)
        for ti, (buf, we, hdst) in enumerate(((kbuf, wek_ref, hk), (vbuf, wev_ref, hv))):
            for p in range(PAGES_PER_GROUP):
                for hf in range(2):
                    tbuf[ti, hf, p * PAGE:(p + 1) * PAGE, :] = buf[slot, p, hf * 128:(hf + 1) * 128, :].T
            hdst[pl.ds(r0, cpg), :] = _compress_hidden(
                lambda k, ti=ti: tbuf[ti, k % 2, pl.ds(k // 2, cpg, stride=CMP_STRIDE), :], cpg, we)
        return carry

    lax.fori_loop(0, ngrp, group, 0)

    col = lax.broadcasted_iota(jnp.int32, (1, nc), 1)
    cmask = (CMP_STRIDE * col + (2 * CMP_STRIDE - 1)) <= past
    row8 = lax.broadcasted_iota(jnp.int32, (Q_ROWS, nc), 0)
    cmpd = []
    for new_ref, we, pe, w1, w2, hdst in ((kcn_ref, wek_ref, pek_ref, w1k_ref, w2k_ref, hk),
                                          (vcn_ref, wev_ref, pev_ref, w1v_ref, w2v_ref, hv)):
        new = jnp.broadcast_to(new_ref[0], (8, KV_W)).astype(BF16)
        new_lo = _dot(new[:, :128], we[0:128, :])
        new_hi = _dot(new[:, 128:], we[0:128, :])
        w = 2 * CMP_HID
        hdst[nc:nc + 8, :] = jnp.concatenate([new_lo[:, :w], new_hi[:, :w], new_lo[:, w:], new_hi[:, w:]], axis=1)
        a = hdst[0:nc, 0:half] + hdst[1:nc + 1, half:2 * half] + _pe_term(pe, w1)
        a = (a * _sigmoid(a)).astype(BF16)
        cmpd.append([_dot(a[:, CMP_HID * g:CMP_HID * (g + 1)], w2[...]).astype(BF16) for g in range(N_KV)])

    nsel = msel_ref.shape[1]
    lane = lax.broadcasted_iota(jnp.int32, (1, nsel), 1)
    n_blocks = past // SLC_BLOCK + 1
    cur = past // SLC_BLOCK
    forced = (lane == 0) | (lane == cur) | (lane == cur - 1)
    valid = (lane * SLC_BLOCK <= past) & (lane < n_blocks)
    k_iota = lax.broadcasted_iota(jnp.int32, (N_SELECT, nsel), 0).astype(F32)
    lane_f = lax.broadcasted_iota(jnp.int32, (N_SELECT, nsel), 1).astype(F32)
    for g in range(N_KV):
        qg = q_ref[0, g].astype(BF16)
        bias = _bias_rows(bkt_ref[...], tbl_ref[g])
        s = _dot_nt(qg, cmpd[0][g]) * SCALE + bias
        p = _softmax_rows(s, cmask)
        ocmp_ref[0, g] = _dot(p.astype(BF16), cmpd[1][g])
        pg = jnp.sum(jnp.where(row8 < HPG, p, 0.0), axis=0, keepdims=True)
        hi, mid, lo = _split3(jnp.broadcast_to(pg, (8, nc)))
        msel = msel_ref[...]
        score = (_dot(hi, msel) + _dot(mid, msel) + _dot(lo, msel))[0:1, :]
        score = jnp.where(forced, jnp.inf, jnp.where(valid, score, NEG_INF))
        rank = _rank_desc(score, nsel)
        hit = jnp.where(jnp.broadcast_to(rank, (N_SELECT, nsel)) == k_iota, lane_f, 0.0)
        idx_ref[0, g] = jnp.sum(hit, axis=1, keepdims=True).astype(jnp.int32)


def _cmp_sample(page_table, cache_k, cache_v, kc_new, vc_new, q8, wk, wv, tbl8, bkt, msel, past):
    nb = page_table.shape[0]
    rows_pg = PAGES_PER_GROUP * PAGE
    nc = past // CMP_STRIDE
    any_spec = pl.BlockSpec(memory_space=pl.ANY)
    new_spec = pl.BlockSpec((1, 1, KV_W), lambda i, pt: (i, 0, 0))
    cs = lambda shape: pl.BlockSpec(shape, lambda i, pt: (0,) * len(shape))
    wspecs = [cs((CMP_STRIDE * 128, 4 * CMP_HID)),
              cs((8, 2 * CMP_STRIDE * HEAD_DIM)),
              cs((2 * CMP_STRIDE * HEAD_DIM, CMP_HID)), cs((CMP_HID, HEAD_DIM))]
    grid_spec = pltpu.PrefetchScalarGridSpec(
        num_scalar_prefetch=1,
        grid=(nb,),
        in_specs=[any_spec, any_spec, new_spec, new_spec,
                  pl.BlockSpec((1, N_KV, Q_ROWS, HEAD_DIM), lambda i, pt: (i, 0, 0, 0))]
                 + wspecs + wspecs + [cs(tbl8.shape), cs(bkt.shape), cs(msel.shape)],
        out_specs=[pl.BlockSpec((1, N_KV, Q_ROWS, HEAD_DIM), lambda i, pt: (i, 0, 0, 0)),
                   pl.BlockSpec((1, N_KV, N_SELECT, 1), lambda i, pt: (i, 0, 0, 0))],
        scratch_shapes=[pltpu.VMEM((2, PAGES_PER_GROUP, 2 * PAGE, 128), F32),
                        pltpu.VMEM((2, PAGES_PER_GROUP, 2 * PAGE, 128), F32),
                        pltpu.VMEM((2, 2, rows_pg, 128), F32),
                        pltpu.VMEM((nc + 8, 2 * N_KV * CMP_HID), F32),
                        pltpu.VMEM((nc + 8, 2 * N_KV * CMP_HID), F32),
                        pltpu.SemaphoreType.DMA((2, 2))])
    return pl.pallas_call(
        functools.partial(_cmp_s_body, past),
        grid_spec=grid_spec,
        out_shape=[jax.ShapeDtypeStruct((nb, N_KV, Q_ROWS, HEAD_DIM), F32),
                   jax.ShapeDtypeStruct((nb, N_KV, N_SELECT, 1), jnp.int32)],
        compiler_params=_params(("arbitrary",)),
        name="cmp_sample",
    )(page_table, cache_k, cache_v, kc_new, vc_new, q8, *wk, *wv, tbl8, bkt, msel)


NEW_LANES = 128


def _attend_kt(qg, kt, vt, bias, mask):
    s = _dot(qg, kt.astype(BF16)) * SCALE + bias
    p = _softmax_rows(s, mask)
    return _dot_nt(p.astype(BF16), vt.astype(BF16))


def _dec_s_body(past, idx_ref, pt_ref, csk_hbm, csv_hbm,
                q_ref, ksn_ref, vsn_ref, kwn_ref, vwn_ref, wk_ref, wv_ref, gt_ref, ocmp_ref,
                tbl_ref, bktw_ref, o_ref, wko_ref, wvo_ref,
                kbuf, vbuf, kwbuf, vwbuf, sem):
    b = pl.program_id(0)
    n_pages = past // PAGE
    cur = past // SLC_BLOCK
    bpp = PAGE // SLC_BLOCK
    lsel = N_SELECT * PAGE
    wb = wk_ref.shape[3]

    @pl.when(b == 0)
    def _():
        kbuf[...] = jnp.zeros_like(kbuf)
        vbuf[...] = jnp.zeros_like(vbuf)
        kwbuf[...] = jnp.zeros_like(kwbuf)
        vwbuf[...] = jnp.zeros_like(vwbuf)

    def gathers(bb):
        half = bb % 2
        out = []
        for g in range(N_KV):
            for k in range(N_SELECT):
                blk = idx_ref[bb, g * N_SELECT + k]
                page = pt_ref[bb, jnp.minimum(blk // bpp, n_pages - 1)]
                dst = pl.ds(k * PAGE, PAGE)
                out.append(pltpu.make_async_copy(csk_hbm.at[page, g], kbuf.at[half, g, :, dst], sem.at[0, half]))
                out.append(pltpu.make_async_copy(csv_hbm.at[page, g], vbuf.at[half, g, :, dst], sem.at[1, half]))
        return out

    @pl.when(b == 0)
    def _():
        for c in gathers(b):
            c.start()

    @pl.when(b + 1 < pl.num_programs(0))
    def _():
        for c in gathers(b + 1):
            c.start()

    cur_half = b % 2

    lane_wb = lax.broadcasted_iota(jnp.int32, (HEAD_DIM, wb), 1)
    for g in range(N_KV):
        for src, new_ref, dst, buf in ((wk_ref, kwn_ref, wko_ref, kwbuf), (wv_ref, vwn_ref, wvo_ref, vwbuf)):
            st = src[0, g]
            newc = new_ref[0, g]
            dst[0, g] = jnp.where(lane_wb == wb - 1, newc, pltpu.roll(st, wb - 1, 1))
            buf[g, :, 0:wb] = st
            buf[g, :, wb:wb + 1] = newc

    for c in gathers(b):
        c.wait()

    ls = lsel + NEW_LANES
    lw = wb + NEW_LANES
    lane_s = lax.broadcasted_iota(jnp.int32, (1, ls), 1)
    lane_w = lax.broadcasted_iota(jnp.int32, (1, lw), 1)
    wmask = lane_w <= wb
    tok = lane_s % PAGE
    for g in range(N_KV):
        kbuf[cur_half, g, :, lsel:lsel + 1] = ksn_ref[0, g]
        vbuf[cur_half, g, :, lsel:lsel + 1] = vsn_ref[0, g]
        qg = q_ref[0, g].astype(BF16)
        tcols = tbl_ref[g]

        kpos = past + (lane_s - lsel)
        gathered = lane_s < 0
        has_new = jnp.int32(0)
        for k in range(N_SELECT):
            blk = idx_ref[b, g * N_SELECT + k]
            in_slot = (lane_s // PAGE) == k
            kpos = jnp.where(in_slot, (blk // bpp) * PAGE + tok, kpos)
            gathered = gathered | (in_slot & (blk < cur) & ((tok // SLC_BLOCK) == (blk % bpp)))
            has_new = has_new | (blk == cur).astype(jnp.int32)
        smask = (gathered | ((lane_s == lsel) & (has_new > 0))) & (kpos <= past)
        sbias = _bias_rows(_rel_bucket(past - kpos), tcols)
        o_slc = _attend_kt(qg, kbuf[cur_half, g], vbuf[cur_half, g], sbias, smask)

        wbias = _bias_rows(bktw_ref[...], tcols)
        o_win = _attend_kt(qg, kwbuf[g], vwbuf[g], wbias, wmask)

        gt = gt_ref[0, g]
        o_ref[0, g] = gt[:, 0:1] * ocmp_ref[0, g] + gt[:, 1:2] * o_slc + gt[:, 2:3] * o_win


def _dec_sample(idx, page_table, cache_sk, cache_sv, q8, ks_new, vs_new, kw_new, vw_new, state_wk, state_wv,
                gates8, ocmp, tbl8, bktw, past):
    nb, wb = state_wk.shape[0], state_wk.shape[3]
    any_spec = pl.BlockSpec(memory_space=pl.ANY)
    new_spec = pl.BlockSpec((1, N_KV, HEAD_DIM, 1), lambda i, *_: (i, 0, 0, 0))
    st_spec = pl.BlockSpec((1, N_KV, HEAD_DIM, wb), lambda i, *_: (i, 0, 0, 0))
    head_spec = pl.BlockSpec((1, N_KV, Q_ROWS, HEAD_DIM), lambda i, *_: (i, 0, 0, 0))
    cs = lambda shape: pl.BlockSpec(shape, lambda i, *_: (0,) * len(shape))
    lsel = N_SELECT * PAGE
    grid_spec = pltpu.PrefetchScalarGridSpec(
        num_scalar_prefetch=2,
        grid=(nb,),
        in_specs=[any_spec, any_spec, head_spec,
                  new_spec, new_spec, new_spec, new_spec, st_spec, st_spec,
                  pl.BlockSpec((1, N_KV, Q_ROWS, 3), lambda i, *_: (i, 0, 0, 0)), head_spec,
                  cs(tbl8.shape), cs(bktw.shape)],
        out_specs=[head_spec, st_spec, st_spec],
        scratch_shapes=[pltpu.VMEM((2, N_KV, HEAD_DIM, lsel + NEW_LANES), F32),
                        pltpu.VMEM((2, N_KV, HEAD_DIM, lsel + NEW_LANES), F32),
                        pltpu.VMEM((N_KV, HEAD_DIM, wb + NEW_LANES), F32),
                        pltpu.VMEM((N_KV, HEAD_DIM, wb + NEW_LANES), F32),
                        pltpu.SemaphoreType.DMA((2, 2))])
    return pl.pallas_call(
        functools.partial(_dec_s_body, past),
        grid_spec=grid_spec,
        out_shape=[jax.ShapeDtypeStruct((nb, N_KV, Q_ROWS, HEAD_DIM), F32),
                   jax.ShapeDtypeStruct(state_wk.shape, F32),
                   jax.ShapeDtypeStruct(state_wv.shape, F32)],
        compiler_params=_params(("arbitrary",)),
        name="dec_sample",
    )(idx, page_table, cache_sk, cache_sv, q8, ks_new, vs_new, kw_new, vw_new,
      state_wk, state_wv, gates8, ocmp, tbl8, bktw)


def _expand_w1(w1):
    eye = jnp.eye(2, dtype=w1.dtype)
    parts = []
    for part in (w1[:CMP_STRIDE * HEAD_DIM], w1[CMP_STRIDE * HEAD_DIM:]):
        w = part.reshape(CMP_STRIDE, HEAD_DIM, CMP_HID)
        parts.append(jnp.einsum("gh,rdn->rgdhn", eye, w).reshape(CMP_STRIDE, 2 * HEAD_DIM, 2 * CMP_HID))
    w = jnp.concatenate(parts, axis=-1).astype(BF16)
    return w.reshape(CMP_STRIDE * 2 * HEAD_DIM, 4 * CMP_HID)


def _expand_w2(w2):
    eye = jnp.eye(N_KV, dtype=w2.dtype)
    return jnp.einsum("gh,nd->gnhd", eye, w2).reshape(N_KV * CMP_HID, KV_W).astype(BF16)


def _pe_rows(pe):
    return jnp.broadcast_to(pe.reshape(1, -1), (8, pe.size)).astype(BF16)


def _select_matrix(n_cmp, n_blocks):
    r = SLC_BLOCK // CMP_STRIDE
    m = np.zeros((n_cmp, n_blocks), np.float32)
    for s in range(n_blocks):
        for a in range(r):
            for bb in range(2):
                c = r * s + a - bb
                if 0 <= c < n_cmp:
                    m[c, s] += 1.0
    return m


def _pad_to(a, size, axis):
    pad = [(0, 0)] * a.ndim
    pad[axis] = (0, size - a.shape[axis])
    return jnp.pad(a, pad)


def kernel(x_prompt, x_sample, cache_cmp_k, cache_cmp_v, cache_slc_k, cache_slc_v, state_win_k, state_win_v, state_conv, page_table, p_prompt, p_sample, rel_table, norm_mix, norm_ffn, norm_ple, norm_final, conv_w1, conv_b1, conv_dw, conv_dwb, conv_ln_g, conv_ln_b, conv_w2, conv_b2, attn_w_in, attn_w_out, cmpk_w1, cmpk_pe, cmpk_w2, cmpv_w1, cmpv_pe, cmpv_w2, mlp_up, mlp_down, ple_proj, ple_gate):
    nbp, t, _ = x_prompt.shape
    nbs = x_sample.shape[0]
    n_p = nbp * t
    n_pool = cache_cmp_k.shape[1]
    past = page_table.shape[1] * cache_cmp_k.shape[2]
    wb = state_win_k.shape[2]
    hist = CONV_W - 1
    tm_p, tf, tt, tq = 512, 1024, 256, ATTN_TK

    row = lambda a: a.reshape(1, -1)
    bf = lambda a: a.astype(BF16)

    w1c, w2c = bf(conv_w1[0]), bf(conv_w2[0])
    dw3 = conv_dw[0].reshape(CONV_W, D_MODEL // 128, 128).transpose(1, 0, 2)
    dwb3 = conv_dwb[0].reshape(D_MODEL // 128, 1, 128)
    up, dn = bf(mlp_up), bf(mlp_down)
    wg, wp = bf(ple_gate), bf(ple_proj)
    w_in = attn_w_in[0]
    wq = bf(w_in[:, :D_MODEL])
    wkv = bf(w_in[:, D_MODEL:D_MODEL + 6 * KV_W])
    wgt = bf(_pad_to(w_in[:, D_MODEL + 6 * KV_W:], 128, 1))
    w_out = bf(attn_w_out[0])
    cw = []
    for w1, pe, w2 in ((cmpk_w1[0], cmpk_pe[0], cmpk_w2[0]), (cmpv_w1[0], cmpv_pe[0], cmpv_w2[0])):
        cw.append((_expand_w1(w1), _pe_rows(pe), bf(w1), _expand_w2(w2), bf(w2)))
    wk_p, wv_p = [c[:4] for c in cw]
    wk_s, wv_s = [c[:3] + c[4:] for c in cw]

    def tail(x, i, p, tm, final, pre=None):
        return _mlp_ple(x, row(norm_ffn[i]), up, dn, row(norm_ple[i]), wg, p, i, wp,
                        row(norm_final), tm, tf, final, pre)

    conv_args = (row(conv_ln_g[0]), row(conv_ln_b[0]), w2c, row(conv_b2[0]))

    xp = x_prompt.reshape(n_p, D_MODEL)
    glu = _conv_in(xp, row(norm_mix[0]), w1c, row(conv_b1[0]), tm_p).reshape(nbp, t, D_MODEL)
    x1 = _conv_out(glu, x_prompt, dw3, dwb3, *conv_args, tt).reshape(n_p, D_MODEL)
    conv_p = glu[:, t - hist:][None]
    pp = p_prompt.reshape(-1, n_p, D_PLE)
    ps = p_sample.reshape(-1, nbs, D_PLE)
    x2 = tail(x1, 0, pp, tm_p, False)
    q, gates, kc, vc, ks, kw, *rest = _attn_in(x2, row(norm_mix[1]), wq, wkv, wgt, tm_p, seq_len=t)
    kv_fm, win_fm = rest[:len(KV_NAMES)], rest[len(KV_NAMES):]
    vs_fm, vw_fm = kv_fm[KV_NAMES.index("vs")], kv_fm[KV_NAMES.index("vw")]
    seq = lambda a: a.reshape(nbp, t, -1)
    half_rows = lambda a: a.reshape(nbp, 2 * t, 128)
    kcmp, vcmp = _compress_prompt(half_rows(kc), half_rows(vc), wk_p, wv_p)

    ncmp = t // CMP_STRIDE
    ii = jnp.arange(tq, dtype=jnp.int32)[None, :]
    jj = jnp.arange(ATTN_TK, dtype=jnp.int32)[:, None]
    r_qk = tq // ATTN_TK
    bkt_toep = jnp.stack([_rel_bucket(d * ATTN_TK + ii - jj) for d in range(1 - r_qk, 2)])
    nshift = (t // tq - 1) * (tq // CMP_STRIDE)
    rr = jnp.arange(ncmp + nshift, dtype=jnp.int32)[:, None]
    bkt_cmp = _rel_bucket(ii - CMP_STRIDE * (rr - nshift) - (2 * CMP_STRIDE - 1))
    msel_p = jnp.asarray(_select_matrix(ncmp, t // SLC_BLOCK).T, BF16)
    o_p = _attn_prompt(seq(q), seq(gates), seq(ks), vs_fm, seq(kw), vw_fm, kcmp, vcmp,
                       rel_table, bkt_toep, bkt_cmp, msel_p, tq)
    y_p = tail(x2, 1, pp, tm_p, True, pre=(o_p.reshape(n_p, D_MODEL), w_out))

    from_fm = lambda a: jnp.transpose(a, (0, 3, 1, 2))[None]
    kv_p = [from_fm(a) for a in kv_fm[:4]] + [from_fm(a) for a in win_fm]

    xs = x_sample.reshape(nbs, D_MODEL)
    glu_s = _conv_in(xs, row(norm_mix[0]), w1c, row(conv_b1[0]), nbs)
    x1s, conv_s = _conv_step(state_conv[0], glu_s, xs, conv_dw[0], row(conv_dwb[0]), *conv_args)
    x2s = tail(x1s, 0, ps, nbs, False)
    qs, gts, kcs, vcs, kss, vss, kws, vws = _attn_in(x2s, row(norm_mix[1]), wq, wkv, wgt, nbs)

    q8 = _pad_to(qs.reshape(nbs, N_KV, HPG, HEAD_DIM), Q_ROWS, 2)
    tbl8 = _pad_to(rel_table.T.reshape(N_KV, HPG, NUM_BUCKETS), Q_ROWS, 1)
    nc_s = past // CMP_STRIDE
    cc = jnp.arange(nc_s, dtype=jnp.int32)[None, :]
    bkt_s = _rel_bucket(past - (CMP_STRIDE * cc + 2 * CMP_STRIDE - 1))
    n_blocks_s = past // SLC_BLOCK + 1
    msel_s = jnp.asarray(_select_matrix(nc_s, -(-n_blocks_s // 128) * 128), BF16)
    new3 = lambda a: a.reshape(nbs, 1, KV_W)
    fmajor = lambda c: jnp.transpose(c[0], (0, 2, 3, 1))
    cache2 = lambda c: fmajor(c).reshape(n_pool, 2 * PAGE, PAGE)
    ocmp, idx = _cmp_sample(page_table, cache2(cache_cmp_k), cache2(cache_cmp_v), new3(kcs), new3(vcs),
                            q8, wk_s, wv_s, tbl8, bkt_s, msel_s, past)

    ww = jnp.arange(wb + NEW_LANES, dtype=jnp.int32)[None, :]
    bkt_w = _rel_bucket(wb - ww)
    gates8 = _pad_to(gts[:, :3 * N_HEADS].reshape(nbs, 3, N_KV, HPG).transpose(0, 2, 3, 1), Q_ROWS, 2)
    col4 = lambda a: a.reshape(nbs, N_KV, HEAD_DIM, 1)
    o8, wk_new, wv_new = _dec_sample(idx.reshape(nbs, N_KV * N_SELECT), page_table,
                                     fmajor(cache_slc_k), fmajor(cache_slc_v), q8,
                                     col4(kss), col4(vss), col4(kws), col4(vws),
                                     fmajor(state_win_k), fmajor(state_win_v), gates8, ocmp, tbl8, bkt_w, past)
    o_s = o8[:, :, :HPG].reshape(nbs, D_MODEL)
    y_s = tail(x2s, 1, ps, nbs, True, pre=(o_s, w_out))

    kv5s = lambda a: a.reshape(1, nbs, 1, N_KV, HEAD_DIM)
    return (y_p.reshape(nbp, t, D_MODEL), y_s.reshape(nbs, 1, D_MODEL), *kv_p, conv_p,
            kv5s(kcs), kv5s(vcs), kv5s(kss), kv5s(vss), from_fm(wk_new), from_fm(wv_new), conv_s[None])
```

```python
import functools
import math

import numpy as np
import jax
import jax.numpy as jnp
from jax import lax
from jax.experimental import pallas as pl
from jax.experimental.pallas import tpu as pltpu

F32 = jnp.float32
BF16 = jnp.bfloat16

D_MODEL = 1024
D_PLE = 256
CONV_W = 31
N_HEADS = 16
HEAD_DIM = 64
N_KV = 4
HPG = 4
KV_W = N_KV * HEAD_DIM
CMP_STRIDE = 16
CMP_HID = 128
SLC_BLOCK = 64
N_SELECT = 16
WINDOW = 512
NUM_BUCKETS = 32
MAX_DISTANCE = 128
D_FF = 4096
EPS = 1e-6
SCALE = HEAD_DIM ** -0.5
assert math.frexp(SCALE)[0] == 0.5
NEG_INF = float("-inf")

ATTN_TK = 128
assert ATTN_TK >= MAX_DISTANCE

VMEM_LIMIT = 56 * 1024 * 1024
HALO = 32


def _sigmoid(x):
    return 1.0 / (1.0 + jnp.exp(-x))


def _rms(x, g):
    return x * lax.rsqrt(jnp.mean(x * x, axis=-1, keepdims=True) + EPS) * g


def _dot(a, b):
    return jnp.dot(a, b, preferred_element_type=F32)


def _dot_lanes(a, b):
    n = b.shape[1]
    return jnp.concatenate([_dot(a, b[:, c:c + 128]) for c in range(0, n, 128)], axis=1)


def _split3(x):
    hi = x.astype(BF16)
    r1 = x - hi.astype(F32)
    mid = r1.astype(BF16)
    lo = (r1 - mid.astype(F32)).astype(BF16)
    return hi, mid, lo


def _rel_bucket(dist):
    n = jnp.maximum(dist, 0)
    max_exact = NUM_BUCKETS // 2
    nf = jnp.maximum(n, 1).astype(F32)
    large = max_exact + (jnp.log(nf / max_exact) / math.log(MAX_DISTANCE / max_exact)
                         * (NUM_BUCKETS - max_exact)).astype(jnp.int32)
    large = jnp.minimum(large, NUM_BUCKETS - 1)
    return jnp.where(n < max_exact, n, large)


def _params(sem):
    return pltpu.CompilerParams(dimension_semantics=sem, vmem_limit_bytes=VMEM_LIMIT)


def _const_spec(shape, single=False):
    n = len(shape)
    if single:
        return pl.BlockSpec(shape, lambda *_: (0,) * n, pipeline_mode=pl.Buffered(1))
    return pl.BlockSpec(shape, lambda *_: (0,) * n)


def _conv_in_body(x_ref, g_ref, w_ref, b_ref, o_ref):
    h = _rms(x_ref[...], g_ref[...]).astype(BF16)
    u = _dot(h, w_ref[...]) + b_ref[...]
    o_ref[...] = u[:, :D_MODEL] * _sigmoid(u[:, D_MODEL:])


def _conv_in(x, g, w1, b1, tm):
    n = x.shape[0]
    return pl.pallas_call(
        _conv_in_body,
        grid=(n // tm,),
        in_specs=[pl.BlockSpec((tm, D_MODEL), lambda i: (i, 0)),
                  _const_spec((1, D_MODEL)),
                  _const_spec((D_MODEL, 2 * D_MODEL)),
                  _const_spec((1, 2 * D_MODEL))],
        out_specs=pl.BlockSpec((tm, D_MODEL), lambda i: (i, 0)),
        out_shape=jax.ShapeDtypeStruct((n, D_MODEL), F32),
        compiler_params=_params(("parallel",)),
        name="conv_in",
    )(x, g, w1, b1)


def _ln_silu_proj(y, lng, lnb, w2, b2, x):
    mu = jnp.mean(y, axis=-1, keepdims=True)
    yc = y - mu
    var = jnp.mean(yc * yc, axis=-1, keepdims=True)
    z = yc * lax.rsqrt(var + EPS) * lng + lnb
    z = z * _sigmoid(z)
    return _dot(z.astype(BF16), w2) + b2 + x


def _conv_out_body(tt, cur_ref, halo_ref, x_ref, dw_ref, dwb_ref, lng_ref, lnb_ref, w2_ref, b2_ref,
                   o_ref, ctx_ref, y_ref):
    i = pl.program_id(1)
    n_strip = D_MODEL // 128
    keep = (i > 0).astype(F32)
    for c in range(n_strip):
        ctx_ref[c, 0:HALO, :] = halo_ref[0, :, 128 * c:128 * (c + 1)] * keep
        ctx_ref[c, HALO:, :] = cur_ref[0, :, 128 * c:128 * (c + 1)]

    def strip(c, carry):
        acc = jnp.broadcast_to(dwb_ref[c], (tt, 128))
        for k in range(CONV_W):
            off = k + HALO - (CONV_W - 1)
            acc = acc + ctx_ref[c, off:off + tt, :] * dw_ref[c, k:k + 1, :]
        y_ref[c] = acc
        return carry

    lax.fori_loop(0, n_strip, strip, 0)
    y = jnp.concatenate([y_ref[c] for c in range(n_strip)], axis=1)
    o_ref[0] = _ln_silu_proj(y, lng_ref[...], lnb_ref[...], w2_ref[...], b2_ref[...], x_ref[0])


def _conv_out(glu, x, dw3, dwb3, lng, lnb, w2, b2, tt):
    b, t, _ = glu.shape
    hb = tt // HALO
    return pl.pallas_call(
        functools.partial(_conv_out_body, tt),
        grid=(b, t // tt),
        in_specs=[pl.BlockSpec((1, tt, D_MODEL), lambda bi, i: (bi, i, 0)),
                  pl.BlockSpec((1, HALO, D_MODEL), lambda bi, i: (bi, jnp.maximum(i * hb - 1, 0), 0)),
                  pl.BlockSpec((1, tt, D_MODEL), lambda bi, i: (bi, i, 0)),
                  _const_spec((D_MODEL // 128, CONV_W, 128)),
                  _const_spec((D_MODEL // 128, 1, 128)),
                  _const_spec((1, D_MODEL)),
                  _const_spec((1, D_MODEL)),
                  _const_spec((D_MODEL, D_MODEL)),
                  _const_spec((1, D_MODEL))],
        out_specs=pl.BlockSpec((1, tt, D_MODEL), lambda bi, i: (bi, i, 0)),
        out_shape=jax.ShapeDtypeStruct((b, t, D_MODEL), F32),
        scratch_shapes=[pltpu.VMEM((D_MODEL // 128, tt + HALO, 128), F32),
                        pltpu.VMEM((D_MODEL // 128, tt, 128), F32)],
        compiler_params=_params(("parallel", "arbitrary")),
        name="conv_out",
    )(glu, glu, x, dw3, dwb3, lng, lnb, w2, b2)


def _conv_step_body(st_ref, u_ref, x_ref, dw_ref, dwb_ref, lng_ref, lnb_ref, w2_ref, b2_ref,
                    o_ref, ns_ref):
    nb = st_ref.shape[0]
    hist = CONV_W - 1
    dwh = dw_ref[0:hist, :]
    rows = []
    for bi in range(nb):
        rows.append(jnp.sum(st_ref[bi] * dwh, axis=0, keepdims=True))
        ns_ref[bi, 0:hist - 1, :] = st_ref[bi, 1:hist, :]
        ns_ref[bi, hist - 1:hist, :] = u_ref[bi:bi + 1, :]
    y = jnp.concatenate(rows, axis=0) + u_ref[...] * dw_ref[hist:hist + 1, :] + dwb_ref[...]
    o_ref[...] = _ln_silu_proj(y, lng_ref[...], lnb_ref[...], w2_ref[...], b2_ref[...], x_ref[...])


def _conv_step(state, glu, x, dw, dwb, lng, lnb, w2, b2):
    nb, hist, _ = state.shape
    return pl.pallas_call(
        _conv_step_body,
        grid=(1,),
        in_specs=[_const_spec((nb, hist, D_MODEL)), _const_spec((nb, D_MODEL)), _const_spec((nb, D_MODEL)),
                  _const_spec((CONV_W, D_MODEL)), _const_spec((1, D_MODEL)), _const_spec((1, D_MODEL)),
                  _const_spec((1, D_MODEL)), _const_spec((D_MODEL, D_MODEL)), _const_spec((1, D_MODEL))],
        out_specs=[_const_spec((nb, D_MODEL)), _const_spec((nb, hist, D_MODEL))],
        out_shape=[jax.ShapeDtypeStruct((nb, D_MODEL), F32),
                   jax.ShapeDtypeStruct((nb, hist, D_MODEL), F32)],
        compiler_params=_params(("arbitrary",)),
        name="conv_step",
    )(state, glu, x, dw, dwb, lng, lnb, w2, b2)


def _mlp_body(final, pre, *refs):
    if pre:
        o_in_ref, wo_ref, refs = refs[0], refs[1], refs[2:]
    (x_ref, gf_ref, up_ref, dn_ref, gp_ref, wg_ref, p_ref, wp_ref, gfin_ref,
     o_ref, h_ref, acc_ref, x1_ref) = refs
    j = pl.program_id(1)

    @pl.when(j == 0)
    def _():
        x1 = x_ref[...]
        if pre:
            x1 = x1 + _dot(o_in_ref[...].astype(BF16), wo_ref[...])
        x1_ref[...] = x1
        h_ref[...] = _rms(x1, gf_ref[...]).astype(BF16)
        acc_ref[...] = jnp.zeros_like(acc_ref)

    a = jnp.maximum(_dot(h_ref[...], up_ref[...]), 0.0)
    acc_ref[...] += _dot((a * a).astype(BF16), dn_ref[...])

    @pl.when(j == pl.num_programs(1) - 1)
    def _():
        x2 = x1_ref[...] + acc_ref[...]
        gate = _sigmoid(_dot(_rms(x2, gp_ref[...]).astype(BF16), wg_ref[...]))
        x3 = x2 + gate * _dot(p_ref[...].astype(BF16), wp_ref[...])
        if final:
            x3 = _rms(x3, gfin_ref[...])
        o_ref[...] = x3


def _mlp_ple(x, gf, up, dn, gp, wg, p, layer, wp, gfin, tm, tf, final, pre=None):
    n = x.shape[0]
    tok = lambda i, j: (i, 0)
    in_specs, args = [], []
    if pre is not None:
        in_specs += [pl.BlockSpec((tm, D_MODEL), tok), _const_spec((D_MODEL, D_MODEL))]
        args += list(pre)
    in_specs += [pl.BlockSpec((tm, D_MODEL), tok),
                 _const_spec((1, D_MODEL)),
                 pl.BlockSpec((None, D_MODEL, tf), lambda i, j: (layer, 0, j)),
                 pl.BlockSpec((None, tf, D_MODEL), lambda i, j: (layer, j, 0)),
                 _const_spec((1, D_MODEL)),
                 pl.BlockSpec((None, D_MODEL, D_MODEL), lambda i, j: (layer, 0, 0)),
                 pl.BlockSpec((None, tm, D_PLE), lambda i, j: (layer, i, 0)),
                 pl.BlockSpec((None, D_PLE, D_MODEL), lambda i, j: (layer, 0, 0)),
                 _const_spec((1, D_MODEL))]
    args += [x, gf, up, dn, gp, wg, p, wp, gfin]
    return pl.pallas_call(
        functools.partial(_mlp_body, final, pre is not None),
        grid=(n // tm, D_FF // tf),
        in_specs=in_specs,
        out_specs=pl.BlockSpec((tm, D_MODEL), tok),
        out_shape=jax.ShapeDtypeStruct((n, D_MODEL), F32),
        scratch_shapes=[pltpu.VMEM((tm, D_MODEL), BF16), pltpu.VMEM((tm, D_MODEL), F32),
                        pltpu.VMEM((tm, D_MODEL), F32)],
        compiler_params=_params(("parallel", "arbitrary")),
        name="mlp_ple",
    )(*args)


KV_NAMES = ("kc", "vc", "ks", "vs", "kw", "vw")
KV_COMPRESSED = ("kc", "vc")
KV_TOKEN_MAJOR = ("ks", "kw")
KV_WINDOWED = ("kw", "vw")


def _attn_in_body(seq_len, x_ref, g_ref, wq_ref, wkv_ref, wg_ref, q_ref, gt_ref, *kv_refs):
    h = _rms(x_ref[...], g_ref[...]).astype(BF16)
    q_ref[...] = _dot(h, wq_ref[...])
    gt_ref[...] = _sigmoid(_dot(h, wg_ref[...]))
    kv = _dot(h, wkv_ref[...])
    blocks = {n: kv[:, KV_W * i:KV_W * (i + 1)] for i, n in enumerate(KV_NAMES)}
    if seq_len is None:
        for n, r in zip(KV_NAMES, kv_refs):
            r[...] = blocks[n]
        return
    n_half = 2 * len(KV_COMPRESSED)
    n_tm = n_half + len(KV_TOKEN_MAJOR)
    half_refs = kv_refs[:n_half]
    tm_refs = kv_refs[n_half:n_tm]
    fm_refs = kv_refs[n_tm:n_tm + len(KV_NAMES)]
    last_refs = dict(zip(KV_WINDOWED, kv_refs[n_tm + len(KV_NAMES):]))
    for i, n in enumerate(KV_COMPRESSED):
        half_refs[2 * i][...] = blocks[n][:, :128]
        half_refs[2 * i + 1][...] = blocks[n][:, 128:]
    for n, r in zip(KV_TOKEN_MAJOR, tm_refs):
        r[...] = blocks[n]
    tps = seq_len // x_ref.shape[0]
    is_last = pl.program_id(0) % tps == tps - 1
    for n, r in zip(KV_NAMES, fm_refs):
        bt = blocks[n].T
        for g in range(N_KV):
            r[0, g] = bt[HEAD_DIM * g:HEAD_DIM * (g + 1), :]
        if n in last_refs:

            @pl.when(is_last)
            def _(bt=bt, dst=last_refs[n]):
                for g in range(N_KV):
                    dst[0, g] = bt[HEAD_DIM * g:HEAD_DIM * (g + 1), :]


def _attn_in(x, g, wq, wkv, wg, tm, seq_len=None):
    n = x.shape[0]
    tok = lambda i: (i, 0)
    kv_spec = pl.BlockSpec((tm, KV_W), tok)
    kv_shape = jax.ShapeDtypeStruct((n, KV_W), F32)
    if seq_len is None:
        kv_specs, kv_shapes = [kv_spec] * 6, [kv_shape] * 6
    else:
        tps = seq_len // tm
        fm_spec = pl.BlockSpec((1, N_KV, HEAD_DIM, tm), lambda i: (i // tps, 0, 0, i % tps))
        fm_shape = jax.ShapeDtypeStruct((n // seq_len, N_KV, HEAD_DIM, seq_len), F32)
        assert tm == min(WINDOW, seq_len)
        last_spec = pl.BlockSpec((1, N_KV, HEAD_DIM, tm), lambda i: (i // tps, 0, 0, 0))
        last_shape = jax.ShapeDtypeStruct((n // seq_len, N_KV, HEAD_DIM, tm), F32)
        half_spec = pl.BlockSpec((tm, 128), tok)
        half_shape = jax.ShapeDtypeStruct((n, 128), F32)
        n_half = 2 * len(KV_COMPRESSED)
        kv_specs = ([half_spec] * n_half + [kv_spec] * len(KV_TOKEN_MAJOR) + [fm_spec] * 6
                    + [last_spec] * len(KV_WINDOWED))
        kv_shapes = ([half_shape] * n_half + [kv_shape] * len(KV_TOKEN_MAJOR) + [fm_shape] * 6
                     + [last_shape] * len(KV_WINDOWED))
    return pl.pallas_call(
        functools.partial(_attn_in_body, seq_len),
        grid=(n // tm,),
        in_specs=[pl.BlockSpec((tm, D_MODEL), tok), _const_spec((1, D_MODEL)),
                  _const_spec((D_MODEL, D_MODEL)), _const_spec((D_MODEL, 6 * KV_W)),
                  _const_spec((D_MODEL, 128))],
        out_specs=[pl.BlockSpec((tm, D_MODEL), tok), pl.BlockSpec((tm, 128), tok)] + kv_specs,
        out_shape=[jax.ShapeDtypeStruct((n, D_MODEL), F32), jax.ShapeDtypeStruct((n, 128), F32)] + kv_shapes,
        compiler_params=_params(("arbitrary",)),
        name="attn_in",
    )(x, g, wq, wkv, wg)


def _compress_hidden(load_rows, n_chunks, wexp_ref):
    lo = jnp.concatenate([load_rows(2 * r) for r in range(CMP_STRIDE)], axis=1)
    hi = jnp.concatenate([load_rows(2 * r + 1) for r in range(CMP_STRIDE)], axis=1)
    acc = _dot(jnp.concatenate([lo, hi], axis=0).astype(BF16), wexp_ref[...])
    lo, hi = acc[:n_chunks], acc[n_chunks:]
    w = 2 * CMP_HID
    return jnp.concatenate([lo[:, :w], hi[:, :w], lo[:, w:], hi[:, w:]], axis=1)


def _pe_term(pe_ref, w1_ref):
    t = _dot(pe_ref[...], w1_ref[...])[0:1, :]
    return jnp.concatenate([t] * N_KV, axis=1)


def _compress_p_body(nc, kcl_ref, kch_ref, vcl_ref, vch_ref, wek_ref, pek_ref, w1k_ref, w2k_ref,
                     wev_ref, pev_ref, w1v_ref, w2v_ref, ok_ref, ov_ref):
    half = N_KV * CMP_HID
    row = lax.broadcasted_iota(jnp.int32, (nc, KV_W), 0)
    for src, we, pe, w1, w2, out in (((kcl_ref, kch_ref), wek_ref, pek_ref, w1k_ref, w2k_ref, ok_ref),
                                     ((vcl_ref, vch_ref), wev_ref, pev_ref, w1v_ref, w2v_ref, ov_ref)):
        hh = _compress_hidden(lambda k: src[k % 2][0, pl.ds(k // 2, nc, stride=CMP_STRIDE), :], nc, we)
        nxt = pltpu.roll(hh[:, half:], nc - 1, 0)
        a = hh[:, :half] + nxt + _pe_term(pe, w1)
        a = a * _sigmoid(a)
        res = _dot(a.astype(BF16), w2[...])
        out[0] = jnp.where(row < nc - 1, res, 0.0)


def _compress_prompt(kc_halves, vc_halves, wk, wv):
    b, t, _ = kc_halves[0].shape
    nc = t // CMP_STRIDE
    seq = pl.BlockSpec((1, t, 128), lambda i: (i, 0, 0))
    wspecs = [_const_spec((CMP_STRIDE * 128, 4 * CMP_HID)),
              _const_spec((8, 2 * CMP_STRIDE * HEAD_DIM)),
              _const_spec((2 * CMP_STRIDE * HEAD_DIM, CMP_HID)), _const_spec((N_KV * CMP_HID, KV_W))]
    out = pl.BlockSpec((1, nc, KV_W), lambda i: (i, 0, 0))
    return pl.pallas_call(
        functools.partial(_compress_p_body, nc),
        grid=(b,),
        in_specs=[seq] * 4 + wspecs + wspecs,
        out_specs=[out, out],
        out_shape=[jax.ShapeDtypeStruct((b, nc, KV_W), F32)] * 2,
        compiler_params=_params(("parallel",)),
        name="compress_prompt",
    )(*kc_halves, *vc_halves, *wk, *wv)


def _softmax_cols(s, mask):
    s = jnp.where(mask, s, NEG_INF)
    m = jnp.max(s, axis=0, keepdims=True)
    m = jnp.where(m == NEG_INF, 0.0, m)
    e = jnp.where(mask, jnp.exp(s - m), 0.0)
    return e * (1.0 / jnp.maximum(jnp.sum(e, axis=0, keepdims=True), 1e-30))


def _attn_p_body(tq, t, q_ref, gt_ref, ks_ref, vs_ref, kw_ref, vw_ref, kcm_ref, vcm_ref,
                 tbl_ref, bt_ref, bc_ref, msel_ref, o_ref,
                 ksb, kwb, vst, vwt, kcb, vct, btoep, bcmp, qt, selt, ot, m_ref, l_ref, acc_ref, gt_s):
    b = pl.program_id(0)
    qi = pl.program_id(1)
    tk = ATTN_TK
    nkt = t // tk
    nqt = t // tq
    r_qk = tq // tk
    ncmp = kcm_ref.shape[1]
    q0 = qi * tq
    n_near = bt_ref.shape[0]

    @pl.when((b == 0) & (qi == 0))
    def _():
        def per_head(h, carry):
            far = tbl_ref[NUM_BUCKETS - 1, h]
            for d in range(n_near):
                bk = bt_ref[d]
                acc = jnp.zeros((tk, tq), F32)
                for bb in range(NUM_BUCKETS):
                    acc = jnp.where(bk == bb, tbl_ref[bb, h] - far, acc)
                btoep[h, d] = acc
            btoep[h, n_near] = jnp.zeros((tk, tq), F32)
            bk = bc_ref[...]
            acc = jnp.zeros(bk.shape, F32)
            for bb in range(NUM_BUCKETS):
                acc = jnp.where(bk == bb, tbl_ref[bb, h], acc)
            bcmp[h] = acc
            return carry
        lax.fori_loop(0, N_HEADS, per_head, 0)

    @pl.when(qi == 0)
    def _():
        for kt in range(nkt):
            rows = slice(kt * tk, (kt + 1) * tk)
            for src, dstk in ((ks_ref, ksb), (kw_ref, kwb)):
                blk = src[0, rows, :]
                for g in range(N_KV):
                    dstk[g, rows, :] = blk[:, HEAD_DIM * g:HEAD_DIM * (g + 1)].astype(BF16)
            for src, dstv in ((vs_ref, vst), (vw_ref, vwt)):
                for g in range(N_KV):
                    dstv[g, kt] = src[0, g, :, rows].astype(BF16)
        kc = kcm_ref[0]
        vc_t = vcm_ref[0].T
        for g in range(N_KV):
            kcb[g] = kc[:, HEAD_DIM * g:HEAD_DIM * (g + 1)].astype(BF16)
            vct[g] = vc_t[HEAD_DIM * g:HEAD_DIM * (g + 1), :].astype(BF16)

    qt[...] = (q_ref[0].T * SCALE).astype(BF16)
    gt_s[...] = gt_ref[0].T

    qpos_row = q0 + lax.broadcasted_iota(jnp.int32, (1, tq), 1)
    i_iota = lax.broadcasted_iota(jnp.int32, (tk, tq), 1)
    j_iota = lax.broadcasted_iota(jnp.int32, (tk, tq), 0)
    rel = i_iota - j_iota

    c_iota = lax.broadcasted_iota(jnp.int32, (ncmp, tq), 0)
    cmask = (CMP_STRIDE * c_iota + (2 * CMP_STRIDE - 1)) <= qpos_row
    coff = pl.multiple_of((nqt - 1 - qi) * (tq // CMP_STRIDE), 8)

    nblk = t // SLC_BLOCK
    blk_iota = lax.broadcasted_iota(jnp.int32, (nblk, tq), 0)
    cur = qpos_row // SLC_BLOCK
    valid = blk_iota * SLC_BLOCK <= qpos_row
    forced = (blk_iota == 0) | (blk_iota == cur) | (blk_iota == cur - 1)

    heads = range(N_HEADS)
    head_rows = [slice(HEAD_DIM * h, HEAD_DIM * (h + 1)) for h in heads]
    s_c = [_dot(kcb[h // HPG], qt[head_rows[h], :]) + bcmp[h, pl.ds(coff, ncmp), :] for h in heads]
    p_c = [_softmax_cols(s, cmask) for s in s_c]
    o_c = [_dot(vct[h // HPG], p_c[h].astype(BF16)) for h in heads]
    for h in heads:
        ot[head_rows[h], :] = gt_s[h:h + 1, :] * o_c[h]
    msel = msel_ref[...]
    scores = []
    for g in range(N_KV):
        pg = p_c[g * HPG]
        for hh in range(1, HPG):
            pg = pg + p_c[g * HPG + hh]
        hi, mid, lo = _split3(pg)
        score = _dot(msel, hi) + _dot(msel, mid) + _dot(msel, lo)
        scores.append(jnp.where(forced, jnp.inf, jnp.where(valid, score, NEG_INF)))
    n_slab = nblk // 8
    slabs = [[sc[8 * v:8 * (v + 1), :] for v in range(n_slab)] for sc in scores]
    ranks = [[jnp.zeros((8, tq), F32) for _ in range(n_slab)] for _ in scores]
    row8 = lax.broadcasted_iota(jnp.int32, (8, tq), 0)
    for i in range(nblk):
        for g in range(N_KV):
            si = scores[g][i:i + 1, :]
            for v, sl in enumerate(slabs[g]):
                if 8 * v > i:
                    beats = si >= sl
                elif 8 * v + 7 <= i:
                    beats = si > sl
                else:
                    beats = (si > sl) | ((row8 > i - 8 * v) & (si == sl))
                ranks[g][v] = ranks[g][v] + jnp.where(beats, 1.0, 0.0)
    for g in range(N_KV):
        for v, rk in enumerate(ranks[g]):
            sel = (rk < N_SELECT).astype(F32)
            for r in range(8):
                selt[g, 8 * v + r] = sel[r:r + 1, :]

    def flash_init(br):
        m_ref[br] = jnp.full(m_ref.shape[1:], NEG_INF, F32)
        l_ref[br] = jnp.zeros(l_ref.shape[1:], F32)
        acc_ref[br] = jnp.zeros(acc_ref.shape[1:], F32)

    def tile_step(br, kb, vt, kt, bias_d, add_fn):
        k0 = pl.multiple_of(kt * tk, tk)
        for g in range(N_KV):
            kblk = kb[g, pl.ds(k0, tk), :]
            vblk = vt[g, kt]
            add = add_fn(g)
            for hh in range(HPG):
                h = g * HPG + hh
                rows = slice(HEAD_DIM * h, HEAD_DIM * (h + 1))
                s = _dot(kblk, qt[rows, :])
                if bias_d is not None:
                    s = s + btoep[h, bias_d]
                if add is not None:
                    s = s + add
                m_old = m_ref[br, h]
                m_new = jnp.maximum(m_old, jnp.max(s, axis=0, keepdims=True))
                m_safe = jnp.where(m_new == NEG_INF, 0.0, m_new)
                alpha = jnp.exp(m_old - m_safe)
                p = jnp.exp(s - m_safe)
                l_ref[br, h] = alpha * l_ref[br, h] + jnp.sum(p, axis=0, keepdims=True)
                acc_ref[br, rows, :] = alpha * acc_ref[br, rows, :] + _dot(vblk, p.astype(BF16))
                m_ref[br, h] = m_new

    def flash_finish(br, gate_row0):
        for h in range(N_HEADS):
            rows = slice(HEAD_DIM * h, HEAD_DIM * (h + 1))
            w = gt_s[gate_row0 + h:gate_row0 + h + 1, :] * (1.0 / jnp.maximum(l_ref[br, h], 1e-30))
            ot[rows, :] = ot[rows, :] + acc_ref[br, rows, :] * w

    def sel_rows(g, kt, ok=None):
        nb = tk // SLC_BLOCK
        rows = [jnp.broadcast_to(selt[g, kt * nb + r], (SLC_BLOCK, tq)) for r in range(nb)]
        thr = 0.5 if ok is None else jnp.where(ok, 0.5, 2.0)
        return jnp.concatenate(rows, axis=0) > thr

    def neg_unless(cond):
        return jnp.where(cond, 0.0, NEG_INF)

    causal = rel >= 0
    SLC, WIN = 0, 1

    flash_init(SLC)
    flash_init(WIN)
    n_far = jnp.maximum(qi - 1, 0)

    def slc_pair(pi, carry):
        for sub in range(2):
            kt = 2 * pi + sub
            ok = kt < n_far
            ktc = jnp.minimum(kt, jnp.maximum(n_far - 1, 0))
            tile_step(SLC, ksb, vst, ktc, None, lambda g: neg_unless(sel_rows(g, ktc, ok)))
        return carry

    lax.fori_loop(0, (n_far + 1) // 2, slc_pair, 0)
    kt1 = jnp.maximum(qi - 1, 0)
    tile_step(SLC, ksb, vst, kt1, 1, lambda g: neg_unless(sel_rows(g, kt1, qi >= 1)))
    tile_step(SLC, ksb, vst, qi, 0, lambda g: neg_unless(sel_rows(g, qi) & causal))

    wt = WINDOW // tk
    for u in range(wt + 1):
        off = wt - u
        kt = qi - off
        ok_add = jnp.where(kt >= 0, 0.0, NEG_INF)
        ktc = jnp.maximum(kt, 0)
        if off == wt:
            add_fn = lambda g, a=ok_add: neg_unless(rel <= 0) + a
        elif off == 0:
            add_fn = lambda g: neg_unless(causal)
        else:
            add_fn = lambda g, a=ok_add: a
        tile_step(WIN, kwb, vwt, ktc, off if off <= 1 else None, add_fn)
    flash_finish(SLC, N_HEADS)
    flash_finish(WIN, 2 * N_HEADS)

    o_ref[0] = ot[...].T


def _attn_prompt(q, gates, ks, vs, kw, vw, kcmp, vcmp, table, bkt_toep, bkt_cmp, msel, tq):
    b, t, _ = q.shape
    tk = ATTN_TK
    nkt = t // tk
    ncmp = kcmp.shape[1]
    n_near = bkt_toep.shape[0]
    qspec = pl.BlockSpec((1, tq, D_MODEL), lambda bi, i: (bi, i, 0))
    seq = pl.BlockSpec((1, t, KV_W), lambda bi, i: (bi, 0, 0))
    seq_fm = pl.BlockSpec((1, N_KV, HEAD_DIM, t), lambda bi, i: (bi, 0, 0, 0))
    cmp_spec = pl.BlockSpec((1, ncmp, KV_W), lambda bi, i: (bi, 0, 0))
    return pl.pallas_call(
        functools.partial(_attn_p_body, tq, t),
        grid=(b, t // tq),
        in_specs=[qspec, pl.BlockSpec((1, tq, 128), lambda bi, i: (bi, i, 0)),
                  seq, seq_fm, seq, seq_fm, cmp_spec, cmp_spec,
                  pl.BlockSpec(memory_space=pltpu.SMEM),
                  _const_spec(bkt_toep.shape), _const_spec(bkt_cmp.shape), _const_spec(msel.shape)],
        out_specs=qspec,
        out_shape=jax.ShapeDtypeStruct((b, t, D_MODEL), F32),
        scratch_shapes=[pltpu.VMEM((N_KV, t, HEAD_DIM), BF16), pltpu.VMEM((N_KV, t, HEAD_DIM), BF16),
                        pltpu.VMEM((N_KV, nkt, HEAD_DIM, tk), BF16), pltpu.VMEM((N_KV, nkt, HEAD_DIM, tk), BF16),
                        pltpu.VMEM((N_KV, ncmp, HEAD_DIM), BF16), pltpu.VMEM((N_KV, HEAD_DIM, ncmp), BF16),
                        pltpu.VMEM((N_HEADS, n_near + 1, tk, tq), F32),
                        pltpu.VMEM((N_HEADS,) + bkt_cmp.shape, F32),
                        pltpu.VMEM((D_MODEL, tq), BF16), pltpu.VMEM((N_KV, t // SLC_BLOCK, 1, tq), F32),
                        pltpu.VMEM((D_MODEL, tq), F32),
                        pltpu.VMEM((2, N_HEADS, 1, tq), F32), pltpu.VMEM((2, N_HEADS, 1, tq), F32),
                        pltpu.VMEM((2, D_MODEL, tq), F32), pltpu.VMEM((128, tq), F32)],
        compiler_params=_params(("arbitrary", "arbitrary")),
        name="attn_prompt",
    )(q, gates, ks, vs, kw, vw, kcmp, vcmp, table, bkt_toep, bkt_cmp, msel)


PAGE = 128
PAGES_PER_GROUP = 32
Q_ROWS = 8


def _rank_desc(score_row, n):
    a = jnp.broadcast_to(score_row, (n, n))
    at = a.T
    i = lax.broadcasted_iota(jnp.int32, (n, n), 0)
    j = lax.broadcasted_iota(jnp.int32, (n, n), 1)
    beats = (at > a) | ((i < j) & (at == a))
    return jnp.sum(beats.astype(F32), axis=0, keepdims=True)


def _bias_rows(bucket, tcols):
    r = tcols.shape[0]
    out = jnp.zeros((r, bucket.shape[1]), F32)
    for bb in range(NUM_BUCKETS):
        out = jnp.where(bucket == bb, tcols[:, bb:bb + 1], out)
    return out


def _softmax_rows(s, mask):
    s = jnp.where(mask, s, NEG_INF)
    m = jnp.max(s, axis=1, keepdims=True)
    m = jnp.where(m == NEG_INF, 0.0, m)
    e = jnp.where(mask, jnp.exp(s - m), 0.0)
    return e / jnp.maximum(jnp.sum(e, axis=1, keepdims=True), 1e-30)


def _dot_nt(a, b):
    return lax.dot_general(a, b, (((1,), (1,)), ((), ())), preferred_element_type=F32)


def _cmp_s_body(past, pt_ref, ck_hbm, cv_hbm, kcn_ref, vcn_ref, q_ref,
                wek_ref, pek_ref, w1k_ref, w2k_ref, wev_ref, pev_ref, w1v_ref, w2v_ref,
                tbl_ref, bkt_ref, msel_ref, ocmp_ref, idx_ref,
                kbuf, vbuf, tbuf, hk, hv, sem):
    b = pl.program_id(0)
    rows_pg = PAGES_PER_GROUP * PAGE
    cpg = rows_pg // CMP_STRIDE
    ngrp = past // rows_pg
    nc = past // CMP_STRIDE
    half = N_KV * CMP_HID

    def copies(bb, gi, slot):
        out = []
        for p in range(PAGES_PER_GROUP):
            page = pt_ref[bb, gi * PAGES_PER_GROUP + p]
            out.append(pltpu.make_async_copy(ck_hbm.at[page], kbuf.at[slot, p], sem.at[0, slot]))
            out.append(pltpu.make_async_copy(cv_hbm.at[page], vbuf.at[slot, p], sem.at[1, slot]))
        return out

    assert ngrp % 2 == 0

    @pl.when(b == 0)
    def _():
        for c in copies(b, 0, 0):
            c.start()

    def group(gi, carry):
        slot = gi % 2

        @pl.when(gi + 1 < ngrp)
        def _():
            for c in copies(b, gi + 1, 1 - slot):
                c.start()

        for c in copies(b, gi, slot):
            c.wait()
        r0 = pl.multiple_of(gi * cpg, cpg)
        for ti, (buf, we, hdst) in enumerate(((kbuf, wek_ref, hk), (vbuf, wev_ref, hv))):
            for p in range(PAGES_PER_GROUP):
                for hf in range(2):
                    tbuf[ti, hf, p * PAGE:(p + 1) * PAGE, :] = buf[slot, p, hf * 128:(hf + 1) * 128, :].T
            hdst[pl.ds(r0, cpg), :] = _compress_hidden(
                lambda k, ti=ti: tbuf[ti, k % 2, pl.ds(k // 2, cpg, stride=CMP_STRIDE), :], cpg, we)
        return carry

    lax.fori_loop(0, ngrp, group, 0)

    @pl.when(b + 1 < pl.num_programs(0))
    def _():
        for c in copies(b + 1, 0, 0):
            c.start()

    col = lax.broadcasted_iota(jnp.int32, (1, nc), 1)
    cmask = (CMP_STRIDE * col + (2 * CMP_STRIDE - 1)) <= past
    row8 = lax.broadcasted_iota(jnp.int32, (Q_ROWS, nc), 0)
    cmpd = []
    for new_ref, we, pe, w1, w2, hdst in ((kcn_ref, wek_ref, pek_ref, w1k_ref, w2k_ref, hk),
                                          (vcn_ref, wev_ref, pev_ref, w1v_ref, w2v_ref, hv)):
        new = jnp.broadcast_to(new_ref[0], (8, KV_W)).astype(BF16)
        new_lo = _dot(new[:, :128], we[0:128, :])
        new_hi = _dot(new[:, 128:], we[0:128, :])
        w = 2 * CMP_HID
        hdst[nc:nc + 8, :] = jnp.concatenate([new_lo[:, :w], new_hi[:, :w], new_lo[:, w:], new_hi[:, w:]], axis=1)
        a = hdst[0:nc, 0:half] + hdst[1:nc + 1, half:2 * half] + _pe_term(pe, w1)
        a = (a * _sigmoid(a)).astype(BF16)
        cmpd.append([_dot(a[:, CMP_HID * g:CMP_HID * (g + 1)], w2[...]).astype(BF16) for g in range(N_KV)])

    nsel = msel_ref.shape[1]
    lane = lax.broadcasted_iota(jnp.int32, (1, nsel), 1)
    n_blocks = past // SLC_BLOCK + 1
    cur = past // SLC_BLOCK
    forced = (lane == 0) | (lane == cur) | (lane == cur - 1)
    valid = (lane * SLC_BLOCK <= past) & (lane < n_blocks)
    k_iota = lax.broadcasted_iota(jnp.int32, (N_SELECT, nsel), 0).astype(F32)
    lane_f = lax.broadcasted_iota(jnp.int32, (N_SELECT, nsel), 1).astype(F32)
    groups = range(N_KV)
    msel = msel_ref[...]
    bkt = bkt_ref[...]
    s_g = [_dot_nt(q_ref[0, g].astype(BF16), cmpd[0][g]) * SCALE + _bias_rows(bkt, tbl_ref[g]) for g in groups]
    p_g = [_softmax_rows(s, cmask) for s in s_g]
    for g in groups:
        ocmp_ref[0, g] = _dot(p_g[g].astype(BF16), cmpd[1][g])
    pgs = [jnp.sum(jnp.where(row8 < HPG, p, 0.0), axis=0, keepdims=True) for p in p_g]
    parts = [_split3(jnp.broadcast_to(pg, (8, nc))) for pg in pgs]
    scores = [(_dot(hi, msel) + _dot(mid, msel) + _dot(lo, msel))[0:1, :] for hi, mid, lo in parts]
    scores = [jnp.where(forced, jnp.inf, jnp.where(valid, sc, NEG_INF)) for sc in scores]
    rank_g = [_rank_desc(sc, nsel) for sc in scores]
    hits = [jnp.where(jnp.broadcast_to(rk, (N_SELECT, nsel)) == k_iota, lane_f, 0.0) for rk in rank_g]
    for g in groups:
        idx_ref[0, g] = jnp.sum(hits[g], axis=1, keepdims=True).astype(jnp.int32)


def _cmp_sample(page_table, cache_k, cache_v, kc_new, vc_new, q8, wk, wv, tbl8, bkt, msel, past):
    nb = page_table.shape[0]
    rows_pg = PAGES_PER_GROUP * PAGE
    nc = past // CMP_STRIDE
    any_spec = pl.BlockSpec(memory_space=pl.ANY)
    new_spec = pl.BlockSpec((1, 1, KV_W), lambda i, pt: (i, 0, 0))
    cs = lambda shape: pl.BlockSpec(shape, lambda i, pt: (0,) * len(shape))
    wspecs = [cs((CMP_STRIDE * 128, 4 * CMP_HID)),
              cs((8, 2 * CMP_STRIDE * HEAD_DIM)),
              cs((2 * CMP_STRIDE * HEAD_DIM, CMP_HID)), cs((CMP_HID, HEAD_DIM))]
    grid_spec = pltpu.PrefetchScalarGridSpec(
        num_scalar_prefetch=1,
        grid=(nb,),
        in_specs=[any_spec, any_spec, new_spec, new_spec,
                  pl.BlockSpec((1, N_KV, Q_ROWS, HEAD_DIM), lambda i, pt: (i, 0, 0, 0))]
                 + wspecs + wspecs + [cs(tbl8.shape), cs(bkt.shape), cs(msel.shape)],
        out_specs=[pl.BlockSpec((1, N_KV, Q_ROWS, HEAD_DIM), lambda i, pt: (i, 0, 0, 0)),
                   pl.BlockSpec((1, N_KV, N_SELECT, 1), lambda i, pt: (i, 0, 0, 0))],
        scratch_shapes=[pltpu.VMEM((2, PAGES_PER_GROUP, 2 * PAGE, 128), F32),
                        pltpu.VMEM((2, PAGES_PER_GROUP, 2 * PAGE, 128), F32),
                        pltpu.VMEM((2, 2, rows_pg, 128), F32),
                        pltpu.VMEM((nc + 8, 2 * N_KV * CMP_HID), F32),
                        pltpu.VMEM((nc + 8, 2 * N_KV * CMP_HID), F32),
                        pltpu.SemaphoreType.DMA((2, 2))])
    return pl.pallas_call(
        functools.partial(_cmp_s_body, past),
        grid_spec=grid_spec,
        out_shape=[jax.ShapeDtypeStruct((nb, N_KV, Q_ROWS, HEAD_DIM), F32),
                   jax.ShapeDtypeStruct((nb, N_KV, N_SELECT, 1), jnp.int32)],
        compiler_params=_params(("arbitrary",)),
        name="cmp_sample",
    )(page_table, cache_k, cache_v, kc_new, vc_new, q8, *wk, *wv, tbl8, bkt, msel)


NEW_LANES = 128


def _dec_s_body(past, idx_ref, pt_ref, csk_hbm, csv_hbm,
                q_ref, ksn_ref, vsn_ref, kwn_ref, vwn_ref, wk_ref, wv_ref, gt_ref, ocmp_ref,
                tbl_ref, bktw_ref, o_ref, wko_ref, wvo_ref,
                kbuf, vbuf, kwbuf, vwbuf, sem):
    b = pl.program_id(0)
    n_pages = past // PAGE
    cur = past // SLC_BLOCK
    bpp = PAGE // SLC_BLOCK
    lsel = N_SELECT * PAGE
    wb = wk_ref.shape[3]

    @pl.when(b == 0)
    def _():
        kbuf[...] = jnp.zeros_like(kbuf)
        vbuf[...] = jnp.zeros_like(vbuf)
        kwbuf[...] = jnp.zeros_like(kwbuf)
        vwbuf[...] = jnp.zeros_like(vwbuf)

    def gathers(bb):
        half = bb % 2
        out = []
        for g in range(N_KV):
            for k in range(N_SELECT):
                blk = idx_ref[bb, g * N_SELECT + k]
                page = pt_ref[bb, jnp.minimum(blk // bpp, n_pages - 1)]
                dst = pl.ds(k * PAGE, PAGE)
                out.append(pltpu.make_async_copy(csk_hbm.at[page, g], kbuf.at[half, g, :, dst], sem.at[0, half]))
                out.append(pltpu.make_async_copy(csv_hbm.at[page, g], vbuf.at[half, g, :, dst], sem.at[1, half]))
        return out

    @pl.when(b == 0)
    def _():
        for c in gathers(b):
            c.start()

    @pl.when(b + 1 < pl.num_programs(0))
    def _():
        for c in gathers(b + 1):
            c.start()

    cur_half = b % 2

    lane_wb = lax.broadcasted_iota(jnp.int32, (HEAD_DIM, wb), 1)
    for g in range(N_KV):
        for src, new_ref, dst, buf in ((wk_ref, kwn_ref, wko_ref, kwbuf), (wv_ref, vwn_ref, wvo_ref, vwbuf)):
            st = src[0, g]
            newc = new_ref[0, g]
            dst[0, g] = jnp.where(lane_wb == wb - 1, newc, pltpu.roll(st, wb - 1, 1))
            buf[g, :, 0:wb] = st
            buf[g, :, wb:wb + 1] = newc

    for c in gathers(b):
        c.wait()

    ls = lsel + NEW_LANES
    lw = wb + NEW_LANES
    lane_s = lax.broadcasted_iota(jnp.int32, (1, ls), 1)
    lane_w = lax.broadcasted_iota(jnp.int32, (1, lw), 1)
    wmask = lane_w <= wb
    tok = lane_s % PAGE
    groups = range(N_KV)
    smasks, sbiases = [], []
    for g in groups:
        kbuf[cur_half, g, :, lsel:lsel + 1] = ksn_ref[0, g]
        vbuf[cur_half, g, :, lsel:lsel + 1] = vsn_ref[0, g]
        kpos = past + (lane_s - lsel)
        gathered = lane_s < 0
        has_new = jnp.int32(0)
        for k in range(N_SELECT):
            blk = idx_ref[b, g * N_SELECT + k]
            in_slot = (lane_s // PAGE) == k
            kpos = jnp.where(in_slot, (blk // bpp) * PAGE + tok, kpos)
            gathered = gathered | (in_slot & (blk < cur) & ((tok // SLC_BLOCK) == (blk % bpp)))
            has_new = has_new | (blk == cur).astype(jnp.int32)
        smasks.append((gathered | ((lane_s == lsel) & (has_new > 0))) & (kpos <= past))
        sbiases.append(_bias_rows(_rel_bucket(past - kpos), tbl_ref[g]))
    bktw = bktw_ref[...]
    qgs = [q_ref[0, g].astype(BF16) for g in groups]
    s_slc = [_dot(qgs[g], kbuf[cur_half, g].astype(BF16)) * SCALE + sbiases[g] for g in groups]
    s_win = [_dot(qgs[g], kwbuf[g].astype(BF16)) * SCALE + _bias_rows(bktw, tbl_ref[g]) for g in groups]
    p_slc = [_softmax_rows(s_slc[g], smasks[g]) for g in groups]
    p_win = [_softmax_rows(s, wmask) for s in s_win]
    o_slc = [_dot_nt(p_slc[g].astype(BF16), vbuf[cur_half, g].astype(BF16)) for g in groups]
    o_win = [_dot_nt(p_win[g].astype(BF16), vwbuf[g].astype(BF16)) for g in groups]
    for g in groups:
        gt = gt_ref[0, g]
        o_ref[0, g] = gt[:, 0:1] * ocmp_ref[0, g] + gt[:, 1:2] * o_slc[g] + gt[:, 2:3] * o_win[g]


def _dec_sample(idx, page_table, cache_sk, cache_sv, q8, ks_new, vs_new, kw_new, vw_new, state_wk, state_wv,
                gates8, ocmp, tbl8, bktw, past):
    nb, wb = state_wk.shape[0], state_wk.shape[3]
    any_spec = pl.BlockSpec(memory_space=pl.ANY)
    new_spec = pl.BlockSpec((1, N_KV, HEAD_DIM, 1), lambda i, *_: (i, 0, 0, 0))
    st_spec = pl.BlockSpec((1, N_KV, HEAD_DIM, wb), lambda i, *_: (i, 0, 0, 0))
    head_spec = pl.BlockSpec((1, N_KV, Q_ROWS, HEAD_DIM), lambda i, *_: (i, 0, 0, 0))
    cs = lambda shape: pl.BlockSpec(shape, lambda i, *_: (0,) * len(shape))
    lsel = N_SELECT * PAGE
    grid_spec = pltpu.PrefetchScalarGridSpec(
        num_scalar_prefetch=2,
        grid=(nb,),
        in_specs=[any_spec, any_spec, head_spec,
                  new_spec, new_spec, new_spec, new_spec, st_spec, st_spec,
                  pl.BlockSpec((1, N_KV, Q_ROWS, 3), lambda i, *_: (i, 0, 0, 0)), head_spec,
                  cs(tbl8.shape), cs(bktw.shape)],
        out_specs=[head_spec, st_spec, st_spec],
        scratch_shapes=[pltpu.VMEM((2, N_KV, HEAD_DIM, lsel + NEW_LANES), F32),
                        pltpu.VMEM((2, N_KV, HEAD_DIM, lsel + NEW_LANES), F32),
                        pltpu.VMEM((N_KV, HEAD_DIM, wb + NEW_LANES), F32),
                        pltpu.VMEM((N_KV, HEAD_DIM, wb + NEW_LANES), F32),
                        pltpu.SemaphoreType.DMA((2, 2))])
    return pl.pallas_call(
        functools.partial(_dec_s_body, past),
        grid_spec=grid_spec,
        out_shape=[jax.ShapeDtypeStruct((nb, N_KV, Q_ROWS, HEAD_DIM), F32),
                   jax.ShapeDtypeStruct(state_wk.shape, F32),
                   jax.ShapeDtypeStruct(state_wv.shape, F32)],
        compiler_params=_params(("arbitrary",)),
        name="dec_sample",
    )(idx, page_table, cache_sk, cache_sv, q8, ks_new, vs_new, kw_new, vw_new,
      state_wk, state_wv, gates8, ocmp, tbl8, bktw)


def _expand_w1(w1):
    eye = jnp.eye(2, dtype=w1.dtype)
    parts = []
    for part in (w1[:CMP_STRIDE * HEAD_DIM], w1[CMP_STRIDE * HEAD_DIM:]):
        w = part.reshape(CMP_STRIDE, HEAD_DIM, CMP_HID)
        parts.append(jnp.einsum("gh,rdn->rgdhn", eye, w).reshape(CMP_STRIDE, 2 * HEAD_DIM, 2 * CMP_HID))
    w = jnp.concatenate(parts, axis=-1).astype(BF16)
    return w.reshape(CMP_STRIDE * 2 * HEAD_DIM, 4 * CMP_HID)


def _expand_w2(w2):
    eye = jnp.eye(N_KV, dtype=w2.dtype)
    return jnp.einsum("gh,nd->gnhd", eye, w2).reshape(N_KV * CMP_HID, KV_W).astype(BF16)


def _pe_rows(pe):
    return jnp.broadcast_to(pe.reshape(1, -1), (8, pe.size)).astype(BF16)


def _select_matrix(n_cmp, n_blocks):
    r = SLC_BLOCK // CMP_STRIDE
    m = np.zeros((n_cmp, n_blocks), np.float32)
    for s in range(n_blocks):
        for a in range(r):
            for bb in range(2):
                c = r * s + a - bb
                if 0 <= c < n_cmp:
                    m[c, s] += 1.0
    return m


def _pad_to(a, size, axis):
    pad = [(0, 0)] * a.ndim
    pad[axis] = (0, size - a.shape[axis])
    return jnp.pad(a, pad)


def kernel(x_prompt, x_sample, cache_cmp_k, cache_cmp_v, cache_slc_k, cache_slc_v, state_win_k, state_win_v, state_conv, page_table, p_prompt, p_sample, rel_table, norm_mix, norm_ffn, norm_ple, norm_final, conv_w1, conv_b1, conv_dw, conv_dwb, conv_ln_g, conv_ln_b, conv_w2, conv_b2, attn_w_in, attn_w_out, cmpk_w1, cmpk_pe, cmpk_w2, cmpv_w1, cmpv_pe, cmpv_w2, mlp_up, mlp_down, ple_proj, ple_gate):
    nbp, t, _ = x_prompt.shape
    nbs = x_sample.shape[0]
    n_p = nbp * t
    n_pool = cache_cmp_k.shape[1]
    past = page_table.shape[1] * cache_cmp_k.shape[2]
    wb = state_win_k.shape[2]
    hist = CONV_W - 1
    tm_p, tf, tt, tq = 512, 1024, 256, ATTN_TK

    row = lambda a: a.reshape(1, -1)
    bf = lambda a: a.astype(BF16)

    w1c, w2c = bf(conv_w1[0]), bf(conv_w2[0])
    dw3 = conv_dw[0].reshape(CONV_W, D_MODEL // 128, 128).transpose(1, 0, 2)
    dwb3 = conv_dwb[0].reshape(D_MODEL // 128, 1, 128)
    up, dn = bf(mlp_up), bf(mlp_down)
    wg, wp = bf(ple_gate), bf(ple_proj)
    w_in = attn_w_in[0]
    wq = bf(w_in[:, :D_MODEL])
    wkv = bf(w_in[:, D_MODEL:D_MODEL + 6 * KV_W])
    wgt = bf(_pad_to(w_in[:, D_MODEL + 6 * KV_W:], 128, 1))
    w_out = bf(attn_w_out[0])
    cw = []
    for w1, pe, w2 in ((cmpk_w1[0], cmpk_pe[0], cmpk_w2[0]), (cmpv_w1[0], cmpv_pe[0], cmpv_w2[0])):
        cw.append((_expand_w1(w1), _pe_rows(pe), bf(w1), _expand_w2(w2), bf(w2)))
    wk_p, wv_p = [c[:4] for c in cw]
    wk_s, wv_s = [c[:3] + c[4:] for c in cw]

    def tail(x, i, p, tm, final, pre=None):
        return _mlp_ple(x, row(norm_ffn[i]), up, dn, row(norm_ple[i]), wg, p, i, wp,
                        row(norm_final), tm, tf, final, pre)

    conv_args = (row(conv_ln_g[0]), row(conv_ln_b[0]), w2c, row(conv_b2[0]))

    xp = x_prompt.reshape(n_p, D_MODEL)
    glu = _conv_in(xp, row(norm_mix[0]), w1c, row(conv_b1[0]), tm_p).reshape(nbp, t, D_MODEL)
    x1 = _conv_out(glu, x_prompt, dw3, dwb3, *conv_args, tt).reshape(n_p, D_MODEL)
    conv_p = glu[:, t - hist:][None]
    pp = p_prompt.reshape(-1, n_p, D_PLE)
    ps = p_sample.reshape(-1, nbs, D_PLE)
    x2 = tail(x1, 0, pp, tm_p, False)
    q, gates, kcl, kch, vcl, vch, ks, kw, *rest = _attn_in(x2, row(norm_mix[1]), wq, wkv, wgt, tm_p, seq_len=t)
    kv_fm, win_fm = rest[:len(KV_NAMES)], rest[len(KV_NAMES):]
    vs_fm, vw_fm = kv_fm[KV_NAMES.index("vs")], kv_fm[KV_NAMES.index("vw")]
    seq = lambda a: a.reshape(nbp, t, -1)
    kcmp, vcmp = _compress_prompt((seq(kcl), seq(kch)), (seq(vcl), seq(vch)), wk_p, wv_p)

    ncmp = t // CMP_STRIDE
    ii = jnp.arange(tq, dtype=jnp.int32)[None, :]
    jj = jnp.arange(ATTN_TK, dtype=jnp.int32)[:, None]
    r_qk = tq // ATTN_TK
    bkt_toep = jnp.stack([_rel_bucket(d * ATTN_TK + ii - jj) for d in range(1 - r_qk, 2)])
    nshift = (t // tq - 1) * (tq // CMP_STRIDE)
    rr = jnp.arange(ncmp + nshift, dtype=jnp.int32)[:, None]
    bkt_cmp = _rel_bucket(ii - CMP_STRIDE * (rr - nshift) - (2 * CMP_STRIDE - 1))
    msel_p = jnp.asarray(_select_matrix(ncmp, t // SLC_BLOCK).T, BF16)
    o_p = _attn_prompt(seq(q), seq(gates), seq(ks), vs_fm, seq(kw), vw_fm, kcmp, vcmp,
                       rel_table, bkt_toep, bkt_cmp, msel_p, tq)
    y_p = tail(x2, 1, pp, tm_p, True, pre=(o_p.reshape(n_p, D_MODEL), w_out))

    from_fm = lambda a: jnp.transpose(a, (0, 3, 1, 2))[None]
    kv_p = [from_fm(a) for a in kv_fm[:4]] + [from_fm(a) for a in win_fm]

    xs = x_sample.reshape(nbs, D_MODEL)
    glu_s = _conv_in(xs, row(norm_mix[0]), w1c, row(conv_b1[0]), nbs)
    x1s, conv_s = _conv_step(state_conv[0], glu_s, xs, conv_dw[0], row(conv_dwb[0]), *conv_args)
    x2s = tail(x1s, 0, ps, nbs, False)
    qs, gts, kcs, vcs, kss, vss, kws, vws = _attn_in(x2s, row(norm_mix[1]), wq, wkv, wgt, nbs)

    q8 = _pad_to(qs.reshape(nbs, N_KV, HPG, HEAD_DIM), Q_ROWS, 2)
    tbl8 = _pad_to(rel_table.T.reshape(N_KV, HPG, NUM_BUCKETS), Q_ROWS, 1)
    nc_s = past // CMP_STRIDE
    cc = jnp.arange(nc_s, dtype=jnp.int32)[None, :]
    bkt_s = _rel_bucket(past - (CMP_STRIDE * cc + 2 * CMP_STRIDE - 1))
    n_blocks_s = past // SLC_BLOCK + 1
    msel_s = jnp.asarray(_select_matrix(nc_s, -(-n_blocks_s // 128) * 128), BF16)
    new3 = lambda a: a.reshape(nbs, 1, KV_W)
    fmajor = lambda c: jnp.transpose(c[0], (0, 2, 3, 1))
    cache2 = lambda c: fmajor(c).reshape(n_pool, 2 * PAGE, PAGE)
    ocmp, idx = _cmp_sample(page_table, cache2(cache_cmp_k), cache2(cache_cmp_v), new3(kcs), new3(vcs),
                            q8, wk_s, wv_s, tbl8, bkt_s, msel_s, past)

    ww = jnp.arange(wb + NEW_LANES, dtype=jnp.int32)[None, :]
    bkt_w = _rel_bucket(wb - ww)
    gates8 = _pad_to(gts[:, :3 * N_HEADS].reshape(nbs, 3, N_KV, HPG).transpose(0, 2, 3, 1), Q_ROWS, 2)
    col4 = lambda a: a.reshape(nbs, N_KV, HEAD_DIM, 1)
    o8, wk_new, wv_new = _dec_sample(idx.reshape(nbs, N_KV * N_SELECT), page_table,
                                     fmajor(cache_slc_k), fmajor(cache_slc_v), q8,
                                     col4(kss), col4(vss), col4(kws), col4(vws),
                                     fmajor(state_win_k), fmajor(state_win_v), gates8, ocmp, tbl8, bkt_w, past)
    o_s = o8[:, :, :HPG].reshape(nbs, D_MODEL)
    y_s = tail(x2s, 1, ps, nbs, True, pre=(o_s, w_out))

    kv5s = lambda a: a.reshape(1, nbs, 1, N_KV, HEAD_DIM)
    return (y_p.reshape(nbp, t, D_MODEL), y_s.reshape(nbs, 1, D_MODEL), *kv_p, conv_p,
            kv5s(kcs), kv5s(vcs), kv5s(kss), kv5s(vss), from_fm(wk_new), from_fm(wv_new), conv_s[None])
```

```python
import functools
import math

import numpy as np
import jax
import jax.numpy as jnp
from jax import lax
from jax.experimental import pallas as pl
from jax.experimental.pallas import tpu as pltpu

F32 = jnp.float32
BF16 = jnp.bfloat16

D_MODEL = 1024
D_PLE = 256
CONV_W = 31
N_HEADS = 16
HEAD_DIM = 64
N_KV = 4
HPG = 4
KV_W = N_KV * HEAD_DIM
CMP_STRIDE = 16
CMP_HID = 128
SLC_BLOCK = 64
N_SELECT = 16
WINDOW = 512
NUM_BUCKETS = 32
MAX_DISTANCE = 128
D_FF = 4096
EPS = 1e-6
SCALE = HEAD_DIM ** -0.5
assert math.frexp(SCALE)[0] == 0.5
NEG_INF = float("-inf")

ATTN_TK = 128
assert ATTN_TK >= MAX_DISTANCE

VMEM_LIMIT = 56 * 1024 * 1024
HALO = 32


def _sigmoid(x):
    return 1.0 / (1.0 + jnp.exp(-x))


def _rms(x, g):
    return x * lax.rsqrt(jnp.mean(x * x, axis=-1, keepdims=True) + EPS) * g


def _dot(a, b):
    return jnp.dot(a, b, preferred_element_type=F32)


def _dot_lanes(a, b):
    n = b.shape[1]
    return jnp.concatenate([_dot(a, b[:, c:c + 128]) for c in range(0, n, 128)], axis=1)


def _split3(x):
    hi = x.astype(BF16)
    r1 = x - hi.astype(F32)
    mid = r1.astype(BF16)
    lo = (r1 - mid.astype(F32)).astype(BF16)
    return hi, mid, lo


def _rel_bucket(dist):
    n = jnp.maximum(dist, 0)
    max_exact = NUM_BUCKETS // 2
    nf = jnp.maximum(n, 1).astype(F32)
    large = max_exact + (jnp.log(nf / max_exact) / math.log(MAX_DISTANCE / max_exact)
                         * (NUM_BUCKETS - max_exact)).astype(jnp.int32)
    large = jnp.minimum(large, NUM_BUCKETS - 1)
    return jnp.where(n < max_exact, n, large)


def _params(sem):
    return pltpu.CompilerParams(dimension_semantics=sem, vmem_limit_bytes=VMEM_LIMIT)


def _const_spec(shape, single=False):
    n = len(shape)
    if single:
        return pl.BlockSpec(shape, lambda *_: (0,) * n, pipeline_mode=pl.Buffered(1))
    return pl.BlockSpec(shape, lambda *_: (0,) * n)


def _conv_in_body(x_ref, g_ref, w_ref, b_ref, o_ref):
    h = _rms(x_ref[...], g_ref[...]).astype(BF16)
    u = _dot(h, w_ref[...]) + b_ref[...]
    o_ref[...] = u[:, :D_MODEL] * _sigmoid(u[:, D_MODEL:])


def _conv_in(x, g, w1, b1, tm):
    n = x.shape[0]
    return pl.pallas_call(
        _conv_in_body,
        grid=(n // tm,),
        in_specs=[pl.BlockSpec((tm, D_MODEL), lambda i: (i, 0)),
                  _const_spec((1, D_MODEL)),
                  _const_spec((D_MODEL, 2 * D_MODEL)),
                  _const_spec((1, 2 * D_MODEL))],
        out_specs=pl.BlockSpec((tm, D_MODEL), lambda i: (i, 0)),
        out_shape=jax.ShapeDtypeStruct((n, D_MODEL), F32),
        compiler_params=_params(("parallel",)),
        name="conv_in",
    )(x, g, w1, b1)


def _ln_silu_proj(y, lng, lnb, w2, b2, x):
    mu = jnp.mean(y, axis=-1, keepdims=True)
    yc = y - mu
    var = jnp.mean(yc * yc, axis=-1, keepdims=True)
    z = yc * lax.rsqrt(var + EPS) * lng + lnb
    z = z * _sigmoid(z)
    return _dot(z.astype(BF16), w2) + b2 + x


def _conv_out_body(tt, cur_ref, halo_ref, x_ref, dw_ref, dwb_ref, lng_ref, lnb_ref, w2_ref, b2_ref,
                   o_ref, ctx_ref, y_ref):
    i = pl.program_id(1)
    n_strip = D_MODEL // 128
    keep = (i > 0).astype(F32)
    for c in range(n_strip):
        ctx_ref[c, 0:HALO, :] = halo_ref[0, :, 128 * c:128 * (c + 1)] * keep
        ctx_ref[c, HALO:, :] = cur_ref[0, :, 128 * c:128 * (c + 1)]

    def strip(c, carry):
        acc = jnp.broadcast_to(dwb_ref[c], (tt, 128))
        for k in range(CONV_W):
            off = k + HALO - (CONV_W - 1)
            acc = acc + ctx_ref[c, off:off + tt, :] * dw_ref[c, k:k + 1, :]
        y_ref[c] = acc
        return carry

    lax.fori_loop(0, n_strip, strip, 0)
    y = jnp.concatenate([y_ref[c] for c in range(n_strip)], axis=1)
    o_ref[0] = _ln_silu_proj(y, lng_ref[...], lnb_ref[...], w2_ref[...], b2_ref[...], x_ref[0])


def _conv_out(glu, x, dw3, dwb3, lng, lnb, w2, b2, tt):
    b, t, _ = glu.shape
    hb = tt // HALO
    return pl.pallas_call(
        functools.partial(_conv_out_body, tt),
        grid=(b, t // tt),
        in_specs=[pl.BlockSpec((1, tt, D_MODEL), lambda bi, i: (bi, i, 0)),
                  pl.BlockSpec((1, HALO, D_MODEL), lambda bi, i: (bi, jnp.maximum(i * hb - 1, 0), 0)),
                  pl.BlockSpec((1, tt, D_MODEL), lambda bi, i: (bi, i, 0)),
                  _const_spec((D_MODEL // 128, CONV_W, 128)),
                  _const_spec((D_MODEL // 128, 1, 128)),
                  _const_spec((1, D_MODEL)),
                  _const_spec((1, D_MODEL)),
                  _const_spec((D_MODEL, D_MODEL)),
                  _const_spec((1, D_MODEL))],
        out_specs=pl.BlockSpec((1, tt, D_MODEL), lambda bi, i: (bi, i, 0)),
        out_shape=jax.ShapeDtypeStruct((b, t, D_MODEL), F32),
        scratch_shapes=[pltpu.VMEM((D_MODEL // 128, tt + HALO, 128), F32),
                        pltpu.VMEM((D_MODEL // 128, tt, 128), F32)],
        compiler_params=_params(("parallel", "arbitrary")),
        name="conv_out",
    )(glu, glu, x, dw3, dwb3, lng, lnb, w2, b2)


def _conv_step_body(st_ref, u_ref, x_ref, dw_ref, dwb_ref, lng_ref, lnb_ref, w2_ref, b2_ref,
                    o_ref, ns_ref):
    nb = st_ref.shape[0]
    hist = CONV_W - 1
    dwh = dw_ref[0:hist, :]
    rows = []
    for bi in range(nb):
        rows.append(jnp.sum(st_ref[bi] * dwh, axis=0, keepdims=True))
        ns_ref[bi, 0:hist - 1, :] = st_ref[bi, 1:hist, :]
        ns_ref[bi, hist - 1:hist, :] = u_ref[bi:bi + 1, :]
    y = jnp.concatenate(rows, axis=0) + u_ref[...] * dw_ref[hist:hist + 1, :] + dwb_ref[...]
    o_ref[...] = _ln_silu_proj(y, lng_ref[...], lnb_ref[...], w2_ref[...], b2_ref[...], x_ref[...])


def _conv_step(state, glu, x, dw, dwb, lng, lnb, w2, b2):
    nb, hist, _ = state.shape
    return pl.pallas_call(
        _conv_step_body,
        grid=(1,),
        in_specs=[_const_spec((nb, hist, D_MODEL)), _const_spec((nb, D_MODEL)), _const_spec((nb, D_MODEL)),
                  _const_spec((CONV_W, D_MODEL)), _const_spec((1, D_MODEL)), _const_spec((1, D_MODEL)),
                  _const_spec((1, D_MODEL)), _const_spec((D_MODEL, D_MODEL)), _const_spec((1, D_MODEL))],
        out_specs=[_const_spec((nb, D_MODEL)), _const_spec((nb, hist, D_MODEL))],
        out_shape=[jax.ShapeDtypeStruct((nb, D_MODEL), F32),
                   jax.ShapeDtypeStruct((nb, hist, D_MODEL), F32)],
        compiler_params=_params(("arbitrary",)),
        name="conv_step",
    )(state, glu, x, dw, dwb, lng, lnb, w2, b2)


def _mlp_body(final, pre, tf, *refs):
    if pre:
        o_in_ref, wo_ref, refs = refs[0], refs[1], refs[2:]
    x_ref, gf_ref, up_ref, dn_ref, gp_ref, wg_ref, p_ref, wp_ref, gfin_ref, o_ref = refs
    x1 = x_ref[...]
    if pre:
        x1 = x1 + _dot(o_in_ref[...].astype(BF16), wo_ref[...])
    h = _rms(x1, gf_ref[...]).astype(BF16)
    ple = _dot(p_ref[...].astype(BF16), wp_ref[...])
    x2 = x1
    for c in range(0, D_FF, tf):
        a = jnp.maximum(_dot(h, up_ref[:, c:c + tf]), 0.0)
        x2 = x2 + _dot((a * a).astype(BF16), dn_ref[c:c + tf, :])
    gate = _sigmoid(_dot(_rms(x2, gp_ref[...]).astype(BF16), wg_ref[...]))
    x3 = x2 + gate * ple
    if final:
        x3 = _rms(x3, gfin_ref[...])
    o_ref[...] = x3


def _mlp_ple(x, gf, up, dn, gp, wg, p, layer, wp, gfin, tm, tf, final, pre=None):
    n = x.shape[0]
    tok = lambda i: (i, 0)
    one = pl.Buffered(1)
    in_specs, args = [], []
    if pre is not None:
        in_specs += [pl.BlockSpec((tm, D_MODEL), tok), _const_spec((D_MODEL, D_MODEL), single=True)]
        args += list(pre)
    in_specs += [pl.BlockSpec((tm, D_MODEL), tok),
                 _const_spec((1, D_MODEL)),
                 pl.BlockSpec((None, D_MODEL, D_FF), lambda i: (layer, 0, 0), pipeline_mode=one),
                 pl.BlockSpec((None, D_FF, D_MODEL), lambda i: (layer, 0, 0), pipeline_mode=one),
                 _const_spec((1, D_MODEL)),
                 pl.BlockSpec((None, D_MODEL, D_MODEL), lambda i: (layer, 0, 0), pipeline_mode=one),
                 pl.BlockSpec((None, tm, D_PLE), lambda i: (layer, i, 0)),
                 pl.BlockSpec((None, D_PLE, D_MODEL), lambda i: (layer, 0, 0), pipeline_mode=one),
                 _const_spec((1, D_MODEL))]
    args += [x, gf, up, dn, gp, wg, p, wp, gfin]
    return pl.pallas_call(
        functools.partial(_mlp_body, final, pre is not None, tf),
        grid=(n // tm,),
        in_specs=in_specs,
        out_specs=pl.BlockSpec((tm, D_MODEL), tok),
        out_shape=jax.ShapeDtypeStruct((n, D_MODEL), F32),
        compiler_params=_params(("parallel",)),
        name="mlp_ple",
    )(*args)


KV_NAMES = ("kc", "vc", "ks", "vs", "kw", "vw")
KV_COMPRESSED = ("kc", "vc")
KV_TOKEN_MAJOR = ("ks", "kw")
KV_WINDOWED = ("kw", "vw")


def _attn_in_body(seq_len, x_ref, g_ref, wq_ref, wkv_ref, wg_ref, q_ref, gt_ref, *kv_refs):
    h = _rms(x_ref[...], g_ref[...]).astype(BF16)
    q_ref[...] = _dot(h, wq_ref[...])
    gt_ref[...] = _sigmoid(_dot(h, wg_ref[...]))
    kv = _dot(h, wkv_ref[...])
    blocks = {n: kv[:, KV_W * i:KV_W * (i + 1)] for i, n in enumerate(KV_NAMES)}
    if seq_len is None:
        for n, r in zip(KV_NAMES, kv_refs):
            r[...] = blocks[n]
        return
    n_half = 2 * len(KV_COMPRESSED)
    n_tm = n_half + len(KV_TOKEN_MAJOR)
    half_refs = kv_refs[:n_half]
    tm_refs = kv_refs[n_half:n_tm]
    fm_refs = kv_refs[n_tm:n_tm + len(KV_NAMES)]
    last_refs = dict(zip(KV_WINDOWED, kv_refs[n_tm + len(KV_NAMES):]))
    for i, n in enumerate(KV_COMPRESSED):
        half_refs[2 * i][...] = blocks[n][:, :128]
        half_refs[2 * i + 1][...] = blocks[n][:, 128:]
    for n, r in zip(KV_TOKEN_MAJOR, tm_refs):
        r[...] = blocks[n]
    tps = seq_len // x_ref.shape[0]
    is_last = pl.program_id(0) % tps == tps - 1
    for n, r in zip(KV_NAMES, fm_refs):
        bt = blocks[n].T
        for g in range(N_KV):
            r[0, g] = bt[HEAD_DIM * g:HEAD_DIM * (g + 1), :]
        if n in last_refs:

            @pl.when(is_last)
            def _(bt=bt, dst=last_refs[n]):
                for g in range(N_KV):
                    dst[0, g] = bt[HEAD_DIM * g:HEAD_DIM * (g + 1), :]


def _attn_in(x, g, wq, wkv, wg, tm, seq_len=None):
    n = x.shape[0]
    tok = lambda i: (i, 0)
    kv_spec = pl.BlockSpec((tm, KV_W), tok)
    kv_shape = jax.ShapeDtypeStruct((n, KV_W), F32)
    if seq_len is None:
        kv_specs, kv_shapes = [kv_spec] * 6, [kv_shape] * 6
    else:
        tps = seq_len // tm
        fm_spec = pl.BlockSpec((1, N_KV, HEAD_DIM, tm), lambda i: (i // tps, 0, 0, i % tps))
        fm_shape = jax.ShapeDtypeStruct((n // seq_len, N_KV, HEAD_DIM, seq_len), F32)
        assert tm == min(WINDOW, seq_len)
        last_spec = pl.BlockSpec((1, N_KV, HEAD_DIM, tm), lambda i: (i // tps, 0, 0, 0))
        last_shape = jax.ShapeDtypeStruct((n // seq_len, N_KV, HEAD_DIM, tm), F32)
        half_spec = pl.BlockSpec((tm, 128), tok)
        half_shape = jax.ShapeDtypeStruct((n, 128), F32)
        n_half = 2 * len(KV_COMPRESSED)
        kv_specs = ([half_spec] * n_half + [kv_spec] * len(KV_TOKEN_MAJOR) + [fm_spec] * 6
                    + [last_spec] * len(KV_WINDOWED))
        kv_shapes = ([half_shape] * n_half + [kv_shape] * len(KV_TOKEN_MAJOR) + [fm_shape] * 6
                     + [last_shape] * len(KV_WINDOWED))
    return pl.pallas_call(
        functools.partial(_attn_in_body, seq_len),
        grid=(n // tm,),
        in_specs=[pl.BlockSpec((tm, D_MODEL), tok), _const_spec((1, D_MODEL)),
                  _const_spec((D_MODEL, D_MODEL)), _const_spec((D_MODEL, 6 * KV_W)),
                  _const_spec((D_MODEL, 128))],
        out_specs=[pl.BlockSpec((tm, D_MODEL), tok), pl.BlockSpec((tm, 128), tok)] + kv_specs,
        out_shape=[jax.ShapeDtypeStruct((n, D_MODEL), F32), jax.ShapeDtypeStruct((n, 128), F32)] + kv_shapes,
        compiler_params=_params(("arbitrary",)),
        name="attn_in",
    )(x, g, wq, wkv, wg)


def _compress_hidden(load_rows, n_chunks, wexp_ref):
    lo = jnp.concatenate([load_rows(2 * r) for r in range(CMP_STRIDE)], axis=1)
    hi = jnp.concatenate([load_rows(2 * r + 1) for r in range(CMP_STRIDE)], axis=1)
    acc = _dot(jnp.concatenate([lo, hi], axis=0).astype(BF16), wexp_ref[...])
    lo, hi = acc[:n_chunks], acc[n_chunks:]
    w = 2 * CMP_HID
    return jnp.concatenate([lo[:, :w], hi[:, :w], lo[:, w:], hi[:, w:]], axis=1)


def _pe_term(pe_ref, w1_ref):
    t = _dot(pe_ref[...], w1_ref[...])[0:1, :]
    return jnp.concatenate([t] * N_KV, axis=1)


def _compress_p_body(nc, kcl_ref, kch_ref, vcl_ref, vch_ref, wek_ref, pek_ref, w1k_ref, w2k_ref,
                     wev_ref, pev_ref, w1v_ref, w2v_ref, ok_ref, ov_ref):
    half = N_KV * CMP_HID
    row = lax.broadcasted_iota(jnp.int32, (nc, KV_W), 0)
    for src, we, pe, w1, w2, out in (((kcl_ref, kch_ref), wek_ref, pek_ref, w1k_ref, w2k_ref, ok_ref),
                                     ((vcl_ref, vch_ref), wev_ref, pev_ref, w1v_ref, w2v_ref, ov_ref)):
        hh = _compress_hidden(lambda k: src[k % 2][0, pl.ds(k // 2, nc, stride=CMP_STRIDE), :], nc, we)
        nxt = pltpu.roll(hh[:, half:], nc - 1, 0)
        a = hh[:, :half] + nxt + _pe_term(pe, w1)
        a = a * _sigmoid(a)
        res = _dot(a.astype(BF16), w2[...])
        out[0] = jnp.where(row < nc - 1, res, 0.0)


def _compress_prompt(kc_halves, vc_halves, wk, wv):
    b, t, _ = kc_halves[0].shape
    nc = t // CMP_STRIDE
    seq = pl.BlockSpec((1, t, 128), lambda i: (i, 0, 0))
    wspecs = [_const_spec((CMP_STRIDE * 128, 4 * CMP_HID)),
              _const_spec((8, 2 * CMP_STRIDE * HEAD_DIM)),
              _const_spec((2 * CMP_STRIDE * HEAD_DIM, CMP_HID)), _const_spec((N_KV * CMP_HID, KV_W))]
    out = pl.BlockSpec((1, nc, KV_W), lambda i: (i, 0, 0))
    return pl.pallas_call(
        functools.partial(_compress_p_body, nc),
        grid=(b,),
        in_specs=[seq] * 4 + wspecs + wspecs,
        out_specs=[out, out],
        out_shape=[jax.ShapeDtypeStruct((b, nc, KV_W), F32)] * 2,
        compiler_params=_params(("parallel",)),
        name="compress_prompt",
    )(*kc_halves, *vc_halves, *wk, *wv)


def _softmax_cols(s, mask):
    s = jnp.where(mask, s, NEG_INF)
    m = jnp.max(s, axis=0, keepdims=True)
    m = jnp.where(m == NEG_INF, 0.0, m)
    e = jnp.where(mask, jnp.exp(s - m), 0.0)
    return e * (1.0 / jnp.maximum(jnp.sum(e, axis=0, keepdims=True), 1e-30))


def _attn_p_body(tq, t, q_ref, gt_ref, ks_ref, vs_ref, kw_ref, vw_ref, kcm_ref, vcm_ref,
                 tbl_ref, bt_ref, bc_ref, msel_ref, o_ref,
                 ksb, kwb, vst, vwt, kcb, vct, btoep, bcmp, qt, selt, ot, m_ref, l_ref, acc_ref, gt_s):
    b = pl.program_id(0)
    qi = pl.program_id(1)
    tk = ATTN_TK
    nkt = t // tk
    nqt = t // tq
    r_qk = tq // tk
    ncmp = kcm_ref.shape[1]
    q0 = qi * tq
    n_near = bt_ref.shape[0]

    @pl.when((b == 0) & (qi == 0))
    def _():
        def per_head(h, carry):
            far = tbl_ref[NUM_BUCKETS - 1, h]
            for d in range(n_near):
                bk = bt_ref[d]
                acc = jnp.zeros((tk, tq), F32)
                for bb in range(NUM_BUCKETS):
                    acc = jnp.where(bk == bb, tbl_ref[bb, h] - far, acc)
                btoep[h, d] = acc
            btoep[h, n_near] = jnp.zeros((tk, tq), F32)
            bk = bc_ref[...]
            acc = jnp.zeros(bk.shape, F32)
            for bb in range(NUM_BUCKETS):
                acc = jnp.where(bk == bb, tbl_ref[bb, h], acc)
            bcmp[h] = acc
            return carry
        lax.fori_loop(0, N_HEADS, per_head, 0)

    @pl.when(qi == 0)
    def _():
        for kt in range(nkt):
            rows = slice(kt * tk, (kt + 1) * tk)
            for src, dstk in ((ks_ref, ksb), (kw_ref, kwb)):
                blk = src[0, rows, :]
                for g in range(N_KV):
                    dstk[g, rows, :] = blk[:, HEAD_DIM * g:HEAD_DIM * (g + 1)].astype(BF16)
            for src, dstv in ((vs_ref, vst), (vw_ref, vwt)):
                for g in range(N_KV):
                    dstv[g, kt] = src[0, g, :, rows].astype(BF16)
        kc = kcm_ref[0]
        vc_t = vcm_ref[0].T
        for g in range(N_KV):
            kcb[g] = kc[:, HEAD_DIM * g:HEAD_DIM * (g + 1)].astype(BF16)
            vct[g] = vc_t[HEAD_DIM * g:HEAD_DIM * (g + 1), :].astype(BF16)

    qt[...] = (q_ref[0].T * SCALE).astype(BF16)
    gt_s[...] = gt_ref[0].T

    qpos_row = q0 + lax.broadcasted_iota(jnp.int32, (1, tq), 1)
    i_iota = lax.broadcasted_iota(jnp.int32, (tk, tq), 1)
    j_iota = lax.broadcasted_iota(jnp.int32, (tk, tq), 0)
    rel = i_iota - j_iota

    c_iota = lax.broadcasted_iota(jnp.int32, (ncmp, tq), 0)
    cmask = (CMP_STRIDE * c_iota + (2 * CMP_STRIDE - 1)) <= qpos_row
    coff = pl.multiple_of((nqt - 1 - qi) * (tq // CMP_STRIDE), 8)

    nblk = t // SLC_BLOCK
    blk_iota = lax.broadcasted_iota(jnp.int32, (nblk, tq), 0)
    cur = qpos_row // SLC_BLOCK
    valid = blk_iota * SLC_BLOCK <= qpos_row
    forced = (blk_iota == 0) | (blk_iota == cur) | (blk_iota == cur - 1)

    heads = range(N_HEADS)
    head_rows = [slice(HEAD_DIM * h, HEAD_DIM * (h + 1)) for h in heads]
    s_c = [_dot(kcb[h // HPG], qt[head_rows[h], :]) + bcmp[h, pl.ds(coff, ncmp), :] for h in heads]
    p_c = [_softmax_cols(s, cmask) for s in s_c]
    o_c = [_dot(vct[h // HPG], p_c[h].astype(BF16)) for h in heads]
    for h in heads:
        ot[head_rows[h], :] = gt_s[h:h + 1, :] * o_c[h]
    msel = msel_ref[...]
    scores = []
    for g in range(N_KV):
        pg = p_c[g * HPG]
        for hh in range(1, HPG):
            pg = pg + p_c[g * HPG + hh]
        hi, mid, lo = _split3(pg)
        score = _dot(msel, hi) + _dot(msel, mid) + _dot(msel, lo)
        scores.append(jnp.where(forced, jnp.inf, jnp.where(valid, score, NEG_INF)))
    n_slab = nblk // 8
    slabs = [[sc[8 * v:8 * (v + 1), :] for v in range(n_slab)] for sc in scores]
    ranks = [[jnp.zeros((8, tq), F32) for _ in range(n_slab)] for _ in scores]
    row8 = lax.broadcasted_iota(jnp.int32, (8, tq), 0)
    for i in range(nblk):
        for g in range(N_KV):
            si = scores[g][i:i + 1, :]
            for v, sl in enumerate(slabs[g]):
                if 8 * v > i:
                    beats = si >= sl
                elif 8 * v + 7 <= i:
                    beats = si > sl
                else:
                    beats = (si > sl) | ((row8 > i - 8 * v) & (si == sl))
                ranks[g][v] = ranks[g][v] + jnp.where(beats, 1.0, 0.0)
    for g in range(N_KV):
        for v, rk in enumerate(ranks[g]):
            sel = (rk < N_SELECT).astype(F32)
            for r in range(8):
                selt[g, 8 * v + r] = sel[r:r + 1, :]

    def flash_init(br):
        m_ref[br] = jnp.full(m_ref.shape[1:], NEG_INF, F32)
        l_ref[br] = jnp.zeros(l_ref.shape[1:], F32)
        acc_ref[br] = jnp.zeros(acc_ref.shape[1:], F32)

    def tile_step(br, kb, vt, kt, bias_d, add_fn):
        k0 = pl.multiple_of(kt * tk, tk)
        for g in range(N_KV):
            kblk = kb[g, pl.ds(k0, tk), :]
            vblk = vt[g, kt]
            add = add_fn(g)
            for hh in range(HPG):
                h = g * HPG + hh
                rows = slice(HEAD_DIM * h, HEAD_DIM * (h + 1))
                s = _dot(kblk, qt[rows, :])
                if bias_d is not None:
                    s = s + btoep[h, bias_d]
                if add is not None:
                    s = s + add
                m_old = m_ref[br, h]
                m_new = jnp.maximum(m_old, jnp.max(s, axis=0, keepdims=True))
                m_safe = jnp.where(m_new == NEG_INF, 0.0, m_new)
                alpha = jnp.exp(m_old - m_safe)
                p = jnp.exp(s - m_safe)
                l_ref[br, h] = alpha * l_ref[br, h] + jnp.sum(p, axis=0, keepdims=True)
                acc_ref[br, rows, :] = alpha * acc_ref[br, rows, :] + _dot(vblk, p.astype(BF16))
                m_ref[br, h] = m_new

    def flash_finish(br, gate_row0):
        for h in range(N_HEADS):
            rows = slice(HEAD_DIM * h, HEAD_DIM * (h + 1))
            w = gt_s[gate_row0 + h:gate_row0 + h + 1, :] * (1.0 / jnp.maximum(l_ref[br, h], 1e-30))
            ot[rows, :] = ot[rows, :] + acc_ref[br, rows, :] * w

    def sel_rows(g, kt, ok=None):
        nb = tk // SLC_BLOCK
        rows = [jnp.broadcast_to(selt[g, kt * nb + r], (SLC_BLOCK, tq)) for r in range(nb)]
        thr = 0.5 if ok is None else jnp.where(ok, 0.5, 2.0)
        return jnp.concatenate(rows, axis=0) > thr

    def neg_unless(cond):
        return jnp.where(cond, 0.0, NEG_INF)

    causal = rel >= 0
    SLC, WIN = 0, 1

    flash_init(SLC)
    flash_init(WIN)
    n_far = jnp.maximum(qi - 1, 0)

    def slc_pair(pi, carry):
        for sub in range(2):
            kt = 2 * pi + sub
            ok = kt < n_far
            ktc = jnp.minimum(kt, jnp.maximum(n_far - 1, 0))
            tile_step(SLC, ksb, vst, ktc, None, lambda g: neg_unless(sel_rows(g, ktc, ok)))
        return carry

    lax.fori_loop(0, (n_far + 1) // 2, slc_pair, 0)
    kt1 = jnp.maximum(qi - 1, 0)
    tile_step(SLC, ksb, vst, kt1, 1, lambda g: neg_unless(sel_rows(g, kt1, qi >= 1)))
    tile_step(SLC, ksb, vst, qi, 0, lambda g: neg_unless(sel_rows(g, qi) & causal))

    wt = WINDOW // tk
    for u in range(wt + 1):
        off = wt - u
        kt = qi - off
        ok_add = jnp.where(kt >= 0, 0.0, NEG_INF)
        ktc = jnp.maximum(kt, 0)
        if off == wt:
            add_fn = lambda g, a=ok_add: neg_unless(rel <= 0) + a
        elif off == 0:
            add_fn = lambda g: neg_unless(causal)
        else:
            add_fn = lambda g, a=ok_add: a
        tile_step(WIN, kwb, vwt, ktc, off if off <= 1 else None, add_fn)
    flash_finish(SLC, N_HEADS)
    flash_finish(WIN, 2 * N_HEADS)

    o_ref[0] = ot[...].T


def _attn_prompt(q, gates, ks, vs, kw, vw, kcmp, vcmp, table, bkt_toep, bkt_cmp, msel, tq):
    b, t, _ = q.shape
    tk = ATTN_TK
    nkt = t // tk
    ncmp = kcmp.shape[1]
    n_near = bkt_toep.shape[0]
    qspec = pl.BlockSpec((1, tq, D_MODEL), lambda bi, i: (bi, i, 0))
    seq = pl.BlockSpec((1, t, KV_W), lambda bi, i: (bi, 0, 0))
    seq_fm = pl.BlockSpec((1, N_KV, HEAD_DIM, t), lambda bi, i: (bi, 0, 0, 0))
    cmp_spec = pl.BlockSpec((1, ncmp, KV_W), lambda bi, i: (bi, 0, 0))
    return pl.pallas_call(
        functools.partial(_attn_p_body, tq, t),
        grid=(b, t // tq),
        in_specs=[qspec, pl.BlockSpec((1, tq, 128), lambda bi, i: (bi, i, 0)),
                  seq, seq_fm, seq, seq_fm, cmp_spec, cmp_spec,
                  pl.BlockSpec(memory_space=pltpu.SMEM),
                  _const_spec(bkt_toep.shape), _const_spec(bkt_cmp.shape), _const_spec(msel.shape)],
        out_specs=qspec,
        out_shape=jax.ShapeDtypeStruct((b, t, D_MODEL), F32),
        scratch_shapes=[pltpu.VMEM((N_KV, t, HEAD_DIM), BF16), pltpu.VMEM((N_KV, t, HEAD_DIM), BF16),
                        pltpu.VMEM((N_KV, nkt, HEAD_DIM, tk), BF16), pltpu.VMEM((N_KV, nkt, HEAD_DIM, tk), BF16),
                        pltpu.VMEM((N_KV, ncmp, HEAD_DIM), BF16), pltpu.VMEM((N_KV, HEAD_DIM, ncmp), BF16),
                        pltpu.VMEM((N_HEADS, n_near + 1, tk, tq), F32),
                        pltpu.VMEM((N_HEADS,) + bkt_cmp.shape, F32),
                        pltpu.VMEM((D_MODEL, tq), BF16), pltpu.VMEM((N_KV, t // SLC_BLOCK, 1, tq), F32),
                        pltpu.VMEM((D_MODEL, tq), F32),
                        pltpu.VMEM((2, N_HEADS, 1, tq), F32), pltpu.VMEM((2, N_HEADS, 1, tq), F32),
                        pltpu.VMEM((2, D_MODEL, tq), F32), pltpu.VMEM((128, tq), F32)],
        compiler_params=_params(("arbitrary", "arbitrary")),
        name="attn_prompt",
    )(q, gates, ks, vs, kw, vw, kcmp, vcmp, table, bkt_toep, bkt_cmp, msel)


PAGE = 128
PAGES_PER_GROUP = 32
Q_ROWS = 8


def _rank_desc(score_row, n):
    a = jnp.broadcast_to(score_row, (n, n))
    at = a.T
    i = lax.broadcasted_iota(jnp.int32, (n, n), 0)
    j = lax.broadcasted_iota(jnp.int32, (n, n), 1)
    beats = (at > a) | ((i < j) & (at == a))
    return jnp.sum(beats.astype(F32), axis=0, keepdims=True)


def _bias_rows(bucket, tcols):
    r = tcols.shape[0]
    out = jnp.zeros((r, bucket.shape[1]), F32)
    for bb in range(NUM_BUCKETS):
        out = jnp.where(bucket == bb, tcols[:, bb:bb + 1], out)
    return out


def _softmax_rows(s, mask):
    s = jnp.where(mask, s, NEG_INF)
    m = jnp.max(s, axis=1, keepdims=True)
    m = jnp.where(m == NEG_INF, 0.0, m)
    e = jnp.where(mask, jnp.exp(s - m), 0.0)
    return e / jnp.maximum(jnp.sum(e, axis=1, keepdims=True), 1e-30)


def _dot_nt(a, b):
    return lax.dot_general(a, b, (((1,), (1,)), ((), ())), preferred_element_type=F32)


def _cmp_s_body(past, pt_ref, ck_hbm, cv_hbm, kcn_ref, vcn_ref, q_ref,
                wek_ref, pek_ref, w1k_ref, w2k_ref, wev_ref, pev_ref, w1v_ref, w2v_ref,
                tbl_ref, bkt_ref, msel_ref, ocmp_ref, idx_ref,
                kbuf, vbuf, tbuf, hk, hv, sem):
    b = pl.program_id(0)
    rows_pg = PAGES_PER_GROUP * PAGE
    cpg = rows_pg // CMP_STRIDE
    ngrp = past // rows_pg
    nc = past // CMP_STRIDE
    half = N_KV * CMP_HID

    def copies(bb, gi, slot):
        out = []
        for p in range(PAGES_PER_GROUP):
            page = pt_ref[bb, gi * PAGES_PER_GROUP + p]
            out.append(pltpu.make_async_copy(ck_hbm.at[page], kbuf.at[slot, p], sem.at[0, slot]))
            out.append(pltpu.make_async_copy(cv_hbm.at[page], vbuf.at[slot, p], sem.at[1, slot]))
        return out

    assert ngrp % 2 == 0

    @pl.when(b == 0)
    def _():
        for c in copies(b, 0, 0):
            c.start()

    def group(gi, carry):
        slot = gi % 2

        @pl.when(gi + 1 < ngrp)
        def _():
            for c in copies(b, gi + 1, 1 - slot):
                c.start()

        for c in copies(b, gi, slot):
            c.wait()
        r0 = pl.multiple_of(gi * cpg, cpg)
        for ti, (buf, we, hdst) in enumerate(((kbuf, wek_ref, hk), (vbuf, wev_ref, hv))):
            for p in range(PAGES_PER_GROUP):
                for hf in range(2):
                    tbuf[ti, hf, p * PAGE:(p + 1) * PAGE, :] = buf[slot, p, hf * 128:(hf + 1) * 128, :].T
            hdst[pl.ds(r0, cpg), :] = _compress_hidden(
                lambda k, ti=ti: tbuf[ti, k % 2, pl.ds(k // 2, cpg, stride=CMP_STRIDE), :], cpg, we)
        return carry

    lax.fori_loop(0, ngrp, group, 0)

    @pl.when(b + 1 < pl.num_programs(0))
    def _():
        for c in copies(b + 1, 0, 0):
            c.start()

    col = lax.broadcasted_iota(jnp.int32, (1, nc), 1)
    cmask = (CMP_STRIDE * col + (2 * CMP_STRIDE - 1)) <= past
    row8 = lax.broadcasted_iota(jnp.int32, (Q_ROWS, nc), 0)
    cmpd = []
    for new_ref, we, pe, w1, w2, hdst in ((kcn_ref, wek_ref, pek_ref, w1k_ref, w2k_ref, hk),
                                          (vcn_ref, wev_ref, pev_ref, w1v_ref, w2v_ref, hv)):
        new = jnp.broadcast_to(new_ref[0], (8, KV_W)).astype(BF16)
        new_lo = _dot(new[:, :128], we[0:128, :])
        new_hi = _dot(new[:, 128:], we[0:128, :])
        w = 2 * CMP_HID
        hdst[nc:nc + 8, :] = jnp.concatenate([new_lo[:, :w], new_hi[:, :w], new_lo[:, w:], new_hi[:, w:]], axis=1)
        a = hdst[0:nc, 0:half] + hdst[1:nc + 1, half:2 * half] + _pe_term(pe, w1)
        a = (a * _sigmoid(a)).astype(BF16)
        cmpd.append([_dot(a[:, CMP_HID * g:CMP_HID * (g + 1)], w2[...]).astype(BF16) for g in range(N_KV)])

    nsel = msel_ref.shape[1]
    lane = lax.broadcasted_iota(jnp.int32, (1, nsel), 1)
    n_blocks = past // SLC_BLOCK + 1
    cur = past // SLC_BLOCK
    forced = (lane == 0) | (lane == cur) | (lane == cur - 1)
    valid = (lane * SLC_BLOCK <= past) & (lane < n_blocks)
    k_iota = lax.broadcasted_iota(jnp.int32, (N_SELECT, nsel), 0).astype(F32)
    lane_f = lax.broadcasted_iota(jnp.int32, (N_SELECT, nsel), 1).astype(F32)
    groups = range(N_KV)
    msel = msel_ref[...]
    bkt = bkt_ref[...]
    s_g = [_dot_nt(q_ref[0, g].astype(BF16), cmpd[0][g]) * SCALE + _bias_rows(bkt, tbl_ref[g]) for g in groups]
    p_g = [_softmax_rows(s, cmask) for s in s_g]
    for g in groups:
        ocmp_ref[0, g] = _dot(p_g[g].astype(BF16), cmpd[1][g])
    pgs = [jnp.sum(jnp.where(row8 < HPG, p, 0.0), axis=0, keepdims=True) for p in p_g]
    parts = [_split3(jnp.broadcast_to(pg, (8, nc))) for pg in pgs]
    scores = [(_dot(hi, msel) + _dot(mid, msel) + _dot(lo, msel))[0:1, :] for hi, mid, lo in parts]
    scores = [jnp.where(forced, jnp.inf, jnp.where(valid, sc, NEG_INF)) for sc in scores]
    rank_g = [_rank_desc(sc, nsel) for sc in scores]
    hits = [jnp.where(jnp.broadcast_to(rk, (N_SELECT, nsel)) == k_iota, lane_f, 0.0) for rk in rank_g]
    for g in groups:
        idx_ref[0, g] = jnp.sum(hits[g], axis=1, keepdims=True).astype(jnp.int32)


def _cmp_sample(page_table, cache_k, cache_v, kc_new, vc_new, q8, wk, wv, tbl8, bkt, msel, past):
    nb = page_table.shape[0]
    rows_pg = PAGES_PER_GROUP * PAGE
    nc = past // CMP_STRIDE
    any_spec = pl.BlockSpec(memory_space=pl.ANY)
    new_spec = pl.BlockSpec((1, 1, KV_W), lambda i, pt: (i, 0, 0))
    cs = lambda shape: pl.BlockSpec(shape, lambda i, pt: (0,) * len(shape))
    wspecs = [cs((CMP_STRIDE * 128, 4 * CMP_HID)),
              cs((8, 2 * CMP_STRIDE * HEAD_DIM)),
              cs((2 * CMP_STRIDE * HEAD_DIM, CMP_HID)), cs((CMP_HID, HEAD_DIM))]
    grid_spec = pltpu.PrefetchScalarGridSpec(
        num_scalar_prefetch=1,
        grid=(nb,),
        in_specs=[any_spec, any_spec, new_spec, new_spec,
                  pl.BlockSpec((1, N_KV, Q_ROWS, HEAD_DIM), lambda i, pt: (i, 0, 0, 0))]
                 + wspecs + wspecs + [cs(tbl8.shape), cs(bkt.shape), cs(msel.shape)],
        out_specs=[pl.BlockSpec((1, N_KV, Q_ROWS, HEAD_DIM), lambda i, pt: (i, 0, 0, 0)),
                   pl.BlockSpec((1, N_KV, N_SELECT, 1), lambda i, pt: (i, 0, 0, 0))],
        scratch_shapes=[pltpu.VMEM((2, PAGES_PER_GROUP, 2 * PAGE, 128), F32),
                        pltpu.VMEM((2, PAGES_PER_GROUP, 2 * PAGE, 128), F32),
                        pltpu.VMEM((2, 2, rows_pg, 128), F32),
                        pltpu.VMEM((nc + 8, 2 * N_KV * CMP_HID), F32),
                        pltpu.VMEM((nc + 8, 2 * N_KV * CMP_HID), F32),
                        pltpu.SemaphoreType.DMA((2, 2))])
    return pl.pallas_call(
        functools.partial(_cmp_s_body, past),
        grid_spec=grid_spec,
        out_shape=[jax.ShapeDtypeStruct((nb, N_KV, Q_ROWS, HEAD_DIM), F32),
                   jax.ShapeDtypeStruct((nb, N_KV, N_SELECT, 1), jnp.int32)],
        compiler_params=_params(("arbitrary",)),
        name="cmp_sample",
    )(page_table, cache_k, cache_v, kc_new, vc_new, q8, *wk, *wv, tbl8, bkt, msel)


NEW_LANES = 128


def _dec_s_body(past, idx_ref, pt_ref, csk_hbm, csv_hbm,
                q_ref, ksn_ref, vsn_ref, kwn_ref, vwn_ref, wk_ref, wv_ref, gt_ref, ocmp_ref,
                tbl_ref, bktw_ref, o_ref, wko_ref, wvo_ref,
                kbuf, vbuf, kwbuf, vwbuf, sem):
    b = pl.program_id(0)
    n_pages = past // PAGE
    cur = past // SLC_BLOCK
    bpp = PAGE // SLC_BLOCK
    lsel = N_SELECT * PAGE
    wb = wk_ref.shape[3]

    @pl.when(b == 0)
    def _():
        kbuf[...] = jnp.zeros_like(kbuf)
        vbuf[...] = jnp.zeros_like(vbuf)
        kwbuf[...] = jnp.zeros_like(kwbuf)
        vwbuf[...] = jnp.zeros_like(vwbuf)

    def gathers(bb):
        half = bb % 2
        out = []
        for g in range(N_KV):
            for k in range(N_SELECT):
                blk = idx_ref[bb, g * N_SELECT + k]
                page = pt_ref[bb, jnp.minimum(blk // bpp, n_pages - 1)]
                dst = pl.ds(k * PAGE, PAGE)
                out.append(pltpu.make_async_copy(csk_hbm.at[page, g], kbuf.at[half, g, :, dst], sem.at[0, half]))
                out.append(pltpu.make_async_copy(csv_hbm.at[page, g], vbuf.at[half, g, :, dst], sem.at[1, half]))
        return out

    @pl.when(b == 0)
    def _():
        for c in gathers(b):
            c.start()

    @pl.when(b + 1 < pl.num_programs(0))
    def _():
        for c in gathers(b + 1):
            c.start()

    cur_half = b % 2

    lane_wb = lax.broadcasted_iota(jnp.int32, (HEAD_DIM, wb), 1)
    for g in range(N_KV):
        for src, new_ref, dst, buf in ((wk_ref, kwn_ref, wko_ref, kwbuf), (wv_ref, vwn_ref, wvo_ref, vwbuf)):
            st = src[0, g]
            newc = new_ref[0, g]
            dst[0, g] = jnp.where(lane_wb == wb - 1, newc, pltpu.roll(st, wb - 1, 1))
            buf[g, :, 0:wb] = st
            buf[g, :, wb:wb + 1] = newc

    for c in gathers(b):
        c.wait()

    ls = lsel + NEW_LANES
    lw = wb + NEW_LANES
    lane_s = lax.broadcasted_iota(jnp.int32, (1, ls), 1)
    lane_w = lax.broadcasted_iota(jnp.int32, (1, lw), 1)
    wmask = lane_w <= wb
    tok = lane_s % PAGE
    groups = range(N_KV)
    smasks, sbiases = [], []
    for g in groups:
        kbuf[cur_half, g, :, lsel:lsel + 1] = ksn_ref[0, g]
        vbuf[cur_half, g, :, lsel:lsel + 1] = vsn_ref[0, g]
        kpos = past + (lane_s - lsel)
        gathered = lane_s < 0
        has_new = jnp.int32(0)
        for k in range(N_SELECT):
            blk = idx_ref[b, g * N_SELECT + k]
            in_slot = (lane_s // PAGE) == k
            kpos = jnp.where(in_slot, (blk // bpp) * PAGE + tok, kpos)
            gathered = gathered | (in_slot & (blk < cur) & ((tok // SLC_BLOCK) == (blk % bpp)))
            has_new = has_new | (blk == cur).astype(jnp.int32)
        smasks.append((gathered | ((lane_s == lsel) & (has_new > 0))) & (kpos <= past))
        sbiases.append(_bias_rows(_rel_bucket(past - kpos), tbl_ref[g]))
    bktw = bktw_ref[...]
    qgs = [q_ref[0, g].astype(BF16) for g in groups]
    s_slc = [_dot(qgs[g], kbuf[cur_half, g].astype(BF16)) * SCALE + sbiases[g] for g in groups]
    s_win = [_dot(qgs[g], kwbuf[g].astype(BF16)) * SCALE + _bias_rows(bktw, tbl_ref[g]) for g in groups]
    p_slc = [_softmax_rows(s_slc[g], smasks[g]) for g in groups]
    p_win = [_softmax_rows(s, wmask) for s in s_win]
    o_slc = [_dot_nt(p_slc[g].astype(BF16), vbuf[cur_half, g].astype(BF16)) for g in groups]
    o_win = [_dot_nt(p_win[g].astype(BF16), vwbuf[g].astype(BF16)) for g in groups]
    for g in groups:
        gt = gt_ref[0, g]
        o_ref[0, g] = gt[:, 0:1] * ocmp_ref[0, g] + gt[:, 1:2] * o_slc[g] + gt[:, 2:3] * o_win[g]


def _dec_sample(idx, page_table, cache_sk, cache_sv, q8, ks_new, vs_new, kw_new, vw_new, state_wk, state_wv,
                gates8, ocmp, tbl8, bktw, past):
    nb, wb = state_wk.shape[0], state_wk.shape[3]
    any_spec = pl.BlockSpec(memory_space=pl.ANY)
    new_spec = pl.BlockSpec((1, N_KV, HEAD_DIM, 1), lambda i, *_: (i, 0, 0, 0))
    st_spec = pl.BlockSpec((1, N_KV, HEAD_DIM, wb), lambda i, *_: (i, 0, 0, 0))
    head_spec = pl.BlockSpec((1, N_KV, Q_ROWS, HEAD_DIM), lambda i, *_: (i, 0, 0, 0))
    cs = lambda shape: pl.BlockSpec(shape, lambda i, *_: (0,) * len(shape))
    lsel = N_SELECT * PAGE
    grid_spec = pltpu.PrefetchScalarGridSpec(
        num_scalar_prefetch=2,
        grid=(nb,),
        in_specs=[any_spec, any_spec, head_spec,
                  new_spec, new_spec, new_spec, new_spec, st_spec, st_spec,
                  pl.BlockSpec((1, N_KV, Q_ROWS, 3), lambda i, *_: (i, 0, 0, 0)), head_spec,
                  cs(tbl8.shape), cs(bktw.shape)],
        out_specs=[head_spec, st_spec, st_spec],
        scratch_shapes=[pltpu.VMEM((2, N_KV, HEAD_DIM, lsel + NEW_LANES), F32),
                        pltpu.VMEM((2, N_KV, HEAD_DIM, lsel + NEW_LANES), F32),
                        pltpu.VMEM((N_KV, HEAD_DIM, wb + NEW_LANES), F32),
                        pltpu.VMEM((N_KV, HEAD_DIM, wb + NEW_LANES), F32),
                        pltpu.SemaphoreType.DMA((2, 2))])
    return pl.pallas_call(
        functools.partial(_dec_s_body, past),
        grid_spec=grid_spec,
        out_shape=[jax.ShapeDtypeStruct((nb, N_KV, Q_ROWS, HEAD_DIM), F32),
                   jax.ShapeDtypeStruct(state_wk.shape, F32),
                   jax.ShapeDtypeStruct(state_wv.shape, F32)],
        compiler_params=_params(("arbitrary",)),
        name="dec_sample",
    )(idx, page_table, cache_sk, cache_sv, q8, ks_new, vs_new, kw_new, vw_new,
      state_wk, state_wv, gates8, ocmp, tbl8, bktw)


def _expand_w1(w1):
    eye = jnp.eye(2, dtype=w1.dtype)
    parts = []
    for part in (w1[:CMP_STRIDE * HEAD_DIM], w1[CMP_STRIDE * HEAD_DIM:]):
        w = part.reshape(CMP_STRIDE, HEAD_DIM, CMP_HID)
        parts.append(jnp.einsum("gh,rdn->rgdhn", eye, w).reshape(CMP_STRIDE, 2 * HEAD_DIM, 2 * CMP_HID))
    w = jnp.concatenate(parts, axis=-1).astype(BF16)
    return w.reshape(CMP_STRIDE * 2 * HEAD_DIM, 4 * CMP_HID)


def _expand_w2(w2):
    eye = jnp.eye(N_KV, dtype=w2.dtype)
    return jnp.einsum("gh,nd->gnhd", eye, w2).reshape(N_KV * CMP_HID, KV_W).astype(BF16)


def _pe_rows(pe):
    return jnp.broadcast_to(pe.reshape(1, -1), (8, pe.size)).astype(BF16)


def _select_matrix(n_cmp, n_blocks):
    r = SLC_BLOCK // CMP_STRIDE
    m = np.zeros((n_cmp, n_blocks), np.float32)
    for s in range(n_blocks):
        for a in range(r):
            for bb in range(2):
                c = r * s + a - bb
                if 0 <= c < n_cmp:
                    m[c, s] += 1.0
    return m


def _pad_to(a, size, axis):
    pad = [(0, 0)] * a.ndim
    pad[axis] = (0, size - a.shape[axis])
    return jnp.pad(a, pad)


def kernel(x_prompt, x_sample, cache_cmp_k, cache_cmp_v, cache_slc_k, cache_slc_v, state_win_k, state_win_v, state_conv, page_table, p_prompt, p_sample, rel_table, norm_mix, norm_ffn, norm_ple, norm_final, conv_w1, conv_b1, conv_dw, conv_dwb, conv_ln_g, conv_ln_b, conv_w2, conv_b2, attn_w_in, attn_w_out, cmpk_w1, cmpk_pe, cmpk_w2, cmpv_w1, cmpv_pe, cmpv_w2, mlp_up, mlp_down, ple_proj, ple_gate):
    nbp, t, _ = x_prompt.shape
    nbs = x_sample.shape[0]
    n_p = nbp * t
    n_pool = cache_cmp_k.shape[1]
    past = page_table.shape[1] * cache_cmp_k.shape[2]
    wb = state_win_k.shape[2]
    hist = CONV_W - 1
    tm_p, tf, tt, tq = 512, 1024, 256, ATTN_TK

    row = lambda a: a.reshape(1, -1)
    bf = lambda a: a.astype(BF16)

    w1c, w2c = bf(conv_w1[0]), bf(conv_w2[0])
    dw3 = conv_dw[0].reshape(CONV_W, D_MODEL // 128, 128).transpose(1, 0, 2)
    dwb3 = conv_dwb[0].reshape(D_MODEL // 128, 1, 128)
    up, dn = bf(mlp_up), bf(mlp_down)
    wg, wp = bf(ple_gate), bf(ple_proj)
    w_in = attn_w_in[0]
    wq = bf(w_in[:, :D_MODEL])
    wkv = bf(w_in[:, D_MODEL:D_MODEL + 6 * KV_W])
    wgt = bf(_pad_to(w_in[:, D_MODEL + 6 * KV_W:], 128, 1))
    w_out = bf(attn_w_out[0])
    cw = []
    for w1, pe, w2 in ((cmpk_w1[0], cmpk_pe[0], cmpk_w2[0]), (cmpv_w1[0], cmpv_pe[0], cmpv_w2[0])):
        cw.append((_expand_w1(w1), _pe_rows(pe), bf(w1), _expand_w2(w2), bf(w2)))
    wk_p, wv_p = [c[:4] for c in cw]
    wk_s, wv_s = [c[:3] + c[4:] for c in cw]

    def tail(x, i, p, tm, final, pre=None):
        return _mlp_ple(x, row(norm_ffn[i]), up, dn, row(norm_ple[i]), wg, p, i, wp,
                        row(norm_final), tm, tf, final, pre)

    conv_args = (row(conv_ln_g[0]), row(conv_ln_b[0]), w2c, row(conv_b2[0]))

    xp = x_prompt.reshape(n_p, D_MODEL)
    glu = _conv_in(xp, row(norm_mix[0]), w1c, row(conv_b1[0]), tm_p).reshape(nbp, t, D_MODEL)
    x1 = _conv_out(glu, x_prompt, dw3, dwb3, *conv_args, tt).reshape(n_p, D_MODEL)
    conv_p = glu[:, t - hist:][None]
    pp = p_prompt.reshape(-1, n_p, D_PLE)
    ps = p_sample.reshape(-1, nbs, D_PLE)
    x2 = tail(x1, 0, pp, tm_p, False)
    q, gates, kcl, kch, vcl, vch, ks, kw, *rest = _attn_in(x2, row(norm_mix[1]), wq, wkv, wgt, tm_p, seq_len=t)
    kv_fm, win_fm = rest[:len(KV_NAMES)], rest[len(KV_NAMES):]
    vs_fm, vw_fm = kv_fm[KV_NAMES.index("vs")], kv_fm[KV_NAMES.index("vw")]
    seq = lambda a: a.reshape(nbp, t, -1)
    kcmp, vcmp = _compress_prompt((seq(kcl), seq(kch)), (seq(vcl), seq(vch)), wk_p, wv_p)

    ncmp = t // CMP_STRIDE
    ii = jnp.arange(tq, dtype=jnp.int32)[None, :]
    jj = jnp.arange(ATTN_TK, dtype=jnp.int32)[:, None]
    r_qk = tq // ATTN_TK
    bkt_toep = jnp.stack([_rel_bucket(d * ATTN_TK + ii - jj) for d in range(1 - r_qk, 2)])
    nshift = (t // tq - 1) * (tq // CMP_STRIDE)
    rr = jnp.arange(ncmp + nshift, dtype=jnp.int32)[:, None]
    bkt_cmp = _rel_bucket(ii - CMP_STRIDE * (rr - nshift) - (2 * CMP_STRIDE - 1))
    msel_p = jnp.asarray(_select_matrix(ncmp, t // SLC_BLOCK).T, BF16)
    o_p = _attn_prompt(seq(q), seq(gates), seq(ks), vs_fm, seq(kw), vw_fm, kcmp, vcmp,
                       rel_table, bkt_toep, bkt_cmp, msel_p, tq)
    y_p = tail(x2, 1, pp, tm_p, True, pre=(o_p.reshape(n_p, D_MODEL), w_out))

    from_fm = lambda a: jnp.transpose(a, (0, 3, 1, 2))[None]
    kv_p = [from_fm(a) for a in kv_fm[:4]] + [from_fm(a) for a in win_fm]

    xs = x_sample.reshape(nbs, D_MODEL)
    glu_s = _conv_in(xs, row(norm_mix[0]), w1c, row(conv_b1[0]), nbs)
    x1s, conv_s = _conv_step(state_conv[0], glu_s, xs, conv_dw[0], row(conv_dwb[0]), *conv_args)
    x2s = tail(x1s, 0, ps, nbs, False)
    qs, gts, kcs, vcs, kss, vss, kws, vws = _attn_in(x2s, row(norm_mix[1]), wq, wkv, wgt, nbs)

    q8 = _pad_to(qs.reshape(nbs, N_KV, HPG, HEAD_DIM), Q_ROWS, 2)
    tbl8 = _pad_to(rel_table.T.reshape(N_KV, HPG, NUM_BUCKETS), Q_ROWS, 1)
    nc_s = past // CMP_STRIDE
    cc = jnp.arange(nc_s, dtype=jnp.int32)[None, :]
    bkt_s = _rel_bucket(past - (CMP_STRIDE * cc + 2 * CMP_STRIDE - 1))
    n_blocks_s = past // SLC_BLOCK + 1
    msel_s = jnp.asarray(_select_matrix(nc_s, -(-n_blocks_s // 128) * 128), BF16)
    new3 = lambda a: a.reshape(nbs, 1, KV_W)
    fmajor = lambda c: jnp.transpose(c[0], (0, 2, 3, 1))
    cache2 = lambda c: fmajor(c).reshape(n_pool, 2 * PAGE, PAGE)
    ocmp, idx = _cmp_sample(page_table, cache2(cache_cmp_k), cache2(cache_cmp_v), new3(kcs), new3(vcs),
                            q8, wk_s, wv_s, tbl8, bkt_s, msel_s, past)

    ww = jnp.arange(wb + NEW_LANES, dtype=jnp.int32)[None, :]
    bkt_w = _rel_bucket(wb - ww)
    gates8 = _pad_to(gts[:, :3 * N_HEADS].reshape(nbs, 3, N_KV, HPG).transpose(0, 2, 3, 1), Q_ROWS, 2)
    col4 = lambda a: a.reshape(nbs, N_KV, HEAD_DIM, 1)
    o8, wk_new, wv_new = _dec_sample(idx.reshape(nbs, N_KV * N_SELECT), page_table,
                                     fmajor(cache_slc_k), fmajor(cache_slc_v), q8,
                                     col4(kss), col4(vss), col4(kws), col4(vws),
                                     fmajor(state_win_k), fmajor(state_win_v), gates8, ocmp, tbl8, bkt_w, past)
    o_s = o8[:, :, :HPG].reshape(nbs, D_MODEL)
    y_s = tail(x2s, 1, ps, nbs, True, pre=(o_s, w_out))

    kv5s = lambda a: a.reshape(1, nbs, 1, N_KV, HEAD_DIM)
    return (y_p.reshape(nbp, t, D_MODEL), y_s.reshape(nbs, 1, D_MODEL), *kv_p, conv_p,
            kv5s(kcs), kv5s(vcs), kv5s(kss), kv5s(vss), from_fm(wk_new), from_fm(wv_new), conv_s[None])
```

```python
import functools
import math

import numpy as np
import jax
import jax.numpy as jnp
from jax import lax
from jax.experimental import pallas as pl
from jax.experimental.pallas import tpu as pltpu

F32 = jnp.float32
BF16 = jnp.bfloat16

D_MODEL = 1024
D_PLE = 256
CONV_W = 31
N_HEADS = 16
HEAD_DIM = 64
N_KV = 4
HPG = 4
KV_W = N_KV * HEAD_DIM
CMP_STRIDE = 16
CMP_HID = 128
SLC_BLOCK = 64
N_SELECT = 16
WINDOW = 512
NUM_BUCKETS = 32
MAX_DISTANCE = 128
D_FF = 4096
EPS = 1e-6
SCALE = HEAD_DIM ** -0.5
assert math.frexp(SCALE)[0] == 0.5
NEG_INF = float("-inf")

ATTN_TK = 128
assert ATTN_TK >= MAX_DISTANCE
LOG2E = math.log2(math.e)

VMEM_LIMIT = 56 * 1024 * 1024
HALO = 32


def _sigmoid(x):
    return 1.0 / (1.0 + jnp.exp(-x))


def _rms(x, g):
    return x * lax.rsqrt(jnp.mean(x * x, axis=-1, keepdims=True) + EPS) * g


def _dot(a, b):
    return jnp.dot(a, b, preferred_element_type=F32)


def _dot_lanes(a, b):
    n = b.shape[1]
    return jnp.concatenate([_dot(a, b[:, c:c + 128]) for c in range(0, n, 128)], axis=1)


def _split3(x):
    hi = x.astype(BF16)
    r1 = x - hi.astype(F32)
    mid = r1.astype(BF16)
    lo = (r1 - mid.astype(F32)).astype(BF16)
    return hi, mid, lo


def _rel_bucket(dist):
    n = jnp.maximum(dist, 0)
    max_exact = NUM_BUCKETS // 2
    nf = jnp.maximum(n, 1).astype(F32)
    large = max_exact + (jnp.log(nf / max_exact) / math.log(MAX_DISTANCE / max_exact)
                         * (NUM_BUCKETS - max_exact)).astype(jnp.int32)
    large = jnp.minimum(large, NUM_BUCKETS - 1)
    return jnp.where(n < max_exact, n, large)


def _params(sem):
    return pltpu.CompilerParams(dimension_semantics=sem, vmem_limit_bytes=VMEM_LIMIT)


def _const_spec(shape, single=False):
    n = len(shape)
    if single:
        return pl.BlockSpec(shape, lambda *_: (0,) * n, pipeline_mode=pl.Buffered(1))
    return pl.BlockSpec(shape, lambda *_: (0,) * n)


def _conv_in_body(x_ref, g_ref, w_ref, b_ref, o_ref):
    h = _rms(x_ref[...], g_ref[...]).astype(BF16)
    u = _dot(h, w_ref[...]) + b_ref[...]
    o_ref[...] = u[:, :D_MODEL] * _sigmoid(u[:, D_MODEL:])


def _conv_in(x, g, w1, b1, tm):
    n = x.shape[0]
    return pl.pallas_call(
        _conv_in_body,
        grid=(n // tm,),
        in_specs=[pl.BlockSpec((tm, D_MODEL), lambda i: (i, 0)),
                  _const_spec((1, D_MODEL)),
                  _const_spec((D_MODEL, 2 * D_MODEL)),
                  _const_spec((1, 2 * D_MODEL))],
        out_specs=pl.BlockSpec((tm, D_MODEL), lambda i: (i, 0)),
        out_shape=jax.ShapeDtypeStruct((n, D_MODEL), F32),
        compiler_params=_params(("parallel",)),
        name="conv_in",
    )(x, g, w1, b1)


def _ln_silu_proj(y, lng, lnb, w2, b2, x):
    mu = jnp.mean(y, axis=-1, keepdims=True)
    yc = y - mu
    var = jnp.mean(yc * yc, axis=-1, keepdims=True)
    z = yc * lax.rsqrt(var + EPS) * lng + lnb
    z = z * _sigmoid(z)
    return _dot(z.astype(BF16), w2) + b2 + x


def _conv_out_body(tt, cur_ref, halo_ref, x_ref, dw_ref, dwb_ref, lng_ref, lnb_ref, w2_ref, b2_ref,
                   o_ref, ctx_ref, y_ref):
    i = pl.program_id(1)
    n_strip = D_MODEL // 128
    keep = (i > 0).astype(F32)
    for c in range(n_strip):
        ctx_ref[c, 0:HALO, :] = halo_ref[0, :, 128 * c:128 * (c + 1)] * keep
        ctx_ref[c, HALO:, :] = cur_ref[0, :, 128 * c:128 * (c + 1)]

    def strip(c, carry):
        acc = jnp.broadcast_to(dwb_ref[c], (tt, 128))
        for k in range(CONV_W):
            off = k + HALO - (CONV_W - 1)
            acc = acc + ctx_ref[c, off:off + tt, :] * dw_ref[c, k:k + 1, :]
        y_ref[c] = acc
        return carry

    lax.fori_loop(0, n_strip, strip, 0)
    y = jnp.concatenate([y_ref[c] for c in range(n_strip)], axis=1)
    o_ref[0] = _ln_silu_proj(y, lng_ref[...], lnb_ref[...], w2_ref[...], b2_ref[...], x_ref[0])


def _conv_out(glu, x, dw3, dwb3, lng, lnb, w2, b2, tt):
    b, t, _ = glu.shape
    hb = tt // HALO
    return pl.pallas_call(
        functools.partial(_conv_out_body, tt),
        grid=(b, t // tt),
        in_specs=[pl.BlockSpec((1, tt, D_MODEL), lambda bi, i: (bi, i, 0)),
                  pl.BlockSpec((1, HALO, D_MODEL), lambda bi, i: (bi, jnp.maximum(i * hb - 1, 0), 0)),
                  pl.BlockSpec((1, tt, D_MODEL), lambda bi, i: (bi, i, 0)),
                  _const_spec((D_MODEL // 128, CONV_W, 128)),
                  _const_spec((D_MODEL // 128, 1, 128)),
                  _const_spec((1, D_MODEL)),
                  _const_spec((1, D_MODEL)),
                  _const_spec((D_MODEL, D_MODEL)),
                  _const_spec((1, D_MODEL))],
        out_specs=pl.BlockSpec((1, tt, D_MODEL), lambda bi, i: (bi, i, 0)),
        out_shape=jax.ShapeDtypeStruct((b, t, D_MODEL), F32),
        scratch_shapes=[pltpu.VMEM((D_MODEL // 128, tt + HALO, 128), F32),
                        pltpu.VMEM((D_MODEL // 128, tt, 128), F32)],
        compiler_params=_params(("parallel", "arbitrary")),
        name="conv_out",
    )(glu, glu, x, dw3, dwb3, lng, lnb, w2, b2)


def _conv_step_body(st_ref, u_ref, x_ref, dw_ref, dwb_ref, lng_ref, lnb_ref, w2_ref, b2_ref,
                    o_ref, ns_ref):
    nb = st_ref.shape[0]
    hist = CONV_W - 1
    dwh = dw_ref[0:hist, :]
    rows = []
    for bi in range(nb):
        rows.append(jnp.sum(st_ref[bi] * dwh, axis=0, keepdims=True))
        ns_ref[bi, 0:hist - 1, :] = st_ref[bi, 1:hist, :]
        ns_ref[bi, hist - 1:hist, :] = u_ref[bi:bi + 1, :]
    y = jnp.concatenate(rows, axis=0) + u_ref[...] * dw_ref[hist:hist + 1, :] + dwb_ref[...]
    o_ref[...] = _ln_silu_proj(y, lng_ref[...], lnb_ref[...], w2_ref[...], b2_ref[...], x_ref[...])


def _conv_step(state, glu, x, dw, dwb, lng, lnb, w2, b2):
    nb, hist, _ = state.shape
    return pl.pallas_call(
        _conv_step_body,
        grid=(1,),
        in_specs=[_const_spec((nb, hist, D_MODEL)), _const_spec((nb, D_MODEL)), _const_spec((nb, D_MODEL)),
                  _const_spec((CONV_W, D_MODEL)), _const_spec((1, D_MODEL)), _const_spec((1, D_MODEL)),
                  _const_spec((1, D_MODEL)), _const_spec((D_MODEL, D_MODEL)), _const_spec((1, D_MODEL))],
        out_specs=[_const_spec((nb, D_MODEL)), _const_spec((nb, hist, D_MODEL))],
        out_shape=[jax.ShapeDtypeStruct((nb, D_MODEL), F32),
                   jax.ShapeDtypeStruct((nb, hist, D_MODEL), F32)],
        compiler_params=_params(("arbitrary",)),
        name="conv_step",
    )(state, glu, x, dw, dwb, lng, lnb, w2, b2)


def _mlp_body(final, pre, tf, *refs):
    if pre:
        o_in_ref, wo_ref, refs = refs[0], refs[1], refs[2:]
    x_ref, gf_ref, up_ref, dn_ref, gp_ref, wg_ref, p_ref, wp_ref, gfin_ref, o_ref = refs
    x1 = x_ref[...]
    if pre:
        x1 = x1 + _dot(o_in_ref[...].astype(BF16), wo_ref[...])
    h = _rms(x1, gf_ref[...]).astype(BF16)
    ple = _dot(p_ref[...].astype(BF16), wp_ref[...])
    x2 = x1
    for c in range(0, D_FF, tf):
        a = jnp.maximum(_dot(h, up_ref[:, c:c + tf]), 0.0)
        x2 = x2 + _dot((a * a).astype(BF16), dn_ref[c:c + tf, :])
    gate = _sigmoid(_dot(_rms(x2, gp_ref[...]).astype(BF16), wg_ref[...]))
    x3 = x2 + gate * ple
    if final:
        x3 = _rms(x3, gfin_ref[...])
    o_ref[...] = x3


def _mlp_ple(x, gf, up, dn, gp, wg, p, layer, wp, gfin, tm, tf, final, pre=None):
    n = x.shape[0]
    tok = lambda i: (i, 0)
    one = pl.Buffered(1)
    in_specs, args = [], []
    if pre is not None:
        in_specs += [pl.BlockSpec((tm, D_MODEL), tok), _const_spec((D_MODEL, D_MODEL), single=True)]
        args += list(pre)
    in_specs += [pl.BlockSpec((tm, D_MODEL), tok),
                 _const_spec((1, D_MODEL)),
                 pl.BlockSpec((None, D_MODEL, D_FF), lambda i: (layer, 0, 0), pipeline_mode=one),
                 pl.BlockSpec((None, D_FF, D_MODEL), lambda i: (layer, 0, 0), pipeline_mode=one),
                 _const_spec((1, D_MODEL)),
                 pl.BlockSpec((None, D_MODEL, D_MODEL), lambda i: (layer, 0, 0), pipeline_mode=one),
                 pl.BlockSpec((None, tm, D_PLE), lambda i: (layer, i, 0)),
                 pl.BlockSpec((None, D_PLE, D_MODEL), lambda i: (layer, 0, 0), pipeline_mode=one),
                 _const_spec((1, D_MODEL))]
    args += [x, gf, up, dn, gp, wg, p, wp, gfin]
    return pl.pallas_call(
        functools.partial(_mlp_body, final, pre is not None, tf),
        grid=(n // tm,),
        in_specs=in_specs,
        out_specs=pl.BlockSpec((tm, D_MODEL), tok),
        out_shape=jax.ShapeDtypeStruct((n, D_MODEL), F32),
        compiler_params=_params(("parallel",)),
        name="mlp_ple",
    )(*args)


KV_NAMES = ("kc", "vc", "ks", "vs", "kw", "vw")
KV_COMPRESSED = ("kc", "vc")
KV_TOKEN_MAJOR = ("ks", "kw")
KV_WINDOWED = ("kw", "vw")


def _attn_in_body(seq_len, x_ref, g_ref, wq_ref, wkv_ref, wg_ref, q_ref, gt_ref, *kv_refs):
    h = _rms(x_ref[...], g_ref[...]).astype(BF16)
    q_ref[...] = _dot(h, wq_ref[...])
    gt_ref[...] = _sigmoid(_dot(h, wg_ref[...]))
    kv = _dot(h, wkv_ref[...])
    blocks = {n: kv[:, KV_W * i:KV_W * (i + 1)] for i, n in enumerate(KV_NAMES)}
    if seq_len is None:
        for n, r in zip(KV_NAMES, kv_refs):
            r[...] = blocks[n]
        return
    n_half = 2 * len(KV_COMPRESSED)
    n_tm = n_half + len(KV_TOKEN_MAJOR)
    half_refs = kv_refs[:n_half]
    tm_refs = kv_refs[n_half:n_tm]
    fm_refs = kv_refs[n_tm:n_tm + len(KV_NAMES)]
    last_refs = dict(zip(KV_WINDOWED, kv_refs[n_tm + len(KV_NAMES):]))
    for i, n in enumerate(KV_COMPRESSED):
        half_refs[2 * i][...] = blocks[n][:, :128]
        half_refs[2 * i + 1][...] = blocks[n][:, 128:]
    for n, r in zip(KV_TOKEN_MAJOR, tm_refs):
        r[...] = blocks[n]
    tps = seq_len // x_ref.shape[0]
    is_last = pl.program_id(0) % tps == tps - 1
    for n, r in zip(KV_NAMES, fm_refs):
        bt = blocks[n].T
        for g in range(N_KV):
            r[0, g] = bt[HEAD_DIM * g:HEAD_DIM * (g + 1), :]
        if n in last_refs:

            @pl.when(is_last)
            def _(bt=bt, dst=last_refs[n]):
                for g in range(N_KV):
                    dst[0, g] = bt[HEAD_DIM * g:HEAD_DIM * (g + 1), :]


def _attn_in(x, g, wq, wkv, wg, tm, seq_len=None):
    n = x.shape[0]
    tok = lambda i: (i, 0)
    kv_spec = pl.BlockSpec((tm, KV_W), tok)
    kv_shape = jax.ShapeDtypeStruct((n, KV_W), F32)
    if seq_len is None:
        kv_specs, kv_shapes = [kv_spec] * 6, [kv_shape] * 6
    else:
        tps = seq_len // tm
        fm_spec = pl.BlockSpec((1, N_KV, HEAD_DIM, tm), lambda i: (i // tps, 0, 0, i % tps))
        fm_shape = jax.ShapeDtypeStruct((n // seq_len, N_KV, HEAD_DIM, seq_len), F32)
        assert tm == min(WINDOW, seq_len)
        last_spec = pl.BlockSpec((1, N_KV, HEAD_DIM, tm), lambda i: (i // tps, 0, 0, 0))
        last_shape = jax.ShapeDtypeStruct((n // seq_len, N_KV, HEAD_DIM, tm), F32)
        half_spec = pl.BlockSpec((tm, 128), tok)
        half_shape = jax.ShapeDtypeStruct((n, 128), F32)
        n_half = 2 * len(KV_COMPRESSED)
        kv_specs = ([half_spec] * n_half + [kv_spec] * len(KV_TOKEN_MAJOR) + [fm_spec] * 6
                    + [last_spec] * len(KV_WINDOWED))
        kv_shapes = ([half_shape] * n_half + [kv_shape] * len(KV_TOKEN_MAJOR) + [fm_shape] * 6
                     + [last_shape] * len(KV_WINDOWED))
    return pl.pallas_call(
        functools.partial(_attn_in_body, seq_len),
        grid=(n // tm,),
        in_specs=[pl.BlockSpec((tm, D_MODEL), tok), _const_spec((1, D_MODEL)),
                  _const_spec((D_MODEL, D_MODEL)), _const_spec((D_MODEL, 6 * KV_W)),
                  _const_spec((D_MODEL, 128))],
        out_specs=[pl.BlockSpec((tm, D_MODEL), tok), pl.BlockSpec((tm, 128), tok)] + kv_specs,
        out_shape=[jax.ShapeDtypeStruct((n, D_MODEL), F32), jax.ShapeDtypeStruct((n, 128), F32)] + kv_shapes,
        compiler_params=_params(("arbitrary",)),
        name="attn_in",
    )(x, g, wq, wkv, wg)


def _compress_hidden(load_rows, n_chunks, wexp_ref):
    lo = jnp.concatenate([load_rows(2 * r) for r in range(CMP_STRIDE)], axis=1)
    hi = jnp.concatenate([load_rows(2 * r + 1) for r in range(CMP_STRIDE)], axis=1)
    acc = _dot(jnp.concatenate([lo, hi], axis=0).astype(BF16), wexp_ref[...])
    lo, hi = acc[:n_chunks], acc[n_chunks:]
    w = 2 * CMP_HID
    return jnp.concatenate([lo[:, :w], hi[:, :w], lo[:, w:], hi[:, w:]], axis=1)


def _pe_term(pe_ref, w1_ref):
    t = _dot(pe_ref[...], w1_ref[...])[0:1, :]
    return jnp.concatenate([t] * N_KV, axis=1)


def _compress_p_body(nc, kcl_ref, kch_ref, vcl_ref, vch_ref, wek_ref, pek_ref, w1k_ref, w2k_ref,
                     wev_ref, pev_ref, w1v_ref, w2v_ref, ok_ref, ov_ref):
    half = N_KV * CMP_HID
    row = lax.broadcasted_iota(jnp.int32, (nc, KV_W), 0)
    for src, we, pe, w1, w2, out in (((kcl_ref, kch_ref), wek_ref, pek_ref, w1k_ref, w2k_ref, ok_ref),
                                     ((vcl_ref, vch_ref), wev_ref, pev_ref, w1v_ref, w2v_ref, ov_ref)):
        hh = _compress_hidden(lambda k: src[k % 2][0, pl.ds(k // 2, nc, stride=CMP_STRIDE), :], nc, we)
        nxt = pltpu.roll(hh[:, half:], nc - 1, 0)
        a = hh[:, :half] + nxt + _pe_term(pe, w1)
        a = a * _sigmoid(a)
        res = _dot(a.astype(BF16), w2[...])
        out[0] = jnp.where(row < nc - 1, res, 0.0)


def _compress_prompt(kc_halves, vc_halves, wk, wv):
    b, t, _ = kc_halves[0].shape
    nc = t // CMP_STRIDE
    seq = pl.BlockSpec((1, t, 128), lambda i: (i, 0, 0))
    wspecs = [_const_spec((CMP_STRIDE * 128, 4 * CMP_HID)),
              _const_spec((8, 2 * CMP_STRIDE * HEAD_DIM)),
              _const_spec((2 * CMP_STRIDE * HEAD_DIM, CMP_HID)), _const_spec((N_KV * CMP_HID, KV_W))]
    out = pl.BlockSpec((1, nc, KV_W), lambda i: (i, 0, 0))
    return pl.pallas_call(
        functools.partial(_compress_p_body, nc),
        grid=(b,),
        in_specs=[seq] * 4 + wspecs + wspecs,
        out_specs=[out, out],
        out_shape=[jax.ShapeDtypeStruct((b, nc, KV_W), F32)] * 2,
        compiler_params=_params(("parallel",)),
        name="compress_prompt",
    )(*kc_halves, *vc_halves, *wk, *wv)


def _softmax_cols(s, mask):
    s = jnp.where(mask, s, NEG_INF)
    m = jnp.max(s, axis=0, keepdims=True)
    m = jnp.where(m == NEG_INF, 0.0, m)
    e = jnp.where(mask, jnp.exp(s - m), 0.0)
    return e * (1.0 / jnp.maximum(jnp.sum(e, axis=0, keepdims=True), 1e-30))


def _attn_p_body(tq, t, q_ref, gt_ref, ks_ref, vs_ref, kw_ref, vw_ref, kcm_ref, vcm_ref,
                 tbl_ref, bt_ref, bc_ref, msel_ref, o_ref,
                 ksb, kwb, vst, vwt, kcb, vct, btoep, bcmp, qt, qt2, selt, ot, m_ref, l_ref, acc_ref, gt_s):
    b = pl.program_id(0)
    qi = pl.program_id(1)
    tk = ATTN_TK
    nkt = t // tk
    nqt = t // tq
    r_qk = tq // tk
    ncmp = kcm_ref.shape[1]
    q0 = qi * tq
    n_near = bt_ref.shape[0]

    @pl.when((b == 0) & (qi == 0))
    def _():
        def per_head(h, carry):
            far = tbl_ref[NUM_BUCKETS - 1, h]
            for d in range(n_near):
                bk = bt_ref[d]
                acc = jnp.zeros((tk, tq), F32)
                for bb in range(NUM_BUCKETS):
                    acc = jnp.where(bk == bb, (tbl_ref[bb, h] - far) * LOG2E, acc)
                btoep[h, d] = acc
            btoep[h, n_near] = jnp.zeros((tk, tq), F32)
            bk = bc_ref[...]
            acc = jnp.zeros(bk.shape, F32)
            for bb in range(NUM_BUCKETS):
                acc = jnp.where(bk == bb, tbl_ref[bb, h], acc)
            bcmp[h] = acc
            return carry
        lax.fori_loop(0, N_HEADS, per_head, 0)

    @pl.when(qi == 0)
    def _():
        for kt in range(nkt):
            rows = slice(kt * tk, (kt + 1) * tk)
            for src, dstk in ((ks_ref, ksb), (kw_ref, kwb)):
                blk = src[0, rows, :]
                for g in range(N_KV):
                    dstk[g, rows, :] = blk[:, HEAD_DIM * g:HEAD_DIM * (g + 1)].astype(BF16)
            for src, dstv in ((vs_ref, vst), (vw_ref, vwt)):
                for g in range(N_KV):
                    dstv[g, kt] = src[0, g, :, rows].astype(BF16)
        kc = kcm_ref[0]
        vc_t = vcm_ref[0].T
        for g in range(N_KV):
            kcb[g] = kc[:, HEAD_DIM * g:HEAD_DIM * (g + 1)].astype(BF16)
            vct[g] = vc_t[HEAD_DIM * g:HEAD_DIM * (g + 1), :].astype(BF16)

    q_t = q_ref[0].T
    qt[...] = (q_t * SCALE).astype(BF16)
    qt2[...] = (q_t * (SCALE * LOG2E)).astype(BF16)
    gt_s[...] = gt_ref[0].T

    qpos_row = q0 + lax.broadcasted_iota(jnp.int32, (1, tq), 1)
    i_iota = lax.broadcasted_iota(jnp.int32, (tk, tq), 1)
    j_iota = lax.broadcasted_iota(jnp.int32, (tk, tq), 0)
    rel = i_iota - j_iota

    c_iota = lax.broadcasted_iota(jnp.int32, (ncmp, tq), 0)
    cmask = (CMP_STRIDE * c_iota + (2 * CMP_STRIDE - 1)) <= qpos_row
    coff = pl.multiple_of((nqt - 1 - qi) * (tq // CMP_STRIDE), 8)

    nblk = t // SLC_BLOCK
    blk_iota = lax.broadcasted_iota(jnp.int32, (nblk, tq), 0)
    cur = qpos_row // SLC_BLOCK
    valid = blk_iota * SLC_BLOCK <= qpos_row
    forced = (blk_iota == 0) | (blk_iota == cur) | (blk_iota == cur - 1)

    heads = range(N_HEADS)
    head_rows = [slice(HEAD_DIM * h, HEAD_DIM * (h + 1)) for h in heads]
    s_c = [_dot(kcb[h // HPG], qt[head_rows[h], :]) + bcmp[h, pl.ds(coff, ncmp), :] for h in heads]
    p_c = [_softmax_cols(s, cmask) for s in s_c]
    o_c = [_dot(vct[h // HPG], p_c[h].astype(BF16)) for h in heads]
    for h in heads:
        ot[head_rows[h], :] = gt_s[h:h + 1, :] * o_c[h]
    msel = msel_ref[...]
    scores = []
    for g in range(N_KV):
        pg = p_c[g * HPG]
        for hh in range(1, HPG):
            pg = pg + p_c[g * HPG + hh]
        hi, mid, lo = _split3(pg)
        score = _dot(msel, hi) + _dot(msel, mid) + _dot(msel, lo)
        scores.append(jnp.where(forced, jnp.inf, jnp.where(valid, score, NEG_INF)))
    n_slab = nblk // 8
    slabs = [[sc[8 * v:8 * (v + 1), :] for v in range(n_slab)] for sc in scores]
    ranks = [[jnp.zeros((8, tq), F32) for _ in range(n_slab)] for _ in scores]
    row8 = lax.broadcasted_iota(jnp.int32, (8, tq), 0)
    for i in range(nblk):
        for g in range(N_KV):
            si = scores[g][i:i + 1, :]
            for v, sl in enumerate(slabs[g]):
                if 8 * v > i:
                    beats = si >= sl
                elif 8 * v + 7 <= i:
                    beats = si > sl
                else:
                    beats = (si > sl) | ((row8 > i - 8 * v) & (si == sl))
                ranks[g][v] = ranks[g][v] + jnp.where(beats, 1.0, 0.0)
    for g in range(N_KV):
        for v, rk in enumerate(ranks[g]):
            sel = (rk < N_SELECT).astype(F32)
            for r in range(8):
                selt[g, 8 * v + r] = sel[r:r + 1, :]

    def flash_init(br):
        m_ref[br] = jnp.full(m_ref.shape[1:], NEG_INF, F32)
        l_ref[br] = jnp.zeros(l_ref.shape[1:], F32)
        acc_ref[br] = jnp.zeros(acc_ref.shape[1:], F32)

    def tile_step(br, kb, vt, kt, bias_d, add_fn):
        k0 = pl.multiple_of(kt * tk, tk)
        for g in range(N_KV):
            kblk = kb[g, pl.ds(k0, tk), :]
            vblk = vt[g, kt]
            add = add_fn(g)
            for hh in range(HPG):
                h = g * HPG + hh
                rows = slice(HEAD_DIM * h, HEAD_DIM * (h + 1))
                s = _dot(kblk, qt2[rows, :])
                if bias_d is not None:
                    s = s + btoep[h, bias_d]
                if add is not None:
                    s = s + add
                m_old = m_ref[br, h]
                m_new = jnp.maximum(m_old, jnp.max(s, axis=0, keepdims=True))
                m_safe = jnp.where(m_new == NEG_INF, 0.0, m_new)
                alpha = jnp.exp2(m_old - m_safe)
                p = jnp.exp2(s - m_safe)
                l_ref[br, h] = alpha * l_ref[br, h] + jnp.sum(p, axis=0, keepdims=True)
                acc_ref[br, rows, :] = alpha * acc_ref[br, rows, :] + _dot(vblk, p.astype(BF16))
                m_ref[br, h] = m_new

    def flash_finish(br, gate_row0):
        for h in range(N_HEADS):
            rows = slice(HEAD_DIM * h, HEAD_DIM * (h + 1))
            w = gt_s[gate_row0 + h:gate_row0 + h + 1, :] * (1.0 / jnp.maximum(l_ref[br, h], 1e-30))
            ot[rows, :] = ot[rows, :] + acc_ref[br, rows, :] * w

    def sel_rows(g, kt, ok=None):
        nb = tk // SLC_BLOCK
        rows = [jnp.broadcast_to(selt[g, kt * nb + r], (SLC_BLOCK, tq)) for r in range(nb)]
        thr = 0.5 if ok is None else jnp.where(ok, 0.5, 2.0)
        return jnp.concatenate(rows, axis=0) > thr

    def neg_unless(cond):
        return jnp.where(cond, 0.0, NEG_INF)

    causal = rel >= 0
    SLC, WIN = 0, 1

    flash_init(SLC)
    flash_init(WIN)
    n_far = jnp.maximum(qi - 1, 0)

    def slc_pair(pi, carry):
        for sub in range(2):
            kt = 2 * pi + sub
            ok = kt < n_far
            ktc = jnp.minimum(kt, jnp.maximum(n_far - 1, 0))
            tile_step(SLC, ksb, vst, ktc, None, lambda g: neg_unless(sel_rows(g, ktc, ok)))
        return carry

    lax.fori_loop(0, (n_far + 1) // 2, slc_pair, 0)
    kt1 = jnp.maximum(qi - 1, 0)
    tile_step(SLC, ksb, vst, kt1, 1, lambda g: neg_unless(sel_rows(g, kt1, qi >= 1)))
    tile_step(SLC, ksb, vst, qi, 0, lambda g: neg_unless(sel_rows(g, qi) & causal))

    wt = WINDOW // tk
    for u in range(wt + 1):
        off = wt - u
        kt = qi - off
        ok_add = jnp.where(kt >= 0, 0.0, NEG_INF)
        ktc = jnp.maximum(kt, 0)
        if off == wt:
            add_fn = lambda g, a=ok_add: neg_unless(rel <= 0) + a
        elif off == 0:
            add_fn = lambda g: neg_unless(causal)
        else:
            add_fn = lambda g, a=ok_add: a
        tile_step(WIN, kwb, vwt, ktc, off if off <= 1 else None, add_fn)
    flash_finish(SLC, N_HEADS)
    flash_finish(WIN, 2 * N_HEADS)

    o_ref[0] = ot[...].T


def _attn_prompt(q, gates, ks, vs, kw, vw, kcmp, vcmp, table, bkt_toep, bkt_cmp, msel, tq):
    b, t, _ = q.shape
    tk = ATTN_TK
    nkt = t // tk
    ncmp = kcmp.shape[1]
    n_near = bkt_toep.shape[0]
    qspec = pl.BlockSpec((1, tq, D_MODEL), lambda bi, i: (bi, i, 0))
    seq = pl.BlockSpec((1, t, KV_W), lambda bi, i: (bi, 0, 0))
    seq_fm = pl.BlockSpec((1, N_KV, HEAD_DIM, t), lambda bi, i: (bi, 0, 0, 0))
    cmp_spec = pl.BlockSpec((1, ncmp, KV_W), lambda bi, i: (bi, 0, 0))
    return pl.pallas_call(
        functools.partial(_attn_p_body, tq, t),
        grid=(b, t // tq),
        in_specs=[qspec, pl.BlockSpec((1, tq, 128), lambda bi, i: (bi, i, 0)),
                  seq, seq_fm, seq, seq_fm, cmp_spec, cmp_spec,
                  pl.BlockSpec(memory_space=pltpu.SMEM),
                  _const_spec(bkt_toep.shape), _const_spec(bkt_cmp.shape), _const_spec(msel.shape)],
        out_specs=qspec,
        out_shape=jax.ShapeDtypeStruct((b, t, D_MODEL), F32),
        scratch_shapes=[pltpu.VMEM((N_KV, t, HEAD_DIM), BF16), pltpu.VMEM((N_KV, t, HEAD_DIM), BF16),
                        pltpu.VMEM((N_KV, nkt, HEAD_DIM, tk), BF16), pltpu.VMEM((N_KV, nkt, HEAD_DIM, tk), BF16),
                        pltpu.VMEM((N_KV, ncmp, HEAD_DIM), BF16), pltpu.VMEM((N_KV, HEAD_DIM, ncmp), BF16),
                        pltpu.VMEM((N_HEADS, n_near + 1, tk, tq), F32),
                        pltpu.VMEM((N_HEADS,) + bkt_cmp.shape, F32),
                        pltpu.VMEM((D_MODEL, tq), BF16), pltpu.VMEM((D_MODEL, tq), BF16),
                        pltpu.VMEM((N_KV, t // SLC_BLOCK, 1, tq), F32),
                        pltpu.VMEM((D_MODEL, tq), F32),
                        pltpu.VMEM((2, N_HEADS, 1, tq), F32), pltpu.VMEM((2, N_HEADS, 1, tq), F32),
                        pltpu.VMEM((2, D_MODEL, tq), F32), pltpu.VMEM((128, tq), F32)],
        compiler_params=_params(("arbitrary", "arbitrary")),
        name="attn_prompt",
    )(q, gates, ks, vs, kw, vw, kcmp, vcmp, table, bkt_toep, bkt_cmp, msel)


PAGE = 128
PAGES_PER_GROUP = 32
Q_ROWS = 8


def _rank_desc(score_row, n):
    a = jnp.broadcast_to(score_row, (n, n))
    at = a.T
    i = lax.broadcasted_iota(jnp.int32, (n, n), 0)
    j = lax.broadcasted_iota(jnp.int32, (n, n), 1)
    beats = (at > a) | ((i < j) & (at == a))
    return jnp.sum(beats.astype(F32), axis=0, keepdims=True)


def _bias_rows(bucket, tcols):
    r = tcols.shape[0]
    out = jnp.zeros((r, bucket.shape[1]), F32)
    for bb in range(NUM_BUCKETS):
        out = jnp.where(bucket == bb, tcols[:, bb:bb + 1], out)
    return out


def _softmax_rows(s, mask):
    s = jnp.where(mask, s, NEG_INF)
    m = jnp.max(s, axis=1, keepdims=True)
    m = jnp.where(m == NEG_INF, 0.0, m)
    e = jnp.where(mask, jnp.exp(s - m), 0.0)
    return e / jnp.maximum(jnp.sum(e, axis=1, keepdims=True), 1e-30)


def _dot_nt(a, b):
    return lax.dot_general(a, b, (((1,), (1,)), ((), ())), preferred_element_type=F32)


def _cmp_s_body(past, pt_ref, ck_hbm, cv_hbm, kcn_ref, vcn_ref, q_ref,
                wek_ref, pek_ref, w1k_ref, w2k_ref, wev_ref, pev_ref, w1v_ref, w2v_ref,
                tbl_ref, bkt_ref, msel_ref, ocmp_ref, idx_ref,
                kbuf, vbuf, tbuf, hk, hv, sem):
    b = pl.program_id(0)
    rows_pg = PAGES_PER_GROUP * PAGE
    cpg = rows_pg // CMP_STRIDE
    ngrp = past // rows_pg
    nc = past // CMP_STRIDE
    half = N_KV * CMP_HID

    def copies(bb, gi, slot):
        out = []
        for p in range(PAGES_PER_GROUP):
            page = pt_ref[bb, gi * PAGES_PER_GROUP + p]
            out.append(pltpu.make_async_copy(ck_hbm.at[page], kbuf.at[slot, p], sem.at[0, slot]))
            out.append(pltpu.make_async_copy(cv_hbm.at[page], vbuf.at[slot, p], sem.at[1, slot]))
        return out

    assert ngrp % 2 == 0

    @pl.when(b == 0)
    def _():
        for c in copies(b, 0, 0):
            c.start()

    def group(gi, carry):
        slot = gi % 2

        @pl.when(gi + 1 < ngrp)
        def _():
            for c in copies(b, gi + 1, 1 - slot):
                c.start()

        for c in copies(b, gi, slot):
            c.wait()
        r0 = pl.multiple_of(gi * cpg, cpg)
        for ti, (buf, we, hdst) in enumerate(((kbuf, wek_ref, hk), (vbuf, wev_ref, hv))):
            for p in range(PAGES_PER_GROUP):
                for hf in range(2):
                    tbuf[ti, hf, p * PAGE:(p + 1) * PAGE, :] = buf[slot, p, hf * 128:(hf + 1) * 128, :].T
            hdst[pl.ds(r0, cpg), :] = _compress_hidden(
                lambda k, ti=ti: tbuf[ti, k % 2, pl.ds(k // 2, cpg, stride=CMP_STRIDE), :], cpg, we)
        return carry

    lax.fori_loop(0, ngrp, group, 0)

    @pl.when(b + 1 < pl.num_programs(0))
    def _():
        for c in copies(b + 1, 0, 0):
            c.start()

    col = lax.broadcasted_iota(jnp.int32, (1, nc), 1)
    cmask = (CMP_STRIDE * col + (2 * CMP_STRIDE - 1)) <= past
    row8 = lax.broadcasted_iota(jnp.int32, (Q_ROWS, nc), 0)
    cmpd = []
    for new_ref, we, pe, w1, w2, hdst in ((kcn_ref, wek_ref, pek_ref, w1k_ref, w2k_ref, hk),
                                          (vcn_ref, wev_ref, pev_ref, w1v_ref, w2v_ref, hv)):
        new = jnp.broadcast_to(new_ref[0], (8, KV_W)).astype(BF16)
        new_lo = _dot(new[:, :128], we[0:128, :])
        new_hi = _dot(new[:, 128:], we[0:128, :])
        w = 2 * CMP_HID
        hdst[nc:nc + 8, :] = jnp.concatenate([new_lo[:, :w], new_hi[:, :w], new_lo[:, w:], new_hi[:, w:]], axis=1)
        a = hdst[0:nc, 0:half] + hdst[1:nc + 1, half:2 * half] + _pe_term(pe, w1)
        a = (a * _sigmoid(a)).astype(BF16)
        cmpd.append([_dot(a[:, CMP_HID * g:CMP_HID * (g + 1)], w2[...]).astype(BF16) for g in range(N_KV)])

    nsel = msel_ref.shape[1]
    lane = lax.broadcasted_iota(jnp.int32, (1, nsel), 1)
    n_blocks = past // SLC_BLOCK + 1
    cur = past // SLC_BLOCK
    forced = (lane == 0) | (lane == cur) | (lane == cur - 1)
    valid = (lane * SLC_BLOCK <= past) & (lane < n_blocks)
    k_iota = lax.broadcasted_iota(jnp.int32, (N_SELECT, nsel), 0).astype(F32)
    lane_f = lax.broadcasted_iota(jnp.int32, (N_SELECT, nsel), 1).astype(F32)
    groups = range(N_KV)
    msel = msel_ref[...]
    bkt = bkt_ref[...]
    s_g = [_dot_nt(q_ref[0, g].astype(BF16), cmpd[0][g]) * SCALE + _bias_rows(bkt, tbl_ref[g]) for g in groups]
    p_g = [_softmax_rows(s, cmask) for s in s_g]
    for g in groups:
        ocmp_ref[0, g] = _dot(p_g[g].astype(BF16), cmpd[1][g])
    pgs = [jnp.sum(jnp.where(row8 < HPG, p, 0.0), axis=0, keepdims=True) for p in p_g]
    parts = [_split3(jnp.broadcast_to(pg, (8, nc))) for pg in pgs]
    scores = [(_dot(hi, msel) + _dot(mid, msel) + _dot(lo, msel))[0:1, :] for hi, mid, lo in parts]
    scores = [jnp.where(forced, jnp.inf, jnp.where(valid, sc, NEG_INF)) for sc in scores]
    rank_g = [_rank_desc(sc, nsel) for sc in scores]
    hits = [jnp.where(jnp.broadcast_to(rk, (N_SELECT, nsel)) == k_iota, lane_f, 0.0) for rk in rank_g]
    for g in groups:
        idx_ref[0, g] = jnp.sum(hits[g], axis=1, keepdims=True).astype(jnp.int32)


def _cmp_sample(page_table, cache_k, cache_v, kc_new, vc_new, q8, wk, wv, tbl8, bkt, msel, past):
    nb = page_table.shape[0]
    rows_pg = PAGES_PER_GROUP * PAGE
    nc = past // CMP_STRIDE
    any_spec = pl.BlockSpec(memory_space=pl.ANY)
    new_spec = pl.BlockSpec((1, 1, KV_W), lambda i, pt: (i, 0, 0))
    cs = lambda shape: pl.BlockSpec(shape, lambda i, pt: (0,) * len(shape))
    wspecs = [cs((CMP_STRIDE * 128, 4 * CMP_HID)),
              cs((8, 2 * CMP_STRIDE * HEAD_DIM)),
              cs((2 * CMP_STRIDE * HEAD_DIM, CMP_HID)), cs((CMP_HID, HEAD_DIM))]
    grid_spec = pltpu.PrefetchScalarGridSpec(
        num_scalar_prefetch=1,
        grid=(nb,),
        in_specs=[any_spec, any_spec, new_spec, new_spec,
                  pl.BlockSpec((1, N_KV, Q_ROWS, HEAD_DIM), lambda i, pt: (i, 0, 0, 0))]
                 + wspecs + wspecs + [cs(tbl8.shape), cs(bkt.shape), cs(msel.shape)],
        out_specs=[pl.BlockSpec((1, N_KV, Q_ROWS, HEAD_DIM), lambda i, pt: (i, 0, 0, 0)),
                   pl.BlockSpec((1, N_KV, N_SELECT, 1), lambda i, pt: (i, 0, 0, 0))],
        scratch_shapes=[pltpu.VMEM((2, PAGES_PER_GROUP, 2 * PAGE, 128), F32),
                        pltpu.VMEM((2, PAGES_PER_GROUP, 2 * PAGE, 128), F32),
                        pltpu.VMEM((2, 2, rows_pg, 128), F32),
                        pltpu.VMEM((nc + 8, 2 * N_KV * CMP_HID), F32),
                        pltpu.VMEM((nc + 8, 2 * N_KV * CMP_HID), F32),
                        pltpu.SemaphoreType.DMA((2, 2))])
    return pl.pallas_call(
        functools.partial(_cmp_s_body, past),
        grid_spec=grid_spec,
        out_shape=[jax.ShapeDtypeStruct((nb, N_KV, Q_ROWS, HEAD_DIM), F32),
                   jax.ShapeDtypeStruct((nb, N_KV, N_SELECT, 1), jnp.int32)],
        compiler_params=_params(("arbitrary",)),
        name="cmp_sample",
    )(page_table, cache_k, cache_v, kc_new, vc_new, q8, *wk, *wv, tbl8, bkt, msel)


NEW_LANES = 128


def _dec_s_body(past, idx_ref, pt_ref, csk_hbm, csv_hbm,
                q_ref, ksn_ref, vsn_ref, kwn_ref, vwn_ref, wk_ref, wv_ref, gt_ref, ocmp_ref,
                tbl_ref, bktw_ref, o_ref, wko_ref, wvo_ref,
                kbuf, vbuf, kwbuf, vwbuf, sem):
    b = pl.program_id(0)
    n_pages = past // PAGE
    cur = past // SLC_BLOCK
    bpp = PAGE // SLC_BLOCK
    lsel = N_SELECT * PAGE
    wb = wk_ref.shape[3]

    @pl.when(b == 0)
    def _():
        kbuf[...] = jnp.zeros_like(kbuf)
        vbuf[...] = jnp.zeros_like(vbuf)
        kwbuf[...] = jnp.zeros_like(kwbuf)
        vwbuf[...] = jnp.zeros_like(vwbuf)

    def gathers(bb):
        half = bb % 2
        out = []
        for g in range(N_KV):
            for k in range(N_SELECT):
                blk = idx_ref[bb, g * N_SELECT + k]
                page = pt_ref[bb, jnp.minimum(blk // bpp, n_pages - 1)]
                dst = pl.ds(k * PAGE, PAGE)
                out.append(pltpu.make_async_copy(csk_hbm.at[page, g], kbuf.at[half, g, :, dst], sem.at[0, half]))
                out.append(pltpu.make_async_copy(csv_hbm.at[page, g], vbuf.at[half, g, :, dst], sem.at[1, half]))
        return out

    @pl.when(b == 0)
    def _():
        for c in gathers(b):
            c.start()

    @pl.when(b + 1 < pl.num_programs(0))
    def _():
        for c in gathers(b + 1):
            c.start()

    cur_half = b % 2

    lane_wb = lax.broadcasted_iota(jnp.int32, (HEAD_DIM, wb), 1)
    for g in range(N_KV):
        for src, new_ref, dst, buf in ((wk_ref, kwn_ref, wko_ref, kwbuf), (wv_ref, vwn_ref, wvo_ref, vwbuf)):
            st = src[0, g]
            newc = new_ref[0, g]
            dst[0, g] = jnp.where(lane_wb == wb - 1, newc, pltpu.roll(st, wb - 1, 1))
            buf[g, :, 0:wb] = st
            buf[g, :, wb:wb + 1] = newc

    for c in gathers(b):
        c.wait()

    ls = lsel + NEW_LANES
    lw = wb + NEW_LANES
    lane_s = lax.broadcasted_iota(jnp.int32, (1, ls), 1)
    lane_w = lax.broadcasted_iota(jnp.int32, (1, lw), 1)
    wmask = lane_w <= wb
    tok = lane_s % PAGE
    groups = range(N_KV)
    smasks, sbiases = [], []
    for g in groups:
        kbuf[cur_half, g, :, lsel:lsel + 1] = ksn_ref[0, g]
        vbuf[cur_half, g, :, lsel:lsel + 1] = vsn_ref[0, g]
        kpos = past + (lane_s - lsel)
        gathered = lane_s < 0
        has_new = jnp.int32(0)
        for k in range(N_SELECT):
            blk = idx_ref[b, g * N_SELECT + k]
            in_slot = (lane_s // PAGE) == k
            kpos = jnp.where(in_slot, (blk // bpp) * PAGE + tok, kpos)
            gathered = gathered | (in_slot & (blk < cur) & ((tok // SLC_BLOCK) == (blk % bpp)))
            has_new = has_new | (blk == cur).astype(jnp.int32)
        smasks.append((gathered | ((lane_s == lsel) & (has_new > 0))) & (kpos <= past))
        sbiases.append(_bias_rows(_rel_bucket(past - kpos), tbl_ref[g]))
    bktw = bktw_ref[...]
    qgs = [q_ref[0, g].astype(BF16) for g in groups]
    s_slc = [_dot(qgs[g], kbuf[cur_half, g].astype(BF16)) * SCALE + sbiases[g] for g in groups]
    s_win = [_dot(qgs[g], kwbuf[g].astype(BF16)) * SCALE + _bias_rows(bktw, tbl_ref[g]) for g in groups]
    p_slc = [_softmax_rows(s_slc[g], smasks[g]) for g in groups]
    p_win = [_softmax_rows(s, wmask) for s in s_win]
    o_slc = [_dot_nt(p_slc[g].astype(BF16), vbuf[cur_half, g].astype(BF16)) for g in groups]
    o_win = [_dot_nt(p_win[g].astype(BF16), vwbuf[g].astype(BF16)) for g in groups]
    for g in groups:
        gt = gt_ref[0, g]
        o_ref[0, g] = gt[:, 0:1] * ocmp_ref[0, g] + gt[:, 1:2] * o_slc[g] + gt[:, 2:3] * o_win[g]


def _dec_sample(idx, page_table, cache_sk, cache_sv, q8, ks_new, vs_new, kw_new, vw_new, state_wk, state_wv,
                gates8, ocmp, tbl8, bktw, past):
    nb, wb = state_wk.shape[0], state_wk.shape[3]
    any_spec = pl.BlockSpec(memory_space=pl.ANY)
    new_spec = pl.BlockSpec((1, N_KV, HEAD_DIM, 1), lambda i, *_: (i, 0, 0, 0))
    st_spec = pl.BlockSpec((1, N_KV, HEAD_DIM, wb), lambda i, *_: (i, 0, 0, 0))
    head_spec = pl.BlockSpec((1, N_KV, Q_ROWS, HEAD_DIM), lambda i, *_: (i, 0, 0, 0))
    cs = lambda shape: pl.BlockSpec(shape, lambda i, *_: (0,) * len(shape))
    lsel = N_SELECT * PAGE
    grid_spec = pltpu.PrefetchScalarGridSpec(
        num_scalar_prefetch=2,
        grid=(nb,),
        in_specs=[any_spec, any_spec, head_spec,
                  new_spec, new_spec, new_spec, new_spec, st_spec, st_spec,
                  pl.BlockSpec((1, N_KV, Q_ROWS, 3), lambda i, *_: (i, 0, 0, 0)), head_spec,
                  cs(tbl8.shape), cs(bktw.shape)],
        out_specs=[head_spec, st_spec, st_spec],
        scratch_shapes=[pltpu.VMEM((2, N_KV, HEAD_DIM, lsel + NEW_LANES), F32),
                        pltpu.VMEM((2, N_KV, HEAD_DIM, lsel + NEW_LANES), F32),
                        pltpu.VMEM((N_KV, HEAD_DIM, wb + NEW_LANES), F32),
                        pltpu.VMEM((N_KV, HEAD_DIM, wb + NEW_LANES), F32),
                        pltpu.SemaphoreType.DMA((2, 2))])
    return pl.pallas_call(
        functools.partial(_dec_s_body, past),
        grid_spec=grid_spec,
        out_shape=[jax.ShapeDtypeStruct((nb, N_KV, Q_ROWS, HEAD_DIM), F32),
                   jax.ShapeDtypeStruct(state_wk.shape, F32),
                   jax.ShapeDtypeStruct(state_wv.shape, F32)],
        compiler_params=_params(("arbitrary",)),
        name="dec_sample",
    )(idx, page_table, cache_sk, cache_sv, q8, ks_new, vs_new, kw_new, vw_new,
      state_wk, state_wv, gates8, ocmp, tbl8, bktw)


def _expand_w1(w1):
    eye = jnp.eye(2, dtype=w1.dtype)
    parts = []
    for part in (w1[:CMP_STRIDE * HEAD_DIM], w1[CMP_STRIDE * HEAD_DIM:]):
        w = part.reshape(CMP_STRIDE, HEAD_DIM, CMP_HID)
        parts.append(jnp.einsum("gh,rdn->rgdhn", eye, w).reshape(CMP_STRIDE, 2 * HEAD_DIM, 2 * CMP_HID))
    w = jnp.concatenate(parts, axis=-1).astype(BF16)
    return w.reshape(CMP_STRIDE * 2 * HEAD_DIM, 4 * CMP_HID)


def _expand_w2(w2):
    eye = jnp.eye(N_KV, dtype=w2.dtype)
    return jnp.einsum("gh,nd->gnhd", eye, w2).reshape(N_KV * CMP_HID, KV_W).astype(BF16)


def _pe_rows(pe):
    return jnp.broadcast_to(pe.reshape(1, -1), (8, pe.size)).astype(BF16)


def _select_matrix(n_cmp, n_blocks):
    r = SLC_BLOCK // CMP_STRIDE
    m = np.zeros((n_cmp, n_blocks), np.float32)
    for s in range(n_blocks):
        for a in range(r):
            for bb in range(2):
                c = r * s + a - bb
                if 0 <= c < n_cmp:
                    m[c, s] += 1.0
    return m


def _pad_to(a, size, axis):
    pad = [(0, 0)] * a.ndim
    pad[axis] = (0, size - a.shape[axis])
    return jnp.pad(a, pad)


def kernel(x_prompt, x_sample, cache_cmp_k, cache_cmp_v, cache_slc_k, cache_slc_v, state_win_k, state_win_v, state_conv, page_table, p_prompt, p_sample, rel_table, norm_mix, norm_ffn, norm_ple, norm_final, conv_w1, conv_b1, conv_dw, conv_dwb, conv_ln_g, conv_ln_b, conv_w2, conv_b2, attn_w_in, attn_w_out, cmpk_w1, cmpk_pe, cmpk_w2, cmpv_w1, cmpv_pe, cmpv_w2, mlp_up, mlp_down, ple_proj, ple_gate):
    nbp, t, _ = x_prompt.shape
    nbs = x_sample.shape[0]
    n_p = nbp * t
    n_pool = cache_cmp_k.shape[1]
    past = page_table.shape[1] * cache_cmp_k.shape[2]
    wb = state_win_k.shape[2]
    hist = CONV_W - 1
    tm_p, tf, tt, tq = 512, 1024, 256, ATTN_TK

    row = lambda a: a.reshape(1, -1)
    bf = lambda a: a.astype(BF16)

    w1c, w2c = bf(conv_w1[0]), bf(conv_w2[0])
    dw3 = conv_dw[0].reshape(CONV_W, D_MODEL // 128, 128).transpose(1, 0, 2)
    dwb3 = conv_dwb[0].reshape(D_MODEL // 128, 1, 128)
    up, dn = bf(mlp_up), bf(mlp_down)
    wg, wp = bf(ple_gate), bf(ple_proj)
    w_in = attn_w_in[0]
    wq = bf(w_in[:, :D_MODEL])
    wkv = bf(w_in[:, D_MODEL:D_MODEL + 6 * KV_W])
    wgt = bf(_pad_to(w_in[:, D_MODEL + 6 * KV_W:], 128, 1))
    w_out = bf(attn_w_out[0])
    cw = []
    for w1, pe, w2 in ((cmpk_w1[0], cmpk_pe[0], cmpk_w2[0]), (cmpv_w1[0], cmpv_pe[0], cmpv_w2[0])):
        cw.append((_expand_w1(w1), _pe_rows(pe), bf(w1), _expand_w2(w2), bf(w2)))
    wk_p, wv_p = [c[:4] for c in cw]
    wk_s, wv_s = [c[:3] + c[4:] for c in cw]

    def tail(x, i, p, tm, final, pre=None):
        return _mlp_ple(x, row(norm_ffn[i]), up, dn, row(norm_ple[i]), wg, p, i, wp,
                        row(norm_final), tm, tf, final, pre)

    conv_args = (row(conv_ln_g[0]), row(conv_ln_b[0]), w2c, row(conv_b2[0]))

    xp = x_prompt.reshape(n_p, D_MODEL)
    glu = _conv_in(xp, row(norm_mix[0]), w1c, row(conv_b1[0]), tm_p).reshape(nbp, t, D_MODEL)
    x1 = _conv_out(glu, x_prompt, dw3, dwb3, *conv_args, tt).reshape(n_p, D_MODEL)
    conv_p = glu[:, t - hist:][None]
    pp = p_prompt.reshape(-1, n_p, D_PLE)
    ps = p_sample.reshape(-1, nbs, D_PLE)
    x2 = tail(x1, 0, pp, tm_p, False)
    q, gates, kcl, kch, vcl, vch, ks, kw, *rest = _attn_in(x2, row(norm_mix[1]), wq, wkv, wgt, tm_p, seq_len=t)
    kv_fm, win_fm = rest[:len(KV_NAMES)], rest[len(KV_NAMES):]
    vs_fm, vw_fm = kv_fm[KV_NAMES.index("vs")], kv_fm[KV_NAMES.index("vw")]
    seq = lambda a: a.reshape(nbp, t, -1)
    kcmp, vcmp = _compress_prompt((seq(kcl), seq(kch)), (seq(vcl), seq(vch)), wk_p, wv_p)

    ncmp = t // CMP_STRIDE
    ii = jnp.arange(tq, dtype=jnp.int32)[None, :]
    jj = jnp.arange(ATTN_TK, dtype=jnp.int32)[:, None]
    r_qk = tq // ATTN_TK
    bkt_toep = jnp.stack([_rel_bucket(d * ATTN_TK + ii - jj) for d in range(1 - r_qk, 2)])
    nshift = (t // tq - 1) * (tq // CMP_STRIDE)
    rr = jnp.arange(ncmp + nshift, dtype=jnp.int32)[:, None]
    bkt_cmp = _rel_bucket(ii - CMP_STRIDE * (rr - nshift) - (2 * CMP_STRIDE - 1))
    msel_p = jnp.asarray(_select_matrix(ncmp, t // SLC_BLOCK).T, BF16)
    o_p = _attn_prompt(seq(q), seq(gates), seq(ks), vs_fm, seq(kw), vw_fm, kcmp, vcmp,
                       rel_table, bkt_toep, bkt_cmp, msel_p, tq)
    y_p = tail(x2, 1, pp, tm_p, True, pre=(o_p.reshape(n_p, D_MODEL), w_out))

    from_fm = lambda a: jnp.transpose(a, (0, 3, 1, 2))[None]
    kv_p = [from_fm(a) for a in kv_fm[:4]] + [from_fm(a) for a in win_fm]

    xs = x_sample.reshape(nbs, D_MODEL)
    glu_s = _conv_in(xs, row(norm_mix[0]), w1c, row(conv_b1[0]), nbs)
    x1s, conv_s = _conv_step(state_conv[0], glu_s, xs, conv_dw[0], row(conv_dwb[0]), *conv_args)
    x2s = tail(x1s, 0, ps, nbs, False)
    qs, gts, kcs, vcs, kss, vss, kws, vws = _attn_in(x2s, row(norm_mix[1]), wq, wkv, wgt, nbs)

    q8 = _pad_to(qs.reshape(nbs, N_KV, HPG, HEAD_DIM), Q_ROWS, 2)
    tbl8 = _pad_to(rel_table.T.reshape(N_KV, HPG, NUM_BUCKETS), Q_ROWS, 1)
    nc_s = past // CMP_STRIDE
    cc = jnp.arange(nc_s, dtype=jnp.int32)[None, :]
    bkt_s = _rel_bucket(past - (CMP_STRIDE * cc + 2 * CMP_STRIDE - 1))
    n_blocks_s = past // SLC_BLOCK + 1
    msel_s = jnp.asarray(_select_matrix(nc_s, -(-n_blocks_s // 128) * 128), BF16)
    new3 = lambda a: a.reshape(nbs, 1, KV_W)
    fmajor = lambda c: jnp.transpose(c[0], (0, 2, 3, 1))
    cache2 = lambda c: fmajor(c).reshape(n_pool, 2 * PAGE, PAGE)
    ocmp, idx = _cmp_sample(page_table, cache2(cache_cmp_k), cache2(cache_cmp_v), new3(kcs), new3(vcs),
                            q8, wk_s, wv_s, tbl8, bkt_s, msel_s, past)

    ww = jnp.arange(wb + NEW_LANES, dtype=jnp.int32)[None, :]
    bkt_w = _rel_bucket(wb - ww)
    gates8 = _pad_to(gts[:, :3 * N_HEADS].reshape(nbs, 3, N_KV, HPG).transpose(0, 2, 3, 1), Q_ROWS, 2)
    col4 = lambda a: a.reshape(nbs, N_KV, HEAD_DIM, 1)
    o8, wk_new, wv_new = _dec_sample(idx.reshape(nbs, N_KV * N_SELECT), page_table,
                                     fmajor(cache_slc_k), fmajor(cache_slc_v), q8,
                                     col4(kss), col4(vss), col4(kws), col4(vws),
                                     fmajor(state_win_k), fmajor(state_win_v), gates8, ocmp, tbl8, bkt_w, past)
    o_s = o8[:, :, :HPG].reshape(nbs, D_MODEL)
    y_s = tail(x2s, 1, ps, nbs, True, pre=(o_s, w_out))

    kv5s = lambda a: a.reshape(1, nbs, 1, N_KV, HEAD_DIM)
    return (y_p.reshape(nbp, t, D_MODEL), y_s.reshape(nbs, 1, D_MODEL), *kv_p, conv_p,
            kv5s(kcs), kv5s(vcs), kv5s(kss), kv5s(vss), from_fm(wk_new), from_fm(wv_new), conv_s[None])
```

```python
import functools
import math

import numpy as np
import jax
import jax.numpy as jnp
from jax import lax
from jax.experimental import pallas as pl
from jax.experimental.pallas import tpu as pltpu

F32 = jnp.float32
BF16 = jnp.bfloat16

D_MODEL = 1024
D_PLE = 256
CONV_W = 31
N_HEADS = 16
HEAD_DIM = 64
N_KV = 4
HPG = 4
KV_W = N_KV * HEAD_DIM
CMP_STRIDE = 16
CMP_HID = 128
SLC_BLOCK = 64
N_SELECT = 16
WINDOW = 512
NUM_BUCKETS = 32
MAX_DISTANCE = 128
D_FF = 4096
EPS = 1e-6
SCALE = HEAD_DIM ** -0.5
assert math.frexp(SCALE)[0] == 0.5
NEG_INF = float("-inf")

ATTN_TK = 128
assert ATTN_TK >= MAX_DISTANCE
LOG2E = math.log2(math.e)

VMEM_LIMIT = 56 * 1024 * 1024
HALO = 32


def _sigmoid(x):
    return 1.0 / (1.0 + jnp.exp(-x))


def _rms(x, g):
    return x * lax.rsqrt(jnp.mean(x * x, axis=-1, keepdims=True) + EPS) * g


def _dot(a, b):
    return jnp.dot(a, b, preferred_element_type=F32)


def _dot_lanes(a, b):
    n = b.shape[1]
    return jnp.concatenate([_dot(a, b[:, c:c + 128]) for c in range(0, n, 128)], axis=1)


def _split3(x):
    hi = x.astype(BF16)
    r1 = x - hi.astype(F32)
    mid = r1.astype(BF16)
    lo = (r1 - mid.astype(F32)).astype(BF16)
    return hi, mid, lo


def _rel_bucket(dist):
    n = jnp.maximum(dist, 0)
    max_exact = NUM_BUCKETS // 2
    nf = jnp.maximum(n, 1).astype(F32)
    large = max_exact + (jnp.log(nf / max_exact) / math.log(MAX_DISTANCE / max_exact)
                         * (NUM_BUCKETS - max_exact)).astype(jnp.int32)
    large = jnp.minimum(large, NUM_BUCKETS - 1)
    return jnp.where(n < max_exact, n, large)


def _params(sem):
    return pltpu.CompilerParams(dimension_semantics=sem, vmem_limit_bytes=VMEM_LIMIT)


def _const_spec(shape, single=False):
    n = len(shape)
    if single:
        return pl.BlockSpec(shape, lambda *_: (0,) * n, pipeline_mode=pl.Buffered(1))
    return pl.BlockSpec(shape, lambda *_: (0,) * n)


def _conv_in_body(x_ref, g_ref, w_ref, b_ref, o_ref):
    h = _rms(x_ref[...], g_ref[...]).astype(BF16)
    u = _dot(h, w_ref[...]) + b_ref[...]
    o_ref[...] = u[:, :D_MODEL] * _sigmoid(u[:, D_MODEL:])


def _conv_in(x, g, w1, b1, tm):
    n = x.shape[0]
    return pl.pallas_call(
        _conv_in_body,
        grid=(n // tm,),
        in_specs=[pl.BlockSpec((tm, D_MODEL), lambda i: (i, 0)),
                  _const_spec((1, D_MODEL)),
                  _const_spec((D_MODEL, 2 * D_MODEL)),
                  _const_spec((1, 2 * D_MODEL))],
        out_specs=pl.BlockSpec((tm, D_MODEL), lambda i: (i, 0)),
        out_shape=jax.ShapeDtypeStruct((n, D_MODEL), F32),
        compiler_params=_params(("parallel",)),
        name="conv_in",
    )(x, g, w1, b1)


def _ln_silu_proj(y, lng, lnb, w2, b2, x):
    mu = jnp.mean(y, axis=-1, keepdims=True)
    yc = y - mu
    var = jnp.mean(yc * yc, axis=-1, keepdims=True)
    z = yc * lax.rsqrt(var + EPS) * lng + lnb
    z = z * _sigmoid(z)
    return _dot(z.astype(BF16), w2) + b2 + x


def _conv_out_body(tt, cur_ref, halo_ref, x_ref, dw_ref, dwb_ref, lng_ref, lnb_ref, w2_ref, b2_ref,
                   o_ref, ctx_ref, y_ref):
    i = pl.program_id(1)
    n_strip = D_MODEL // 128
    keep = (i > 0).astype(F32)
    for c in range(n_strip):
        ctx_ref[c, 0:HALO, :] = halo_ref[0, :, 128 * c:128 * (c + 1)] * keep
        ctx_ref[c, HALO:, :] = cur_ref[0, :, 128 * c:128 * (c + 1)]

    def strip(c, carry):
        acc = jnp.broadcast_to(dwb_ref[c], (tt, 128))
        for k in range(CONV_W):
            off = k + HALO - (CONV_W - 1)
            acc = acc + ctx_ref[c, off:off + tt, :] * dw_ref[c, k:k + 1, :]
        y_ref[c] = acc
        return carry

    lax.fori_loop(0, n_strip, strip, 0)
    y = jnp.concatenate([y_ref[c] for c in range(n_strip)], axis=1)
    o_ref[0] = _ln_silu_proj(y, lng_ref[...], lnb_ref[...], w2_ref[...], b2_ref[...], x_ref[0])


def _conv_out(glu, x, dw3, dwb3, lng, lnb, w2, b2, tt):
    b, t, _ = glu.shape
    hb = tt // HALO
    return pl.pallas_call(
        functools.partial(_conv_out_body, tt),
        grid=(b, t // tt),
        in_specs=[pl.BlockSpec((1, tt, D_MODEL), lambda bi, i: (bi, i, 0)),
                  pl.BlockSpec((1, HALO, D_MODEL), lambda bi, i: (bi, jnp.maximum(i * hb - 1, 0), 0)),
                  pl.BlockSpec((1, tt, D_MODEL), lambda bi, i: (bi, i, 0)),
                  _const_spec((D_MODEL // 128, CONV_W, 128)),
                  _const_spec((D_MODEL // 128, 1, 128)),
                  _const_spec((1, D_MODEL)),
                  _const_spec((1, D_MODEL)),
                  _const_spec((D_MODEL, D_MODEL)),
                  _const_spec((1, D_MODEL))],
        out_specs=pl.BlockSpec((1, tt, D_MODEL), lambda bi, i: (bi, i, 0)),
        out_shape=jax.ShapeDtypeStruct((b, t, D_MODEL), F32),
        scratch_shapes=[pltpu.VMEM((D_MODEL // 128, tt + HALO, 128), F32),
                        pltpu.VMEM((D_MODEL // 128, tt, 128), F32)],
        compiler_params=_params(("parallel", "arbitrary")),
        name="conv_out",
    )(glu, glu, x, dw3, dwb3, lng, lnb, w2, b2)


def _conv_step_body(st_ref, u_ref, x_ref, dw_ref, dwb_ref, lng_ref, lnb_ref, w2_ref, b2_ref,
                    o_ref, ns_ref):
    nb = st_ref.shape[0]
    hist = CONV_W - 1
    dwh = dw_ref[0:hist, :]
    rows = []
    for bi in range(nb):
        rows.append(jnp.sum(st_ref[bi] * dwh, axis=0, keepdims=True))
        ns_ref[bi, 0:hist - 1, :] = st_ref[bi, 1:hist, :]
        ns_ref[bi, hist - 1:hist, :] = u_ref[bi:bi + 1, :]
    y = jnp.concatenate(rows, axis=0) + u_ref[...] * dw_ref[hist:hist + 1, :] + dwb_ref[...]
    o_ref[...] = _ln_silu_proj(y, lng_ref[...], lnb_ref[...], w2_ref[...], b2_ref[...], x_ref[...])


def _conv_step(state, glu, x, dw, dwb, lng, lnb, w2, b2):
    nb, hist, _ = state.shape
    return pl.pallas_call(
        _conv_step_body,
        grid=(1,),
        in_specs=[_const_spec((nb, hist, D_MODEL)), _const_spec((nb, D_MODEL)), _const_spec((nb, D_MODEL)),
                  _const_spec((CONV_W, D_MODEL)), _const_spec((1, D_MODEL)), _const_spec((1, D_MODEL)),
                  _const_spec((1, D_MODEL)), _const_spec((D_MODEL, D_MODEL)), _const_spec((1, D_MODEL))],
        out_specs=[_const_spec((nb, D_MODEL)), _const_spec((nb, hist, D_MODEL))],
        out_shape=[jax.ShapeDtypeStruct((nb, D_MODEL), F32),
                   jax.ShapeDtypeStruct((nb, hist, D_MODEL), F32)],
        compiler_params=_params(("arbitrary",)),
        name="conv_step",
    )(state, glu, x, dw, dwb, lng, lnb, w2, b2)


def _mlp_body(final, pre, tf, *refs):
    if pre:
        o_in_ref, wo_ref, refs = refs[0], refs[1], refs[2:]
    x_ref, gf_ref, up_ref, dn_ref, gp_ref, wg_ref, p_ref, wp_ref, gfin_ref, o_ref = refs
    x1 = x_ref[...]
    if pre:
        x1 = x1 + _dot(o_in_ref[...].astype(BF16), wo_ref[...])
    h = _rms(x1, gf_ref[...]).astype(BF16)
    ple = _dot(p_ref[...].astype(BF16), wp_ref[...])
    x2 = x1
    for c in range(0, D_FF, tf):
        a = jnp.maximum(_dot(h, up_ref[:, c:c + tf]), 0.0)
        x2 = x2 + _dot((a * a).astype(BF16), dn_ref[c:c + tf, :])
    gate = _sigmoid(_dot(_rms(x2, gp_ref[...]).astype(BF16), wg_ref[...]))
    x3 = x2 + gate * ple
    if final:
        x3 = _rms(x3, gfin_ref[...])
    o_ref[...] = x3


def _mlp_ple(x, gf, up, dn, gp, wg, p, layer, wp, gfin, tm, tf, final, pre=None):
    n = x.shape[0]
    tok = lambda i: (i, 0)
    one = pl.Buffered(1)
    in_specs, args = [], []
    if pre is not None:
        in_specs += [pl.BlockSpec((tm, D_MODEL), tok), _const_spec((D_MODEL, D_MODEL), single=True)]
        args += list(pre)
    in_specs += [pl.BlockSpec((tm, D_MODEL), tok),
                 _const_spec((1, D_MODEL)),
                 pl.BlockSpec((None, D_MODEL, D_FF), lambda i: (layer, 0, 0), pipeline_mode=one),
                 pl.BlockSpec((None, D_FF, D_MODEL), lambda i: (layer, 0, 0), pipeline_mode=one),
                 _const_spec((1, D_MODEL)),
                 pl.BlockSpec((None, D_MODEL, D_MODEL), lambda i: (layer, 0, 0), pipeline_mode=one),
                 pl.BlockSpec((None, tm, D_PLE), lambda i: (layer, i, 0)),
                 pl.BlockSpec((None, D_PLE, D_MODEL), lambda i: (layer, 0, 0), pipeline_mode=one),
                 _const_spec((1, D_MODEL))]
    args += [x, gf, up, dn, gp, wg, p, wp, gfin]
    return pl.pallas_call(
        functools.partial(_mlp_body, final, pre is not None, tf),
        grid=(n // tm,),
        in_specs=in_specs,
        out_specs=pl.BlockSpec((tm, D_MODEL), tok),
        out_shape=jax.ShapeDtypeStruct((n, D_MODEL), F32),
        compiler_params=_params(("parallel",)),
        name="mlp_ple",
    )(*args)


KV_NAMES = ("kc", "vc", "ks", "vs", "kw", "vw")
KV_COMPRESSED = ("kc", "vc")
KV_TOKEN_MAJOR = ("ks", "kw")
KV_WINDOWED = ("kw", "vw")


def _attn_in_body(seq_len, x_ref, g_ref, wq_ref, wkv_ref, wg_ref, q_ref, gt_ref, *kv_refs):
    h = _rms(x_ref[...], g_ref[...]).astype(BF16)
    q_ref[...] = _dot(h, wq_ref[...])
    gt_ref[...] = _sigmoid(_dot(h, wg_ref[...]))
    kv = _dot(h, wkv_ref[...])
    blocks = {n: kv[:, KV_W * i:KV_W * (i + 1)] for i, n in enumerate(KV_NAMES)}
    if seq_len is None:
        for n, r in zip(KV_NAMES, kv_refs):
            r[...] = blocks[n]
        return
    n_half = 2 * len(KV_COMPRESSED)
    n_tm = n_half + len(KV_TOKEN_MAJOR)
    half_refs = kv_refs[:n_half]
    tm_refs = kv_refs[n_half:n_tm]
    fm_refs = kv_refs[n_tm:n_tm + len(KV_NAMES)]
    last_refs = dict(zip(KV_WINDOWED, kv_refs[n_tm + len(KV_NAMES):]))
    for i, n in enumerate(KV_COMPRESSED):
        half_refs[2 * i][...] = blocks[n][:, :128]
        half_refs[2 * i + 1][...] = blocks[n][:, 128:]
    for n, r in zip(KV_TOKEN_MAJOR, tm_refs):
        r[...] = blocks[n]
    tps = seq_len // x_ref.shape[0]
    is_last = pl.program_id(0) % tps == tps - 1
    for n, r in zip(KV_NAMES, fm_refs):
        bt = blocks[n].T
        for g in range(N_KV):
            r[0, g] = bt[HEAD_DIM * g:HEAD_DIM * (g + 1), :]
        if n in last_refs:

            @pl.when(is_last)
            def _(bt=bt, dst=last_refs[n]):
                for g in range(N_KV):
                    dst[0, g] = bt[HEAD_DIM * g:HEAD_DIM * (g + 1), :]


def _attn_in(x, g, wq, wkv, wg, tm, seq_len=None):
    n = x.shape[0]
    tok = lambda i: (i, 0)
    kv_spec = pl.BlockSpec((tm, KV_W), tok)
    kv_shape = jax.ShapeDtypeStruct((n, KV_W), F32)
    if seq_len is None:
        kv_specs, kv_shapes = [kv_spec] * 6, [kv_shape] * 6
    else:
        tps = seq_len // tm
        fm_spec = pl.BlockSpec((1, N_KV, HEAD_DIM, tm), lambda i: (i // tps, 0, 0, i % tps))
        fm_shape = jax.ShapeDtypeStruct((n // seq_len, N_KV, HEAD_DIM, seq_len), F32)
        assert tm == min(WINDOW, seq_len)
        last_spec = pl.BlockSpec((1, N_KV, HEAD_DIM, tm), lambda i: (i // tps, 0, 0, 0))
        last_shape = jax.ShapeDtypeStruct((n // seq_len, N_KV, HEAD_DIM, tm), F32)
        half_spec = pl.BlockSpec((tm, 128), tok)
        half_shape = jax.ShapeDtypeStruct((n, 128), F32)
        n_half = 2 * len(KV_COMPRESSED)
        kv_specs = ([half_spec] * n_half + [kv_spec] * len(KV_TOKEN_MAJOR) + [fm_spec] * 6
                    + [last_spec] * len(KV_WINDOWED))
        kv_shapes = ([half_shape] * n_half + [kv_shape] * len(KV_TOKEN_MAJOR) + [fm_shape] * 6
                     + [last_shape] * len(KV_WINDOWED))
    return pl.pallas_call(
        functools.partial(_attn_in_body, seq_len),
        grid=(n // tm,),
        in_specs=[pl.BlockSpec((tm, D_MODEL), tok), _const_spec((1, D_MODEL)),
                  _const_spec((D_MODEL, D_MODEL)), _const_spec((D_MODEL, 6 * KV_W)),
                  _const_spec((D_MODEL, 128))],
        out_specs=[pl.BlockSpec((tm, D_MODEL), tok), pl.BlockSpec((tm, 128), tok)] + kv_specs,
        out_shape=[jax.ShapeDtypeStruct((n, D_MODEL), F32), jax.ShapeDtypeStruct((n, 128), F32)] + kv_shapes,
        compiler_params=_params(("arbitrary",)),
        name="attn_in",
    )(x, g, wq, wkv, wg)


def _compress_hidden(load_rows, n_chunks, wexp_ref):
    lo = jnp.concatenate([load_rows(2 * r) for r in range(CMP_STRIDE)], axis=1)
    hi = jnp.concatenate([load_rows(2 * r + 1) for r in range(CMP_STRIDE)], axis=1)
    acc = _dot(jnp.concatenate([lo, hi], axis=0).astype(BF16), wexp_ref[...])
    lo, hi = acc[:n_chunks], acc[n_chunks:]
    w = 2 * CMP_HID
    return jnp.concatenate([lo[:, :w], hi[:, :w], lo[:, w:], hi[:, w:]], axis=1)


def _pe_term(pe_ref, w1_ref):
    t = _dot(pe_ref[...], w1_ref[...])[0:1, :]
    return jnp.concatenate([t] * N_KV, axis=1)


def _compress_p_body(nc, kcl_ref, kch_ref, vcl_ref, vch_ref, wek_ref, pek_ref, w1k_ref, w2k_ref,
                     wev_ref, pev_ref, w1v_ref, w2v_ref, ok_ref, ov_ref):
    half = N_KV * CMP_HID
    row = lax.broadcasted_iota(jnp.int32, (nc, KV_W), 0)
    for src, we, pe, w1, w2, out in (((kcl_ref, kch_ref), wek_ref, pek_ref, w1k_ref, w2k_ref, ok_ref),
                                     ((vcl_ref, vch_ref), wev_ref, pev_ref, w1v_ref, w2v_ref, ov_ref)):
        hh = _compress_hidden(lambda k: src[k % 2][0, pl.ds(k // 2, nc, stride=CMP_STRIDE), :], nc, we)
        nxt = pltpu.roll(hh[:, half:], nc - 1, 0)
        a = hh[:, :half] + nxt + _pe_term(pe, w1)
        a = a * _sigmoid(a)
        res = _dot(a.astype(BF16), w2[...])
        out[0] = jnp.where(row < nc - 1, res, 0.0)


def _compress_prompt(kc_halves, vc_halves, wk, wv):
    b, t, _ = kc_halves[0].shape
    nc = t // CMP_STRIDE
    seq = pl.BlockSpec((1, t, 128), lambda i: (i, 0, 0))
    wspecs = [_const_spec((CMP_STRIDE * 128, 4 * CMP_HID)),
              _const_spec((8, 2 * CMP_STRIDE * HEAD_DIM)),
              _const_spec((2 * CMP_STRIDE * HEAD_DIM, CMP_HID)), _const_spec((N_KV * CMP_HID, KV_W))]
    out = pl.BlockSpec((1, nc, KV_W), lambda i: (i, 0, 0))
    return pl.pallas_call(
        functools.partial(_compress_p_body, nc),
        grid=(b,),
        in_specs=[seq] * 4 + wspecs + wspecs,
        out_specs=[out, out],
        out_shape=[jax.ShapeDtypeStruct((b, nc, KV_W), F32)] * 2,
        compiler_params=_params(("parallel",)),
        name="compress_prompt",
    )(*kc_halves, *vc_halves, *wk, *wv)


def _softmax_cols(s, mask):
    s = jnp.where(mask, s, NEG_INF)
    m = jnp.max(s, axis=0, keepdims=True)
    m = jnp.where(m == NEG_INF, 0.0, m)
    e = jnp.where(mask, jnp.exp(s - m), 0.0)
    return e * (1.0 / jnp.maximum(jnp.sum(e, axis=0, keepdims=True), 1e-30))


def _attn_p_body(tq, t, q_ref, gt_ref, ks_ref, vs_ref, kw_ref, vw_ref, kcm_ref, vcm_ref,
                 tbl_ref, bt_ref, bc_ref, msel_ref, o_ref,
                 ksb, kwb, vst, vwt, kcb, vct, btoep, bcmp, qt, qt2, selt, ot, m_ref, l_ref, acc_ref, gt_s):
    b = pl.program_id(0)
    qi = pl.program_id(1)
    tk = ATTN_TK
    nkt = t // tk
    nqt = t // tq
    r_qk = tq // tk
    ncmp = kcm_ref.shape[1]
    q0 = qi * tq
    n_near = bt_ref.shape[0]

    @pl.when((b == 0) & (qi == 0))
    def _():
        def per_head(h, carry):
            far = tbl_ref[NUM_BUCKETS - 1, h]
            for d in range(n_near):
                bk = bt_ref[d]
                acc = jnp.zeros((tk, tq), F32)
                for bb in range(NUM_BUCKETS):
                    acc = jnp.where(bk == bb, (tbl_ref[bb, h] - far) * LOG2E, acc)
                btoep[h, d] = acc
            btoep[h, n_near] = jnp.zeros((tk, tq), F32)
            bk = bc_ref[...]
            acc = jnp.zeros(bk.shape, F32)
            for bb in range(NUM_BUCKETS):
                acc = jnp.where(bk == bb, tbl_ref[bb, h], acc)
            bcmp[h] = acc
            return carry
        lax.fori_loop(0, N_HEADS, per_head, 0)

    @pl.when(qi == 0)
    def _():
        for kt in range(nkt):
            rows = slice(kt * tk, (kt + 1) * tk)
            for src, dstk in ((ks_ref, ksb), (kw_ref, kwb)):
                blk = src[0, rows, :]
                for g in range(N_KV):
                    dstk[g, rows, :] = blk[:, HEAD_DIM * g:HEAD_DIM * (g + 1)].astype(BF16)
            for src, dstv in ((vs_ref, vst), (vw_ref, vwt)):
                for g in range(N_KV):
                    dstv[g, kt] = src[0, g, :, rows].astype(BF16)
        kc = kcm_ref[0]
        vc_t = vcm_ref[0].T
        for g in range(N_KV):
            kcb[g] = kc[:, HEAD_DIM * g:HEAD_DIM * (g + 1)].astype(BF16)
            vct[g] = vc_t[HEAD_DIM * g:HEAD_DIM * (g + 1), :].astype(BF16)

    q_t = q_ref[0].T
    qt[...] = (q_t * SCALE).astype(BF16)
    qt2[...] = (q_t * (SCALE * LOG2E)).astype(BF16)
    gt_s[...] = gt_ref[0].T

    qpos_row = q0 + lax.broadcasted_iota(jnp.int32, (1, tq), 1)
    i_iota = lax.broadcasted_iota(jnp.int32, (tk, tq), 1)
    j_iota = lax.broadcasted_iota(jnp.int32, (tk, tq), 0)
    rel = i_iota - j_iota

    c_iota = lax.broadcasted_iota(jnp.int32, (ncmp, tq), 0)
    cmask = (CMP_STRIDE * c_iota + (2 * CMP_STRIDE - 1)) <= qpos_row
    coff = pl.multiple_of((nqt - 1 - qi) * (tq // CMP_STRIDE), 8)

    nblk = t // SLC_BLOCK
    blk_iota = lax.broadcasted_iota(jnp.int32, (nblk, tq), 0)
    cur = qpos_row // SLC_BLOCK
    valid = blk_iota * SLC_BLOCK <= qpos_row
    forced = (blk_iota == 0) | (blk_iota == cur) | (blk_iota == cur - 1)

    heads = range(N_HEADS)
    head_rows = [slice(HEAD_DIM * h, HEAD_DIM * (h + 1)) for h in heads]
    s_c = [_dot(kcb[h // HPG], qt[head_rows[h], :]) + bcmp[h, pl.ds(coff, ncmp), :] for h in heads]
    p_c = [_softmax_cols(s, cmask) for s in s_c]
    o_c = [_dot(vct[h // HPG], p_c[h].astype(BF16)) for h in heads]
    for h in heads:
        ot[head_rows[h], :] = gt_s[h:h + 1, :] * o_c[h]
    msel = msel_ref[...]
    scores = []
    for g in range(N_KV):
        pg = p_c[g * HPG]
        for hh in range(1, HPG):
            pg = pg + p_c[g * HPG + hh]
        hi, mid, lo = _split3(pg)
        score = _dot(msel, hi) + _dot(msel, mid) + _dot(msel, lo)
        scores.append(jnp.where(forced, jnp.inf, jnp.where(valid, score, NEG_INF)))
    n_slab = nblk // 8
    slabs = [[sc[8 * v:8 * (v + 1), :] for v in range(n_slab)] for sc in scores]
    ranks = [[jnp.zeros((8, tq), F32) for _ in range(n_slab)] for _ in scores]
    row8 = lax.broadcasted_iota(jnp.int32, (8, tq), 0)
    for i in range(nblk):
        for g in range(N_KV):
            si = scores[g][i:i + 1, :]
            for v, sl in enumerate(slabs[g]):
                if 8 * v > i:
                    beats = si >= sl
                elif 8 * v + 7 <= i:
                    beats = si > sl
                else:
                    beats = (si > sl) | ((row8 > i - 8 * v) & (si == sl))
                ranks[g][v] = ranks[g][v] + jnp.where(beats, 1.0, 0.0)
    for g in range(N_KV):
        for v, rk in enumerate(ranks[g]):
            sel = (rk < N_SELECT).astype(F32)
            for r in range(8):
                selt[g, 8 * v + r] = sel[r:r + 1, :]

    def flash_init(br):
        m_ref[br] = jnp.full(m_ref.shape[1:], NEG_INF, F32)
        l_ref[br] = jnp.zeros(l_ref.shape[1:], F32)
        acc_ref[br] = jnp.zeros(acc_ref.shape[1:], F32)

    def tile_step(br, kb, vt, kt, bias_d, add_fn):
        k0 = pl.multiple_of(kt * tk, tk)
        for g in range(N_KV):
            kblk = kb[g, pl.ds(k0, tk), :]
            vblk = vt[g, kt]
            add = add_fn(g)
            for hh in range(HPG):
                h = g * HPG + hh
                rows = slice(HEAD_DIM * h, HEAD_DIM * (h + 1))
                s = _dot(kblk, qt2[rows, :])
                if bias_d is not None:
                    s = s + btoep[h, bias_d]
                if add is not None:
                    s = s + add
                m_old = m_ref[br, h]
                m_new = jnp.maximum(m_old, jnp.max(s, axis=0, keepdims=True))
                m_safe = jnp.where(m_new == NEG_INF, 0.0, m_new)
                alpha = jnp.exp2(m_old - m_safe)
                p = jnp.exp2(s - m_safe)
                l_ref[br, h] = alpha * l_ref[br, h] + jnp.sum(p, axis=0, keepdims=True)
                acc_ref[br, rows, :] = alpha * acc_ref[br, rows, :] + _dot(vblk, p.astype(BF16))
                m_ref[br, h] = m_new

    def flash_finish(br, gate_row0):
        for h in range(N_HEADS):
            rows = slice(HEAD_DIM * h, HEAD_DIM * (h + 1))
            w = gt_s[gate_row0 + h:gate_row0 + h + 1, :] * (1.0 / jnp.maximum(l_ref[br, h], 1e-30))
            ot[rows, :] = ot[rows, :] + acc_ref[br, rows, :] * w

    def sel_rows(g, kt, ok=None):
        nb = tk // SLC_BLOCK
        rows = [jnp.broadcast_to(selt[g, kt * nb + r], (SLC_BLOCK, tq)) for r in range(nb)]
        thr = 0.5 if ok is None else jnp.where(ok, 0.5, 2.0)
        return jnp.concatenate(rows, axis=0) > thr

    def neg_unless(cond):
        return jnp.where(cond, 0.0, NEG_INF)

    causal = rel >= 0
    SLC, WIN = 0, 1

    flash_init(SLC)
    flash_init(WIN)
    n_far = jnp.maximum(qi - 1, 0)

    def slc_pair(pi, carry):
        for sub in range(2):
            kt = 2 * pi + sub
            ok = kt < n_far
            ktc = jnp.minimum(kt, jnp.maximum(n_far - 1, 0))
            tile_step(SLC, ksb, vst, ktc, None, lambda g: neg_unless(sel_rows(g, ktc, ok)))
        return carry

    lax.fori_loop(0, (n_far + 1) // 2, slc_pair, 0)
    kt1 = jnp.maximum(qi - 1, 0)
    tile_step(SLC, ksb, vst, kt1, 1, lambda g: neg_unless(sel_rows(g, kt1, qi >= 1)))
    tile_step(SLC, ksb, vst, qi, 0, lambda g: neg_unless(sel_rows(g, qi) & causal))

    wt = WINDOW // tk
    for u in range(wt + 1):
        off = wt - u
        kt = qi - off
        ok_add = jnp.where(kt >= 0, 0.0, NEG_INF)
        ktc = jnp.maximum(kt, 0)
        if off == wt:
            add_fn = lambda g, a=ok_add: neg_unless(rel <= 0) + a
        elif off == 0:
            add_fn = lambda g: neg_unless(causal)
        else:
            add_fn = lambda g, a=ok_add: a
        tile_step(WIN, kwb, vwt, ktc, off if off <= 1 else None, add_fn)
    flash_finish(SLC, N_HEADS)
    flash_finish(WIN, 2 * N_HEADS)

    o_ref[0] = ot[...].T


def _attn_prompt(q, gates, ks, vs, kw, vw, kcmp, vcmp, table, bkt_toep, bkt_cmp, msel, tq):
    b, t, _ = q.shape
    tk = ATTN_TK
    nkt = t // tk
    ncmp = kcmp.shape[1]
    n_near = bkt_toep.shape[0]
    qspec = pl.BlockSpec((1, tq, D_MODEL), lambda bi, i: (bi, i, 0))
    seq = pl.BlockSpec((1, t, KV_W), lambda bi, i: (bi, 0, 0))
    seq_fm = pl.BlockSpec((1, N_KV, HEAD_DIM, t), lambda bi, i: (bi, 0, 0, 0))
    cmp_spec = pl.BlockSpec((1, ncmp, KV_W), lambda bi, i: (bi, 0, 0))
    return pl.pallas_call(
        functools.partial(_attn_p_body, tq, t),
        grid=(b, t // tq),
        in_specs=[qspec, pl.BlockSpec((1, tq, 128), lambda bi, i: (bi, i, 0)),
                  seq, seq_fm, seq, seq_fm, cmp_spec, cmp_spec,
                  pl.BlockSpec(memory_space=pltpu.SMEM),
                  _const_spec(bkt_toep.shape), _const_spec(bkt_cmp.shape), _const_spec(msel.shape)],
        out_specs=qspec,
        out_shape=jax.ShapeDtypeStruct((b, t, D_MODEL), F32),
        scratch_shapes=[pltpu.VMEM((N_KV, t, HEAD_DIM), BF16), pltpu.VMEM((N_KV, t, HEAD_DIM), BF16),
                        pltpu.VMEM((N_KV, nkt, HEAD_DIM, tk), BF16), pltpu.VMEM((N_KV, nkt, HEAD_DIM, tk), BF16),
                        pltpu.VMEM((N_KV, ncmp, HEAD_DIM), BF16), pltpu.VMEM((N_KV, HEAD_DIM, ncmp), BF16),
                        pltpu.VMEM((N_HEADS, n_near + 1, tk, tq), F32),
                        pltpu.VMEM((N_HEADS,) + bkt_cmp.shape, F32),
                        pltpu.VMEM((D_MODEL, tq), BF16), pltpu.VMEM((D_MODEL, tq), BF16),
                        pltpu.VMEM((N_KV, t // SLC_BLOCK, 1, tq), F32),
                        pltpu.VMEM((D_MODEL, tq), F32),
                        pltpu.VMEM((2, N_HEADS, 1, tq), F32), pltpu.VMEM((2, N_HEADS, 1, tq), F32),
                        pltpu.VMEM((2, D_MODEL, tq), F32), pltpu.VMEM((128, tq), F32)],
        compiler_params=_params(("arbitrary", "arbitrary")),
        name="attn_prompt",
    )(q, gates, ks, vs, kw, vw, kcmp, vcmp, table, bkt_toep, bkt_cmp, msel)


PAGE = 128
PAGES_PER_GROUP = 32
Q_ROWS = 8


def _rank_desc(score_row, n):
    a = jnp.broadcast_to(score_row, (n, n))
    at = a.T
    i = lax.broadcasted_iota(jnp.int32, (n, n), 0)
    j = lax.broadcasted_iota(jnp.int32, (n, n), 1)
    beats = (at > a) | ((i < j) & (at == a))
    return jnp.sum(beats.astype(F32), axis=0, keepdims=True)


def _bias_rows(bucket, tcols):
    r = tcols.shape[0]
    out = jnp.zeros((r, bucket.shape[1]), F32)
    for bb in range(NUM_BUCKETS):
        out = jnp.where(bucket == bb, tcols[:, bb:bb + 1], out)
    return out


def _softmax_rows(s, mask):
    s = jnp.where(mask, s, NEG_INF)
    m = jnp.max(s, axis=1, keepdims=True)
    m = jnp.where(m == NEG_INF, 0.0, m)
    e = jnp.where(mask, jnp.exp(s - m), 0.0)
    return e / jnp.maximum(jnp.sum(e, axis=1, keepdims=True), 1e-30)


def _dot_nt(a, b):
    return lax.dot_general(a, b, (((1,), (1,)), ((), ())), preferred_element_type=F32)


def _cmp_s_body(past, pt_ref, ck_hbm, cv_hbm, kcn_ref, vcn_ref, q_ref,
                wek_ref, pek_ref, w1k_ref, w2k_ref, wev_ref, pev_ref, w1v_ref, w2v_ref,
                tbl_ref, bkt_ref, msel_ref, ocmp_ref, idx_ref,
                kbuf, vbuf, tbuf_k, tbuf_v, hk, hv, sem):
    b = pl.program_id(0)
    rows_pg = PAGES_PER_GROUP * PAGE
    cpg = rows_pg // CMP_STRIDE
    ngrp = past // rows_pg
    nc = past // CMP_STRIDE
    half = N_KV * CMP_HID

    def copies(bb, gi, slot):
        out = []
        for p in range(PAGES_PER_GROUP):
            page = pt_ref[bb, gi * PAGES_PER_GROUP + p]
            out.append(pltpu.make_async_copy(ck_hbm.at[page], kbuf.at[slot, p], sem.at[0, slot]))
            out.append(pltpu.make_async_copy(cv_hbm.at[page], vbuf.at[slot, p], sem.at[1, slot]))
        return out

    assert ngrp % 2 == 0

    @pl.when(b == 0)
    def _():
        for c in copies(b, 0, 0):
            c.start()

    def group(gi, carry):
        slot = gi % 2

        @pl.when(gi + 1 < ngrp)
        def _():
            for c in copies(b, gi + 1, 1 - slot):
                c.start()

        for c in copies(b, gi, slot):
            c.wait()
        transpose(kbuf, tbuf_k, slot)
        matmul(tbuf_v, wev_ref, hv, jnp.maximum(gi - 1, 0))
        matmul(tbuf_k, wek_ref, hk, gi)
        transpose(vbuf, tbuf_v, slot)
        return carry

    def transpose(buf, tb, slot):
        for p in range(PAGES_PER_GROUP):
            for hf in range(2):
                tb[hf, p * PAGE:(p + 1) * PAGE, :] = buf[slot, p, hf * 128:(hf + 1) * 128, :].T

    def matmul(tb, we, hdst, gi):
        r0 = pl.multiple_of(gi * cpg, cpg)
        hdst[pl.ds(r0, cpg), :] = _compress_hidden(
            lambda k: tb[k % 2, pl.ds(k // 2, cpg, stride=CMP_STRIDE), :], cpg, we)

    @pl.when(b == 0)
    def _():
        tbuf_v[...] = jnp.zeros_like(tbuf_v)

    lax.fori_loop(0, ngrp, group, 0)
    matmul(tbuf_v, wev_ref, hv, ngrp - 1)

    @pl.when(b + 1 < pl.num_programs(0))
    def _():
        for c in copies(b + 1, 0, 0):
            c.start()

    col = lax.broadcasted_iota(jnp.int32, (1, nc), 1)
    cmask = (CMP_STRIDE * col + (2 * CMP_STRIDE - 1)) <= past
    row8 = lax.broadcasted_iota(jnp.int32, (Q_ROWS, nc), 0)
    cmpd = []
    for new_ref, we, pe, w1, w2, hdst in ((kcn_ref, wek_ref, pek_ref, w1k_ref, w2k_ref, hk),
                                          (vcn_ref, wev_ref, pev_ref, w1v_ref, w2v_ref, hv)):
        new = jnp.broadcast_to(new_ref[0], (8, KV_W)).astype(BF16)
        new_lo = _dot(new[:, :128], we[0:128, :])
        new_hi = _dot(new[:, 128:], we[0:128, :])
        w = 2 * CMP_HID
        hdst[nc:nc + 8, :] = jnp.concatenate([new_lo[:, :w], new_hi[:, :w], new_lo[:, w:], new_hi[:, w:]], axis=1)
        a = hdst[0:nc, 0:half] + hdst[1:nc + 1, half:2 * half] + _pe_term(pe, w1)
        a = (a * _sigmoid(a)).astype(BF16)
        cmpd.append([_dot(a[:, CMP_HID * g:CMP_HID * (g + 1)], w2[...]).astype(BF16) for g in range(N_KV)])

    nsel = msel_ref.shape[1]
    lane = lax.broadcasted_iota(jnp.int32, (1, nsel), 1)
    n_blocks = past // SLC_BLOCK + 1
    cur = past // SLC_BLOCK
    forced = (lane == 0) | (lane == cur) | (lane == cur - 1)
    valid = (lane * SLC_BLOCK <= past) & (lane < n_blocks)
    k_iota = lax.broadcasted_iota(jnp.int32, (N_SELECT, nsel), 0).astype(F32)
    lane_f = lax.broadcasted_iota(jnp.int32, (N_SELECT, nsel), 1).astype(F32)
    groups = range(N_KV)
    msel = msel_ref[...]
    bkt = bkt_ref[...]
    s_g = [_dot_nt(q_ref[0, g].astype(BF16), cmpd[0][g]) * SCALE + _bias_rows(bkt, tbl_ref[g]) for g in groups]
    p_g = [_softmax_rows(s, cmask) for s in s_g]
    for g in groups:
        ocmp_ref[0, g] = _dot(p_g[g].astype(BF16), cmpd[1][g])
    pgs = [jnp.sum(jnp.where(row8 < HPG, p, 0.0), axis=0, keepdims=True) for p in p_g]
    parts = [_split3(jnp.broadcast_to(pg, (8, nc))) for pg in pgs]
    scores = [(_dot(hi, msel) + _dot(mid, msel) + _dot(lo, msel))[0:1, :] for hi, mid, lo in parts]
    scores = [jnp.where(forced, jnp.inf, jnp.where(valid, sc, NEG_INF)) for sc in scores]
    rank_g = [_rank_desc(sc, nsel) for sc in scores]
    hits = [jnp.where(jnp.broadcast_to(rk, (N_SELECT, nsel)) == k_iota, lane_f, 0.0) for rk in rank_g]
    for g in groups:
        idx_ref[0, g] = jnp.sum(hits[g], axis=1, keepdims=True).astype(jnp.int32)


def _cmp_sample(page_table, cache_k, cache_v, kc_new, vc_new, q8, wk, wv, tbl8, bkt, msel, past):
    nb = page_table.shape[0]
    rows_pg = PAGES_PER_GROUP * PAGE
    nc = past // CMP_STRIDE
    any_spec = pl.BlockSpec(memory_space=pl.ANY)
    new_spec = pl.BlockSpec((1, 1, KV_W), lambda i, pt: (i, 0, 0))
    cs = lambda shape: pl.BlockSpec(shape, lambda i, pt: (0,) * len(shape))
    wspecs = [cs((CMP_STRIDE * 128, 4 * CMP_HID)),
              cs((8, 2 * CMP_STRIDE * HEAD_DIM)),
              cs((2 * CMP_STRIDE * HEAD_DIM, CMP_HID)), cs((CMP_HID, HEAD_DIM))]
    grid_spec = pltpu.PrefetchScalarGridSpec(
        num_scalar_prefetch=1,
        grid=(nb,),
        in_specs=[any_spec, any_spec, new_spec, new_spec,
                  pl.BlockSpec((1, N_KV, Q_ROWS, HEAD_DIM), lambda i, pt: (i, 0, 0, 0))]
                 + wspecs + wspecs + [cs(tbl8.shape), cs(bkt.shape), cs(msel.shape)],
        out_specs=[pl.BlockSpec((1, N_KV, Q_ROWS, HEAD_DIM), lambda i, pt: (i, 0, 0, 0)),
                   pl.BlockSpec((1, N_KV, N_SELECT, 1), lambda i, pt: (i, 0, 0, 0))],
        scratch_shapes=[pltpu.VMEM((2, PAGES_PER_GROUP, 2 * PAGE, 128), F32),
                        pltpu.VMEM((2, PAGES_PER_GROUP, 2 * PAGE, 128), F32),
                        pltpu.VMEM((2, rows_pg, 128), F32), pltpu.VMEM((2, rows_pg, 128), F32),
                        pltpu.VMEM((nc + 8, 2 * N_KV * CMP_HID), F32),
                        pltpu.VMEM((nc + 8, 2 * N_KV * CMP_HID), F32),
                        pltpu.SemaphoreType.DMA((2, 2))])
    return pl.pallas_call(
        functools.partial(_cmp_s_body, past),
        grid_spec=grid_spec,
        out_shape=[jax.ShapeDtypeStruct((nb, N_KV, Q_ROWS, HEAD_DIM), F32),
                   jax.ShapeDtypeStruct((nb, N_KV, N_SELECT, 1), jnp.int32)],
        compiler_params=_params(("arbitrary",)),
        name="cmp_sample",
    )(page_table, cache_k, cache_v, kc_new, vc_new, q8, *wk, *wv, tbl8, bkt, msel)


NEW_LANES = 128


def _dec_s_body(past, idx_ref, pt_ref, csk_hbm, csv_hbm,
                q_ref, ksn_ref, vsn_ref, kwn_ref, vwn_ref, wk_ref, wv_ref, gt_ref, ocmp_ref,
                tbl_ref, bktw_ref, o_ref, wko_ref, wvo_ref,
                kbuf, vbuf, kwbuf, vwbuf, sem):
    b = pl.program_id(0)
    n_pages = past // PAGE
    cur = past // SLC_BLOCK
    bpp = PAGE // SLC_BLOCK
    lsel = N_SELECT * PAGE
    wb = wk_ref.shape[3]

    @pl.when(b == 0)
    def _():
        kbuf[...] = jnp.zeros_like(kbuf)
        vbuf[...] = jnp.zeros_like(vbuf)
        kwbuf[...] = jnp.zeros_like(kwbuf)
        vwbuf[...] = jnp.zeros_like(vwbuf)

    def gathers(bb):
        half = bb % 2
        out = []
        for g in range(N_KV):
            for k in range(N_SELECT):
                blk = idx_ref[bb, g * N_SELECT + k]
                page = pt_ref[bb, jnp.minimum(blk // bpp, n_pages - 1)]
                dst = pl.ds(k * PAGE, PAGE)
                out.append(pltpu.make_async_copy(csk_hbm.at[page, g], kbuf.at[half, g, :, dst], sem.at[0, half]))
                out.append(pltpu.make_async_copy(csv_hbm.at[page, g], vbuf.at[half, g, :, dst], sem.at[1, half]))
        return out

    @pl.when(b == 0)
    def _():
        for c in gathers(b):
            c.start()

    @pl.when(b + 1 < pl.num_programs(0))
    def _():
        for c in gathers(b + 1):
            c.start()

    cur_half = b % 2

    lane_wb = lax.broadcasted_iota(jnp.int32, (HEAD_DIM, wb), 1)
    for g in range(N_KV):
        for src, new_ref, dst, buf in ((wk_ref, kwn_ref, wko_ref, kwbuf), (wv_ref, vwn_ref, wvo_ref, vwbuf)):
            st = src[0, g]
            newc = new_ref[0, g]
            dst[0, g] = jnp.where(lane_wb == wb - 1, newc, pltpu.roll(st, wb - 1, 1))
            buf[g, :, 0:wb] = st
            buf[g, :, wb:wb + 1] = newc

    for c in gathers(b):
        c.wait()

    ls = lsel + NEW_LANES
    lw = wb + NEW_LANES
    lane_s = lax.broadcasted_iota(jnp.int32, (1, ls), 1)
    lane_w = lax.broadcasted_iota(jnp.int32, (1, lw), 1)
    wmask = lane_w <= wb
    tok = lane_s % PAGE
    groups = range(N_KV)
    smasks, sbiases = [], []
    for g in groups:
        kbuf[cur_half, g, :, lsel:lsel + 1] = ksn_ref[0, g]
        vbuf[cur_half, g, :, lsel:lsel + 1] = vsn_ref[0, g]
        kpos = past + (lane_s - lsel)
        gathered = lane_s < 0
        has_new = jnp.int32(0)
        for k in range(N_SELECT):
            blk = idx_ref[b, g * N_SELECT + k]
            in_slot = (lane_s // PAGE) == k
            kpos = jnp.where(in_slot, (blk // bpp) * PAGE + tok, kpos)
            gathered = gathered | (in_slot & (blk < cur) & ((tok // SLC_BLOCK) == (blk % bpp)))
            has_new = has_new | (blk == cur).astype(jnp.int32)
        smasks.append((gathered | ((lane_s == lsel) & (has_new > 0))) & (kpos <= past))
        sbiases.append(_bias_rows(_rel_bucket(past - kpos), tbl_ref[g]))
    bktw = bktw_ref[...]
    qgs = [q_ref[0, g].astype(BF16) for g in groups]
    s_slc = [_dot(qgs[g], kbuf[cur_half, g].astype(BF16)) * SCALE + sbiases[g] for g in groups]
    s_win = [_dot(qgs[g], kwbuf[g].astype(BF16)) * SCALE + _bias_rows(bktw, tbl_ref[g]) for g in groups]
    p_slc = [_softmax_rows(s_slc[g], smasks[g]) for g in groups]
    p_win = [_softmax_rows(s, wmask) for s in s_win]
    o_slc = [_dot_nt(p_slc[g].astype(BF16), vbuf[cur_half, g].astype(BF16)) for g in groups]
    o_win = [_dot_nt(p_win[g].astype(BF16), vwbuf[g].astype(BF16)) for g in groups]
    for g in groups:
        gt = gt_ref[0, g]
        o_ref[0, g] = gt[:, 0:1] * ocmp_ref[0, g] + gt[:, 1:2] * o_slc[g] + gt[:, 2:3] * o_win[g]


def _dec_sample(idx, page_table, cache_sk, cache_sv, q8, ks_new, vs_new, kw_new, vw_new, state_wk, state_wv,
                gates8, ocmp, tbl8, bktw, past):
    nb, wb = state_wk.shape[0], state_wk.shape[3]
    any_spec = pl.BlockSpec(memory_space=pl.ANY)
    new_spec = pl.BlockSpec((1, N_KV, HEAD_DIM, 1), lambda i, *_: (i, 0, 0, 0))
    st_spec = pl.BlockSpec((1, N_KV, HEAD_DIM, wb), lambda i, *_: (i, 0, 0, 0))
    head_spec = pl.BlockSpec((1, N_KV, Q_ROWS, HEAD_DIM), lambda i, *_: (i, 0, 0, 0))
    cs = lambda shape: pl.BlockSpec(shape, lambda i, *_: (0,) * len(shape))
    lsel = N_SELECT * PAGE
    grid_spec = pltpu.PrefetchScalarGridSpec(
        num_scalar_prefetch=2,
        grid=(nb,),
        in_specs=[any_spec, any_spec, head_spec,
                  new_spec, new_spec, new_spec, new_spec, st_spec, st_spec,
                  pl.BlockSpec((1, N_KV, Q_ROWS, 3), lambda i, *_: (i, 0, 0, 0)), head_spec,
                  cs(tbl8.shape), cs(bktw.shape)],
        out_specs=[head_spec, st_spec, st_spec],
        scratch_shapes=[pltpu.VMEM((2, N_KV, HEAD_DIM, lsel + NEW_LANES), F32),
                        pltpu.VMEM((2, N_KV, HEAD_DIM, lsel + NEW_LANES), F32),
                        pltpu.VMEM((N_KV, HEAD_DIM, wb + NEW_LANES), F32),
                        pltpu.VMEM((N_KV, HEAD_DIM, wb + NEW_LANES), F32),
                        pltpu.SemaphoreType.DMA((2, 2))])
    return pl.pallas_call(
        functools.partial(_dec_s_body, past),
        grid_spec=grid_spec,
        out_shape=[jax.ShapeDtypeStruct((nb, N_KV, Q_ROWS, HEAD_DIM), F32),
                   jax.ShapeDtypeStruct(state_wk.shape, F32),
                   jax.ShapeDtypeStruct(state_wv.shape, F32)],
        compiler_params=_params(("arbitrary",)),
        name="dec_sample",
    )(idx, page_table, cache_sk, cache_sv, q8, ks_new, vs_new, kw_new, vw_new,
      state_wk, state_wv, gates8, ocmp, tbl8, bktw)


def _expand_w1(w1):
    eye = jnp.eye(2, dtype=w1.dtype)
    parts = []
    for part in (w1[:CMP_STRIDE * HEAD_DIM], w1[CMP_STRIDE * HEAD_DIM:]):
        w = part.reshape(CMP_STRIDE, HEAD_DIM, CMP_HID)
        parts.append(jnp.einsum("gh,rdn->rgdhn", eye, w).reshape(CMP_STRIDE, 2 * HEAD_DIM, 2 * CMP_HID))
    w = jnp.concatenate(parts, axis=-1).astype(BF16)
    return w.reshape(CMP_STRIDE * 2 * HEAD_DIM, 4 * CMP_HID)


def _expand_w2(w2):
    eye = jnp.eye(N_KV, dtype=w2.dtype)
    return jnp.einsum("gh,nd->gnhd", eye, w2).reshape(N_KV * CMP_HID, KV_W).astype(BF16)


def _pe_rows(pe):
    return jnp.broadcast_to(pe.reshape(1, -1), (8, pe.size)).astype(BF16)


def _select_matrix(n_cmp, n_blocks):
    r = SLC_BLOCK // CMP_STRIDE
    m = np.zeros((n_cmp, n_blocks), np.float32)
    for s in range(n_blocks):
        for a in range(r):
            for bb in range(2):
                c = r * s + a - bb
                if 0 <= c < n_cmp:
                    m[c, s] += 1.0
    return m


def _pad_to(a, size, axis):
    pad = [(0, 0)] * a.ndim
    pad[axis] = (0, size - a.shape[axis])
    return jnp.pad(a, pad)


def kernel(x_prompt, x_sample, cache_cmp_k, cache_cmp_v, cache_slc_k, cache_slc_v, state_win_k, state_win_v, state_conv, page_table, p_prompt, p_sample, rel_table, norm_mix, norm_ffn, norm_ple, norm_final, conv_w1, conv_b1, conv_dw, conv_dwb, conv_ln_g, conv_ln_b, conv_w2, conv_b2, attn_w_in, attn_w_out, cmpk_w1, cmpk_pe, cmpk_w2, cmpv_w1, cmpv_pe, cmpv_w2, mlp_up, mlp_down, ple_proj, ple_gate):
    nbp, t, _ = x_prompt.shape
    nbs = x_sample.shape[0]
    n_p = nbp * t
    n_pool = cache_cmp_k.shape[1]
    past = page_table.shape[1] * cache_cmp_k.shape[2]
    wb = state_win_k.shape[2]
    hist = CONV_W - 1
    tm_p, tf, tt, tq = 512, 1024, 256, ATTN_TK

    row = lambda a: a.reshape(1, -1)
    bf = lambda a: a.astype(BF16)

    w1c, w2c = bf(conv_w1[0]), bf(conv_w2[0])
    dw3 = conv_dw[0].reshape(CONV_W, D_MODEL // 128, 128).transpose(1, 0, 2)
    dwb3 = conv_dwb[0].reshape(D_MODEL // 128, 1, 128)
    up, dn = bf(mlp_up), bf(mlp_down)
    wg, wp = bf(ple_gate), bf(ple_proj)
    w_in = attn_w_in[0]
    wq = bf(w_in[:, :D_MODEL])
    wkv = bf(w_in[:, D_MODEL:D_MODEL + 6 * KV_W])
    wgt = bf(_pad_to(w_in[:, D_MODEL + 6 * KV_W:], 128, 1))
    w_out = bf(attn_w_out[0])
    cw = []
    for w1, pe, w2 in ((cmpk_w1[0], cmpk_pe[0], cmpk_w2[0]), (cmpv_w1[0], cmpv_pe[0], cmpv_w2[0])):
        cw.append((_expand_w1(w1), _pe_rows(pe), bf(w1), _expand_w2(w2), bf(w2)))
    wk_p, wv_p = [c[:4] for c in cw]
    wk_s, wv_s = [c[:3] + c[4:] for c in cw]

    def tail(x, i, p, tm, final, pre=None):
        return _mlp_ple(x, row(norm_ffn[i]), up, dn, row(norm_ple[i]), wg, p, i, wp,
                        row(norm_final), tm, tf, final, pre)

    conv_args = (row(conv_ln_g[0]), row(conv_ln_b[0]), w2c, row(conv_b2[0]))

    xp = x_prompt.reshape(n_p, D_MODEL)
    glu = _conv_in(xp, row(norm_mix[0]), w1c, row(conv_b1[0]), tm_p).reshape(nbp, t, D_MODEL)
    x1 = _conv_out(glu, x_prompt, dw3, dwb3, *conv_args, tt).reshape(n_p, D_MODEL)
    conv_p = glu[:, t - hist:][None]
    pp = p_prompt.reshape(-1, n_p, D_PLE)
    ps = p_sample.reshape(-1, nbs, D_PLE)
    x2 = tail(x1, 0, pp, tm_p, False)
    q, gates, kcl, kch, vcl, vch, ks, kw, *rest = _attn_in(x2, row(norm_mix[1]), wq, wkv, wgt, tm_p, seq_len=t)
    kv_fm, win_fm = rest[:len(KV_NAMES)], rest[len(KV_NAMES):]
    vs_fm, vw_fm = kv_fm[KV_NAMES.index("vs")], kv_fm[KV_NAMES.index("vw")]
    seq = lambda a: a.reshape(nbp, t, -1)
    kcmp, vcmp = _compress_prompt((seq(kcl), seq(kch)), (seq(vcl), seq(vch)), wk_p, wv_p)

    ncmp = t // CMP_STRIDE
    ii = jnp.arange(tq, dtype=jnp.int32)[None, :]
    jj = jnp.arange(ATTN_TK, dtype=jnp.int32)[:, None]
    r_qk = tq // ATTN_TK
    bkt_toep = jnp.stack([_rel_bucket(d * ATTN_TK + ii - jj) for d in range(1 - r_qk, 2)])
    nshift = (t // tq - 1) * (tq // CMP_STRIDE)
    rr = jnp.arange(ncmp + nshift, dtype=jnp.int32)[:, None]
    bkt_cmp = _rel_bucket(ii - CMP_STRIDE * (rr - nshift) - (2 * CMP_STRIDE - 1))
    msel_p = jnp.asarray(_select_matrix(ncmp, t // SLC_BLOCK).T, BF16)
    o_p = _attn_prompt(seq(q), seq(gates), seq(ks), vs_fm, seq(kw), vw_fm, kcmp, vcmp,
                       rel_table, bkt_toep, bkt_cmp, msel_p, tq)
    y_p = tail(x2, 1, pp, tm_p, True, pre=(o_p.reshape(n_p, D_MODEL), w_out))

    from_fm = lambda a: jnp.transpose(a, (0, 3, 1, 2))[None]
    kv_p = [from_fm(a) for a in kv_fm[:4]] + [from_fm(a) for a in win_fm]

    xs = x_sample.reshape(nbs, D_MODEL)
    glu_s = _conv_in(xs, row(norm_mix[0]), w1c, row(conv_b1[0]), nbs)
    x1s, conv_s = _conv_step(state_conv[0], glu_s, xs, conv_dw[0], row(conv_dwb[0]), *conv_args)
    x2s = tail(x1s, 0, ps, nbs, False)
    qs, gts, kcs, vcs, kss, vss, kws, vws = _attn_in(x2s, row(norm_mix[1]), wq, wkv, wgt, nbs)

    q8 = _pad_to(qs.reshape(nbs, N_KV, HPG, HEAD_DIM), Q_ROWS, 2)
    tbl8 = _pad_to(rel_table.T.reshape(N_KV, HPG, NUM_BUCKETS), Q_ROWS, 1)
    nc_s = past // CMP_STRIDE
    cc = jnp.arange(nc_s, dtype=jnp.int32)[None, :]
    bkt_s = _rel_bucket(past - (CMP_STRIDE * cc + 2 * CMP_STRIDE - 1))
    n_blocks_s = past // SLC_BLOCK + 1
    msel_s = jnp.asarray(_select_matrix(nc_s, -(-n_blocks_s // 128) * 128), BF16)
    new3 = lambda a: a.reshape(nbs, 1, KV_W)
    fmajor = lambda c: jnp.transpose(c[0], (0, 2, 3, 1))
    cache2 = lambda c: fmajor(c).reshape(n_pool, 2 * PAGE, PAGE)
    ocmp, idx = _cmp_sample(page_table, cache2(cache_cmp_k), cache2(cache_cmp_v), new3(kcs), new3(vcs),
                            q8, wk_s, wv_s, tbl8, bkt_s, msel_s, past)

    ww = jnp.arange(wb + NEW_LANES, dtype=jnp.int32)[None, :]
    bkt_w = _rel_bucket(wb - ww)
    gates8 = _pad_to(gts[:, :3 * N_HEADS].reshape(nbs, 3, N_KV, HPG).transpose(0, 2, 3, 1), Q_ROWS, 2)
    col4 = lambda a: a.reshape(nbs, N_KV, HEAD_DIM, 1)
    o8, wk_new, wv_new = _dec_sample(idx.reshape(nbs, N_KV * N_SELECT), page_table,
                                     fmajor(cache_slc_k), fmajor(cache_slc_v), q8,
                                     col4(kss), col4(vss), col4(kws), col4(vws),
                                     fmajor(state_win_k), fmajor(state_win_v), gates8, ocmp, tbl8, bkt_w, past)
    o_s = o8[:, :, :HPG].reshape(nbs, D_MODEL)
    y_s = tail(x2s, 1, ps, nbs, True, pre=(o_s, w_out))

    kv5s = lambda a: a.reshape(1, nbs, 1, N_KV, HEAD_DIM)
    return (y_p.reshape(nbp, t, D_MODEL), y_s.reshape(nbs, 1, D_MODEL), *kv_p, conv_p,
            kv5s(kcs), kv5s(vcs), kv5s(kss), kv5s(vss), from_fm(wk_new), from_fm(wv_new), conv_s[None])
```

```python
import functools
import math

import numpy as np
import jax
import jax.numpy as jnp
from jax import lax
from jax.experimental import pallas as pl
from jax.experimental.pallas import tpu as pltpu

F32 = jnp.float32
BF16 = jnp.bfloat16

D_MODEL = 1024
D_PLE = 256
CONV_W = 31
N_HEADS = 16
HEAD_DIM = 64
N_KV = 4
HPG = 4
KV_W = N_KV * HEAD_DIM
CMP_STRIDE = 16
CMP_HID = 128
SLC_BLOCK = 64
N_SELECT = 16
WINDOW = 512
NUM_BUCKETS = 32
MAX_DISTANCE = 128
D_FF = 4096
EPS = 1e-6
SCALE = HEAD_DIM ** -0.5
assert math.frexp(SCALE)[0] == 0.5
NEG_INF = float("-inf")

ATTN_TK = 128
assert ATTN_TK >= MAX_DISTANCE
LOG2E = math.log2(math.e)

VMEM_LIMIT = 56 * 1024 * 1024
HALO = 32


def _sigmoid(x):
    return 1.0 / (1.0 + jnp.exp(-x))


def _rms(x, g):
    return x * lax.rsqrt(jnp.mean(x * x, axis=-1, keepdims=True) + EPS) * g


def _dot(a, b):
    return jnp.dot(a, b, preferred_element_type=F32)


def _split3(x):
    hi = x.astype(BF16)
    r1 = x - hi.astype(F32)
    mid = r1.astype(BF16)
    lo = (r1 - mid.astype(F32)).astype(BF16)
    return hi, mid, lo


def _rel_bucket(dist):
    n = jnp.maximum(dist, 0)
    max_exact = NUM_BUCKETS // 2
    nf = jnp.maximum(n, 1).astype(F32)
    large = max_exact + (jnp.log(nf / max_exact) / math.log(MAX_DISTANCE / max_exact)
                         * (NUM_BUCKETS - max_exact)).astype(jnp.int32)
    large = jnp.minimum(large, NUM_BUCKETS - 1)
    return jnp.where(n < max_exact, n, large)


def _params(sem):
    return pltpu.CompilerParams(dimension_semantics=sem, vmem_limit_bytes=VMEM_LIMIT)


def _const_spec(shape, single=False):
    n = len(shape)
    if single:
        return pl.BlockSpec(shape, lambda *_: (0,) * n, pipeline_mode=pl.Buffered(1))
    return pl.BlockSpec(shape, lambda *_: (0,) * n)


def _conv_in_body(x_ref, g_ref, w_ref, b_ref, o_ref):
    h = _rms(x_ref[...], g_ref[...]).astype(BF16)
    u = _dot(h, w_ref[...]) + b_ref[...]
    o_ref[...] = u[:, :D_MODEL] * _sigmoid(u[:, D_MODEL:])


def _conv_in(x, g, w1, b1, tm):
    n = x.shape[0]
    return pl.pallas_call(
        _conv_in_body,
        grid=(n // tm,),
        in_specs=[pl.BlockSpec((tm, D_MODEL), lambda i: (i, 0)),
                  _const_spec((1, D_MODEL)),
                  _const_spec((D_MODEL, 2 * D_MODEL)),
                  _const_spec((1, 2 * D_MODEL))],
        out_specs=pl.BlockSpec((tm, D_MODEL), lambda i: (i, 0)),
        out_shape=jax.ShapeDtypeStruct((n, D_MODEL), F32),
        compiler_params=_params(("parallel",)),
        name="conv_in",
    )(x, g, w1, b1)


def _ln_silu_proj(y, lng, lnb, w2, b2, x):
    mu = jnp.mean(y, axis=-1, keepdims=True)
    yc = y - mu
    var = jnp.mean(yc * yc, axis=-1, keepdims=True)
    z = yc * lax.rsqrt(var + EPS) * lng + lnb
    z = z * _sigmoid(z)
    return _dot(z.astype(BF16), w2) + b2 + x


def _conv_out_body(tt, cur_ref, halo_ref, x_ref, dw_ref, dwb_ref, lng_ref, lnb_ref, w2_ref, b2_ref,
                   o_ref, ctx_ref, y_ref):
    i = pl.program_id(1)
    n_strip = D_MODEL // 128
    keep = (i > 0).astype(F32)
    for c in range(n_strip):
        ctx_ref[c, 0:HALO, :] = halo_ref[0, :, 128 * c:128 * (c + 1)] * keep
        ctx_ref[c, HALO:, :] = cur_ref[0, :, 128 * c:128 * (c + 1)]

    def strip(c, carry):
        acc = jnp.broadcast_to(dwb_ref[c], (tt, 128))
        for k in range(CONV_W):
            off = k + HALO - (CONV_W - 1)
            acc = acc + ctx_ref[c, off:off + tt, :] * dw_ref[c, k:k + 1, :]
        y_ref[c] = acc
        return carry

    lax.fori_loop(0, n_strip, strip, 0)
    y = jnp.concatenate([y_ref[c] for c in range(n_strip)], axis=1)
    o_ref[0] = _ln_silu_proj(y, lng_ref[...], lnb_ref[...], w2_ref[...], b2_ref[...], x_ref[0])


def _conv_out(glu, x, dw3, dwb3, lng, lnb, w2, b2, tt):
    b, t, _ = glu.shape
    hb = tt // HALO
    return pl.pallas_call(
        functools.partial(_conv_out_body, tt),
        grid=(b, t // tt),
        in_specs=[pl.BlockSpec((1, tt, D_MODEL), lambda bi, i: (bi, i, 0)),
                  pl.BlockSpec((1, HALO, D_MODEL), lambda bi, i: (bi, jnp.maximum(i * hb - 1, 0), 0)),
                  pl.BlockSpec((1, tt, D_MODEL), lambda bi, i: (bi, i, 0)),
                  _const_spec((D_MODEL // 128, CONV_W, 128)),
                  _const_spec((D_MODEL // 128, 1, 128)),
                  _const_spec((1, D_MODEL)),
                  _const_spec((1, D_MODEL)),
                  _const_spec((D_MODEL, D_MODEL)),
                  _const_spec((1, D_MODEL))],
        out_specs=pl.BlockSpec((1, tt, D_MODEL), lambda bi, i: (bi, i, 0)),
        out_shape=jax.ShapeDtypeStruct((b, t, D_MODEL), F32),
        scratch_shapes=[pltpu.VMEM((D_MODEL // 128, tt + HALO, 128), F32),
                        pltpu.VMEM((D_MODEL // 128, tt, 128), F32)],
        compiler_params=_params(("parallel", "arbitrary")),
        name="conv_out",
    )(glu, glu, x, dw3, dwb3, lng, lnb, w2, b2)


def _conv_step_body(st_ref, u_ref, x_ref, dw_ref, dwb_ref, lng_ref, lnb_ref, w2_ref, b2_ref,
                    o_ref, ns_ref):
    hist = st_ref.shape[0]
    u = u_ref[...]
    y = u * dw_ref[hist:hist + 1, :] + dwb_ref[...]
    for k in range(hist):
        y = y + st_ref[k] * dw_ref[k:k + 1, :]
        ns_ref[k] = st_ref[k + 1] if k + 1 < hist else u
    o_ref[...] = _ln_silu_proj(y, lng_ref[...], lnb_ref[...], w2_ref[...], b2_ref[...], x_ref[...])


def _conv_step(state, glu, x, dw, dwb, lng, lnb, w2, b2):
    hist, nb, _ = state.shape
    return pl.pallas_call(
        _conv_step_body,
        grid=(1,),
        in_specs=[_const_spec((hist, nb, D_MODEL)), _const_spec((nb, D_MODEL)), _const_spec((nb, D_MODEL)),
                  _const_spec((CONV_W, D_MODEL)), _const_spec((1, D_MODEL)), _const_spec((1, D_MODEL)),
                  _const_spec((1, D_MODEL)), _const_spec((D_MODEL, D_MODEL)), _const_spec((1, D_MODEL))],
        out_specs=[_const_spec((nb, D_MODEL)), _const_spec((hist, nb, D_MODEL))],
        out_shape=[jax.ShapeDtypeStruct((nb, D_MODEL), F32),
                   jax.ShapeDtypeStruct((hist, nb, D_MODEL), F32)],
        compiler_params=_params(("arbitrary",)),
        name="conv_step",
    )(state, glu, x, dw, dwb, lng, lnb, w2, b2)


def _mlp_body(final, pre, tf, *refs):
    if pre:
        o_in_ref, wo_ref, refs = refs[0], refs[1], refs[2:]
    x_ref, gf_ref, up_ref, dn_ref, gp_ref, wg_ref, p_ref, wp_ref, gfin_ref, o_ref = refs
    x1 = x_ref[...]
    if pre:
        x1 = x1 + _dot(o_in_ref[...].astype(BF16), wo_ref[...])
    h = _rms(x1, gf_ref[...]).astype(BF16)
    ple = _dot(p_ref[...].astype(BF16), wp_ref[...])
    x2 = x1
    for c in range(0, D_FF, tf):
        a = jnp.maximum(_dot(h, up_ref[:, c:c + tf]), 0.0)
        x2 = x2 + _dot((a * a).astype(BF16), dn_ref[c:c + tf, :])
    gate = _sigmoid(_dot(_rms(x2, gp_ref[...]).astype(BF16), wg_ref[...]))
    x3 = x2 + gate * ple
    if final:
        x3 = _rms(x3, gfin_ref[...])
    o_ref[...] = x3


def _mlp_ple(x, gf, up, dn, gp, wg, p, layer, wp, gfin, tm, tf, final, pre=None):
    n = x.shape[0]
    tok = lambda i: (i, 0)
    one = pl.Buffered(1)
    in_specs, args = [], []
    if pre is not None:
        in_specs += [pl.BlockSpec((tm, D_MODEL), tok), _const_spec((D_MODEL, D_MODEL), single=True)]
        args += list(pre)
    in_specs += [pl.BlockSpec((tm, D_MODEL), tok),
                 _const_spec((1, D_MODEL)),
                 pl.BlockSpec((None, D_MODEL, D_FF), lambda i: (layer, 0, 0), pipeline_mode=one),
                 pl.BlockSpec((None, D_FF, D_MODEL), lambda i: (layer, 0, 0), pipeline_mode=one),
                 _const_spec((1, D_MODEL)),
                 pl.BlockSpec((None, D_MODEL, D_MODEL), lambda i: (layer, 0, 0), pipeline_mode=one),
                 pl.BlockSpec((None, tm, D_PLE), lambda i: (layer, i, 0)),
                 pl.BlockSpec((None, D_PLE, D_MODEL), lambda i: (layer, 0, 0), pipeline_mode=one),
                 _const_spec((1, D_MODEL))]
    args += [x, gf, up, dn, gp, wg, p, wp, gfin]
    return pl.pallas_call(
        functools.partial(_mlp_body, final, pre is not None, tf),
        grid=(n // tm,),
        in_specs=in_specs,
        out_specs=pl.BlockSpec((tm, D_MODEL), tok),
        out_shape=jax.ShapeDtypeStruct((n, D_MODEL), F32),
        compiler_params=_params(("parallel",)),
        name="mlp_ple",
    )(*args)


KV_NAMES = ("kc", "vc", "ks", "vs", "kw", "vw")
KV_COMPRESSED = ("kc", "vc")
KV_TOKEN_MAJOR = ("ks", "kw")
KV_WINDOWED = ("kw", "vw")


def _attn_in_body(seq_len, x_ref, g_ref, wq_ref, wkv_ref, wg_ref, q_ref, gt_ref, *kv_refs):
    h = _rms(x_ref[...], g_ref[...]).astype(BF16)
    q_ref[...] = _dot(h, wq_ref[...])
    gt_ref[...] = _sigmoid(_dot(h, wg_ref[...]))
    kv = _dot(h, wkv_ref[...])
    blocks = {n: kv[:, KV_W * i:KV_W * (i + 1)] for i, n in enumerate(KV_NAMES)}
    if seq_len is None:
        for n, r in zip(KV_NAMES, kv_refs):
            r[...] = blocks[n]
        return
    n_half = 2 * len(KV_COMPRESSED)
    n_tm = n_half + len(KV_TOKEN_MAJOR)
    half_refs = kv_refs[:n_half]
    tm_refs = kv_refs[n_half:n_tm]
    fm_refs = kv_refs[n_tm:n_tm + len(KV_NAMES)]
    last_refs = dict(zip(KV_WINDOWED, kv_refs[n_tm + len(KV_NAMES):]))
    for i, n in enumerate(KV_COMPRESSED):
        half_refs[2 * i][...] = blocks[n][:, :128]
        half_refs[2 * i + 1][...] = blocks[n][:, 128:]
    for n, r in zip(KV_TOKEN_MAJOR, tm_refs):
        r[...] = blocks[n]
    tps = seq_len // x_ref.shape[0]
    is_last = pl.program_id(0) % tps == tps - 1
    for n, r in zip(KV_NAMES, fm_refs):
        bt = blocks[n].T
        for g in range(N_KV):
            r[0, g] = bt[HEAD_DIM * g:HEAD_DIM * (g + 1), :]
        if n in last_refs:

            @pl.when(is_last)
            def _(bt=bt, dst=last_refs[n]):
                for g in range(N_KV):
                    dst[0, g] = bt[HEAD_DIM * g:HEAD_DIM * (g + 1), :]


def _attn_in(x, g, wq, wkv, wg, tm, seq_len=None):
    n = x.shape[0]
    tok = lambda i: (i, 0)
    kv_spec = pl.BlockSpec((tm, KV_W), tok)
    kv_shape = jax.ShapeDtypeStruct((n, KV_W), F32)
    if seq_len is None:
        kv_specs, kv_shapes = [kv_spec] * 6, [kv_shape] * 6
    else:
        tps = seq_len // tm
        fm_spec = pl.BlockSpec((1, N_KV, HEAD_DIM, tm), lambda i: (i // tps, 0, 0, i % tps))
        fm_shape = jax.ShapeDtypeStruct((n // seq_len, N_KV, HEAD_DIM, seq_len), F32)
        assert tm == min(WINDOW, seq_len)
        last_spec = pl.BlockSpec((1, N_KV, HEAD_DIM, tm), lambda i: (i // tps, 0, 0, 0))
        last_shape = jax.ShapeDtypeStruct((n // seq_len, N_KV, HEAD_DIM, tm), F32)
        half_spec = pl.BlockSpec((tm, 128), tok)
        half_shape = jax.ShapeDtypeStruct((n, 128), F32)
        n_half = 2 * len(KV_COMPRESSED)
        kv_specs = ([half_spec] * n_half + [kv_spec] * len(KV_TOKEN_MAJOR) + [fm_spec] * 6
                    + [last_spec] * len(KV_WINDOWED))
        kv_shapes = ([half_shape] * n_half + [kv_shape] * len(KV_TOKEN_MAJOR) + [fm_shape] * 6
                     + [last_shape] * len(KV_WINDOWED))
    return pl.pallas_call(
        functools.partial(_attn_in_body, seq_len),
        grid=(n // tm,),
        in_specs=[pl.BlockSpec((tm, D_MODEL), tok), _const_spec((1, D_MODEL)),
                  _const_spec((D_MODEL, D_MODEL)), _const_spec((D_MODEL, 6 * KV_W)),
                  _const_spec((D_MODEL, 128))],
        out_specs=[pl.BlockSpec((tm, D_MODEL), tok), pl.BlockSpec((tm, 128), tok)] + kv_specs,
        out_shape=[jax.ShapeDtypeStruct((n, D_MODEL), F32), jax.ShapeDtypeStruct((n, 128), F32)] + kv_shapes,
        compiler_params=_params(("arbitrary",)),
        name="attn_in",
    )(x, g, wq, wkv, wg)


def _compress_hidden(load_rows, n_chunks, wexp_ref):
    lo = jnp.concatenate([load_rows(2 * r) for r in range(CMP_STRIDE)], axis=1)
    hi = jnp.concatenate([load_rows(2 * r + 1) for r in range(CMP_STRIDE)], axis=1)
    acc = _dot(jnp.concatenate([lo, hi], axis=0).astype(BF16), wexp_ref[...])
    lo, hi = acc[:n_chunks], acc[n_chunks:]
    w = 2 * CMP_HID
    return jnp.concatenate([lo[:, :w], hi[:, :w], lo[:, w:], hi[:, w:]], axis=1)


def _pe_term(pe_ref, w1_ref):
    t = _dot(pe_ref[...], w1_ref[...])[0:1, :]
    return jnp.concatenate([t] * N_KV, axis=1)


def _compress_p_body(nc, kcl_ref, kch_ref, vcl_ref, vch_ref, wek_ref, pek_ref, w1k_ref, w2k_ref,
                     wev_ref, pev_ref, w1v_ref, w2v_ref, ok_ref, ov_ref):
    half = N_KV * CMP_HID
    row = lax.broadcasted_iota(jnp.int32, (nc, KV_W), 0)
    for src, we, pe, w1, w2, out in (((kcl_ref, kch_ref), wek_ref, pek_ref, w1k_ref, w2k_ref, ok_ref),
                                     ((vcl_ref, vch_ref), wev_ref, pev_ref, w1v_ref, w2v_ref, ov_ref)):
        hh = _compress_hidden(lambda k: src[k % 2][0, pl.ds(k // 2, nc, stride=CMP_STRIDE), :], nc, we)
        nxt = pltpu.roll(hh[:, half:], nc - 1, 0)
        a = hh[:, :half] + nxt + _pe_term(pe, w1)
        a = a * _sigmoid(a)
        res = _dot(a.astype(BF16), w2[...])
        out[0] = jnp.where(row < nc - 1, res, 0.0)


def _compress_prompt(kc_halves, vc_halves, wk, wv):
    b, t, _ = kc_halves[0].shape
    nc = t // CMP_STRIDE
    seq = pl.BlockSpec((1, t, 128), lambda i: (i, 0, 0))
    wspecs = [_const_spec((CMP_STRIDE * 128, 4 * CMP_HID)),
              _const_spec((8, 2 * CMP_STRIDE * HEAD_DIM)),
              _const_spec((2 * CMP_STRIDE * HEAD_DIM, CMP_HID)), _const_spec((N_KV * CMP_HID, KV_W))]
    out = pl.BlockSpec((1, nc, KV_W), lambda i: (i, 0, 0))
    return pl.pallas_call(
        functools.partial(_compress_p_body, nc),
        grid=(b,),
        in_specs=[seq] * 4 + wspecs + wspecs,
        out_specs=[out, out],
        out_shape=[jax.ShapeDtypeStruct((b, nc, KV_W), F32)] * 2,
        compiler_params=_params(("parallel",)),
        name="compress_prompt",
    )(*kc_halves, *vc_halves, *wk, *wv)


def _softmax_cols(s, mask):
    s = jnp.where(mask, s, NEG_INF)
    m = jnp.max(s, axis=0, keepdims=True)
    m = jnp.where(m == NEG_INF, 0.0, m)
    e = jnp.where(mask, jnp.exp(s - m), 0.0)
    return e * (1.0 / jnp.maximum(jnp.sum(e, axis=0, keepdims=True), 1e-30))


def _attn_p_body(tq, t, q_ref, gt_ref, ks_ref, vs_ref, kw_ref, vw_ref, kcm_ref, vcm_ref,
                 tbl_ref, bt_ref, bc_ref, msel_ref, o_ref,
                 ksb, kwb, vst, vwt, kcb, vct, btoep, bcmp, qt, qt2, selt, ot, m_ref, l_ref, acc_ref, gt_s):
    b = pl.program_id(0)
    qi = pl.program_id(1)
    tk = ATTN_TK
    nkt = t // tk
    nqt = t // tq
    assert tq == tk
    ncmp = kcm_ref.shape[1]
    q0 = qi * tq
    n_near = bt_ref.shape[0]

    @pl.when((b == 0) & (qi == 0))
    def _():
        def per_head(h, carry):
            far = tbl_ref[NUM_BUCKETS - 1, h]
            for d in range(n_near):
                bk = bt_ref[d]
                acc = jnp.zeros((tk, tq), F32)
                for bb in range(NUM_BUCKETS):
                    acc = jnp.where(bk == bb, (tbl_ref[bb, h] - far) * LOG2E, acc)
                btoep[h, d] = acc
            btoep[h, n_near] = jnp.zeros((tk, tq), F32)
            bk = bc_ref[...]
            acc = jnp.zeros(bk.shape, F32)
            for bb in range(NUM_BUCKETS):
                acc = jnp.where(bk == bb, tbl_ref[bb, h], acc)
            bcmp[h] = acc
            return carry
        lax.fori_loop(0, N_HEADS, per_head, 0)

    @pl.when(qi == 0)
    def _():
        for kt in range(nkt):
            rows = slice(kt * tk, (kt + 1) * tk)
            for src, dstk in ((ks_ref, ksb), (kw_ref, kwb)):
                blk = src[0, rows, :]
                for g in range(N_KV):
                    dstk[g, rows, :] = blk[:, HEAD_DIM * g:HEAD_DIM * (g + 1)].astype(BF16)
            for src, dstv in ((vs_ref, vst), (vw_ref, vwt)):
                for g in range(N_KV):
                    dstv[g, kt] = src[0, g, :, rows].astype(BF16)
        kc = kcm_ref[0]
        vc_t = vcm_ref[0].T
        for g in range(N_KV):
            kcb[g] = kc[:, HEAD_DIM * g:HEAD_DIM * (g + 1)].astype(BF16)
            vct[g] = vc_t[HEAD_DIM * g:HEAD_DIM * (g + 1), :].astype(BF16)

    q_t = q_ref[0].T
    qt[...] = (q_t * SCALE).astype(BF16)
    qt2[...] = (q_t * (SCALE * LOG2E)).astype(BF16)
    gt_s[...] = gt_ref[0].T

    qpos_row = q0 + lax.broadcasted_iota(jnp.int32, (1, tq), 1)
    i_iota = lax.broadcasted_iota(jnp.int32, (tk, tq), 1)
    j_iota = lax.broadcasted_iota(jnp.int32, (tk, tq), 0)
    rel = i_iota - j_iota

    c_iota = lax.broadcasted_iota(jnp.int32, (ncmp, tq), 0)
    cmask = (CMP_STRIDE * c_iota + (2 * CMP_STRIDE - 1)) <= qpos_row
    coff = pl.multiple_of((nqt - 1 - qi) * (tq // CMP_STRIDE), 8)

    nblk = t // SLC_BLOCK
    blk_iota = lax.broadcasted_iota(jnp.int32, (nblk, tq), 0)
    cur = qpos_row // SLC_BLOCK
    valid = blk_iota * SLC_BLOCK <= qpos_row
    forced = (blk_iota == 0) | (blk_iota == cur) | (blk_iota == cur - 1)

    heads = range(N_HEADS)
    head_rows = [slice(HEAD_DIM * h, HEAD_DIM * (h + 1)) for h in heads]
    s_c = [_dot(kcb[h // HPG], qt[head_rows[h], :]) + bcmp[h, pl.ds(coff, ncmp), :] for h in heads]
    p_c = [_softmax_cols(s, cmask) for s in s_c]
    o_c = [_dot(vct[h // HPG], p_c[h].astype(BF16)) for h in heads]
    for h in heads:
        ot[head_rows[h], :] = gt_s[h:h + 1, :] * o_c[h]
    msel = msel_ref[...]
    scores = []
    for g in range(N_KV):
        pg = p_c[g * HPG]
        for hh in range(1, HPG):
            pg = pg + p_c[g * HPG + hh]
        hi, mid, lo = _split3(pg)
        score = _dot(msel, hi) + _dot(msel, mid) + _dot(msel, lo)
        scores.append(jnp.where(forced, jnp.inf, jnp.where(valid, score, NEG_INF)))
    n_slab = nblk // 8
    slabs = [[sc[8 * v:8 * (v + 1), :] for v in range(n_slab)] for sc in scores]
    ranks = [[jnp.zeros((8, tq), F32) for _ in range(n_slab)] for _ in scores]
    row8 = lax.broadcasted_iota(jnp.int32, (8, tq), 0)
    for i in range(nblk):
        for g in range(N_KV):
            si = scores[g][i:i + 1, :]
            for v, sl in enumerate(slabs[g]):
                if 8 * v > i:
                    beats = si >= sl
                elif 8 * v + 7 <= i:
                    beats = si > sl
                else:
                    beats = (si > sl) | ((row8 > i - 8 * v) & (si == sl))
                ranks[g][v] = ranks[g][v] + jnp.where(beats, 1.0, 0.0)
    for g in range(N_KV):
        for v, rk in enumerate(ranks[g]):
            sel = (rk < N_SELECT).astype(F32)
            for r in range(8):
                selt[g, 8 * v + r] = sel[r:r + 1, :]

    def flash_init(br):
        m_ref[br] = jnp.full(m_ref.shape[1:], NEG_INF, F32)
        l_ref[br] = jnp.zeros(l_ref.shape[1:], F32)
        acc_ref[br] = jnp.zeros(acc_ref.shape[1:], F32)

    def tile_step(br, kb, vt, kt, bias_d, add_fn):
        k0 = pl.multiple_of(kt * tk, tk)
        for g in range(N_KV):
            kblk = kb[g, pl.ds(k0, tk), :]
            vblk = vt[g, kt]
            add = add_fn(g)
            for hh in range(HPG):
                h = g * HPG + hh
                rows = slice(HEAD_DIM * h, HEAD_DIM * (h + 1))
                s = _dot(kblk, qt2[rows, :])
                if bias_d is not None:
                    s = s + btoep[h, bias_d]
                if add is not None:
                    s = s + add
                m_old = m_ref[br, h]
                m_new = jnp.maximum(m_old, jnp.max(s, axis=0, keepdims=True))
                m_safe = jnp.where(m_new == NEG_INF, 0.0, m_new)
                alpha = jnp.exp2(m_old - m_safe)
                p = jnp.exp2(s - m_safe)
                l_ref[br, h] = alpha * l_ref[br, h] + jnp.sum(p, axis=0, keepdims=True)
                acc_ref[br, rows, :] = alpha * acc_ref[br, rows, :] + _dot(vblk, p.astype(BF16))
                m_ref[br, h] = m_new

    def flash_finish(br, gate_row0):
        for h in range(N_HEADS):
            rows = slice(HEAD_DIM * h, HEAD_DIM * (h + 1))
            w = gt_s[gate_row0 + h:gate_row0 + h + 1, :] * (1.0 / jnp.maximum(l_ref[br, h], 1e-30))
            ot[rows, :] = ot[rows, :] + acc_ref[br, rows, :] * w

    def sel_rows(g, kt, ok=None):
        nb = tk // SLC_BLOCK
        rows = [jnp.broadcast_to(selt[g, kt * nb + r], (SLC_BLOCK, tq)) for r in range(nb)]
        thr = 0.5 if ok is None else jnp.where(ok, 0.5, 2.0)
        return jnp.concatenate(rows, axis=0) > thr

    def neg_unless(cond):
        return jnp.where(cond, 0.0, NEG_INF)

    causal = rel >= 0
    SLC, WIN = 0, 1

    flash_init(SLC)
    flash_init(WIN)
    n_far = jnp.maximum(qi - 1, 0)

    def slc_pair(pi, carry):
        for sub in range(2):
            kt = 2 * pi + sub
            ok = kt < n_far
            ktc = jnp.minimum(kt, jnp.maximum(n_far - 1, 0))
            tile_step(SLC, ksb, vst, ktc, None, lambda g: neg_unless(sel_rows(g, ktc, ok)))
        return carry

    lax.fori_loop(0, (n_far + 1) // 2, slc_pair, 0)
    kt1 = jnp.maximum(qi - 1, 0)
    tile_step(SLC, ksb, vst, kt1, 1, lambda g: neg_unless(sel_rows(g, kt1, qi >= 1)))
    tile_step(SLC, ksb, vst, qi, 0, lambda g: neg_unless(sel_rows(g, qi) & causal))

    wt = WINDOW // tk
    for u in range(wt + 1):
        off = wt - u
        kt = qi - off
        ok_add = jnp.where(kt >= 0, 0.0, NEG_INF)
        ktc = jnp.maximum(kt, 0)
        if off == wt:
            add_fn = lambda g, a=ok_add: neg_unless(rel <= 0) + a
        elif off == 0:
            add_fn = lambda g: neg_unless(causal)
        else:
            add_fn = lambda g, a=ok_add: a
        tile_step(WIN, kwb, vwt, ktc, off if off <= 1 else None, add_fn)
    flash_finish(SLC, N_HEADS)
    flash_finish(WIN, 2 * N_HEADS)

    o_ref[0] = ot[...].T


def _attn_prompt(q, gates, ks, vs, kw, vw, kcmp, vcmp, table, bkt_toep, bkt_cmp, msel, tq):
    b, t, _ = q.shape
    tk = ATTN_TK
    nkt = t // tk
    ncmp = kcmp.shape[1]
    n_near = bkt_toep.shape[0]
    qspec = pl.BlockSpec((1, tq, D_MODEL), lambda bi, i: (bi, i, 0))
    seq = pl.BlockSpec((1, t, KV_W), lambda bi, i: (bi, 0, 0))
    seq_fm = pl.BlockSpec((1, N_KV, HEAD_DIM, t), lambda bi, i: (bi, 0, 0, 0))
    cmp_spec = pl.BlockSpec((1, ncmp, KV_W), lambda bi, i: (bi, 0, 0))
    return pl.pallas_call(
        functools.partial(_attn_p_body, tq, t),
        grid=(b, t // tq),
        in_specs=[qspec, pl.BlockSpec((1, tq, 128), lambda bi, i: (bi, i, 0)),
                  seq, seq_fm, seq, seq_fm, cmp_spec, cmp_spec,
                  pl.BlockSpec(memory_space=pltpu.SMEM),
                  _const_spec(bkt_toep.shape), _const_spec(bkt_cmp.shape), _const_spec(msel.shape)],
        out_specs=qspec,
        out_shape=jax.ShapeDtypeStruct((b, t, D_MODEL), F32),
        scratch_shapes=[pltpu.VMEM((N_KV, t, HEAD_DIM), BF16), pltpu.VMEM((N_KV, t, HEAD_DIM), BF16),
                        pltpu.VMEM((N_KV, nkt, HEAD_DIM, tk), BF16), pltpu.VMEM((N_KV, nkt, HEAD_DIM, tk), BF16),
                        pltpu.VMEM((N_KV, ncmp, HEAD_DIM), BF16), pltpu.VMEM((N_KV, HEAD_DIM, ncmp), BF16),
                        pltpu.VMEM((N_HEADS, n_near + 1, tk, tq), F32),
                        pltpu.VMEM((N_HEADS,) + bkt_cmp.shape, F32),
                        pltpu.VMEM((D_MODEL, tq), BF16), pltpu.VMEM((D_MODEL, tq), BF16),
                        pltpu.VMEM((N_KV, t // SLC_BLOCK, 1, tq), F32),
                        pltpu.VMEM((D_MODEL, tq), F32),
                        pltpu.VMEM((2, N_HEADS, 1, tq), F32), pltpu.VMEM((2, N_HEADS, 1, tq), F32),
                        pltpu.VMEM((2, D_MODEL, tq), F32), pltpu.VMEM((128, tq), F32)],
        compiler_params=_params(("arbitrary", "arbitrary")),
        name="attn_prompt",
    )(q, gates, ks, vs, kw, vw, kcmp, vcmp, table, bkt_toep, bkt_cmp, msel)


PAGE = 128
PAGES_PER_GROUP = 32
Q_ROWS = 8


def _rank_desc(score_row, n):
    a = jnp.broadcast_to(score_row, (n, n))
    at = a.T
    i = lax.broadcasted_iota(jnp.int32, (n, n), 0)
    j = lax.broadcasted_iota(jnp.int32, (n, n), 1)
    beats = (at > a) | ((i < j) & (at == a))
    return jnp.sum(beats.astype(F32), axis=0, keepdims=True)


def _bias_rows(bucket, tcols):
    r = tcols.shape[0]
    out = jnp.zeros((r, bucket.shape[1]), F32)
    for bb in range(NUM_BUCKETS):
        out = jnp.where(bucket == bb, tcols[:, bb:bb + 1], out)
    return out


def _softmax_rows(s, mask):
    s = jnp.where(mask, s, NEG_INF)
    m = jnp.max(s, axis=1, keepdims=True)
    m = jnp.where(m == NEG_INF, 0.0, m)
    e = jnp.where(mask, jnp.exp(s - m), 0.0)
    return e / jnp.maximum(jnp.sum(e, axis=1, keepdims=True), 1e-30)


def _dot_nt(a, b):
    return lax.dot_general(a, b, (((1,), (1,)), ((), ())), preferred_element_type=F32)


def _cmp_s_body(past, pt_ref, ck_hbm, cv_hbm, kcn_ref, vcn_ref, q_ref,
                wek_ref, pek_ref, w1k_ref, w2k_ref, wev_ref, pev_ref, w1v_ref, w2v_ref,
                tbl_ref, bkt_ref, msel_ref, ocmp_ref, idx_ref,
                kbuf, vbuf, tbuf_k, tbuf_v, hk, hv, sem):
    b = pl.program_id(0)
    rows_pg = PAGES_PER_GROUP * PAGE
    cpg = rows_pg // CMP_STRIDE
    ngrp = past // rows_pg
    nc = past // CMP_STRIDE
    half = N_KV * CMP_HID

    def copies(bb, gi, slot):
        out = []
        for p in range(PAGES_PER_GROUP):
            page = pt_ref[bb, gi * PAGES_PER_GROUP + p]
            out.append(pltpu.make_async_copy(ck_hbm.at[page], kbuf.at[slot, p], sem.at[0, slot]))
            out.append(pltpu.make_async_copy(cv_hbm.at[page], vbuf.at[slot, p], sem.at[1, slot]))
        return out

    assert ngrp % 2 == 0

    @pl.when(b == 0)
    def _():
        for c in copies(b, 0, 0):
            c.start()

    def group(gi, carry):
        slot = gi % 2

        @pl.when(gi + 1 < ngrp)
        def _():
            for c in copies(b, gi + 1, 1 - slot):
                c.start()

        for c in copies(b, gi, slot):
            c.wait()
        transpose(kbuf, tbuf_k, slot)
        matmul(tbuf_v, wev_ref, hv, jnp.maximum(gi - 1, 0))
        matmul(tbuf_k, wek_ref, hk, gi)
        transpose(vbuf, tbuf_v, slot)
        return carry

    def transpose(buf, tb, slot):
        for p in range(PAGES_PER_GROUP):
            for hf in range(2):
                tb[hf, p * PAGE:(p + 1) * PAGE, :] = buf[slot, p, hf * 128:(hf + 1) * 128, :].T

    def matmul(tb, we, hdst, gi):
        r0 = pl.multiple_of(gi * cpg, cpg)
        hdst[pl.ds(r0, cpg), :] = _compress_hidden(
            lambda k: tb[k % 2, pl.ds(k // 2, cpg, stride=CMP_STRIDE), :], cpg, we)

    @pl.when(b == 0)
    def _():
        tbuf_v[...] = jnp.zeros_like(tbuf_v)

    lax.fori_loop(0, ngrp, group, 0)
    matmul(tbuf_v, wev_ref, hv, ngrp - 1)

    @pl.when(b + 1 < pl.num_programs(0))
    def _():
        for c in copies(b + 1, 0, 0):
            c.start()

    col = lax.broadcasted_iota(jnp.int32, (1, nc), 1)
    cmask = (CMP_STRIDE * col + (2 * CMP_STRIDE - 1)) <= past
    row8 = lax.broadcasted_iota(jnp.int32, (Q_ROWS, nc), 0)
    cmpd = []
    for new_ref, we, pe, w1, w2, hdst in ((kcn_ref, wek_ref, pek_ref, w1k_ref, w2k_ref, hk),
                                          (vcn_ref, wev_ref, pev_ref, w1v_ref, w2v_ref, hv)):
        new = jnp.broadcast_to(new_ref[0], (8, KV_W)).astype(BF16)
        new_lo = _dot(new[:, :128], we[0:128, :])
        new_hi = _dot(new[:, 128:], we[0:128, :])
        w = 2 * CMP_HID
        hdst[nc:nc + 8, :] = jnp.concatenate([new_lo[:, :w], new_hi[:, :w], new_lo[:, w:], new_hi[:, w:]], axis=1)
        a = hdst[0:nc, 0:half] + hdst[1:nc + 1, half:2 * half] + _pe_term(pe, w1)
        a = (a * _sigmoid(a)).astype(BF16)
        cmpd.append([_dot(a[:, CMP_HID * g:CMP_HID * (g + 1)], w2[...]).astype(BF16) for g in range(N_KV)])

    nsel = msel_ref.shape[1]
    lane = lax.broadcasted_iota(jnp.int32, (1, nsel), 1)
    n_blocks = past // SLC_BLOCK + 1
    cur = past // SLC_BLOCK
    forced = (lane == 0) | (lane == cur) | (lane == cur - 1)
    valid = (lane * SLC_BLOCK <= past) & (lane < n_blocks)
    k_iota = lax.broadcasted_iota(jnp.int32, (N_SELECT, nsel), 0).astype(F32)
    lane_f = lax.broadcasted_iota(jnp.int32, (N_SELECT, nsel), 1).astype(F32)
    groups = range(N_KV)
    msel = msel_ref[...]
    bkt = bkt_ref[...]
    s_g = [_dot_nt(q_ref[0, g].astype(BF16), cmpd[0][g]) * SCALE + _bias_rows(bkt, tbl_ref[g]) for g in groups]
    p_g = [_softmax_rows(s, cmask) for s in s_g]
    for g in groups:
        ocmp_ref[0, g] = _dot(p_g[g].astype(BF16), cmpd[1][g])
    pgs = [jnp.sum(jnp.where(row8 < HPG, p, 0.0), axis=0, keepdims=True) for p in p_g]
    parts = [_split3(jnp.broadcast_to(pg, (8, nc))) for pg in pgs]
    scores = [(_dot(hi, msel) + _dot(mid, msel) + _dot(lo, msel))[0:1, :] for hi, mid, lo in parts]
    scores = [jnp.where(forced, jnp.inf, jnp.where(valid, sc, NEG_INF)) for sc in scores]
    rank_g = [_rank_desc(sc, nsel) for sc in scores]
    hits = [jnp.where(jnp.broadcast_to(rk, (N_SELECT, nsel)) == k_iota, lane_f, 0.0) for rk in rank_g]
    for g in groups:
        idx_ref[0, g] = jnp.sum(hits[g], axis=1, keepdims=True).astype(jnp.int32)


def _cmp_sample(page_table, cache_k, cache_v, kc_new, vc_new, q8, wk, wv, tbl8, bkt, msel, past):
    nb = page_table.shape[0]
    rows_pg = PAGES_PER_GROUP * PAGE
    nc = past // CMP_STRIDE
    any_spec = pl.BlockSpec(memory_space=pl.ANY)
    new_spec = pl.BlockSpec((1, 1, KV_W), lambda i, pt: (i, 0, 0))
    cs = lambda shape: pl.BlockSpec(shape, lambda i, pt: (0,) * len(shape))
    wspecs = [cs((CMP_STRIDE * 128, 4 * CMP_HID)),
              cs((8, 2 * CMP_STRIDE * HEAD_DIM)),
              cs((2 * CMP_STRIDE * HEAD_DIM, CMP_HID)), cs((CMP_HID, HEAD_DIM))]
    grid_spec = pltpu.PrefetchScalarGridSpec(
        num_scalar_prefetch=1,
        grid=(nb,),
        in_specs=[any_spec, any_spec, new_spec, new_spec,
                  pl.BlockSpec((1, N_KV, Q_ROWS, HEAD_DIM), lambda i, pt: (i, 0, 0, 0))]
                 + wspecs + wspecs + [cs(tbl8.shape), cs(bkt.shape), cs(msel.shape)],
        out_specs=[pl.BlockSpec((1, N_KV, Q_ROWS, HEAD_DIM), lambda i, pt: (i, 0, 0, 0)),
                   pl.BlockSpec((1, N_KV, N_SELECT, 1), lambda i, pt: (i, 0, 0, 0))],
        scratch_shapes=[pltpu.VMEM((2, PAGES_PER_GROUP, 2 * PAGE, 128), F32),
                        pltpu.VMEM((2, PAGES_PER_GROUP, 2 * PAGE, 128), F32),
                        pltpu.VMEM((2, rows_pg, 128), F32), pltpu.VMEM((2, rows_pg, 128), F32),
                        pltpu.VMEM((nc + 8, 2 * N_KV * CMP_HID), F32),
                        pltpu.VMEM((nc + 8, 2 * N_KV * CMP_HID), F32),
                        pltpu.SemaphoreType.DMA((2, 2))])
    return pl.pallas_call(
        functools.partial(_cmp_s_body, past),
        grid_spec=grid_spec,
        out_shape=[jax.ShapeDtypeStruct((nb, N_KV, Q_ROWS, HEAD_DIM), F32),
                   jax.ShapeDtypeStruct((nb, N_KV, N_SELECT, 1), jnp.int32)],
        compiler_params=_params(("arbitrary",)),
        name="cmp_sample",
    )(page_table, cache_k, cache_v, kc_new, vc_new, q8, *wk, *wv, tbl8, bkt, msel)


NEW_LANES = 128


def _dec_s_body(past, idx_ref, pt_ref, csk_hbm, csv_hbm,
                q_ref, ksn_ref, vsn_ref, kwn_ref, vwn_ref, wk_ref, wv_ref, gt_ref, ocmp_ref,
                tbl_ref, bktw_ref, o_ref, wko_ref, wvo_ref,
                kbuf, vbuf, kwbuf, vwbuf, sem):
    b = pl.program_id(0)
    n_pages = past // PAGE
    cur = past // SLC_BLOCK
    bpp = PAGE // SLC_BLOCK
    lsel = N_SELECT * PAGE
    wb = wk_ref.shape[3]

    @pl.when(b == 0)
    def _():
        kbuf[...] = jnp.zeros_like(kbuf)
        vbuf[...] = jnp.zeros_like(vbuf)
        kwbuf[...] = jnp.zeros_like(kwbuf)
        vwbuf[...] = jnp.zeros_like(vwbuf)

    def gathers(bb):
        half = bb % 2
        out = []
        for g in range(N_KV):
            for k in range(N_SELECT):
                blk = idx_ref[bb, g * N_SELECT + k]
                page = pt_ref[bb, jnp.minimum(blk // bpp, n_pages - 1)]
                dst = pl.ds(k * PAGE, PAGE)
                out.append(pltpu.make_async_copy(csk_hbm.at[page, g], kbuf.at[half, g, :, dst], sem.at[0, half]))
                out.append(pltpu.make_async_copy(csv_hbm.at[page, g], vbuf.at[half, g, :, dst], sem.at[1, half]))
        return out

    @pl.when(b == 0)
    def _():
        for c in gathers(b):
            c.start()

    @pl.when(b + 1 < pl.num_programs(0))
    def _():
        for c in gathers(b + 1):
            c.start()

    cur_half = b % 2

    lane_wb = lax.broadcasted_iota(jnp.int32, (HEAD_DIM, wb), 1)
    eye = (lax.broadcasted_iota(jnp.int32, (HEAD_DIM, HEAD_DIM), 0)
           == lax.broadcasted_iota(jnp.int32, (HEAD_DIM, HEAD_DIM), 1))

    def group_col(new_ref, g):
        r = new_ref[0][:, HEAD_DIM * g:HEAD_DIM * (g + 1)]
        return jnp.sum(jnp.where(eye, jnp.broadcast_to(r, (HEAD_DIM, HEAD_DIM)), 0.0), axis=1, keepdims=True)

    for g in range(N_KV):
        for src, new_ref, dst, buf in ((wk_ref, kwn_ref, wko_ref, kwbuf), (wv_ref, vwn_ref, wvo_ref, vwbuf)):
            st = src[0, g]
            newc = group_col(new_ref, g)
            dst[0, g] = jnp.where(lane_wb == wb - 1, newc, pltpu.roll(st, wb - 1, 1))
            buf[g, :, 0:wb] = st
            buf[g, :, wb:wb + 1] = newc

    for c in gathers(b):
        c.wait()

    ls = lsel + NEW_LANES
    lw = wb + NEW_LANES
    lane_s = lax.broadcasted_iota(jnp.int32, (1, ls), 1)
    lane_w = lax.broadcasted_iota(jnp.int32, (1, lw), 1)
    wmask = lane_w <= wb
    tok = lane_s % PAGE
    groups = range(N_KV)
    smasks, sbiases = [], []
    for g in groups:
        kbuf[cur_half, g, :, lsel:lsel + 1] = group_col(ksn_ref, g)
        vbuf[cur_half, g, :, lsel:lsel + 1] = group_col(vsn_ref, g)
        kpos = past + (lane_s - lsel)
        gathered = lane_s < 0
        has_new = jnp.int32(0)
        for k in range(N_SELECT):
            blk = idx_ref[b, g * N_SELECT + k]
            in_slot = (lane_s // PAGE) == k
            kpos = jnp.where(in_slot, (blk // bpp) * PAGE + tok, kpos)
            gathered = gathered | (in_slot & (blk < cur) & ((tok // SLC_BLOCK) == (blk % bpp)))
            has_new = has_new | (blk == cur).astype(jnp.int32)
        smasks.append((gathered | ((lane_s == lsel) & (has_new > 0))) & (kpos <= past))
        sbiases.append(_bias_rows(_rel_bucket(past - kpos), tbl_ref[g]))
    bktw = bktw_ref[...]
    qgs = [q_ref[0, g].astype(BF16) for g in groups]
    s_slc = [_dot(qgs[g], kbuf[cur_half, g].astype(BF16)) * SCALE + sbiases[g] for g in groups]
    s_win = [_dot(qgs[g], kwbuf[g].astype(BF16)) * SCALE + _bias_rows(bktw, tbl_ref[g]) for g in groups]
    p_slc = [_softmax_rows(s_slc[g], smasks[g]) for g in groups]
    p_win = [_softmax_rows(s, wmask) for s in s_win]
    o_slc = [_dot_nt(p_slc[g].astype(BF16), vbuf[cur_half, g].astype(BF16)) for g in groups]
    o_win = [_dot_nt(p_win[g].astype(BF16), vwbuf[g].astype(BF16)) for g in groups]
    for g in groups:
        gt = gt_ref[0, g]
        o_ref[0, g] = gt[:, 0:1] * ocmp_ref[0, g] + gt[:, 1:2] * o_slc[g] + gt[:, 2:3] * o_win[g]


def _dec_sample(idx, page_table, cache_sk, cache_sv, q8, ks_new, vs_new, kw_new, vw_new, state_wk, state_wv,
                gates8, ocmp, tbl8, bktw, past):
    nb, wb = state_wk.shape[0], state_wk.shape[3]
    any_spec = pl.BlockSpec(memory_space=pl.ANY)
    new_spec = pl.BlockSpec((1, 1, KV_W), lambda i, *_: (i, 0, 0))
    st_spec = pl.BlockSpec((1, N_KV, HEAD_DIM, wb), lambda i, *_: (i, 0, 0, 0))
    head_spec = pl.BlockSpec((1, N_KV, Q_ROWS, HEAD_DIM), lambda i, *_: (i, 0, 0, 0))
    cs = lambda shape: pl.BlockSpec(shape, lambda i, *_: (0,) * len(shape))
    lsel = N_SELECT * PAGE
    grid_spec = pltpu.PrefetchScalarGridSpec(
        num_scalar_prefetch=2,
        grid=(nb,),
        in_specs=[any_spec, any_spec, head_spec,
                  new_spec, new_spec, new_spec, new_spec, st_spec, st_spec,
                  pl.BlockSpec((1, N_KV, Q_ROWS, 3), lambda i, *_: (i, 0, 0, 0)), head_spec,
                  cs(tbl8.shape), cs(bktw.shape)],
        out_specs=[head_spec, st_spec, st_spec],
        scratch_shapes=[pltpu.VMEM((2, N_KV, HEAD_DIM, lsel + NEW_LANES), F32),
                        pltpu.VMEM((2, N_KV, HEAD_DIM, lsel + NEW_LANES), F32),
                        pltpu.VMEM((N_KV, HEAD_DIM, wb + NEW_LANES), F32),
                        pltpu.VMEM((N_KV, HEAD_DIM, wb + NEW_LANES), F32),
                        pltpu.SemaphoreType.DMA((2, 2))])
    return pl.pallas_call(
        functools.partial(_dec_s_body, past),
        grid_spec=grid_spec,
        out_shape=[jax.ShapeDtypeStruct((nb, N_KV, Q_ROWS, HEAD_DIM), F32),
                   jax.ShapeDtypeStruct(state_wk.shape, F32),
                   jax.ShapeDtypeStruct(state_wv.shape, F32)],
        compiler_params=_params(("arbitrary",)),
        name="dec_sample",
    )(idx, page_table, cache_sk, cache_sv, q8, ks_new, vs_new, kw_new, vw_new,
      state_wk, state_wv, gates8, ocmp, tbl8, bktw)


def _expand_w1(w1):
    eye = jnp.eye(2, dtype=w1.dtype)
    parts = []
    for part in (w1[:CMP_STRIDE * HEAD_DIM], w1[CMP_STRIDE * HEAD_DIM:]):
        w = part.reshape(CMP_STRIDE, HEAD_DIM, CMP_HID)
        parts.append(jnp.einsum("gh,rdn->rgdhn", eye, w).reshape(CMP_STRIDE, 2 * HEAD_DIM, 2 * CMP_HID))
    w = jnp.concatenate(parts, axis=-1).astype(BF16)
    return w.reshape(CMP_STRIDE * 2 * HEAD_DIM, 4 * CMP_HID)


def _expand_w2(w2):
    eye = jnp.eye(N_KV, dtype=w2.dtype)
    return jnp.einsum("gh,nd->gnhd", eye, w2).reshape(N_KV * CMP_HID, KV_W).astype(BF16)


def _pe_rows(pe):
    return jnp.broadcast_to(pe.reshape(1, -1), (8, pe.size)).astype(BF16)


def _select_matrix(n_cmp, n_blocks):
    r = SLC_BLOCK // CMP_STRIDE
    m = np.zeros((n_cmp, n_blocks), np.float32)
    for s in range(n_blocks):
        for a in range(r):
            for bb in range(2):
                c = r * s + a - bb
                if 0 <= c < n_cmp:
                    m[c, s] += 1.0
    return m


def _pad_to(a, size, axis):
    pad = [(0, 0)] * a.ndim
    pad[axis] = (0, size - a.shape[axis])
    return jnp.pad(a, pad)


def kernel(x_prompt, x_sample, cache_cmp_k, cache_cmp_v, cache_slc_k, cache_slc_v, state_win_k, state_win_v, state_conv, page_table, p_prompt, p_sample, rel_table, norm_mix, norm_ffn, norm_ple, norm_final, conv_w1, conv_b1, conv_dw, conv_dwb, conv_ln_g, conv_ln_b, conv_w2, conv_b2, attn_w_in, attn_w_out, cmpk_w1, cmpk_pe, cmpk_w2, cmpv_w1, cmpv_pe, cmpv_w2, mlp_up, mlp_down, ple_proj, ple_gate):
    nbp, t, _ = x_prompt.shape
    nbs = x_sample.shape[0]
    n_p = nbp * t
    n_pool = cache_cmp_k.shape[1]
    past = page_table.shape[1] * cache_cmp_k.shape[2]
    wb = state_win_k.shape[2]
    hist = CONV_W - 1
    tm_p, tf, tt, tq = 512, 1024, 256, ATTN_TK

    row = lambda a: a.reshape(1, -1)
    bf = lambda a: a.astype(BF16)

    w1c, w2c = bf(conv_w1[0]), bf(conv_w2[0])
    dw3 = conv_dw[0].reshape(CONV_W, D_MODEL // 128, 128).transpose(1, 0, 2)
    dwb3 = conv_dwb[0].reshape(D_MODEL // 128, 1, 128)
    up, dn = bf(mlp_up), bf(mlp_down)
    wg, wp = bf(ple_gate), bf(ple_proj)
    w_in = attn_w_in[0]
    wq = bf(w_in[:, :D_MODEL])
    wkv = bf(w_in[:, D_MODEL:D_MODEL + 6 * KV_W])
    wgt = bf(_pad_to(w_in[:, D_MODEL + 6 * KV_W:], 128, 1))
    w_out = bf(attn_w_out[0])
    cw = []
    for w1, pe, w2 in ((cmpk_w1[0], cmpk_pe[0], cmpk_w2[0]), (cmpv_w1[0], cmpv_pe[0], cmpv_w2[0])):
        cw.append((_expand_w1(w1), _pe_rows(pe), bf(w1), _expand_w2(w2), bf(w2)))
    wk_p, wv_p = [c[:4] for c in cw]
    wk_s, wv_s = [c[:3] + c[4:] for c in cw]

    def tail(x, i, p, tm, final, pre=None):
        return _mlp_ple(x, row(norm_ffn[i]), up, dn, row(norm_ple[i]), wg, p, i, wp,
                        row(norm_final), tm, tf, final, pre)

    conv_args = (row(conv_ln_g[0]), row(conv_ln_b[0]), w2c, row(conv_b2[0]))

    xp = x_prompt.reshape(n_p, D_MODEL)
    glu = _conv_in(xp, row(norm_mix[0]), w1c, row(conv_b1[0]), tm_p).reshape(nbp, t, D_MODEL)
    x1 = _conv_out(glu, x_prompt, dw3, dwb3, *conv_args, tt).reshape(n_p, D_MODEL)
    conv_p = glu[:, t - hist:][None]
    pp = p_prompt.reshape(-1, n_p, D_PLE)
    ps = p_sample.reshape(-1, nbs, D_PLE)
    x2 = tail(x1, 0, pp, tm_p, False)
    q, gates, kcl, kch, vcl, vch, ks, kw, *rest = _attn_in(x2, row(norm_mix[1]), wq, wkv, wgt, tm_p, seq_len=t)
    kv_fm, win_fm = rest[:len(KV_NAMES)], rest[len(KV_NAMES):]
    vs_fm, vw_fm = kv_fm[KV_NAMES.index("vs")], kv_fm[KV_NAMES.index("vw")]
    seq = lambda a: a.reshape(nbp, t, -1)
    kcmp, vcmp = _compress_prompt((seq(kcl), seq(kch)), (seq(vcl), seq(vch)), wk_p, wv_p)

    ncmp = t // CMP_STRIDE
    ii = jnp.arange(tq, dtype=jnp.int32)[None, :]
    jj = jnp.arange(ATTN_TK, dtype=jnp.int32)[:, None]
    r_qk = tq // ATTN_TK
    bkt_toep = jnp.stack([_rel_bucket(d * ATTN_TK + ii - jj) for d in range(1 - r_qk, 2)])
    nshift = (t // tq - 1) * (tq // CMP_STRIDE)
    rr = jnp.arange(ncmp + nshift, dtype=jnp.int32)[:, None]
    bkt_cmp = _rel_bucket(ii - CMP_STRIDE * (rr - nshift) - (2 * CMP_STRIDE - 1))
    msel_p = jnp.asarray(_select_matrix(ncmp, t // SLC_BLOCK).T, BF16)
    o_p = _attn_prompt(seq(q), seq(gates), seq(ks), vs_fm, seq(kw), vw_fm, kcmp, vcmp,
                       rel_table, bkt_toep, bkt_cmp, msel_p, tq)
    y_p = tail(x2, 1, pp, tm_p, True, pre=(o_p.reshape(n_p, D_MODEL), w_out))

    from_fm = lambda a: jnp.transpose(a, (0, 3, 1, 2))[None]
    kv_p = [from_fm(a) for a in kv_fm[:4]] + [from_fm(a) for a in win_fm]

    xs = x_sample.reshape(nbs, D_MODEL)
    glu_s = _conv_in(xs, row(norm_mix[0]), w1c, row(conv_b1[0]), nbs)
    x1s, conv_s = _conv_step(jnp.transpose(state_conv[0], (1, 0, 2)), glu_s, xs, conv_dw[0],
                             row(conv_dwb[0]), *conv_args)
    conv_s = jnp.transpose(conv_s, (1, 0, 2))
    x2s = tail(x1s, 0, ps, nbs, False)
    qs, gts, kcs, vcs, kss, vss, kws, vws = _attn_in(x2s, row(norm_mix[1]), wq, wkv, wgt, nbs)

    q8 = _pad_to(qs.reshape(nbs, N_KV, HPG, HEAD_DIM), Q_ROWS, 2)
    tbl8 = _pad_to(rel_table.T.reshape(N_KV, HPG, NUM_BUCKETS), Q_ROWS, 1)
    nc_s = past // CMP_STRIDE
    cc = jnp.arange(nc_s, dtype=jnp.int32)[None, :]
    bkt_s = _rel_bucket(past - (CMP_STRIDE * cc + 2 * CMP_STRIDE - 1))
    n_blocks_s = past // SLC_BLOCK + 1
    msel_s = jnp.asarray(_select_matrix(nc_s, -(-n_blocks_s // 128) * 128), BF16)
    new3 = lambda a: a.reshape(nbs, 1, KV_W)
    fmajor = lambda c: jnp.transpose(c[0], (0, 2, 3, 1))
    cache2 = lambda c: fmajor(c).reshape(n_pool, 2 * PAGE, PAGE)
    ocmp, idx = _cmp_sample(page_table, cache2(cache_cmp_k), cache2(cache_cmp_v), new3(kcs), new3(vcs),
                            q8, wk_s, wv_s, tbl8, bkt_s, msel_s, past)

    ww = jnp.arange(wb + NEW_LANES, dtype=jnp.int32)[None, :]
    bkt_w = _rel_bucket(wb - ww)
    gates8 = _pad_to(gts[:, :3 * N_HEADS].reshape(nbs, 3, N_KV, HPG).transpose(0, 2, 3, 1), Q_ROWS, 2)
    o8, wk_new, wv_new = _dec_sample(idx.reshape(nbs, N_KV * N_SELECT), page_table,
                                     fmajor(cache_slc_k), fmajor(cache_slc_v), q8,
                                     new3(kss), new3(vss), new3(kws), new3(vws),
                                     fmajor(state_win_k), fmajor(state_win_v), gates8, ocmp, tbl8, bkt_w, past)
    o_s = o8[:, :, :HPG].reshape(nbs, D_MODEL)
    y_s = tail(x2s, 1, ps, nbs, True, pre=(o_s, w_out))

    kv5s = lambda a: a.reshape(1, nbs, 1, N_KV, HEAD_DIM)
    return (y_p.reshape(nbp, t, D_MODEL), y_s.reshape(nbs, 1, D_MODEL), *kv_p, conv_p,
            kv5s(kcs), kv5s(vcs), kv5s(kss), kv5s(vss), from_fm(wk_new), from_fm(wv_new), conv_s[None])
```

```python
import functools
import math

import numpy as np
import jax
import jax.numpy as jnp
from jax import lax
from jax.experimental import pallas as pl
from jax.experimental.pallas import tpu as pltpu

F32 = jnp.float32
BF16 = jnp.bfloat16

D_MODEL = 1024
D_PLE = 256
CONV_W = 31
N_HEADS = 16
HEAD_DIM = 64
N_KV = 4
HPG = 4
KV_W = N_KV * HEAD_DIM
CMP_STRIDE = 16
CMP_HID = 128
SLC_BLOCK = 64
N_SELECT = 16
WINDOW = 512
NUM_BUCKETS = 32
MAX_DISTANCE = 128
D_FF = 4096
EPS = 1e-6
SCALE = HEAD_DIM ** -0.5
assert math.frexp(SCALE)[0] == 0.5
NEG_INF = float("-inf")

ATTN_TK = 128
assert ATTN_TK >= MAX_DISTANCE
LOG2E = math.log2(math.e)

VMEM_LIMIT = 56 * 1024 * 1024
HALO = 32


def _sigmoid(x):
    return 1.0 / (1.0 + jnp.exp(-x))


def _rms(x, g):
    return x * lax.rsqrt(jnp.mean(x * x, axis=-1, keepdims=True) + EPS) * g


def _dot(a, b):
    return jnp.dot(a, b, preferred_element_type=F32)


def _split3(x):
    hi = x.astype(BF16)
    r1 = x - hi.astype(F32)
    mid = r1.astype(BF16)
    lo = (r1 - mid.astype(F32)).astype(BF16)
    return hi, mid, lo


def _rel_bucket(dist):
    n = jnp.maximum(dist, 0)
    max_exact = NUM_BUCKETS // 2
    nf = jnp.maximum(n, 1).astype(F32)
    large = max_exact + (jnp.log(nf / max_exact) / math.log(MAX_DISTANCE / max_exact)
                         * (NUM_BUCKETS - max_exact)).astype(jnp.int32)
    large = jnp.minimum(large, NUM_BUCKETS - 1)
    return jnp.where(n < max_exact, n, large)


def _params(sem):
    return pltpu.CompilerParams(dimension_semantics=sem, vmem_limit_bytes=VMEM_LIMIT)


def _const_spec(shape, single=False):
    n = len(shape)
    if single:
        return pl.BlockSpec(shape, lambda *_: (0,) * n, pipeline_mode=pl.Buffered(1))
    return pl.BlockSpec(shape, lambda *_: (0,) * n)


def _conv_in_body(x_ref, g_ref, w_ref, b_ref, o_ref):
    h = _rms(x_ref[...], g_ref[...]).astype(BF16)
    u = _dot(h, w_ref[...]) + b_ref[...]
    o_ref[...] = u[:, :D_MODEL] * _sigmoid(u[:, D_MODEL:])


def _conv_in(x, g, w1, b1, tm):
    n = x.shape[0]
    return pl.pallas_call(
        _conv_in_body,
        grid=(n // tm,),
        in_specs=[pl.BlockSpec((tm, D_MODEL), lambda i: (i, 0)),
                  _const_spec((1, D_MODEL)),
                  _const_spec((D_MODEL, 2 * D_MODEL)),
                  _const_spec((1, 2 * D_MODEL))],
        out_specs=pl.BlockSpec((tm, D_MODEL), lambda i: (i, 0)),
        out_shape=jax.ShapeDtypeStruct((n, D_MODEL), F32),
        compiler_params=_params(("parallel",)),
        name="conv_in",
    )(x, g, w1, b1)


def _ln_silu_proj(y, lng, lnb, w2, b2, x):
    mu = jnp.mean(y, axis=-1, keepdims=True)
    yc = y - mu
    var = jnp.mean(yc * yc, axis=-1, keepdims=True)
    z = yc * lax.rsqrt(var + EPS) * lng + lnb
    z = z * _sigmoid(z)
    return _dot(z.astype(BF16), w2) + b2 + x


def _conv_out_body(tt, cur_ref, halo_ref, x_ref, dw_ref, dwb_ref, lng_ref, lnb_ref, w2_ref, b2_ref,
                   o_ref, ctx_ref, y_ref):
    i = pl.program_id(1)
    n_strip = D_MODEL // 128
    keep = (i > 0).astype(F32)
    for c in range(n_strip):
        ctx_ref[c, 0:HALO, :] = halo_ref[0, :, 128 * c:128 * (c + 1)] * keep
        ctx_ref[c, HALO:, :] = cur_ref[0, :, 128 * c:128 * (c + 1)]

    def strip(c, carry):
        acc = jnp.broadcast_to(dwb_ref[c], (tt, 128))
        for k in range(CONV_W):
            off = k + HALO - (CONV_W - 1)
            acc = acc + ctx_ref[c, off:off + tt, :] * dw_ref[c, k:k + 1, :]
        y_ref[c] = acc
        return carry

    lax.fori_loop(0, n_strip, strip, 0)
    y = jnp.concatenate([y_ref[c] for c in range(n_strip)], axis=1)
    o_ref[0] = _ln_silu_proj(y, lng_ref[...], lnb_ref[...], w2_ref[...], b2_ref[...], x_ref[0])


def _conv_out(glu, x, dw3, dwb3, lng, lnb, w2, b2, tt):
    b, t, _ = glu.shape
    hb = tt // HALO
    return pl.pallas_call(
        functools.partial(_conv_out_body, tt),
        grid=(b, t // tt),
        in_specs=[pl.BlockSpec((1, tt, D_MODEL), lambda bi, i: (bi, i, 0)),
                  pl.BlockSpec((1, HALO, D_MODEL), lambda bi, i: (bi, jnp.maximum(i * hb - 1, 0), 0)),
                  pl.BlockSpec((1, tt, D_MODEL), lambda bi, i: (bi, i, 0)),
                  _const_spec((D_MODEL // 128, CONV_W, 128)),
                  _const_spec((D_MODEL // 128, 1, 128)),
                  _const_spec((1, D_MODEL)),
                  _const_spec((1, D_MODEL)),
                  _const_spec((D_MODEL, D_MODEL)),
                  _const_spec((1, D_MODEL))],
        out_specs=pl.BlockSpec((1, tt, D_MODEL), lambda bi, i: (bi, i, 0)),
        out_shape=jax.ShapeDtypeStruct((b, t, D_MODEL), F32),
        scratch_shapes=[pltpu.VMEM((D_MODEL // 128, tt + HALO, 128), F32),
                        pltpu.VMEM((D_MODEL // 128, tt, 128), F32)],
        compiler_params=_params(("parallel", "arbitrary")),
        name="conv_out",
    )(glu, glu, x, dw3, dwb3, lng, lnb, w2, b2)


def _conv_step_body(st_ref, u_ref, x_ref, dw_ref, dwb_ref, lng_ref, lnb_ref, w2_ref, b2_ref,
                    o_ref, ns_ref):
    hist = st_ref.shape[0]
    u = u_ref[...]
    y = u * dw_ref[hist:hist + 1, :] + dwb_ref[...]
    for k in range(hist):
        y = y + st_ref[k] * dw_ref[k:k + 1, :]
        ns_ref[k] = st_ref[k + 1] if k + 1 < hist else u
    o_ref[...] = _ln_silu_proj(y, lng_ref[...], lnb_ref[...], w2_ref[...], b2_ref[...], x_ref[...])


def _conv_step(state, glu, x, dw, dwb, lng, lnb, w2, b2):
    hist, nb, _ = state.shape
    return pl.pallas_call(
        _conv_step_body,
        grid=(1,),
        in_specs=[_const_spec((hist, nb, D_MODEL)), _const_spec((nb, D_MODEL)), _const_spec((nb, D_MODEL)),
                  _const_spec((CONV_W, D_MODEL)), _const_spec((1, D_MODEL)), _const_spec((1, D_MODEL)),
                  _const_spec((1, D_MODEL)), _const_spec((D_MODEL, D_MODEL)), _const_spec((1, D_MODEL))],
        out_specs=[_const_spec((nb, D_MODEL)), _const_spec((hist, nb, D_MODEL))],
        out_shape=[jax.ShapeDtypeStruct((nb, D_MODEL), F32),
                   jax.ShapeDtypeStruct((hist, nb, D_MODEL), F32)],
        compiler_params=_params(("arbitrary",)),
        name="conv_step",
    )(state, glu, x, dw, dwb, lng, lnb, w2, b2)


def _mlp_body(final, pre, tf, *refs):
    if pre:
        o_in_ref, wo_ref, refs = refs[0], refs[1], refs[2:]
    x_ref, gf_ref, up_ref, dn_ref, gp_ref, wg_ref, p_ref, wp_ref, gfin_ref, o_ref = refs
    x1 = x_ref[...]
    if pre:
        x1 = x1 + _dot(o_in_ref[...].astype(BF16), wo_ref[...])
    h = _rms(x1, gf_ref[...]).astype(BF16)
    ple = _dot(p_ref[...].astype(BF16), wp_ref[...])
    x2 = x1
    for c in range(0, D_FF, tf):
        a = jnp.maximum(_dot(h, up_ref[:, c:c + tf]), 0.0)
        x2 = x2 + _dot((a * a).astype(BF16), dn_ref[c:c + tf, :])
    gate = _sigmoid(_dot(_rms(x2, gp_ref[...]).astype(BF16), wg_ref[...]))
    x3 = x2 + gate * ple
    if final:
        x3 = _rms(x3, gfin_ref[...])
    o_ref[...] = x3


def _mlp_ple(x, gf, up, dn, gp, wg, p, layer, wp, gfin, tm, tf, final, pre=None):
    n = x.shape[0]
    tok = lambda i: (i, 0)
    one = pl.Buffered(1)
    in_specs, args = [], []
    if pre is not None:
        in_specs += [pl.BlockSpec((tm, D_MODEL), tok), _const_spec((D_MODEL, D_MODEL), single=True)]
        args += list(pre)
    in_specs += [pl.BlockSpec((tm, D_MODEL), tok),
                 _const_spec((1, D_MODEL)),
                 pl.BlockSpec((None, D_MODEL, D_FF), lambda i: (layer, 0, 0), pipeline_mode=one),
                 pl.BlockSpec((None, D_FF, D_MODEL), lambda i: (layer, 0, 0), pipeline_mode=one),
                 _const_spec((1, D_MODEL)),
                 pl.BlockSpec((None, D_MODEL, D_MODEL), lambda i: (layer, 0, 0), pipeline_mode=one),
                 pl.BlockSpec((None, tm, D_PLE), lambda i: (layer, i, 0)),
                 pl.BlockSpec((None, D_PLE, D_MODEL), lambda i: (layer, 0, 0), pipeline_mode=one),
                 _const_spec((1, D_MODEL))]
    args += [x, gf, up, dn, gp, wg, p, wp, gfin]
    return pl.pallas_call(
        functools.partial(_mlp_body, final, pre is not None, tf),
        grid=(n // tm,),
        in_specs=in_specs,
        out_specs=pl.BlockSpec((tm, D_MODEL), tok),
        out_shape=jax.ShapeDtypeStruct((n, D_MODEL), F32),
        compiler_params=_params(("parallel",)),
        name="mlp_ple",
    )(*args)


KV_NAMES = ("kc", "vc", "ks", "vs", "kw", "vw")
KV_COMPRESSED = ("kc", "vc")
KV_TOKEN_MAJOR = ("ks", "kw")
KV_WINDOWED = ("kw", "vw")


def _attn_in_body(seq_len, x_ref, g_ref, wq_ref, wkv_ref, wg_ref, q_ref, gt_ref, *kv_refs):
    h = _rms(x_ref[...], g_ref[...]).astype(BF16)
    q_ref[...] = _dot(h, wq_ref[...])
    gt_ref[...] = _sigmoid(_dot(h, wg_ref[...]))
    kv = _dot(h, wkv_ref[...])
    blocks = {n: kv[:, KV_W * i:KV_W * (i + 1)] for i, n in enumerate(KV_NAMES)}
    if seq_len is None:
        for n, r in zip(KV_NAMES, kv_refs):
            r[...] = blocks[n]
        return
    n_half = 2 * len(KV_COMPRESSED)
    n_tm = n_half + len(KV_TOKEN_MAJOR)
    half_refs = kv_refs[:n_half]
    tm_refs = kv_refs[n_half:n_tm]
    fm_refs = kv_refs[n_tm:n_tm + len(KV_NAMES)]
    last_refs = dict(zip(KV_WINDOWED, kv_refs[n_tm + len(KV_NAMES):]))
    for i, n in enumerate(KV_COMPRESSED):
        half_refs[2 * i][...] = blocks[n][:, :128]
        half_refs[2 * i + 1][...] = blocks[n][:, 128:]
    for n, r in zip(KV_TOKEN_MAJOR, tm_refs):
        r[...] = blocks[n]
    tps = seq_len // x_ref.shape[0]
    is_last = pl.program_id(0) % tps == tps - 1
    for n, r in zip(KV_NAMES, fm_refs):
        bt = blocks[n].T
        for g in range(N_KV):
            r[0, g] = bt[HEAD_DIM * g:HEAD_DIM * (g + 1), :]
        if n in last_refs:

            @pl.when(is_last)
            def _(bt=bt, dst=last_refs[n]):
                for g in range(N_KV):
                    dst[0, g] = bt[HEAD_DIM * g:HEAD_DIM * (g + 1), :]


def _attn_in(x, g, wq, wkv, wg, tm, seq_len=None):
    n = x.shape[0]
    tok = lambda i: (i, 0)
    kv_spec = pl.BlockSpec((tm, KV_W), tok)
    kv_shape = jax.ShapeDtypeStruct((n, KV_W), F32)
    if seq_len is None:
        kv_specs, kv_shapes = [kv_spec] * 6, [kv_shape] * 6
    else:
        tps = seq_len // tm
        fm_spec = pl.BlockSpec((1, N_KV, HEAD_DIM, tm), lambda i: (i // tps, 0, 0, i % tps))
        fm_shape = jax.ShapeDtypeStruct((n // seq_len, N_KV, HEAD_DIM, seq_len), F32)
        assert tm == min(WINDOW, seq_len)
        last_spec = pl.BlockSpec((1, N_KV, HEAD_DIM, tm), lambda i: (i // tps, 0, 0, 0))
        last_shape = jax.ShapeDtypeStruct((n // seq_len, N_KV, HEAD_DIM, tm), F32)
        half_spec = pl.BlockSpec((tm, 128), tok)
        half_shape = jax.ShapeDtypeStruct((n, 128), F32)
        n_half = 2 * len(KV_COMPRESSED)
        kv_specs = ([half_spec] * n_half + [kv_spec] * len(KV_TOKEN_MAJOR) + [fm_spec] * 6
                    + [last_spec] * len(KV_WINDOWED))
        kv_shapes = ([half_shape] * n_half + [kv_shape] * len(KV_TOKEN_MAJOR) + [fm_shape] * 6
                     + [last_shape] * len(KV_WINDOWED))
    return pl.pallas_call(
        functools.partial(_attn_in_body, seq_len),
        grid=(n // tm,),
        in_specs=[pl.BlockSpec((tm, D_MODEL), tok), _const_spec((1, D_MODEL)),
                  _const_spec((D_MODEL, D_MODEL)), _const_spec((D_MODEL, 6 * KV_W)),
                  _const_spec((D_MODEL, 128))],
        out_specs=[pl.BlockSpec((tm, D_MODEL), tok), pl.BlockSpec((tm, 128), tok)] + kv_specs,
        out_shape=[jax.ShapeDtypeStruct((n, D_MODEL), F32), jax.ShapeDtypeStruct((n, 128), F32)] + kv_shapes,
        compiler_params=_params(("arbitrary",)),
        name="attn_in",
    )(x, g, wq, wkv, wg)


def _compress_hidden(load_rows, n_chunks, wexp_ref):
    lo = jnp.concatenate([load_rows(2 * r) for r in range(CMP_STRIDE)], axis=1)
    hi = jnp.concatenate([load_rows(2 * r + 1) for r in range(CMP_STRIDE)], axis=1)
    acc = _dot(jnp.concatenate([lo, hi], axis=0).astype(BF16), wexp_ref[...])
    lo, hi = acc[:n_chunks], acc[n_chunks:]
    w = 2 * CMP_HID
    return jnp.concatenate([lo[:, :w], hi[:, :w], lo[:, w:], hi[:, w:]], axis=1)


def _pe_term(pe_ref, w1_ref):
    t = _dot(pe_ref[...], w1_ref[...])[0:1, :]
    return jnp.concatenate([t] * N_KV, axis=1)


def _compress_p_body(nc, kcl_ref, kch_ref, vcl_ref, vch_ref, wek_ref, pek_ref, w1k_ref, w2k_ref,
                     wev_ref, pev_ref, w1v_ref, w2v_ref, ok_ref, ov_ref):
    half = N_KV * CMP_HID
    row = lax.broadcasted_iota(jnp.int32, (nc, KV_W), 0)
    for src, we, pe, w1, w2, out in (((kcl_ref, kch_ref), wek_ref, pek_ref, w1k_ref, w2k_ref, ok_ref),
                                     ((vcl_ref, vch_ref), wev_ref, pev_ref, w1v_ref, w2v_ref, ov_ref)):
        hh = _compress_hidden(lambda k: src[k % 2][0, pl.ds(k // 2, nc, stride=CMP_STRIDE), :], nc, we)
        nxt = pltpu.roll(hh[:, half:], nc - 1, 0)
        a = hh[:, :half] + nxt + _pe_term(pe, w1)
        a = a * _sigmoid(a)
        res = _dot(a.astype(BF16), w2[...])
        out[0] = jnp.where(row < nc - 1, res, 0.0)


def _compress_prompt(kc_halves, vc_halves, wk, wv):
    b, t, _ = kc_halves[0].shape
    nc = t // CMP_STRIDE
    seq = pl.BlockSpec((1, t, 128), lambda i: (i, 0, 0))
    wspecs = [_const_spec((CMP_STRIDE * 128, 4 * CMP_HID)),
              _const_spec((8, 2 * CMP_STRIDE * HEAD_DIM)),
              _const_spec((2 * CMP_STRIDE * HEAD_DIM, CMP_HID)), _const_spec((N_KV * CMP_HID, KV_W))]
    out = pl.BlockSpec((1, nc, KV_W), lambda i: (i, 0, 0))
    return pl.pallas_call(
        functools.partial(_compress_p_body, nc),
        grid=(b,),
        in_specs=[seq] * 4 + wspecs + wspecs,
        out_specs=[out, out],
        out_shape=[jax.ShapeDtypeStruct((b, nc, KV_W), F32)] * 2,
        compiler_params=_params(("parallel",)),
        name="compress_prompt",
    )(*kc_halves, *vc_halves, *wk, *wv)


def _softmax_cols(s, mask):
    s = jnp.where(mask, s, NEG_INF)
    m = jnp.max(s, axis=0, keepdims=True)
    m = jnp.where(m == NEG_INF, 0.0, m)
    e = jnp.where(mask, jnp.exp(s - m), 0.0)
    return e * (1.0 / jnp.maximum(jnp.sum(e, axis=0, keepdims=True), 1e-30))


def _attn_p_body(tq, t, q_ref, gt_ref, ks_ref, vs_ref, kw_ref, vw_ref, kcm_ref, vcm_ref,
                 tbl_ref, bt_ref, bc_ref, msel_ref, o_ref,
                 ksb, kwb, vst, vwt, kcb, vct, btoep, bcmp, qt, qt2, selt, ot, m_ref, l_ref, acc_ref, gt_s):
    b = pl.program_id(0)
    qi = pl.program_id(1)
    tk = ATTN_TK
    nkt = t // tk
    nqt = t // tq
    assert tq == tk
    ncmp = kcm_ref.shape[1]
    q0 = qi * tq
    n_near = bt_ref.shape[0]

    @pl.when((b == 0) & (qi == 0))
    def _():
        def per_head(h, carry):
            far = tbl_ref[NUM_BUCKETS - 1, h]
            for d in range(n_near):
                bk = bt_ref[d]
                acc = jnp.zeros((tk, tq), F32)
                for bb in range(NUM_BUCKETS):
                    acc = jnp.where(bk == bb, (tbl_ref[bb, h] - far) * LOG2E, acc)
                btoep[h, d] = acc
            btoep[h, n_near] = jnp.zeros((tk, tq), F32)
            bk = bc_ref[...]
            acc = jnp.zeros(bk.shape, F32)
            for bb in range(NUM_BUCKETS):
                acc = jnp.where(bk == bb, tbl_ref[bb, h], acc)
            bcmp[h] = acc
            return carry
        lax.fori_loop(0, N_HEADS, per_head, 0)

    @pl.when(qi == 0)
    def _():
        for kt in range(nkt):
            rows = slice(kt * tk, (kt + 1) * tk)
            for src, dstk in ((ks_ref, ksb), (kw_ref, kwb)):
                blk = src[0, rows, :]
                for g in range(N_KV):
                    dstk[g, rows, :] = blk[:, HEAD_DIM * g:HEAD_DIM * (g + 1)].astype(BF16)
            for src, dstv in ((vs_ref, vst), (vw_ref, vwt)):
                for g in range(N_KV):
                    dstv[g, kt] = src[0, g, :, rows].astype(BF16)
        kc = kcm_ref[0]
        vc_t = vcm_ref[0].T
        for g in range(N_KV):
            kcb[g] = kc[:, HEAD_DIM * g:HEAD_DIM * (g + 1)].astype(BF16)
            vct[g] = vc_t[HEAD_DIM * g:HEAD_DIM * (g + 1), :].astype(BF16)

    q_t = q_ref[0].T
    qt[...] = (q_t * SCALE).astype(BF16)
    qt2[...] = (q_t * (SCALE * LOG2E)).astype(BF16)
    gt_s[...] = gt_ref[0].T

    qpos_row = q0 + lax.broadcasted_iota(jnp.int32, (1, tq), 1)
    i_iota = lax.broadcasted_iota(jnp.int32, (tk, tq), 1)
    j_iota = lax.broadcasted_iota(jnp.int32, (tk, tq), 0)
    rel = i_iota - j_iota

    c_iota = lax.broadcasted_iota(jnp.int32, (ncmp, tq), 0)
    cmask = (CMP_STRIDE * c_iota + (2 * CMP_STRIDE - 1)) <= qpos_row
    coff = pl.multiple_of((nqt - 1 - qi) * (tq // CMP_STRIDE), 8)

    nblk = t // SLC_BLOCK
    blk_iota = lax.broadcasted_iota(jnp.int32, (nblk, tq), 0)
    cur = qpos_row // SLC_BLOCK
    valid = blk_iota * SLC_BLOCK <= qpos_row
    forced = (blk_iota == 0) | (blk_iota == cur) | (blk_iota == cur - 1)

    heads = range(N_HEADS)
    head_rows = [slice(HEAD_DIM * h, HEAD_DIM * (h + 1)) for h in heads]
    s_c = [_dot(kcb[h // HPG], qt[head_rows[h], :]) + bcmp[h, pl.ds(coff, ncmp), :] for h in heads]
    p_c = [_softmax_cols(s, cmask) for s in s_c]
    o_c = [_dot(vct[h // HPG], p_c[h].astype(BF16)) for h in heads]
    for h in heads:
        ot[head_rows[h], :] = gt_s[h:h + 1, :] * o_c[h]
    msel = msel_ref[...]
    scores = []
    for g in range(N_KV):
        pg = p_c[g * HPG]
        for hh in range(1, HPG):
            pg = pg + p_c[g * HPG + hh]
        hi, mid, lo = _split3(pg)
        score = _dot(msel, hi) + _dot(msel, mid) + _dot(msel, lo)
        scores.append(jnp.where(forced, jnp.inf, jnp.where(valid, score, NEG_INF)))
    n_slab = nblk // 8
    slabs = [[sc[8 * v:8 * (v + 1), :] for v in range(n_slab)] for sc in scores]
    ranks = [[jnp.zeros((8, tq), F32) for _ in range(n_slab)] for _ in scores]
    row8 = lax.broadcasted_iota(jnp.int32, (8, tq), 0)
    for i in range(nblk):
        for g in range(N_KV):
            si = scores[g][i:i + 1, :]
            for v, sl in enumerate(slabs[g]):
                if 8 * v > i:
                    beats = si >= sl
                elif 8 * v + 7 <= i:
                    beats = si > sl
                else:
                    beats = (si > sl) | ((row8 > i - 8 * v) & (si == sl))
                ranks[g][v] = ranks[g][v] + jnp.where(beats, 1.0, 0.0)
    for g in range(N_KV):
        for v, rk in enumerate(ranks[g]):
            sel = (rk < N_SELECT).astype(F32)
            for r in range(8):
                selt[g, 8 * v + r] = sel[r:r + 1, :]

    def flash_init(br):
        m_ref[br] = jnp.full(m_ref.shape[1:], NEG_INF, F32)
        l_ref[br] = jnp.zeros(l_ref.shape[1:], F32)
        acc_ref[br] = jnp.zeros(acc_ref.shape[1:], F32)

    def tile_step(br, kb, vt, kt, bias_d, add_fn):
        k0 = pl.multiple_of(kt * tk, tk)
        for g in range(N_KV):
            kblk = kb[g, pl.ds(k0, tk), :]
            vblk = vt[g, kt]
            add = add_fn(g)
            for hh in range(HPG):
                h = g * HPG + hh
                rows = slice(HEAD_DIM * h, HEAD_DIM * (h + 1))
                s = _dot(kblk, qt2[rows, :])
                if bias_d is not None:
                    s = s + btoep[h, bias_d]
                if add is not None:
                    s = s + add
                m_old = m_ref[br, h]
                m_new = jnp.maximum(m_old, jnp.max(s, axis=0, keepdims=True))
                m_safe = jnp.where(m_new == NEG_INF, 0.0, m_new)
                alpha = jnp.exp2(m_old - m_safe)
                p = jnp.exp2(s - m_safe)
                l_ref[br, h] = alpha * l_ref[br, h] + jnp.sum(p, axis=0, keepdims=True)
                acc_ref[br, rows, :] = alpha * acc_ref[br, rows, :] + _dot(vblk, p.astype(BF16))
                m_ref[br, h] = m_new

    def flash_finish(br, gate_row0):
        for h in range(N_HEADS):
            rows = slice(HEAD_DIM * h, HEAD_DIM * (h + 1))
            w = gt_s[gate_row0 + h:gate_row0 + h + 1, :] * (1.0 / jnp.maximum(l_ref[br, h], 1e-30))
            ot[rows, :] = ot[rows, :] + acc_ref[br, rows, :] * w

    def sel_rows(g, kt, ok=None):
        nb = tk // SLC_BLOCK
        rows = [jnp.broadcast_to(selt[g, kt * nb + r], (SLC_BLOCK, tq)) for r in range(nb)]
        thr = 0.5 if ok is None else jnp.where(ok, 0.5, 2.0)
        return jnp.concatenate(rows, axis=0) > thr

    def neg_unless(cond):
        return jnp.where(cond, 0.0, NEG_INF)

    causal = rel >= 0
    SLC, WIN = 0, 1

    flash_init(SLC)
    flash_init(WIN)
    n_far = jnp.maximum(qi - 1, 0)

    def slc_pair(pi, carry):
        for sub in range(3):
            kt = 3 * pi + sub
            ok = kt < n_far
            ktc = jnp.minimum(kt, jnp.maximum(n_far - 1, 0))
            tile_step(SLC, ksb, vst, ktc, None, lambda g: neg_unless(sel_rows(g, ktc, ok)))
        return carry

    lax.fori_loop(0, (n_far + 2) // 3, slc_pair, 0)
    kt1 = jnp.maximum(qi - 1, 0)
    tile_step(SLC, ksb, vst, kt1, 1, lambda g: neg_unless(sel_rows(g, kt1, qi >= 1)))
    tile_step(SLC, ksb, vst, qi, 0, lambda g: neg_unless(sel_rows(g, qi) & causal))

    wt = WINDOW // tk
    for u in range(wt + 1):
        off = wt - u
        kt = qi - off
        ok_add = jnp.where(kt >= 0, 0.0, NEG_INF)
        ktc = jnp.maximum(kt, 0)
        if off == wt:
            add_fn = lambda g, a=ok_add: neg_unless(rel <= 0) + a
        elif off == 0:
            add_fn = lambda g: neg_unless(causal)
        else:
            add_fn = lambda g, a=ok_add: a
        tile_step(WIN, kwb, vwt, ktc, off if off <= 1 else None, add_fn)
    flash_finish(SLC, N_HEADS)
    flash_finish(WIN, 2 * N_HEADS)

    o_ref[0] = ot[...].T


def _attn_prompt(q, gates, ks, vs, kw, vw, kcmp, vcmp, table, bkt_toep, bkt_cmp, msel, tq):
    b, t, _ = q.shape
    tk = ATTN_TK
    nkt = t // tk
    ncmp = kcmp.shape[1]
    n_near = bkt_toep.shape[0]
    qspec = pl.BlockSpec((1, tq, D_MODEL), lambda bi, i: (bi, i, 0))
    seq = pl.BlockSpec((1, t, KV_W), lambda bi, i: (bi, 0, 0))
    seq_fm = pl.BlockSpec((1, N_KV, HEAD_DIM, t), lambda bi, i: (bi, 0, 0, 0))
    cmp_spec = pl.BlockSpec((1, ncmp, KV_W), lambda bi, i: (bi, 0, 0))
    return pl.pallas_call(
        functools.partial(_attn_p_body, tq, t),
        grid=(b, t // tq),
        in_specs=[qspec, pl.BlockSpec((1, tq, 128), lambda bi, i: (bi, i, 0)),
                  seq, seq_fm, seq, seq_fm, cmp_spec, cmp_spec,
                  pl.BlockSpec(memory_space=pltpu.SMEM),
                  _const_spec(bkt_toep.shape), _const_spec(bkt_cmp.shape), _const_spec(msel.shape)],
        out_specs=qspec,
        out_shape=jax.ShapeDtypeStruct((b, t, D_MODEL), F32),
        scratch_shapes=[pltpu.VMEM((N_KV, t, HEAD_DIM), BF16), pltpu.VMEM((N_KV, t, HEAD_DIM), BF16),
                        pltpu.VMEM((N_KV, nkt, HEAD_DIM, tk), BF16), pltpu.VMEM((N_KV, nkt, HEAD_DIM, tk), BF16),
                        pltpu.VMEM((N_KV, ncmp, HEAD_DIM), BF16), pltpu.VMEM((N_KV, HEAD_DIM, ncmp), BF16),
                        pltpu.VMEM((N_HEADS, n_near + 1, tk, tq), F32),
                        pltpu.VMEM((N_HEADS,) + bkt_cmp.shape, F32),
                        pltpu.VMEM((D_MODEL, tq), BF16), pltpu.VMEM((D_MODEL, tq), BF16),
                        pltpu.VMEM((N_KV, t // SLC_BLOCK, 1, tq), F32),
                        pltpu.VMEM((D_MODEL, tq), F32),
                        pltpu.VMEM((2, N_HEADS, 1, tq), F32), pltpu.VMEM((2, N_HEADS, 1, tq), F32),
                        pltpu.VMEM((2, D_MODEL, tq), F32), pltpu.VMEM((128, tq), F32)],
        compiler_params=_params(("arbitrary", "arbitrary")),
        name="attn_prompt",
    )(q, gates, ks, vs, kw, vw, kcmp, vcmp, table, bkt_toep, bkt_cmp, msel)


PAGE = 128
PAGES_PER_GROUP = 32
Q_ROWS = 8


def _rank_desc(score_row, n):
    a = jnp.broadcast_to(score_row, (n, n))
    at = a.T
    i = lax.broadcasted_iota(jnp.int32, (n, n), 0)
    j = lax.broadcasted_iota(jnp.int32, (n, n), 1)
    beats = (at > a) | ((i < j) & (at == a))
    return jnp.sum(beats.astype(F32), axis=0, keepdims=True)


def _bias_rows(bucket, tcols):
    r = tcols.shape[0]
    out = jnp.zeros((r, bucket.shape[1]), F32)
    for bb in range(NUM_BUCKETS):
        out = jnp.where(bucket == bb, tcols[:, bb:bb + 1], out)
    return out


def _softmax_rows(s, mask):
    s = jnp.where(mask, s, NEG_INF)
    m = jnp.max(s, axis=1, keepdims=True)
    m = jnp.where(m == NEG_INF, 0.0, m)
    e = jnp.where(mask, jnp.exp(s - m), 0.0)
    return e / jnp.maximum(jnp.sum(e, axis=1, keepdims=True), 1e-30)


def _dot_nt(a, b):
    return lax.dot_general(a, b, (((1,), (1,)), ((), ())), preferred_element_type=F32)


def _cmp_s_body(past, pt_ref, ck_hbm, cv_hbm, kcn_ref, vcn_ref, q_ref,
                wek_ref, pek_ref, w1k_ref, w2k_ref, wev_ref, pev_ref, w1v_ref, w2v_ref,
                tbl_ref, bkt_ref, msel_ref, ocmp_ref, idx_ref,
                kbuf, vbuf, tbuf_k, tbuf_v, hk, hv, sem):
    b = pl.program_id(0)
    rows_pg = PAGES_PER_GROUP * PAGE
    cpg = rows_pg // CMP_STRIDE
    ngrp = past // rows_pg
    nc = past // CMP_STRIDE
    half = N_KV * CMP_HID

    def copies(bb, gi, slot):
        out = []
        for p in range(PAGES_PER_GROUP):
            page = pt_ref[bb, gi * PAGES_PER_GROUP + p]
            out.append(pltpu.make_async_copy(ck_hbm.at[page], kbuf.at[slot, p], sem.at[0, slot]))
            out.append(pltpu.make_async_copy(cv_hbm.at[page], vbuf.at[slot, p], sem.at[1, slot]))
        return out

    assert ngrp % 2 == 0

    @pl.when(b == 0)
    def _():
        for c in copies(b, 0, 0):
            c.start()

    def group(gi, carry):
        slot = gi % 2

        @pl.when(gi + 1 < ngrp)
        def _():
            for c in copies(b, gi + 1, 1 - slot):
                c.start()

        for c in copies(b, gi, slot):
            c.wait()
        transpose(kbuf, tbuf_k, slot)
        matmul(tbuf_v, wev_ref, hv, jnp.maximum(gi - 1, 0))
        matmul(tbuf_k, wek_ref, hk, gi)
        transpose(vbuf, tbuf_v, slot)
        return carry

    def transpose(buf, tb, slot):
        for p in range(PAGES_PER_GROUP):
            for hf in range(2):
                tb[hf, p * PAGE:(p + 1) * PAGE, :] = buf[slot, p, hf * 128:(hf + 1) * 128, :].T

    def matmul(tb, we, hdst, gi):
        r0 = pl.multiple_of(gi * cpg, cpg)
        hdst[pl.ds(r0, cpg), :] = _compress_hidden(
            lambda k: tb[k % 2, pl.ds(k // 2, cpg, stride=CMP_STRIDE), :], cpg, we)

    @pl.when(b == 0)
    def _():
        tbuf_v[...] = jnp.zeros_like(tbuf_v)

    lax.fori_loop(0, ngrp, group, 0)
    matmul(tbuf_v, wev_ref, hv, ngrp - 1)

    @pl.when(b + 1 < pl.num_programs(0))
    def _():
        for c in copies(b + 1, 0, 0):
            c.start()

    col = lax.broadcasted_iota(jnp.int32, (1, nc), 1)
    cmask = (CMP_STRIDE * col + (2 * CMP_STRIDE - 1)) <= past
    row8 = lax.broadcasted_iota(jnp.int32, (Q_ROWS, nc), 0)
    cmpd = []
    for new_ref, we, pe, w1, w2, hdst in ((kcn_ref, wek_ref, pek_ref, w1k_ref, w2k_ref, hk),
                                          (vcn_ref, wev_ref, pev_ref, w1v_ref, w2v_ref, hv)):
        new = jnp.broadcast_to(new_ref[0], (8, KV_W)).astype(BF16)
        new_lo = _dot(new[:, :128], we[0:128, :])
        new_hi = _dot(new[:, 128:], we[0:128, :])
        w = 2 * CMP_HID
        hdst[nc:nc + 8, :] = jnp.concatenate([new_lo[:, :w], new_hi[:, :w], new_lo[:, w:], new_hi[:, w:]], axis=1)
        a = hdst[0:nc, 0:half] + hdst[1:nc + 1, half:2 * half] + _pe_term(pe, w1)
        a = (a * _sigmoid(a)).astype(BF16)
        cmpd.append([_dot(a[:, CMP_HID * g:CMP_HID * (g + 1)], w2[...]).astype(BF16) for g in range(N_KV)])

    nsel = msel_ref.shape[1]
    lane = lax.broadcasted_iota(jnp.int32, (1, nsel), 1)
    n_blocks = past // SLC_BLOCK + 1
    cur = past // SLC_BLOCK
    forced = (lane == 0) | (lane == cur) | (lane == cur - 1)
    valid = (lane * SLC_BLOCK <= past) & (lane < n_blocks)
    k_iota = lax.broadcasted_iota(jnp.int32, (N_SELECT, nsel), 0).astype(F32)
    lane_f = lax.broadcasted_iota(jnp.int32, (N_SELECT, nsel), 1).astype(F32)
    groups = range(N_KV)
    msel = msel_ref[...]
    bkt = bkt_ref[...]
    s_g = [_dot_nt(q_ref[0, g].astype(BF16), cmpd[0][g]) * SCALE + _bias_rows(bkt, tbl_ref[g]) for g in groups]
    p_g = [_softmax_rows(s, cmask) for s in s_g]
    for g in groups:
        ocmp_ref[0, g] = _dot(p_g[g].astype(BF16), cmpd[1][g])
    pgs = [jnp.sum(jnp.where(row8 < HPG, p, 0.0), axis=0, keepdims=True) for p in p_g]
    parts = [_split3(jnp.broadcast_to(pg, (8, nc))) for pg in pgs]
    scores = [(_dot(hi, msel) + _dot(mid, msel) + _dot(lo, msel))[0:1, :] for hi, mid, lo in parts]
    scores = [jnp.where(forced, jnp.inf, jnp.where(valid, sc, NEG_INF)) for sc in scores]
    rank_g = [_rank_desc(sc, nsel) for sc in scores]
    hits = [jnp.where(jnp.broadcast_to(rk, (N_SELECT, nsel)) == k_iota, lane_f, 0.0) for rk in rank_g]
    for g in groups:
        idx_ref[0, g] = jnp.sum(hits[g], axis=1, keepdims=True).astype(jnp.int32)


def _cmp_sample(page_table, cache_k, cache_v, kc_new, vc_new, q8, wk, wv, tbl8, bkt, msel, past):
    nb = page_table.shape[0]
    rows_pg = PAGES_PER_GROUP * PAGE
    nc = past // CMP_STRIDE
    any_spec = pl.BlockSpec(memory_space=pl.ANY)
    new_spec = pl.BlockSpec((1, 1, KV_W), lambda i, pt: (i, 0, 0))
    cs = lambda shape: pl.BlockSpec(shape, lambda i, pt: (0,) * len(shape))
    wspecs = [cs((CMP_STRIDE * 128, 4 * CMP_HID)),
              cs((8, 2 * CMP_STRIDE * HEAD_DIM)),
              cs((2 * CMP_STRIDE * HEAD_DIM, CMP_HID)), cs((CMP_HID, HEAD_DIM))]
    grid_spec = pltpu.PrefetchScalarGridSpec(
        num_scalar_prefetch=1,
        grid=(nb,),
        in_specs=[any_spec, any_spec, new_spec, new_spec,
                  pl.BlockSpec((1, N_KV, Q_ROWS, HEAD_DIM), lambda i, pt: (i, 0, 0, 0))]
                 + wspecs + wspecs + [cs(tbl8.shape), cs(bkt.shape), cs(msel.shape)],
        out_specs=[pl.BlockSpec((1, N_KV, Q_ROWS, HEAD_DIM), lambda i, pt: (i, 0, 0, 0)),
                   pl.BlockSpec((1, N_KV, N_SELECT, 1), lambda i, pt: (i, 0, 0, 0))],
        scratch_shapes=[pltpu.VMEM((2, PAGES_PER_GROUP, 2 * PAGE, 128), F32),
                        pltpu.VMEM((2, PAGES_PER_GROUP, 2 * PAGE, 128), F32),
                        pltpu.VMEM((2, rows_pg, 128), F32), pltpu.VMEM((2, rows_pg, 128), F32),
                        pltpu.VMEM((nc + 8, 2 * N_KV * CMP_HID), F32),
                        pltpu.VMEM((nc + 8, 2 * N_KV * CMP_HID), F32),
                        pltpu.SemaphoreType.DMA((2, 2))])
    return pl.pallas_call(
        functools.partial(_cmp_s_body, past),
        grid_spec=grid_spec,
        out_shape=[jax.ShapeDtypeStruct((nb, N_KV, Q_ROWS, HEAD_DIM), F32),
                   jax.ShapeDtypeStruct((nb, N_KV, N_SELECT, 1), jnp.int32)],
        compiler_params=_params(("arbitrary",)),
        name="cmp_sample",
    )(page_table, cache_k, cache_v, kc_new, vc_new, q8, *wk, *wv, tbl8, bkt, msel)


NEW_LANES = 128


def _dec_s_body(past, idx_ref, pt_ref, csk_hbm, csv_hbm,
                q_ref, ksn_ref, vsn_ref, kwn_ref, vwn_ref, wk_ref, wv_ref, gt_ref, ocmp_ref,
                tbl_ref, bktw_ref, o_ref, wko_ref, wvo_ref,
                kbuf, vbuf, kwbuf, vwbuf, sem):
    b = pl.program_id(0)
    n_pages = past // PAGE
    cur = past // SLC_BLOCK
    bpp = PAGE // SLC_BLOCK
    lsel = N_SELECT * PAGE
    wb = wk_ref.shape[3]

    @pl.when(b == 0)
    def _():
        kbuf[...] = jnp.zeros_like(kbuf)
        vbuf[...] = jnp.zeros_like(vbuf)
        kwbuf[...] = jnp.zeros_like(kwbuf)
        vwbuf[...] = jnp.zeros_like(vwbuf)

    def gathers(bb):
        half = bb % 2
        out = []
        for g in range(N_KV):
            for k in range(N_SELECT):
                blk = idx_ref[bb, g * N_SELECT + k]
                page = pt_ref[bb, jnp.minimum(blk // bpp, n_pages - 1)]
                dst = pl.ds(k * PAGE, PAGE)
                out.append(pltpu.make_async_copy(csk_hbm.at[page, g], kbuf.at[half, g, :, dst], sem.at[0, half]))
                out.append(pltpu.make_async_copy(csv_hbm.at[page, g], vbuf.at[half, g, :, dst], sem.at[1, half]))
        return out

    @pl.when(b == 0)
    def _():
        for c in gathers(b):
            c.start()

    @pl.when(b + 1 < pl.num_programs(0))
    def _():
        for c in gathers(b + 1):
            c.start()

    cur_half = b % 2

    lane_wb = lax.broadcasted_iota(jnp.int32, (HEAD_DIM, wb), 1)
    eye = (lax.broadcasted_iota(jnp.int32, (HEAD_DIM, HEAD_DIM), 0)
           == lax.broadcasted_iota(jnp.int32, (HEAD_DIM, HEAD_DIM), 1))

    def group_col(new_ref, g):
        r = new_ref[0][:, HEAD_DIM * g:HEAD_DIM * (g + 1)]
        return jnp.sum(jnp.where(eye, jnp.broadcast_to(r, (HEAD_DIM, HEAD_DIM)), 0.0), axis=1, keepdims=True)

    for g in range(N_KV):
        for src, new_ref, dst, buf in ((wk_ref, kwn_ref, wko_ref, kwbuf), (wv_ref, vwn_ref, wvo_ref, vwbuf)):
            st = src[0, g]
            newc = group_col(new_ref, g)
            dst[0, g] = jnp.where(lane_wb == wb - 1, newc, pltpu.roll(st, wb - 1, 1))
            buf[g, :, 0:wb] = st
            buf[g, :, wb:wb + 1] = newc

    for c in gathers(b):
        c.wait()

    ls = lsel + NEW_LANES
    lw = wb + NEW_LANES
    lane_s = lax.broadcasted_iota(jnp.int32, (1, ls), 1)
    lane_w = lax.broadcasted_iota(jnp.int32, (1, lw), 1)
    wmask = lane_w <= wb
    tok = lane_s % PAGE
    groups = range(N_KV)
    smasks, sbiases = [], []
    for g in groups:
        kbuf[cur_half, g, :, lsel:lsel + 1] = group_col(ksn_ref, g)
        vbuf[cur_half, g, :, lsel:lsel + 1] = group_col(vsn_ref, g)
        kpos = past + (lane_s - lsel)
        gathered = lane_s < 0
        has_new = jnp.int32(0)
        for k in range(N_SELECT):
            blk = idx_ref[b, g * N_SELECT + k]
            in_slot = (lane_s // PAGE) == k
            kpos = jnp.where(in_slot, (blk // bpp) * PAGE + tok, kpos)
            gathered = gathered | (in_slot & (blk < cur) & ((tok // SLC_BLOCK) == (blk % bpp)))
            has_new = has_new | (blk == cur).astype(jnp.int32)
        smasks.append((gathered | ((lane_s == lsel) & (has_new > 0))) & (kpos <= past))
        sbiases.append(_bias_rows(_rel_bucket(past - kpos), tbl_ref[g]))
    bktw = bktw_ref[...]
    qgs = [q_ref[0, g].astype(BF16) for g in groups]
    s_slc = [_dot(qgs[g], kbuf[cur_half, g].astype(BF16)) * SCALE + sbiases[g] for g in groups]
    s_win = [_dot(qgs[g], kwbuf[g].astype(BF16)) * SCALE + _bias_rows(bktw, tbl_ref[g]) for g in groups]
    p_slc = [_softmax_rows(s_slc[g], smasks[g]) for g in groups]
    p_win = [_softmax_rows(s, wmask) for s in s_win]
    o_slc = [_dot_nt(p_slc[g].astype(BF16), vbuf[cur_half, g].astype(BF16)) for g in groups]
    o_win = [_dot_nt(p_win[g].astype(BF16), vwbuf[g].astype(BF16)) for g in groups]
    for g in groups:
        gt = gt_ref[0, g]
        o_ref[0, g] = gt[:, 0:1] * ocmp_ref[0, g] + gt[:, 1:2] * o_slc[g] + gt[:, 2:3] * o_win[g]


def _dec_sample(idx, page_table, cache_sk, cache_sv, q8, ks_new, vs_new, kw_new, vw_new, state_wk, state_wv,
                gates8, ocmp, tbl8, bktw, past):
    nb, wb = state_wk.shape[0], state_wk.shape[3]
    any_spec = pl.BlockSpec(memory_space=pl.ANY)
    new_spec = pl.BlockSpec((1, 1, KV_W), lambda i, *_: (i, 0, 0))
    st_spec = pl.BlockSpec((1, N_KV, HEAD_DIM, wb), lambda i, *_: (i, 0, 0, 0))
    head_spec = pl.BlockSpec((1, N_KV, Q_ROWS, HEAD_DIM), lambda i, *_: (i, 0, 0, 0))
    cs = lambda shape: pl.BlockSpec(shape, lambda i, *_: (0,) * len(shape))
    lsel = N_SELECT * PAGE
    grid_spec = pltpu.PrefetchScalarGridSpec(
        num_scalar_prefetch=2,
        grid=(nb,),
        in_specs=[any_spec, any_spec, head_spec,
                  new_spec, new_spec, new_spec, new_spec, st_spec, st_spec,
                  pl.BlockSpec((1, N_KV, Q_ROWS, 3), lambda i, *_: (i, 0, 0, 0)), head_spec,
                  cs(tbl8.shape), cs(bktw.shape)],
        out_specs=[head_spec, st_spec, st_spec],
        scratch_shapes=[pltpu.VMEM((2, N_KV, HEAD_DIM, lsel + NEW_LANES), F32),
                        pltpu.VMEM((2, N_KV, HEAD_DIM, lsel + NEW_LANES), F32),
                        pltpu.VMEM((N_KV, HEAD_DIM, wb + NEW_LANES), F32),
                        pltpu.VMEM((N_KV, HEAD_DIM, wb + NEW_LANES), F32),
                        pltpu.SemaphoreType.DMA((2, 2))])
    return pl.pallas_call(
        functools.partial(_dec_s_body, past),
        grid_spec=grid_spec,
        out_shape=[jax.ShapeDtypeStruct((nb, N_KV, Q_ROWS, HEAD_DIM), F32),
                   jax.ShapeDtypeStruct(state_wk.shape, F32),
                   jax.ShapeDtypeStruct(state_wv.shape, F32)],
        compiler_params=_params(("arbitrary",)),
        name="dec_sample",
    )(idx, page_table, cache_sk, cache_sv, q8, ks_new, vs_new, kw_new, vw_new,
      state_wk, state_wv, gates8, ocmp, tbl8, bktw)


def _expand_w1(w1):
    eye = jnp.eye(2, dtype=w1.dtype)
    parts = []
    for part in (w1[:CMP_STRIDE * HEAD_DIM], w1[CMP_STRIDE * HEAD_DIM:]):
        w = part.reshape(CMP_STRIDE, HEAD_DIM, CMP_HID)
        parts.append(jnp.einsum("gh,rdn->rgdhn", eye, w).reshape(CMP_STRIDE, 2 * HEAD_DIM, 2 * CMP_HID))
    w = jnp.concatenate(parts, axis=-1).astype(BF16)
    return w.reshape(CMP_STRIDE * 2 * HEAD_DIM, 4 * CMP_HID)


def _expand_w2(w2):
    eye = jnp.eye(N_KV, dtype=w2.dtype)
    return jnp.einsum("gh,nd->gnhd", eye, w2).reshape(N_KV * CMP_HID, KV_W).astype(BF16)


def _pe_rows(pe):
    return jnp.broadcast_to(pe.reshape(1, -1), (8, pe.size)).astype(BF16)


def _select_matrix(n_cmp, n_blocks):
    r = SLC_BLOCK // CMP_STRIDE
    m = np.zeros((n_cmp, n_blocks), np.float32)
    for s in range(n_blocks):
        for a in range(r):
            for bb in range(2):
                c = r * s + a - bb
                if 0 <= c < n_cmp:
                    m[c, s] += 1.0
    return m


def _pad_to(a, size, axis):
    pad = [(0, 0)] * a.ndim
    pad[axis] = (0, size - a.shape[axis])
    return jnp.pad(a, pad)


def kernel(x_prompt, x_sample, cache_cmp_k, cache_cmp_v, cache_slc_k, cache_slc_v, state_win_k, state_win_v, state_conv, page_table, p_prompt, p_sample, rel_table, norm_mix, norm_ffn, norm_ple, norm_final, conv_w1, conv_b1, conv_dw, conv_dwb, conv_ln_g, conv_ln_b, conv_w2, conv_b2, attn_w_in, attn_w_out, cmpk_w1, cmpk_pe, cmpk_w2, cmpv_w1, cmpv_pe, cmpv_w2, mlp_up, mlp_down, ple_proj, ple_gate):
    nbp, t, _ = x_prompt.shape
    nbs = x_sample.shape[0]
    n_p = nbp * t
    n_pool = cache_cmp_k.shape[1]
    past = page_table.shape[1] * cache_cmp_k.shape[2]
    wb = state_win_k.shape[2]
    hist = CONV_W - 1
    tm_p, tf, tt, tq = 512, 1024, 256, ATTN_TK

    row = lambda a: a.reshape(1, -1)
    bf = lambda a: a.astype(BF16)

    w1c, w2c = bf(conv_w1[0]), bf(conv_w2[0])
    dw3 = conv_dw[0].reshape(CONV_W, D_MODEL // 128, 128).transpose(1, 0, 2)
    dwb3 = conv_dwb[0].reshape(D_MODEL // 128, 1, 128)
    up, dn = bf(mlp_up), bf(mlp_down)
    wg, wp = bf(ple_gate), bf(ple_proj)
    w_in = attn_w_in[0]
    wq = bf(w_in[:, :D_MODEL])
    wkv = bf(w_in[:, D_MODEL:D_MODEL + 6 * KV_W])
    wgt = bf(_pad_to(w_in[:, D_MODEL + 6 * KV_W:], 128, 1))
    w_out = bf(attn_w_out[0])
    cw = []
    for w1, pe, w2 in ((cmpk_w1[0], cmpk_pe[0], cmpk_w2[0]), (cmpv_w1[0], cmpv_pe[0], cmpv_w2[0])):
        cw.append((_expand_w1(w1), _pe_rows(pe), bf(w1), _expand_w2(w2), bf(w2)))
    wk_p, wv_p = [c[:4] for c in cw]
    wk_s, wv_s = [c[:3] + c[4:] for c in cw]

    def tail(x, i, p, tm, final, pre=None):
        return _mlp_ple(x, row(norm_ffn[i]), up, dn, row(norm_ple[i]), wg, p, i, wp,
                        row(norm_final), tm, tf, final, pre)

    conv_args = (row(conv_ln_g[0]), row(conv_ln_b[0]), w2c, row(conv_b2[0]))

    xp = x_prompt.reshape(n_p, D_MODEL)
    glu = _conv_in(xp, row(norm_mix[0]), w1c, row(conv_b1[0]), tm_p).reshape(nbp, t, D_MODEL)
    x1 = _conv_out(glu, x_prompt, dw3, dwb3, *conv_args, tt).reshape(n_p, D_MODEL)
    conv_p = glu[:, t - hist:][None]
    pp = p_prompt.reshape(-1, n_p, D_PLE)
    ps = p_sample.reshape(-1, nbs, D_PLE)
    x2 = tail(x1, 0, pp, tm_p, False)
    q, gates, kcl, kch, vcl, vch, ks, kw, *rest = _attn_in(x2, row(norm_mix[1]), wq, wkv, wgt, tm_p, seq_len=t)
    kv_fm, win_fm = rest[:len(KV_NAMES)], rest[len(KV_NAMES):]
    vs_fm, vw_fm = kv_fm[KV_NAMES.index("vs")], kv_fm[KV_NAMES.index("vw")]
    seq = lambda a: a.reshape(nbp, t, -1)
    kcmp, vcmp = _compress_prompt((seq(kcl), seq(kch)), (seq(vcl), seq(vch)), wk_p, wv_p)

    ncmp = t // CMP_STRIDE
    ii = jnp.arange(tq, dtype=jnp.int32)[None, :]
    jj = jnp.arange(ATTN_TK, dtype=jnp.int32)[:, None]
    r_qk = tq // ATTN_TK
    bkt_toep = jnp.stack([_rel_bucket(d * ATTN_TK + ii - jj) for d in range(1 - r_qk, 2)])
    nshift = (t // tq - 1) * (tq // CMP_STRIDE)
    rr = jnp.arange(ncmp + nshift, dtype=jnp.int32)[:, None]
    bkt_cmp = _rel_bucket(ii - CMP_STRIDE * (rr - nshift) - (2 * CMP_STRIDE - 1))
    msel_p = jnp.asarray(_select_matrix(ncmp, t // SLC_BLOCK).T, BF16)
    o_p = _attn_prompt(seq(q), seq(gates), seq(ks), vs_fm, seq(kw), vw_fm, kcmp, vcmp,
                       rel_table, bkt_toep, bkt_cmp, msel_p, tq)
    y_p = tail(x2, 1, pp, tm_p, True, pre=(o_p.reshape(n_p, D_MODEL), w_out))

    from_fm = lambda a: jnp.transpose(a, (0, 3, 1, 2))[None]
    kv_p = [from_fm(a) for a in kv_fm[:4]] + [from_fm(a) for a in win_fm]

    xs = x_sample.reshape(nbs, D_MODEL)
    glu_s = _conv_in(xs, row(norm_mix[0]), w1c, row(conv_b1[0]), nbs)
    x1s, conv_s = _conv_step(jnp.transpose(state_conv[0], (1, 0, 2)), glu_s, xs, conv_dw[0],
                             row(conv_dwb[0]), *conv_args)
    conv_s = jnp.transpose(conv_s, (1, 0, 2))
    x2s = tail(x1s, 0, ps, nbs, False)
    qs, gts, kcs, vcs, kss, vss, kws, vws = _attn_in(x2s, row(norm_mix[1]), wq, wkv, wgt, nbs)

    q8 = _pad_to(qs.reshape(nbs, N_KV, HPG, HEAD_DIM), Q_ROWS, 2)
    tbl8 = _pad_to(rel_table.T.reshape(N_KV, HPG, NUM_BUCKETS), Q_ROWS, 1)
    nc_s = past // CMP_STRIDE
    cc = jnp.arange(nc_s, dtype=jnp.int32)[None, :]
    bkt_s = _rel_bucket(past - (CMP_STRIDE * cc + 2 * CMP_STRIDE - 1))
    n_blocks_s = past // SLC_BLOCK + 1
    msel_s = jnp.asarray(_select_matrix(nc_s, -(-n_blocks_s // 128) * 128), BF16)
    new3 = lambda a: a.reshape(nbs, 1, KV_W)
    fmajor = lambda c: jnp.transpose(c[0], (0, 2, 3, 1))
    cache2 = lambda c: fmajor(c).reshape(n_pool, 2 * PAGE, PAGE)
    ocmp, idx = _cmp_sample(page_table, cache2(cache_cmp_k), cache2(cache_cmp_v), new3(kcs), new3(vcs),
                            q8, wk_s, wv_s, tbl8, bkt_s, msel_s, past)

    ww = jnp.arange(wb + NEW_LANES, dtype=jnp.int32)[None, :]
    bkt_w = _rel_bucket(wb - ww)
    gates8 = _pad_to(gts[:, :3 * N_HEADS].reshape(nbs, 3, N_KV, HPG).transpose(0, 2, 3, 1), Q_ROWS, 2)
    o8, wk_new, wv_new = _dec_sample(idx.reshape(nbs, N_KV * N_SELECT), page_table,
                                     fmajor(cache_slc_k), fmajor(cache_slc_v), q8,
                                     new3(kss), new3(vss), new3(kws), new3(vws),
                                     fmajor(state_win_k), fmajor(state_win_v), gates8, ocmp, tbl8, bkt_w, past)
    o_s = o8[:, :, :HPG].reshape(nbs, D_MODEL)
    y_s = tail(x2s, 1, ps, nbs, True, pre=(o_s, w_out))

    kv5s = lambda a: a.reshape(1, nbs, 1, N_KV, HEAD_DIM)
    return (y_p.reshape(nbp, t, D_MODEL), y_s.reshape(nbs, 1, D_MODEL), *kv_p, conv_p,
            kv5s(kcs), kv5s(vcs), kv5s(kss), kv5s(vss), from_fm(wk_new), from_fm(wv_new), conv_s[None])
```
